```python
import jax
import jax.numpy as jnp
from jax import lax
import numpy as np

D_MODEL = 1024
BATCH = 4
SEQ = 4096
DEPTH = 4

FOX_HEADS = 8
FOX_DIM = 64
DSA_HEADS = 8
DSA_DIM = 64
IDX_HEADS = 8
IDX_DIM = 32
DSA_MAX_TOPK = 256
RET_HEADS = 4
RET_QK_DIM = 64
RET_V_DIM = 128
RET_CHUNK = 128
Q_BLOCK = 128
FFN_HIDDEN = ((8 * D_MODEL + 3 * 256 - 1) // (3 * 256)) * 256
ROPE_THETA = 10000.0
NORM_EPS = 1e-6

FOX_W = FOX_HEADS * FOX_DIM
DSA_W = DSA_HEADS * DSA_DIM
RET_QK_W = RET_HEADS * RET_QK_DIM
RET_V_W = RET_HEADS * RET_V_DIM
N_BRANCH = 3
IN_SIZES = (FOX_W, FOX_W, FOX_W, FOX_HEADS,
            DSA_W, DSA_DIM, DSA_DIM, IDX_HEADS * IDX_DIM, IDX_DIM, IDX_HEADS,
            RET_QK_W, RET_QK_W, RET_V_W, RET_V_W,
            N_BRANCH * D_MODEL)
IN_COLS = sum(IN_SIZES)

kernel_name = 'hybrid_fox_dsa_retention_block'


def split_columns(z):
    points = []
    acc = 0
    for s in IN_SIZES[:-1]:
        acc += s
        points.append(acc)
    return jnp.split(z, points, axis=-1)


def rms_norm(x, g):
    xf = x.astype(jnp.float32)
    y = xf * lax.rsqrt(jnp.mean(xf * xf, axis=-1, keepdims=True) + NORM_EPS)
    return (y * g.astype(jnp.float32)).astype(x.dtype)


def apply_rope(x, pos):
    half = x.shape[-1] // 2
    inv_freq = ROPE_THETA ** (-jnp.arange(half, dtype=jnp.float32) / half)
    ang = pos.astype(jnp.float32)[:, None] * inv_freq[None, :]
    cos = jnp.cos(ang)[:, None, :]
    sin = jnp.sin(ang)[:, None, :]
    xf = x.astype(jnp.float32)
    x1, x2 = xf[..., :half], xf[..., half:]
    return jnp.concatenate([x1 * cos - x2 * sin, x2 * cos + x1 * sin], axis=-1).astype(x.dtype)


def fox_attention(q, k, v, log_f):
    b, l, h, d = q.shape
    nb = l // Q_BLOCK
    scale = d ** -0.5
    c = jnp.cumsum(log_f, axis=1).transpose(0, 2, 1)
    kpos = jnp.arange(l)
    qb = q.reshape(b, nb, Q_BLOCK, h, d).swapaxes(0, 1)
    cb = c.reshape(b, h, nb, Q_BLOCK).transpose(2, 0, 1, 3)

    def block(args):
        i, qi, ci = args
        qpos = i * Q_BLOCK + jnp.arange(Q_BLOCK)
        s = jnp.einsum('bqhd,bkhd->bhqk', qi, k).astype(jnp.float32) * scale
        s = s + ci[..., :, None] - c[:, :, None, :]
        s = jnp.where(kpos[None, :] <= qpos[:, None], s, -jnp.inf)
        p = jax.nn.softmax(s, axis=-1).astype(v.dtype)
        return jnp.einsum('bhqk,bkhd->bqhd', p, v)

    out = lax.map(block, (jnp.arange(nb), qb, cb))
    return out.swapaxes(0, 1).reshape(b, l, h, d)


def dsa_attention(q, k, v, iq, ik, iw):
    b, l, h, d = q.shape
    nb = l // Q_BLOCK
    topk = min(DSA_MAX_TOPK, l // 4)
    scale = d ** -0.5
    idx_scale = (IDX_DIM * IDX_HEADS) ** -0.5
    kpos = jnp.arange(l)
    qb = q.reshape(b, nb, Q_BLOCK, h, d).swapaxes(0, 1)
    iqb = iq.reshape(b, nb, Q_BLOCK, IDX_HEADS, IDX_DIM).swapaxes(0, 1)
    iwb = iw.reshape(b, nb, Q_BLOCK, IDX_HEADS).swapaxes(0, 1)
    gather = jax.vmap(lambda table, rows: table[rows])

    def block(args):
        i, qi, iqi, iwi = args
        qpos = i * Q_BLOCK + jnp.arange(Q_BLOCK)
        causal = kpos[None, :] <= qpos[:, None]
        rel = jax.nn.relu(jnp.einsum('bqhd,bkd->bqhk', iqi, ik).astype(jnp.float32))
        score = jnp.einsum('bqhk,bqh->bqk', rel, iwi.astype(jnp.float32)) * idx_scale
        score = jnp.where(causal[None], score, -jnp.inf)
        _, sel = lax.top_k(score, topk)
        valid = sel <= qpos[None, :, None]
        kg = gather(k, sel)
        vg = gather(v, sel)
        s = jnp.einsum('bqhd,bqkd->bqhk', qi, kg).astype(jnp.float32) * scale
        s = jnp.where(valid[:, :, None, :], s, -jnp.inf)
        p = jax.nn.softmax(s, axis=-1).astype(v.dtype)
        return jnp.einsum('bqhk,bqkd->bqhd', p, vg)

    out = lax.map(block, (jnp.arange(nb), qb, iqb, iwb))
    return out.swapaxes(0, 1).reshape(b, l, h, d)


def retention(q, k, v):
    b, l, h, dk = q.shape
    dv = v.shape[-1]
    n = l // RET_CHUNK
    log_g = jnp.log1p(-(2.0 ** (-5.0 - jnp.arange(h, dtype=jnp.float32))))
    pos = jnp.arange(RET_CHUNK, dtype=jnp.float32)
    diff = pos[:, None] - pos[None, :]
    decay_in = jnp.where(diff >= 0, jnp.exp(jnp.maximum(diff, 0.0)[None] * log_g[:, None, None]), 0.0)
    decay_q = jnp.exp((pos + 1.0)[None] * log_g[:, None])
    decay_k = jnp.exp((RET_CHUNK - 1.0 - pos)[None] * log_g[:, None])
    decay_chunk = jnp.exp(RET_CHUNK * log_g)

    def chunks(t):
        return t.astype(jnp.float32).reshape(b, n, RET_CHUNK, h, t.shape[-1]).transpose(1, 0, 3, 2, 4)

    qc = chunks(q) * (dk ** -0.5)
    kc = chunks(k)
    vc = chunks(v)

    def step(state, inp):
        qi, ki, vi = inp
        attn = jnp.einsum('bhid,bhjd->bhij', qi, ki) * decay_in
        inner = jnp.einsum('bhij,bhje->bhie', attn, vi)
        cross = jnp.einsum('bhid,bhde->bhie', qi, state) * decay_q[..., None]
        state = decay_chunk[:, None, None] * state + jnp.einsum('bhjd,bhje->bhde', ki * decay_k[..., None], vi)
        return state, inner + cross

    state0 = jnp.zeros((b, h, dk, dv), jnp.float32)
    _, out = lax.scan(step, state0, (qc, kc, vc))
    return out.transpose(1, 0, 3, 2, 4).reshape(b, l, h, dv).astype(v.dtype)


def hybrid_layer(x, ln1_g, w_in, fox_b_f, fox_q_norm, fox_k_norm, dsa_q_norm, dsa_k_norm,
                 ret_out_norm, w_fox_out, w_dsa_out, w_ret_out, w_o, ln2_g, w_ffn_in, w_ffn_out):
    b, l, _ = x.shape
    pos = jnp.arange(l)
    hdn = rms_norm(x, ln1_g)
    (fq, fk, fv, ff, dq, dk, dv, iq, ik, iw, rq, rk, rv, rg, zg) = split_columns(hdn @ w_in)

    fq = rms_norm(fq.reshape(b, l, FOX_HEADS, FOX_DIM), fox_q_norm)
    fk = rms_norm(fk.reshape(b, l, FOX_HEADS, FOX_DIM), fox_k_norm)
    fv = fv.reshape(b, l, FOX_HEADS, FOX_DIM)
    log_f = jax.nn.log_sigmoid(ff.astype(jnp.float32) + fox_b_f.astype(jnp.float32))
    y_a = fox_attention(fq, fk, fv, log_f).reshape(b, l, FOX_W)

    dq = apply_rope(rms_norm(dq.reshape(b, l, DSA_HEADS, DSA_DIM), dsa_q_norm), pos)
    dk = apply_rope(rms_norm(dk, dsa_k_norm)[:, :, None, :], pos)[:, :, 0, :]
    iq = apply_rope(iq.reshape(b, l, IDX_HEADS, IDX_DIM), pos)
    ik = apply_rope(ik[:, :, None, :], pos)[:, :, 0, :]
    y_b = dsa_attention(dq, dk, dv, iq, ik, iw).reshape(b, l, DSA_W)

    rq = apply_rope(rq.reshape(b, l, RET_HEADS, RET_QK_DIM), pos)
    rk = apply_rope(rk.reshape(b, l, RET_HEADS, RET_QK_DIM), pos)
    rv = rv.reshape(b, l, RET_HEADS, RET_V_DIM)
    y_c = rms_norm(retention(rq, rk, rv), ret_out_norm).reshape(b, l, RET_V_W) * jax.nn.silu(rg)

    gates = jax.nn.sigmoid(zg).reshape(b, l, N_BRANCH, D_MODEL)
    merged = (gates[:, :, 0] * (y_a @ w_fox_out)
              + gates[:, :, 1] * (y_b @ w_dsa_out)
              + gates[:, :, 2] * (y_c @ w_ret_out))
    x = x + merged @ w_o

    g, u = jnp.split(rms_norm(x, ln2_g) @ w_ffn_in, 2, axis=-1)
    return x + (jax.nn.silu(g) * u) @ w_ffn_out


def setup_inputs(seed: int = 0) -> dict:
    key = jax.random.key(seed)
    ks = jax.random.split(key, 17)

    def nrm(k, shape, scale):
        return jax.random.normal(k, shape, jnp.float32) * scale

    res_scale = (2.0 * DEPTH) ** -0.5
    return {
        'x': nrm(ks[0], (BATCH, SEQ, D_MODEL), 1.0),
        'ln1_g': 1.0 + nrm(ks[1], (DEPTH, D_MODEL), 0.05),
        'w_in': nrm(ks[2], (DEPTH, D_MODEL, IN_COLS), D_MODEL ** -0.5),
        'fox_b_f': 3.0 + nrm(ks[3], (DEPTH, FOX_HEADS), 0.5),
        'fox_q_norm': 1.0 + nrm(ks[4], (DEPTH, FOX_DIM), 0.05),
        'fox_k_norm': 1.0 + nrm(ks[5], (DEPTH, FOX_DIM), 0.05),
        'dsa_q_norm': 1.0 + nrm(ks[6], (DEPTH, DSA_DIM), 0.05),
        'dsa_k_norm': 1.0 + nrm(ks[7], (DEPTH, DSA_DIM), 0.05),
        'ret_out_norm': 1.0 + nrm(ks[8], (DEPTH, RET_HEADS, RET_V_DIM), 0.05),
        'w_fox_out': nrm(ks[9], (DEPTH, FOX_W, D_MODEL), FOX_W ** -0.5),
        'w_dsa_out': nrm(ks[10], (DEPTH, DSA_W, D_MODEL), DSA_W ** -0.5),
        'w_ret_out': nrm(ks[11], (DEPTH, RET_V_W, D_MODEL), RET_V_W ** -0.5),
        'w_o': nrm(ks[12], (DEPTH, D_MODEL, D_MODEL), D_MODEL ** -0.5 * res_scale),
        'ln2_g': 1.0 + nrm(ks[13], (DEPTH, D_MODEL), 0.05),
        'w_ffn_in': nrm(ks[14], (DEPTH, D_MODEL, 2 * FFN_HIDDEN), D_MODEL ** -0.5),
        'w_ffn_out': nrm(ks[15], (DEPTH, FFN_HIDDEN, D_MODEL), FFN_HIDDEN ** -0.5 * res_scale),
    }


def reference(x, ln1_g, w_in, fox_b_f, fox_q_norm, fox_k_norm, dsa_q_norm, dsa_k_norm,
              ret_out_norm, w_fox_out, w_dsa_out, w_ret_out, w_o, ln2_g, w_ffn_in, w_ffn_out):
    for i in range(DEPTH):
        x = hybrid_layer(x, ln1_g[i], w_in[i], fox_b_f[i], fox_q_norm[i], fox_k_norm[i],
                         dsa_q_norm[i], dsa_k_norm[i], ret_out_norm[i], w_fox_out[i],
                         w_dsa_out[i], w_ret_out[i], w_o[i], ln2_g[i], w_ffn_in[i], w_ffn_out[i])
    return x
```

```python
import functools

import jax
import jax.numpy as jnp
import numpy as np
from jax import lax
from jax.experimental import pallas as pl
from jax.experimental.pallas import tpu as pltpu

D_MODEL = 1024
FOX_HEADS = 8
FOX_DIM = 64
DSA_HEADS = 8
DSA_DIM = 64
IDX_HEADS = 8
IDX_DIM = 32
DSA_MAX_TOPK = 256
RET_HEADS = 4
RET_QK_DIM = 64
RET_V_DIM = 128
RET_CHUNK = 128
FFN_HIDDEN = 2816
ROPE_THETA = 10000.0
NORM_EPS = 1e-6
N_BRANCH = 3

FOX_W = FOX_HEADS * FOX_DIM
DSA_W = DSA_HEADS * DSA_DIM
IDX_W = IDX_HEADS * IDX_DIM
RET_QK_W = RET_HEADS * RET_QK_DIM
RET_V_W = RET_HEADS * RET_V_DIM
IN_SIZES = (FOX_W, FOX_W, FOX_W, FOX_HEADS,
            DSA_W, DSA_DIM, DSA_DIM, IDX_W, IDX_DIM, IDX_HEADS,
            RET_QK_W, RET_QK_W, RET_V_W, RET_V_W,
            N_BRANCH * D_MODEL)
IN_OFFS = tuple(int(v) for v in np.cumsum((0,) + IN_SIZES))
(O_FQ, O_FK, O_FV, O_FF, O_DQ, O_DK, O_DV, O_IQ, O_IK, O_IW,
 O_RQ, O_RK, O_RV, O_RG, O_ZG, _) = IN_OFFS

LANES = 128
VMEM_LIMIT = 56 * 1024 * 1024
NEG_BIG = -1e30

T_FQ = 0
T_FK = T_FQ + FOX_HEADS * LANES
T_FV = T_FK + FOX_HEADS * LANES
T_FF = T_FV + FOX_W
T_DK = T_FF + LANES
T_RQ = T_DK + LANES
T_RK = T_RQ + RET_QK_W
T_RV = T_RK + RET_QK_W
T_RG = T_RV + RET_V_W
T_COLS = T_RG + RET_V_W
F_DQ = 0
F_IQ = F_DQ + DSA_W
F_DV = F_IQ + IDX_W
F_IW = F_DV + DSA_DIM
F_RK = F_IW + 16
F_ROWS = F_RK + RET_QK_W
FOX_EXTRA = 6

TM_IN = 512
TQ_FOX = 512
TQ_DSA = 128
KC_DSA = 512
TM_POST = 512


def _cparams(sem):
    return pltpu.CompilerParams(dimension_semantics=sem, vmem_limit_bytes=VMEM_LIMIT)


def _split3(v):
    hi = v.astype(jnp.bfloat16)
    r1 = v - hi.astype(jnp.float32)
    mid = r1.astype(jnp.bfloat16)
    lo = (r1 - mid.astype(jnp.float32)).astype(jnp.bfloat16)
    return hi, mid, lo


def _inproj_kernel(x_ref, g_ref, wtm_ref, wfm_ref, fb_ref, fqg_ref, fkg_ref, dkg_ref,
                   ltri_ref, eq_ref, ek_ref, oneq_ref, onek_ref,
                   kc_ref, ks_ref, rc_ref, rs_ref,
                   dqa_ref, dqb_ref, iqa_ref, iqb_ref, rka_ref, rkb_ref,
                   fq_out, fk_out, fv_out, dk_out, rq_out, rk_out, rv_out, rg_out,
                   qt_out, iqt_out, vt_out, iw_out, rkt_out,
                   carry_ref, *, tiles_per_seq):
    tm = x_ref.shape[0]
    i = pl.program_id(0)

    @pl.when(i % tiles_per_seq == 0)
    def _():
        carry_ref[...] = jnp.zeros_like(carry_ref)

    x = x_ref[...]
    ms = jnp.mean(x * x, axis=-1, keepdims=True)
    h = (x * lax.rsqrt(ms + NORM_EPS) * g_ref[...]).astype(jnp.bfloat16)

    def tm_dot(lo, width):
        return jnp.dot(h, wtm_ref[:, lo:lo + width], preferred_element_type=jnp.float32)

    lane = lax.broadcasted_iota(jnp.int32, (tm, LANES), 1)

    ffb = tm_dot(T_FF, LANES) + fb_ref[...]
    lf = jnp.minimum(ffb, 0.0) - jnp.log1p(jnp.exp(-jnp.abs(ffb)))
    parts = jnp.concatenate(_split3(lf), axis=1)
    cs = jnp.dot(ltri_ref[...], parts, preferred_element_type=jnp.float32)
    c = cs[:, :LANES] + cs[:, LANES:2 * LANES] + cs[:, 2 * LANES:] + carry_ref[...]
    carry_ref[...] = c[tm - 1:tm, :]
    cparts = jnp.concatenate(_split3(c), axis=1)
    scat_q = jnp.dot(cparts, eq_ref[...], preferred_element_type=jnp.float32) + oneq_ref[...]
    scat_k = jnp.dot(cparts, ek_ref[...], preferred_element_type=jnp.float32) + onek_ref[...]

    for (lo, gain_ref, scat, out, scale) in ((T_FQ, fqg_ref, scat_q, fq_out, FOX_DIM ** -0.5),
                                             (T_FK, fkg_ref, scat_k, fk_out, 1.0)):
        z = tm_dot(lo, FOX_HEADS * LANES)
        for hd in range(FOX_HEADS):
            blk = z[:, hd * LANES:(hd + 1) * LANES]
            ss = jnp.sum(blk * blk, axis=-1, keepdims=True) * (1.0 / FOX_DIM)
            nb = blk * lax.rsqrt(ss + NORM_EPS) * (gain_ref[...] * scale)
            out[:, hd * LANES:(hd + 1) * LANES] = (
                nb + scat[:, hd * LANES:(hd + 1) * LANES]).astype(out.dtype)

    fv_out[...] = tm_dot(T_FV, FOX_W).astype(fv_out.dtype)

    zk = tm_dot(T_DK, LANES)
    ssk = jnp.sum(jnp.where(lane < DSA_DIM, zk * zk, 0.0), axis=-1, keepdims=True) * (1.0 / DSA_DIM)
    nk = zk * jnp.where(lane < DSA_DIM, lax.rsqrt(ssk + NORM_EPS), 1.0) * dkg_ref[...]
    partner = jnp.where(
        lane < 32, pltpu.roll(nk, LANES - 32, 1),
        jnp.where(lane < 64, pltpu.roll(nk, 32, 1),
                  jnp.where(lane < 80, pltpu.roll(nk, LANES - 16, 1), pltpu.roll(nk, 16, 1))))
    dk_out[...] = (nk * kc_ref[...] + partner * ks_ref[...]).astype(dk_out.dtype)

    first_half = (lane % RET_QK_DIM) < (RET_QK_DIM // 2)
    for (lo, out, scale) in ((T_RQ, rq_out, RET_QK_DIM ** -0.5), (T_RK, rk_out, 1.0)):
        z = tm_dot(lo, RET_QK_W)
        for j in range(RET_QK_W // LANES):
            blk = z[:, j * LANES:(j + 1) * LANES]
            pr = jnp.where(first_half, pltpu.roll(blk, LANES - 32, 1), pltpu.roll(blk, 32, 1))
            out[:, j * LANES:(j + 1) * LANES] = (
                (blk * rc_ref[...] + pr * rs_ref[...]) * scale).astype(out.dtype)

    rv_out[...] = tm_dot(T_RV, RET_V_W).astype(rv_out.dtype)
    rg_out[...] = tm_dot(T_RG, RET_V_W)

    zt = lax.dot_general(wfm_ref[...], h, (((1,), (1,)), ((), ())),
                         preferred_element_type=jnp.float32)
    nq = tm // TQ_DSA

    def swap_halves(v):
        half = v.shape[0] // 2
        return jnp.concatenate([v[half:], v[:half]], axis=0)

    zeros_q = jnp.zeros((LANES - DSA_DIM, DSA_HEADS * TQ_DSA), qt_out.dtype)
    zeros_i0 = jnp.zeros((DSA_DIM, IDX_HEADS * TQ_DSA), iqt_out.dtype)
    zeros_i1 = jnp.zeros((LANES - DSA_DIM - IDX_DIM, IDX_HEADS * TQ_DSA), iqt_out.dtype)
    for j in range(nq):
        qt_out[j, DSA_DIM:, :] = zeros_q
        iqt_out[j, :DSA_DIM, :] = zeros_i0
        iqt_out[j, DSA_DIM + IDX_DIM:, :] = zeros_i1
    for hd in range(DSA_HEADS):
        xh = zt[F_DQ + hd * DSA_DIM:F_DQ + (hd + 1) * DSA_DIM, :]
        r = lax.rsqrt(jnp.sum(xh * xh, axis=0, keepdims=True) * (1.0 / DSA_DIM) + NORM_EPS)
        o = ((xh * dqa_ref[...] + swap_halves(xh) * dqb_ref[...]) * r).astype(qt_out.dtype)
        for j in range(nq):
            qt_out[j, :DSA_DIM, hd * TQ_DSA:(hd + 1) * TQ_DSA] = o[:, j * TQ_DSA:(j + 1) * TQ_DSA]
    for hd in range(IDX_HEADS):
        xh = zt[F_IQ + hd * IDX_DIM:F_IQ + (hd + 1) * IDX_DIM, :]
        o = (xh * iqa_ref[...] + swap_halves(xh) * iqb_ref[...]).astype(iqt_out.dtype)
        for j in range(nq):
            iqt_out[j, DSA_DIM:DSA_DIM + IDX_DIM, hd * TQ_DSA:(hd + 1) * TQ_DSA] = (
                o[:, j * TQ_DSA:(j + 1) * TQ_DSA])
    vt_out[0, 0, :DSA_DIM, :] = zt[F_DV:F_DV + DSA_DIM, :].astype(vt_out.dtype)
    vt_out[0, 0, DSA_DIM:, :] = jnp.zeros((LANES - DSA_DIM, tm), vt_out.dtype)
    iw_out[0] = zt[F_IW:F_IW + IDX_HEADS, :] * ((IDX_DIM * IDX_HEADS) ** -0.5)
    for hd in range(RET_HEADS):
        xh = zt[F_RK + hd * RET_QK_DIM:F_RK + (hd + 1) * RET_QK_DIM, :]
        rkt_out[0, hd * RET_QK_DIM:(hd + 1) * RET_QK_DIM, :] = (
            xh * rka_ref[...] + swap_halves(xh) * rkb_ref[...])


def _inproj(x2d, lw, consts, batch, seq):
    n = x2d.shape[0]
    tm = min(TM_IN, seq)
    tps = seq // tm
    nqt = tm // TQ_DSA
    grid = (n // tm,)
    bf = jnp.bfloat16

    def full(a):
        return pl.BlockSpec(a.shape, lambda i: (0,) * a.ndim)

    def tok(width):
        return pl.BlockSpec((tm, width), lambda i: (i, 0))

    def pos_tm(width):
        return pl.BlockSpec((tm, width), lambda i: (i % tps, 0))

    def pos_fm(rows):
        return pl.BlockSpec((rows, tm), lambda i: (0, i % tps))

    def fm_out(rows):
        return pl.BlockSpec((1, rows, tm), lambda i: (i // tps, 0, i % tps))

    in_arrays = [x2d, lw["ln1_g"], lw["w_tm"], lw["w_fm"], lw["fox_b"], lw["fq_gain"], lw["fk_gain"],
                 lw["dk_gain"], consts["ltri"], consts["eq"], consts["ek"], consts["oneq"], consts["onek"],
                 consts["kc"], consts["ks"], consts["rc"], consts["rs"],
                 lw["dqa"], lw["dqb"], consts["iqa"], consts["iqb"], consts["rka"], consts["rkb"]]
    in_specs = [tok(D_MODEL), full(lw["ln1_g"]), full(lw["w_tm"]), full(lw["w_fm"]), full(lw["fox_b"]),
                full(lw["fq_gain"]), full(lw["fk_gain"]), full(lw["dk_gain"]),
                full(consts["ltri"]), full(consts["eq"]), full(consts["ek"]),
                full(consts["oneq"]), full(consts["onek"]),
                pos_tm(LANES), pos_tm(LANES), pos_tm(LANES), pos_tm(LANES),
                pos_fm(DSA_DIM), pos_fm(DSA_DIM), pos_fm(IDX_DIM), pos_fm(IDX_DIM),
                pos_fm(RET_QK_DIM), pos_fm(RET_QK_DIM)]
    out_shape = [
        jax.ShapeDtypeStruct((n, FOX_HEADS * LANES), bf),
        jax.ShapeDtypeStruct((n, FOX_HEADS * LANES), bf),
        jax.ShapeDtypeStruct((n, FOX_W), bf),
        jax.ShapeDtypeStruct((n, LANES), bf),
        jax.ShapeDtypeStruct((n, RET_QK_W), bf),
        jax.ShapeDtypeStruct((n, RET_QK_W), bf),
        jax.ShapeDtypeStruct((n, RET_V_W), bf),
        jax.ShapeDtypeStruct((n, RET_V_W), jnp.float32),
        jax.ShapeDtypeStruct((n // TQ_DSA, LANES, DSA_HEADS * TQ_DSA), bf),
        jax.ShapeDtypeStruct((n // TQ_DSA, LANES, IDX_HEADS * TQ_DSA), bf),
        jax.ShapeDtypeStruct((batch, tps, LANES, tm), bf),
        jax.ShapeDtypeStruct((batch, IDX_HEADS, seq), jnp.float32),
        jax.ShapeDtypeStruct((batch, RET_QK_W, seq), jnp.float32),
    ]
    out_specs = [tok(FOX_HEADS * LANES), tok(FOX_HEADS * LANES), tok(FOX_W), tok(LANES),
                 tok(RET_QK_W), tok(RET_QK_W), tok(RET_V_W), tok(RET_V_W),
                 pl.BlockSpec((nqt, LANES, DSA_HEADS * TQ_DSA), lambda i: (i, 0, 0)),
                 pl.BlockSpec((nqt, LANES, IDX_HEADS * TQ_DSA), lambda i: (i, 0, 0)),
                 pl.BlockSpec((1, 1, LANES, tm), lambda i: (i // tps, i % tps, 0, 0)),
                 fm_out(IDX_HEADS), fm_out(RET_QK_W)]
    return pl.pallas_call(
        functools.partial(_inproj_kernel, tiles_per_seq=tps),
        grid=grid, in_specs=in_specs, out_specs=out_specs, out_shape=out_shape,
        scratch_shapes=[pltpu.VMEM((1, LANES), jnp.float32)],
        compiler_params=_cparams(("arbitrary",)), name="inproj",
    )(*in_arrays)


def _fox_kernel(q_ref, k_ref, v_ref, o_ref, m_scr, l_scr, acc_scr):
    tq = q_ref.shape[0]
    tk = k_ref.shape[0]
    qi = pl.program_id(2)
    ki = pl.program_id(3)

    @pl.when(ki == 0)
    def _():
        m_scr[...] = jnp.full_like(m_scr, NEG_BIG)
        l_scr[...] = jnp.zeros_like(l_scr)
        acc_scr[...] = jnp.zeros_like(acc_scr)

    def step(masked):
        v = v_ref[...]
        for hh in range(2):
            q = q_ref[:, hh * LANES:(hh + 1) * LANES]
            k = k_ref[:, hh * LANES:(hh + 1) * LANES]
            s = lax.dot_general(q, k, (((1,), (1,)), ((), ())), preferred_element_type=jnp.float32)
            if masked:
                row = lax.broadcasted_iota(jnp.int32, (tq, tk), 0)
                col = lax.broadcasted_iota(jnp.int32, (tq, tk), 1)
                s = jnp.where(row >= col, s, NEG_BIG)
            m_prev = m_scr[hh]
            m_next = jnp.maximum(m_prev, jnp.max(s, axis=1, keepdims=True))
            alpha = jnp.exp(m_prev - m_next)
            p = jnp.exp(s - pltpu.repeat(m_next, tk // LANES, axis=1))
            l_scr[hh] = alpha * l_scr[hh] + jnp.sum(p, axis=1, keepdims=True)
            m_scr[hh] = m_next
            acc_scr[hh] = acc_scr[hh] * alpha + jnp.dot(
                p.astype(v.dtype), v, preferred_element_type=jnp.float32)

    @pl.when(ki < qi)
    def _():
        step(False)

    @pl.when(ki == qi)
    def _():
        step(True)
        lane = lax.broadcasted_iota(jnp.int32, (tq, LANES), 1)
        o0 = acc_scr[0] / l_scr[0]
        o1 = acc_scr[1] / l_scr[1]
        o_ref[...] = jnp.where(lane < FOX_DIM, o0, o1).astype(o_ref.dtype)


def _fox(fq, fk, fv, batch, seq):
    t = min(TQ_FOX, seq)
    nt = seq // t
    grid = (batch, FOX_HEADS // 2, nt, nt)
    return pl.pallas_call(
        _fox_kernel, grid=grid,
        in_specs=[pl.BlockSpec((t, 2 * LANES), lambda b, hp, qi, ki: (b * nt + qi, hp)),
                  pl.BlockSpec((t, 2 * LANES), lambda b, hp, qi, ki: (b * nt + jnp.minimum(ki, qi), hp)),
                  pl.BlockSpec((t, LANES), lambda b, hp, qi, ki: (b * nt + jnp.minimum(ki, qi), hp))],
        out_specs=pl.BlockSpec((t, LANES), lambda b, hp, qi, ki: (b * nt + qi, hp)),
        out_shape=jax.ShapeDtypeStruct((batch * seq, FOX_W), jnp.bfloat16),
        scratch_shapes=[pltpu.VMEM((2, t, LANES), jnp.float32),
                        pltpu.VMEM((2, t, LANES), jnp.float32),
                        pltpu.VMEM((2, t, LANES), jnp.float32)],
        compiler_params=_cparams(("parallel", "parallel", "parallel", "arbitrary")), name="fox",
    )(fq, fk, fv)


def _dsa_kernel(k_ref, vt_ref, qt_ref, iqt_ref, iw_ref, ltri_ref, o_ref, key_scr, bias_scr, *, topk):
    tq = TQ_DSA
    kc = ltri_ref.shape[0]
    qi = pl.program_id(1)
    n_chunks = (qi * tq + tq + kc - 1) // kc
    nh = DSA_HEADS
    int_min = jnp.int32(-2 ** 31)

    qpos = qi * tq + lax.broadcasted_iota(jnp.int32, (kc, tq), 1)
    krow = lax.broadcasted_iota(jnp.int32, (kc, tq), 0)

    def score_chunk(c, carry):
        off = pl.multiple_of(c * kc, kc)
        kblk = k_ref[pl.ds(off, kc), :]
        rel = jnp.dot(kblk, iqt_ref[0], preferred_element_type=jnp.float32)
        score = jnp.zeros((kc, tq), jnp.float32)
        for hd in range(IDX_HEADS):
            score = score + jnp.maximum(rel[:, hd * tq:(hd + 1) * tq], 0.0) * iw_ref[0, hd:hd + 1, :]
        score = jnp.where(score == 0.0, 0.0, score)
        score = jnp.where(krow + off <= qpos, score, -jnp.inf)
        bits = pltpu.bitcast(score, jnp.int32)
        key_scr[pl.ds(off, kc), :] = bits ^ ((bits >> 31) & jnp.int32(0x7FFFFFFF))
        return carry

    lax.fori_loop(0, n_chunks, score_chunk, 0)

    def count_ge(thr):
        def body(c, acc):
            off = pl.multiple_of(c * kc, kc)
            ks = key_scr[pl.ds(off, kc), :]
            hit = jnp.where(ks >= thr, 1, 0).astype(jnp.int32)
            return acc + jnp.sum(hit.reshape(kc // 8, 8, tq), axis=0)
        acc = lax.fori_loop(0, n_chunks, body, jnp.zeros((8, tq), jnp.int32))
        return jnp.sum(acc, axis=0, keepdims=True)

    def bit_step(b, thr):
        bit = jnp.left_shift(jnp.int32(1), 31 - b)
        cand = jnp.where(b == 0, jnp.zeros_like(thr), thr | bit)
        return jnp.where(count_ge(cand) >= topk, cand, thr)

    thr = lax.fori_loop(0, 32, bit_step, jnp.full((1, tq), int_min, jnp.int32))
    n_ge = count_ge(thr)
    has_ties = jnp.max(n_ge) > topk

    @pl.when(jnp.logical_not(has_ties))
    def _():
        def body(c, carry):
            off = pl.multiple_of(c * kc, kc)
            bias_scr[pl.ds(off, kc), :] = jnp.where(key_scr[pl.ds(off, kc), :] >= thr, 0.0, NEG_BIG)
            return carry
        lax.fori_loop(0, n_chunks, body, 0)

    @pl.when(has_ties)
    def _():
        thr_next = jnp.where(thr == jnp.int32(2 ** 31 - 1), thr, thr + 1)
        n_gt = jnp.where(thr == jnp.int32(2 ** 31 - 1), 0, count_ge(thr_next))
        need = (topk - n_gt).astype(jnp.float32)

        def body(c, seen):
            off = pl.multiple_of(c * kc, kc)
            ks = key_scr[pl.ds(off, kc), :]
            eq = ks == thr
            eqf = jnp.where(eq, 1.0, 0.0)
            before = jnp.dot(ltri_ref[...], eqf.astype(jnp.bfloat16),
                             preferred_element_type=jnp.float32) + seen
            sel = jnp.logical_or(ks > thr, jnp.logical_and(eq, before < need))
            sel = jnp.logical_and(sel, krow + off <= qpos)
            bias_scr[pl.ds(off, kc), :] = jnp.where(sel, 0.0, NEG_BIG)
            return seen + jnp.sum(eqf, axis=0, keepdims=True)
        lax.fori_loop(0, n_chunks, body, jnp.zeros((1, tq), jnp.float32))

    def attn_chunk(c, carry):
        m_prev, l_prev, acc = carry
        off = pl.multiple_of(c * kc, kc)
        kblk = k_ref[pl.ds(off, kc), :]
        s = jnp.dot(kblk, qt_ref[0], preferred_element_type=jnp.float32)
        bias = bias_scr[pl.ds(off, kc), :]
        s = s + jnp.concatenate([bias] * nh, axis=1)
        m_next = jnp.maximum(m_prev, jnp.max(s, axis=0, keepdims=True))
        alpha = jnp.exp(m_prev - m_next)
        p = jnp.exp(s - m_next)
        l_next = alpha * l_prev + jnp.sum(p, axis=0, keepdims=True)
        pv = jnp.dot(vt_ref[0, c], p.astype(jnp.bfloat16),
                     preferred_element_type=jnp.float32)
        return m_next, l_next, acc * alpha + pv

    init = (jnp.full((1, nh * tq), NEG_BIG, jnp.float32), jnp.zeros((1, nh * tq), jnp.float32),
            jnp.zeros((LANES, nh * tq), jnp.float32))
    _, l_fin, acc = lax.fori_loop(0, n_chunks, attn_chunk, init)
    out_t = acc / l_fin
    for hd in range(nh):
        o_ref[:, hd * LANES:(hd + 1) * LANES] = out_t[:, hd * tq:(hd + 1) * tq].T.astype(o_ref.dtype)


def _dsa(dk, vt, qt, iqt, iwt, ltri_strict, batch, seq):
    nq = seq // TQ_DSA
    kc = ltri_strict.shape[0]
    assert vt.shape[3] == kc and seq % kc == 0
    seq_pad = seq
    topk = min(DSA_MAX_TOPK, seq // 4)
    return pl.pallas_call(
        functools.partial(_dsa_kernel, topk=topk),
        grid=(batch, nq),
        in_specs=[pl.BlockSpec((seq, LANES), lambda b, qi: (b, 0)),
                  pl.BlockSpec((1,) + vt.shape[1:], lambda b, qi: (b, 0, 0, 0)),
                  pl.BlockSpec((1, LANES, DSA_HEADS * TQ_DSA), lambda b, qi: (b * nq + qi, 0, 0)),
                  pl.BlockSpec((1, LANES, IDX_HEADS * TQ_DSA), lambda b, qi: (b * nq + qi, 0, 0)),
                  pl.BlockSpec((1, IDX_HEADS, TQ_DSA), lambda b, qi: (b, 0, qi)),
                  pl.BlockSpec(ltri_strict.shape, lambda b, qi: (0, 0))],
        out_specs=pl.BlockSpec((TQ_DSA, DSA_HEADS * LANES), lambda b, qi: (b * nq + qi, 0)),
        out_shape=jax.ShapeDtypeStruct((batch * seq, DSA_HEADS * LANES), jnp.bfloat16),
        scratch_shapes=[pltpu.VMEM((seq_pad, TQ_DSA), jnp.int32),
                        pltpu.VMEM((seq_pad, TQ_DSA), jnp.float32)],
        compiler_params=_cparams(("parallel", "arbitrary")), name="dsa",
    )(dk, vt, qt, iqt, iwt, ltri_strict)


def _ret_kernel(q_ref, k_ref, kt_ref, v_ref, g_ref, din_ref, dq_ref, dk_ref, dc_ref, gain_ref,
                o_ref, state_scr):
    t = pl.program_id(1)

    @pl.when(t == 0)
    def _():
        state_scr[...] = jnp.zeros_like(state_scr)

    for hd in range(RET_HEADS):
        q = q_ref[:, hd * RET_QK_DIM:(hd + 1) * RET_QK_DIM]
        k = k_ref[:, hd * RET_QK_DIM:(hd + 1) * RET_QK_DIM]
        v = v_ref[:, hd * RET_V_DIM:(hd + 1) * RET_V_DIM]
        state = state_scr[hd]
        attn = lax.dot_general(q, k, (((1,), (1,)), ((), ())),
                               preferred_element_type=jnp.float32) * din_ref[hd]
        inner = jnp.dot(attn.astype(jnp.bfloat16), v, preferred_element_type=jnp.float32)
        cross = jnp.dot(q, state.astype(jnp.bfloat16),
                        preferred_element_type=jnp.float32) * dq_ref[hd]
        ktd = (kt_ref[0, hd * RET_QK_DIM:(hd + 1) * RET_QK_DIM, :] * dk_ref[hd]).astype(jnp.bfloat16)
        state_scr[hd] = dc_ref[hd] * state + jnp.dot(ktd, v, preferred_element_type=jnp.float32)
        y = inner + cross
        yn = y * lax.rsqrt(jnp.mean(y * y, axis=-1, keepdims=True) + NORM_EPS) * gain_ref[hd]
        gate = g_ref[:, hd * RET_V_DIM:(hd + 1) * RET_V_DIM]
        o_ref[:, hd * RET_V_DIM:(hd + 1) * RET_V_DIM] = (
            yn * (gate * jax.nn.sigmoid(gate))).astype(o_ref.dtype)


def _ret(rq, rk, rkt, rv, rg, consts, gain, batch, seq):
    c = RET_CHUNK
    n = seq // c

    def tok(width):
        return pl.BlockSpec((c, width), lambda b, t: (b * n + t, 0))

    def full(a):
        return pl.BlockSpec(a.shape, lambda b, t: (0,) * a.ndim)

    return pl.pallas_call(
        _ret_kernel, grid=(batch, n),
        in_specs=[tok(RET_QK_W), tok(RET_QK_W),
                  pl.BlockSpec((1, RET_QK_W, c), lambda b, t: (b, 0, t)),
                  tok(RET_V_W), tok(RET_V_W),
                  full(consts["ret_din"]), full(consts["ret_dq"]), full(consts["ret_dk"]),
                  full(consts["ret_dc"]), full(gain)],
        out_specs=tok(RET_V_W),
        out_shape=jax.ShapeDtypeStruct((batch * seq, RET_V_W), jnp.bfloat16),
        scratch_shapes=[pltpu.VMEM((RET_HEADS, RET_QK_DIM, RET_V_DIM), jnp.float32)],
        compiler_params=_cparams(("parallel", "arbitrary")), name="ret",
    )(rq, rk, rkt, rv, rg, consts["ret_din"], consts["ret_dq"], consts["ret_dk"], consts["ret_dc"], gain)


def _merge_kernel(x_ref, g_ref, ya_ref, yb_ref, yc_ref, wzg_ref, wa_ref, wb_ref, wc_ref, wo_ref, o_ref):
    x = x_ref[...]
    ms = jnp.mean(x * x, axis=-1, keepdims=True)
    h = (x * lax.rsqrt(ms + NORM_EPS) * g_ref[...]).astype(jnp.bfloat16)
    merged = None
    for j, (y_ref, w_ref) in enumerate(((ya_ref, wa_ref), (yb_ref, wb_ref), (yc_ref, wc_ref))):
        gate = jax.nn.sigmoid(jnp.dot(h, wzg_ref[:, j * D_MODEL:(j + 1) * D_MODEL],
                                      preferred_element_type=jnp.float32))
        term = gate * jnp.dot(y_ref[...], w_ref[...], preferred_element_type=jnp.float32)
        merged = term if merged is None else merged + term
    o_ref[...] = x + jnp.dot(merged.astype(jnp.bfloat16), wo_ref[...],
                             preferred_element_type=jnp.float32)


def _ffn_kernel(x_ref, g_ref, wg_ref, wu_ref, wd_ref, o_ref, *, chunk):
    x = x_ref[...]
    ms = jnp.mean(x * x, axis=-1, keepdims=True)
    h = (x * lax.rsqrt(ms + NORM_EPS) * g_ref[...]).astype(jnp.bfloat16)
    acc = x
    for lo in range(0, FFN_HIDDEN, chunk):
        gt = jnp.dot(h, wg_ref[:, lo:lo + chunk], preferred_element_type=jnp.float32)
        up = jnp.dot(h, wu_ref[:, lo:lo + chunk], preferred_element_type=jnp.float32)
        act = (gt * jax.nn.sigmoid(gt) * up).astype(jnp.bfloat16)
        acc = acc + jnp.dot(act, wd_ref[lo:lo + chunk, :], preferred_element_type=jnp.float32)
    o_ref[...] = acc


def _row_call(kernel, name, x2d, row_inputs, full_inputs, order):
    n = x2d.shape[0]
    tm = min(TM_POST, n)
    arrays, specs = [], []
    for kind, a in order:
        arrays.append(a)
        if kind == "row":
            specs.append(pl.BlockSpec((tm, a.shape[1]), lambda i: (i, 0)))
        else:
            specs.append(pl.BlockSpec(a.shape, lambda i, nd=a.ndim: (0,) * nd))
    return pl.pallas_call(
        kernel, grid=(n // tm,), in_specs=specs,
        out_specs=pl.BlockSpec((tm, D_MODEL), lambda i: (i, 0)),
        out_shape=jax.ShapeDtypeStruct((n, D_MODEL), jnp.float32),
        compiler_params=_cparams(("parallel",)), name=name,
    )(*arrays)


def _merge(x2d, ya, yb, yc, lw):
    order = [("row", x2d), ("full", lw["ln1_g"]), ("row", ya), ("row", yb), ("row", yc),
             ("full", lw["w_zg"]), ("full", lw["w_a"]), ("full", lw["w_b"]), ("full", lw["w_c"]),
             ("full", lw["w_o"])]
    return _row_call(_merge_kernel, "merge", x2d, None, None, order)


def _ffn(x2d, lw):
    order = [("row", x2d), ("full", lw["ln2_g"]), ("full", lw["w_g"]), ("full", lw["w_u"]),
             ("full", lw["w_d"])]
    return _row_call(functools.partial(_ffn_kernel, chunk=256), "ffn", x2d, None, None, order)


def _rope_cs(seq, dim):
    half = dim // 2
    inv_freq = ROPE_THETA ** (-jnp.arange(half, dtype=jnp.float32) / half)
    ang = jnp.arange(seq, dtype=jnp.float32)[:, None] * inv_freq[None, :]
    return jnp.cos(ang), jnp.sin(ang)


def _constants(seq):
    tm = min(TM_IN, seq)
    bf = jnp.bfloat16
    cos64, sin64 = _rope_cs(seq, 64)
    cos32, sin32 = _rope_cs(seq, 32)
    z32 = jnp.zeros((seq, 32), jnp.float32)
    c64 = jnp.concatenate([cos64, cos64], axis=1)
    s64 = jnp.concatenate([-sin64, sin64], axis=1)
    c32 = jnp.concatenate([cos32, cos32], axis=1)
    s32 = jnp.concatenate([-sin32, sin32], axis=1)
    consts = {
        "kc": jnp.concatenate([c64, c32, z32], axis=1), "ks": jnp.concatenate([s64, s32, z32], axis=1),
        "rc": jnp.concatenate([c64, c64], axis=1), "rs": jnp.concatenate([s64, s64], axis=1),
        "c64t": c64.T, "s64t": s64.T,
        "iqa": c32.T, "iqb": s32.T, "rka": c64.T, "rkb": s64.T,
        "ltri": jnp.tril(jnp.ones((tm, tm), jnp.float32)).astype(bf),
        "ltri_strict": jnp.tril(jnp.ones((KC_DSA, KC_DSA), jnp.float32), -1).astype(bf),
    }
    eq = np.zeros((3 * LANES, FOX_HEADS * LANES), np.float32)
    ek = np.zeros((3 * LANES, FOX_HEADS * LANES), np.float32)
    oneq = np.zeros((1, FOX_HEADS * LANES), np.float32)
    onek = np.zeros((1, FOX_HEADS * LANES), np.float32)
    for hd in range(FOX_HEADS):
        base = hd * LANES + FOX_DIM
        for part in range(3):
            eq[part * LANES + hd, base + part] = 1.0
            ek[part * LANES + hd, base + 3 + part] = -1.0
            oneq[0, base + 3 + part] = 1.0
            onek[0, base + part] = 1.0
    consts.update(eq=jnp.asarray(eq, bf), ek=jnp.asarray(ek, bf), oneq=jnp.asarray(oneq), onek=jnp.asarray(onek))
    log_g = jnp.log1p(-(2.0 ** (-5.0 - jnp.arange(RET_HEADS, dtype=jnp.float32))))
    pos = jnp.arange(RET_CHUNK, dtype=jnp.float32)
    diff = pos[:, None] - pos[None, :]
    din = jnp.where(diff >= 0, jnp.exp(jnp.maximum(diff, 0.0)[None] * log_g[:, None, None]), 0.0)
    dq = jnp.exp((pos + 1.0)[None] * log_g[:, None])
    dk = jnp.exp((RET_CHUNK - 1.0 - pos)[None] * log_g[:, None])
    dc = jnp.exp(RET_CHUNK * log_g)
    consts.update(ret_din=din, ret_dq=dq[:, :, None], ret_dk=dk[:, None, :],
                  ret_dc=jnp.broadcast_to(dc[:, None, None], (RET_HEADS, 1, LANES)))
    return consts


def _pad_heads(w, heads, dim):
    k = w.shape[0]
    w3 = w.reshape(k, heads, dim)
    return jnp.pad(w3, ((0, 0), (0, 0), (0, LANES - dim))).reshape(k, heads * LANES)


def _layer_weights(p, consts):
    bf = jnp.bfloat16
    w_in = p["w_in"]

    def cols(off, size):
        return w_in[:, off:off + size]

    zeros = lambda n: jnp.zeros((D_MODEL, n), jnp.float32)
    w_tm = jnp.concatenate([
        _pad_heads(cols(O_FQ, FOX_W), FOX_HEADS, FOX_DIM),
        _pad_heads(cols(O_FK, FOX_W), FOX_HEADS, FOX_DIM),
        cols(O_FV, FOX_W),
        cols(O_FF, FOX_HEADS), zeros(LANES - FOX_HEADS),
        cols(O_DK, DSA_DIM), cols(O_IK, IDX_DIM), zeros(LANES - DSA_DIM - IDX_DIM),
        cols(O_RQ, RET_QK_W), cols(O_RK, RET_QK_W), cols(O_RV, RET_V_W), cols(O_RG, RET_V_W),
    ], axis=1).astype(bf)
    w_fm = jnp.concatenate([
        cols(O_DQ, DSA_W), cols(O_IQ, IDX_W), cols(O_DV, DSA_DIM),
        cols(O_IW, IDX_HEADS), zeros(16 - IDX_HEADS), cols(O_RK, RET_QK_W),
    ], axis=1).T.astype(bf)

    def lane_pad(v, fill=0.0):
        return jnp.concatenate([v, jnp.full((LANES - v.shape[0],), fill, jnp.float32)])[None, :]

    g = p["dsa_q_norm"]
    g_sw = jnp.concatenate([g[DSA_DIM // 2:], g[:DSA_DIM // 2]])
    scale = DSA_DIM ** -0.5
    w_b = jnp.pad(p["w_dsa_out"].reshape(DSA_HEADS, DSA_DIM, D_MODEL),
                  ((0, 0), (0, LANES - DSA_DIM), (0, 0))).reshape(DSA_HEADS * LANES, D_MODEL)
    return {
        "ln1_g": p["ln1_g"][None, :], "ln2_g": p["ln2_g"][None, :],
        "w_tm": w_tm, "w_fm": w_fm,
        "fox_b": lane_pad(p["fox_b_f"]),
        "fq_gain": lane_pad(p["fox_q_norm"]), "fk_gain": lane_pad(p["fox_k_norm"]),
        "dk_gain": jnp.concatenate([p["dsa_k_norm"], jnp.ones((IDX_DIM,), jnp.float32),
                                    jnp.zeros((LANES - DSA_DIM - IDX_DIM,), jnp.float32)])[None, :],
        "dqa": consts["c64t"] * (g * scale)[:, None], "dqb": consts["s64t"] * (g_sw * scale)[:, None],
        "ret_gain": p["ret_out_norm"][:, None, :],
        "w_zg": cols(O_ZG, N_BRANCH * D_MODEL).astype(bf),
        "w_a": p["w_fox_out"].astype(bf), "w_b": w_b.astype(bf), "w_c": p["w_ret_out"].astype(bf),
        "w_o": p["w_o"].astype(bf),
        "w_g": p["w_ffn_in"][:, :FFN_HIDDEN].astype(bf), "w_u": p["w_ffn_in"][:, FFN_HIDDEN:].astype(bf),
        "w_d": p["w_ffn_out"].astype(bf),
    }


def _layer(x2d, lw, consts, batch, seq):
    (fq, fk, fv, dk, rq, rk, rv, rg, qt, iqt, vt, iwt, rkt) = _inproj(x2d, lw, consts, batch, seq)
    ya = _fox(fq, fk, fv, batch, seq)
    yb = _dsa(dk, vt, qt, iqt, iwt, consts["ltri_strict"], batch, seq)
    yc = _ret(rq, rk, rkt, rv, rg, consts, lw["ret_gain"], batch, seq)
    x2d = _merge(x2d, ya, yb, yc, lw)
    return _ffn(x2d, lw)


def kernel(x, ln1_g, w_in, fox_b_f, fox_q_norm, fox_k_norm, dsa_q_norm, dsa_k_norm, ret_out_norm,
           w_fox_out, w_dsa_out, w_ret_out, w_o, ln2_g, w_ffn_in, w_ffn_out):
    batch, seq, _ = x.shape
    depth = w_in.shape[0]
    consts = _constants(seq)
    params = dict(ln1_g=ln1_g, w_in=w_in, fox_b_f=fox_b_f, fox_q_norm=fox_q_norm, fox_k_norm=fox_k_norm,
                  dsa_q_norm=dsa_q_norm, dsa_k_norm=dsa_k_norm, ret_out_norm=ret_out_norm,
                  w_fox_out=w_fox_out, w_dsa_out=w_dsa_out, w_ret_out=w_ret_out, w_o=w_o,
                  ln2_g=ln2_g, w_ffn_in=w_ffn_in, w_ffn_out=w_ffn_out)
    x2d = x.reshape(batch * seq, D_MODEL)
    for layer in range(depth):
        lw = _layer_weights({k: v[layer] for k, v in params.items()}, consts)
        x2d = _layer(x2d, lw, consts, batch, seq)
    return x2d.reshape(batch, seq, D_MODEL)
```

```python
import functools
import math

import jax
import jax.numpy as jnp
import numpy as np
from jax import lax
from jax.experimental import pallas as pl
from jax.experimental.pallas import tpu as pltpu

D_MODEL = 1024
FOX_HEADS = 8
FOX_DIM = 64
DSA_HEADS = 8
DSA_DIM = 64
IDX_HEADS = 8
IDX_DIM = 32
DSA_MAX_TOPK = 256
RET_HEADS = 4
RET_QK_DIM = 64
RET_V_DIM = 128
RET_CHUNK = 128
FFN_HIDDEN = 2816
ROPE_THETA = 10000.0
NORM_EPS = 1e-6
N_BRANCH = 3

FOX_W = FOX_HEADS * FOX_DIM
DSA_W = DSA_HEADS * DSA_DIM
IDX_W = IDX_HEADS * IDX_DIM
RET_QK_W = RET_HEADS * RET_QK_DIM
RET_V_W = RET_HEADS * RET_V_DIM
IN_SIZES = (FOX_W, FOX_W, FOX_W, FOX_HEADS,
            DSA_W, DSA_DIM, DSA_DIM, IDX_W, IDX_DIM, IDX_HEADS,
            RET_QK_W, RET_QK_W, RET_V_W, RET_V_W,
            N_BRANCH * D_MODEL)
IN_OFFS = tuple(int(v) for v in np.cumsum((0,) + IN_SIZES))
(O_FQ, O_FK, O_FV, O_FF, O_DQ, O_DK, O_DV, O_IQ, O_IK, O_IW,
 O_RQ, O_RK, O_RV, O_RG, O_ZG, _) = IN_OFFS

LANES = 128
SUBLANES = 8
BF16_ROWS = 16
VMEM_LIMIT = 56 * 1024 * 1024
NEG_BIG = -1e30
LOG2E = math.log2(math.e)

T_FQ = 0
T_FK = T_FQ + FOX_HEADS * LANES
T_FF = T_FK + FOX_HEADS * LANES
T_DK = T_FF + LANES
T_RQ = T_DK + LANES
T_RK = T_RQ + RET_QK_W
T_RV = T_RK + RET_QK_W
T_RG = T_RV + RET_V_W
T_COLS = T_RG + RET_V_W
F_DQ = 0
F_IQ = F_DQ + DSA_W
F_DV = F_IQ + IDX_W
F_IW = F_DV + DSA_DIM
F_RK = F_IW + 16
F_FV = F_RK + RET_QK_W
F_ROWS = F_FV + FOX_W

TM_IN = 512
KC = TM_IN
HALF = KC // 2
RB = 32
TQ_FOX = 256
FOX_VROWS = 2 * FOX_DIM + BF16_ROWS
DSA_VROWS = DSA_DIM + BF16_ROWS
TQ_DSA = 128
TM_POST = 512
MAX_PROBE_ROUNDS = 20


def _cparams(sem):
    return pltpu.CompilerParams(dimension_semantics=sem, vmem_limit_bytes=VMEM_LIMIT)


def _split3(v):
    hi = v.astype(jnp.bfloat16)
    r1 = v - hi.astype(jnp.float32)
    mid = r1.astype(jnp.bfloat16)
    lo = (r1 - mid.astype(jnp.float32)).astype(jnp.bfloat16)
    return hi, mid, lo


def _col_reduce(v, op):
    return op(v.reshape(v.shape[0] // SUBLANES, SUBLANES, v.shape[1]), axis=0)


def _store_logits(s_ref, t):
    s_ref[...] = t
    return _col_reduce(t, jnp.max)


def _softmax_update(s_ref, p_ref, m_prev, mx):
    m_next = jnp.maximum(m_prev, jnp.max(mx, axis=0, keepdims=True))
    for r0 in range(0, s_ref.shape[0], RB):
        p_ref[r0:r0 + RB, :] = jnp.exp2(s_ref[r0:r0 + RB, :] - m_next).astype(p_ref.dtype)
    return m_next, jnp.exp2(m_prev - m_next)


def _inproj_kernel(x_ref, g_ref, wtm_ref, wfm_ref, fb_ref, fqg_ref, fkg_ref, dkg_ref,
                   ltri_ref, eq_ref, ek_ref, oneq_ref, onek_ref,
                   kc_ref, ks_ref, rc_ref, rs_ref,
                   dqa_ref, dqb_ref, iqa_ref, iqb_ref, rka_ref, rkb_ref,
                   fq_out, fk_out, fvt_out, dk_out, rq_out, rk_out, rv_out, rg_out,
                   qt_out, iqt_out, vt_out, iw_out, rkt_out,
                   carry_ref, *, tiles_per_seq):
    tm = x_ref.shape[0]
    i = pl.program_id(0)

    @pl.when(i % tiles_per_seq == 0)
    def _():
        carry_ref[...] = jnp.zeros_like(carry_ref)

    x = x_ref[...]
    ms = jnp.mean(x * x, axis=-1, keepdims=True)
    h = (x * lax.rsqrt(ms + NORM_EPS) * g_ref[...]).astype(jnp.bfloat16)

    def tm_dot(lo, width):
        return jnp.dot(h, wtm_ref[:, lo:lo + width], preferred_element_type=jnp.float32)

    lane = lax.broadcasted_iota(jnp.int32, (tm, LANES), 1)

    ffb = tm_dot(T_FF, LANES) + fb_ref[...]
    lf = (jnp.minimum(ffb, 0.0) - jnp.log1p(jnp.exp(-jnp.abs(ffb)))) * LOG2E
    parts = jnp.concatenate(_split3(lf), axis=1)
    cs = jnp.dot(ltri_ref[...], parts, preferred_element_type=jnp.float32)
    c = cs[:, :LANES] + cs[:, LANES:2 * LANES] + cs[:, 2 * LANES:] + carry_ref[...]
    carry_ref[...] = c[tm - 1:tm, :]
    cparts = jnp.concatenate(_split3(c), axis=1)
    scat_q = jnp.dot(cparts, eq_ref[...], preferred_element_type=jnp.float32) + oneq_ref[...]
    scat_k = jnp.dot(cparts, ek_ref[...], preferred_element_type=jnp.float32) + onek_ref[...]

    for (lo, gain_ref, scat, out, scale) in ((T_FQ, fqg_ref, scat_q, fq_out, FOX_DIM ** -0.5 * LOG2E),
                                             (T_FK, fkg_ref, scat_k, fk_out, 1.0)):
        z = tm_dot(lo, FOX_HEADS * LANES)
        for hd in range(FOX_HEADS):
            blk = z[:, hd * LANES:(hd + 1) * LANES]
            ss = jnp.sum(blk * blk, axis=-1, keepdims=True) * (1.0 / FOX_DIM)
            nb = blk * lax.rsqrt(ss + NORM_EPS) * (gain_ref[...] * scale)
            out[:, hd * LANES:(hd + 1) * LANES] = (
                nb + scat[:, hd * LANES:(hd + 1) * LANES]).astype(out.dtype)

    zk = tm_dot(T_DK, LANES)
    ssk = jnp.sum(jnp.where(lane < DSA_DIM, zk * zk, 0.0), axis=-1, keepdims=True) * (1.0 / DSA_DIM)
    nk = zk * jnp.where(lane < DSA_DIM, lax.rsqrt(ssk + NORM_EPS), 1.0) * dkg_ref[...]
    partner = jnp.where(
        lane < 32, pltpu.roll(nk, LANES - 32, 1),
        jnp.where(lane < 64, pltpu.roll(nk, 32, 1),
                  jnp.where(lane < 80, pltpu.roll(nk, LANES - 16, 1), pltpu.roll(nk, 16, 1))))
    dk_out[...] = (nk * kc_ref[...] + partner * ks_ref[...]).astype(dk_out.dtype)

    first_half = (lane % RET_QK_DIM) < (RET_QK_DIM // 2)
    for (lo, out, scale) in ((T_RQ, rq_out, RET_QK_DIM ** -0.5), (T_RK, rk_out, 1.0)):
        z = tm_dot(lo, RET_QK_W)
        for j in range(RET_QK_W // LANES):
            blk = z[:, j * LANES:(j + 1) * LANES]
            pr = jnp.where(first_half, pltpu.roll(blk, LANES - 32, 1), pltpu.roll(blk, 32, 1))
            out[:, j * LANES:(j + 1) * LANES] = (
                (blk * rc_ref[...] + pr * rs_ref[...]) * scale).astype(out.dtype)

    rv_out[...] = tm_dot(T_RV, RET_V_W).astype(rv_out.dtype)
    rg_out[...] = tm_dot(T_RG, RET_V_W)

    zt = lax.dot_general(wfm_ref[...], h, (((1,), (1,)), ((), ())),
                         preferred_element_type=jnp.float32)
    nq = tm // TQ_DSA

    def swap_halves(v):
        half = v.shape[0] // 2
        return jnp.concatenate([v[half:], v[:half]], axis=0)

    def ones_row_block(rows, dtype):
        first = lax.broadcasted_iota(jnp.int32, (rows, tm), 0) == 0
        return jnp.where(first, 1.0, 0.0).astype(dtype)

    zeros_q = jnp.zeros((LANES - DSA_DIM, DSA_HEADS * TQ_DSA), qt_out.dtype)
    zeros_i0 = jnp.zeros((DSA_DIM, IDX_HEADS * TQ_DSA), iqt_out.dtype)
    zeros_i1 = jnp.zeros((LANES - DSA_DIM - IDX_DIM, IDX_HEADS * TQ_DSA), iqt_out.dtype)
    for j in range(nq):
        qt_out[j, DSA_DIM:, :] = zeros_q
        iqt_out[j, :DSA_DIM, :] = zeros_i0
        iqt_out[j, DSA_DIM + IDX_DIM:, :] = zeros_i1
    for hd in range(DSA_HEADS):
        xh = zt[F_DQ + hd * DSA_DIM:F_DQ + (hd + 1) * DSA_DIM, :]
        r = lax.rsqrt(jnp.sum(xh * xh, axis=0, keepdims=True) * (1.0 / DSA_DIM) + NORM_EPS)
        o = ((xh * dqa_ref[...] + swap_halves(xh) * dqb_ref[...]) * r).astype(qt_out.dtype)
        for j in range(nq):
            qt_out[j, :DSA_DIM, hd * TQ_DSA:(hd + 1) * TQ_DSA] = o[:, j * TQ_DSA:(j + 1) * TQ_DSA]
    for hd in range(IDX_HEADS):
        xh = zt[F_IQ + hd * IDX_DIM:F_IQ + (hd + 1) * IDX_DIM, :]
        o = (xh * iqa_ref[...] + swap_halves(xh) * iqb_ref[...]).astype(iqt_out.dtype)
        for j in range(nq):
            iqt_out[j, DSA_DIM:DSA_DIM + IDX_DIM, hd * TQ_DSA:(hd + 1) * TQ_DSA] = (
                o[:, j * TQ_DSA:(j + 1) * TQ_DSA])
    vt_out[0, 0, :DSA_DIM, :] = zt[F_DV:F_DV + DSA_DIM, :].astype(vt_out.dtype)
    vt_out[0, 0, DSA_DIM:, :] = ones_row_block(BF16_ROWS, vt_out.dtype)
    iw_out[0] = zt[F_IW:F_IW + IDX_HEADS, :] * ((IDX_DIM * IDX_HEADS) ** -0.5)
    for hd in range(RET_HEADS):
        xh = zt[F_RK + hd * RET_QK_DIM:F_RK + (hd + 1) * RET_QK_DIM, :]
        rkt_out[0, hd * RET_QK_DIM:(hd + 1) * RET_QK_DIM, :] = (
            xh * rka_ref[...] + swap_halves(xh) * rkb_ref[...])
    for hp in range(FOX_HEADS // 2):
        fvt_out[0, hp, 0, :2 * FOX_DIM, :] = (
            zt[F_FV + hp * 2 * FOX_DIM:F_FV + (hp + 1) * 2 * FOX_DIM, :].astype(fvt_out.dtype))
        fvt_out[0, hp, 0, 2 * FOX_DIM:, :] = ones_row_block(BF16_ROWS, fvt_out.dtype)


def _inproj(x2d, lw, consts, batch, seq):
    n = x2d.shape[0]
    tm = min(TM_IN, seq)
    tps = seq // tm
    nqt = tm // TQ_DSA
    grid = (n // tm,)
    bf = jnp.bfloat16

    def full(a):
        return pl.BlockSpec(a.shape, lambda i: (0,) * a.ndim)

    def tok(width):
        return pl.BlockSpec((tm, width), lambda i: (i, 0))

    def pos_tm(width):
        return pl.BlockSpec((tm, width), lambda i: (i % tps, 0))

    def pos_fm(rows):
        return pl.BlockSpec((rows, tm), lambda i: (0, i % tps))

    def fm_out(rows):
        return pl.BlockSpec((1, rows, tm), lambda i: (i // tps, 0, i % tps))

    in_arrays = [x2d, lw["ln1_g"], lw["w_tm"], lw["w_fm"], lw["fox_b"], lw["fq_gain"], lw["fk_gain"],
                 lw["dk_gain"], consts["ltri"], consts["eq"], consts["ek"], consts["oneq"], consts["onek"],
                 consts["kc"], consts["ks"], consts["rc"], consts["rs"],
                 lw["dqa"], lw["dqb"], consts["iqa"], consts["iqb"], consts["rka"], consts["rkb"]]
    in_specs = [tok(D_MODEL), full(lw["ln1_g"]), full(lw["w_tm"]), full(lw["w_fm"]), full(lw["fox_b"]),
                full(lw["fq_gain"]), full(lw["fk_gain"]), full(lw["dk_gain"]),
                full(consts["ltri"]), full(consts["eq"]), full(consts["ek"]),
                full(consts["oneq"]), full(consts["onek"]),
                pos_tm(LANES), pos_tm(LANES), pos_tm(LANES), pos_tm(LANES),
                pos_fm(DSA_DIM), pos_fm(DSA_DIM), pos_fm(IDX_DIM), pos_fm(IDX_DIM),
                pos_fm(RET_QK_DIM), pos_fm(RET_QK_DIM)]
    out_shape = [
        jax.ShapeDtypeStruct((n, FOX_HEADS * LANES), bf),
        jax.ShapeDtypeStruct((n, FOX_HEADS * LANES), bf),
        jax.ShapeDtypeStruct((batch, FOX_HEADS // 2, tps, FOX_VROWS, tm), bf),
        jax.ShapeDtypeStruct((n, LANES), bf),
        jax.ShapeDtypeStruct((n, RET_QK_W), bf),
        jax.ShapeDtypeStruct((n, RET_QK_W), bf),
        jax.ShapeDtypeStruct((n, RET_V_W), bf),
        jax.ShapeDtypeStruct((n, RET_V_W), jnp.float32),
        jax.ShapeDtypeStruct((n // TQ_DSA, LANES, DSA_HEADS * TQ_DSA), bf),
        jax.ShapeDtypeStruct((n // TQ_DSA, LANES, IDX_HEADS * TQ_DSA), bf),
        jax.ShapeDtypeStruct((batch, tps, DSA_VROWS, tm), bf),
        jax.ShapeDtypeStruct((batch, IDX_HEADS, seq), jnp.float32),
        jax.ShapeDtypeStruct((batch, RET_QK_W, seq), jnp.float32),
    ]
    out_specs = [tok(FOX_HEADS * LANES), tok(FOX_HEADS * LANES),
                 pl.BlockSpec((1, FOX_HEADS // 2, 1, FOX_VROWS, tm), lambda i: (i // tps, 0, i % tps, 0, 0)),
                 tok(LANES), tok(RET_QK_W), tok(RET_QK_W), tok(RET_V_W), tok(RET_V_W),
                 pl.BlockSpec((nqt, LANES, DSA_HEADS * TQ_DSA), lambda i: (i, 0, 0)),
                 pl.BlockSpec((nqt, LANES, IDX_HEADS * TQ_DSA), lambda i: (i, 0, 0)),
                 pl.BlockSpec((1, 1, DSA_VROWS, tm), lambda i: (i // tps, i % tps, 0, 0)),
                 fm_out(IDX_HEADS), fm_out(RET_QK_W)]
    return pl.pallas_call(
        functools.partial(_inproj_kernel, tiles_per_seq=tps),
        grid=grid, in_specs=in_specs, out_specs=out_specs, out_shape=out_shape,
        scratch_shapes=[pltpu.VMEM((1, LANES), jnp.float32)],
        compiler_params=_cparams(("arbitrary",)), name="inproj",
    )(*in_arrays)


def _fox_kernel(q_ref, k_ref, vt_ref, o_ref, sa_scr, sb_scr, pa_scr, pb_scr, acc_scr):
    tq = q_ref.shape[0]
    qi = pl.program_id(2)
    n_chunks = (qi * tq + tq + KC - 1) // KC
    width = 2 * tq
    nt = (((1,), (1,)), ((), ()))

    qcol = qi * tq + lax.broadcasted_iota(jnp.int32, (HALF, tq), 1)
    krow = lax.broadcasted_iota(jnp.int32, (HALF, tq), 0)

    def qk(c, half, s_ref):
        off = pl.multiple_of(c * KC + half * HALF, HALF)
        is_causal = krow + off <= qcol
        mx = []
        for hh in range(2):
            s = lax.dot_general(k_ref[pl.ds(off, HALF), hh * LANES:(hh + 1) * LANES],
                                q_ref[:, hh * LANES:(hh + 1) * LANES], nt,
                                preferred_element_type=jnp.float32)
            mx.append(_store_logits(s_ref.at[:, hh * tq:(hh + 1) * tq], jnp.where(is_causal, s, NEG_BIG)))
        return jnp.concatenate(mx, axis=1)

    def pv(c, half, p_ref):
        vt = vt_ref[0, 0, c, :, half * HALF:(half + 1) * HALF]
        return jnp.concatenate(
            [jnp.dot(vt, p_ref[:, hh * tq:(hh + 1) * tq], preferred_element_type=jnp.float32)
             for hh in range(2)], axis=1)

    acc_scr[...] = jnp.zeros_like(acc_scr)
    pb_scr[...] = jnp.zeros_like(pb_scr)
    mx_first = qk(0, 0, sa_scr)

    def chunk(c, carry):
        m_prev, mx_a = carry
        mx_b = qk(c, 1, sb_scr)
        acc = acc_scr[...] + pv(jnp.maximum(c - 1, 0), 1, pb_scr)
        m_a, alpha_a = _softmax_update(sa_scr, pa_scr, m_prev, mx_a)
        acc = acc * alpha_a
        mx_next = qk(jnp.minimum(c + 1, n_chunks - 1), 0, sa_scr)
        acc = acc + pv(c, 0, pa_scr)
        m_b, alpha_b = _softmax_update(sb_scr, pb_scr, m_a, mx_b)
        acc_scr[...] = acc * alpha_b
        return m_b, mx_next

    lax.fori_loop(0, n_chunks, chunk, (jnp.full((1, width), NEG_BIG, jnp.float32), mx_first))
    acc = acc_scr[...] + pv(n_chunks - 1, 1, pb_scr)
    den = acc[2 * FOX_DIM:2 * FOX_DIM + 1, :]
    out_t = jnp.concatenate([acc[:FOX_DIM, :tq] / den[:, :tq],
                             acc[FOX_DIM:2 * FOX_DIM, tq:] / den[:, tq:]], axis=0)
    for j in range(tq // LANES):
        o_ref[j * LANES:(j + 1) * LANES, :] = out_t[:, j * LANES:(j + 1) * LANES].T.astype(o_ref.dtype)


def _fox(fq, fk, fvt, batch, seq):
    tq = min(TQ_FOX, seq)
    nq = seq // tq
    width = 2 * tq
    return pl.pallas_call(
        _fox_kernel, grid=(batch, FOX_HEADS // 2, nq),
        in_specs=[pl.BlockSpec((tq, 2 * LANES), lambda b, hp, qi: (b * nq + qi, hp)),
                  pl.BlockSpec((seq, 2 * LANES), lambda b, hp, qi: (b, hp)),
                  pl.BlockSpec((1, 1) + fvt.shape[2:], lambda b, hp, qi: (b, hp, 0, 0, 0))],
        out_specs=pl.BlockSpec((tq, LANES), lambda b, hp, qi: (b * nq + qi, hp)),
        out_shape=jax.ShapeDtypeStruct((batch * seq, FOX_W), jnp.bfloat16),
        scratch_shapes=[pltpu.VMEM((HALF, width), jnp.float32), pltpu.VMEM((HALF, width), jnp.float32),
                        pltpu.VMEM((HALF, width), jnp.bfloat16), pltpu.VMEM((HALF, width), jnp.bfloat16),
                        pltpu.VMEM((FOX_VROWS, width), jnp.float32)],
        compiler_params=_cparams(("parallel", "parallel", "arbitrary")), name="fox",
    )(fq, fk, fvt)


def _dsa_kernel(k_ref, vt_ref, qt_ref, iqt_ref, iw_ref, ltri_ref, o_ref,
                score_scr, bias_scr, sa_scr, sb_scr, pa_scr, pb_scr, acc_scr, *, topk):
    tq = TQ_DSA
    nh = DSA_HEADS
    width = nh * tq
    qi = pl.program_id(1)
    n_chunks = (qi * tq + tq + KC - 1) // KC
    k_t = float(topk)

    qpos = qi * tq + lax.broadcasted_iota(jnp.int32, (KC, tq), 1)
    krow = lax.broadcasted_iota(jnp.int32, (KC, tq), 0)
    lane1 = lax.broadcasted_iota(jnp.int32, (1, tq), 1)

    def score_chunk(c, carry):
        mx, mn = carry
        off = pl.multiple_of(c * KC, KC)
        rel = jnp.dot(k_ref[pl.ds(off, KC), :], iqt_ref[0], preferred_element_type=jnp.float32)
        score = jnp.maximum(rel[:, :tq], 0.0) * iw_ref[0, 0:1, :]
        for hd in range(1, IDX_HEADS):
            score = score + jnp.maximum(rel[:, hd * tq:(hd + 1) * tq], 0.0) * iw_ref[0, hd:hd + 1, :]
        is_causal = krow + off <= qpos
        score_scr[pl.ds(off, KC), :] = jnp.where(is_causal, score, -jnp.inf)
        mx = jnp.maximum(mx, _col_reduce(jnp.where(is_causal, score, -jnp.inf), jnp.max))
        mn = jnp.minimum(mn, _col_reduce(jnp.where(is_causal, score, jnp.inf), jnp.min))
        return mx, mn

    mx, mn = lax.fori_loop(0, n_chunks, score_chunk,
                           (jnp.full((SUBLANES, tq), -jnp.inf, jnp.float32),
                            jnp.full((SUBLANES, tq), jnp.inf, jnp.float32)))
    hi0 = jnp.max(mx, axis=0, keepdims=True)
    lo0 = jnp.min(mn, axis=0, keepdims=True)

    def count_where(pred):
        def body(c, acc):
            off = pl.multiple_of(c * KC, KC)
            hit = jnp.where(pred(score_scr[pl.ds(off, KC), :]), 1, 0).astype(jnp.int32)
            return acc + _col_reduce(hit, jnp.sum)
        acc = lax.fori_loop(0, n_chunks, body, jnp.zeros((SUBLANES, tq), jnp.int32))
        return jnp.sum(acc, axis=0, keepdims=True).astype(jnp.float32)

    def probe(state, bisect):
        lo, hi, c_lo, c_hi, t_sel, done = state
        frac = jnp.clip((c_lo - (k_t + 0.5)) / (c_lo - c_hi), 0.02, 0.98)
        frac = jnp.where(bisect, 0.5, frac)
        g = lo + (hi - lo) * frac
        cnt = count_where(lambda s: s >= g)
        hit = cnt == k_t
        t_sel = jnp.where(jnp.logical_and(done < 0.5, hit), g, t_sel)
        done = jnp.where(hit, 1.0, done)
        above = cnt > k_t
        below = cnt < k_t
        return (jnp.where(above, g, lo), jnp.where(below, g, hi),
                jnp.where(above, cnt, c_lo), jnp.where(below, cnt, c_hi), t_sel, done)

    def search_cond(carry):
        rnd, state = carry
        return jnp.logical_and(rnd < MAX_PROBE_ROUNDS, jnp.min(state[5]) < 0.5)

    def search_body(carry):
        rnd, state = carry
        state = probe(state, False)
        state = probe(state, rnd % 2 == 1)
        return rnd + 1, state

    first_round = jnp.where(qi * tq >= topk, 0, MAX_PROBE_ROUNDS)
    n_causal = (qi * tq + lane1 + 1).astype(jnp.float32)
    zeros1 = jnp.zeros((1, tq), jnp.float32)
    _, state = lax.while_loop(search_cond, search_body,
                              (first_round, (lo0, hi0, n_causal, zeros1, zeros1, zeros1)))
    t_sel = state[4]
    found = jnp.min(state[5]) > 0.5

    @pl.when(found)
    def _():
        def body(c, carry):
            off = pl.multiple_of(c * KC, KC)
            bias_scr[pl.ds(off, KC), :] = jnp.where(score_scr[pl.ds(off, KC), :] >= t_sel, 0.0, NEG_BIG)
            return carry
        lax.fori_loop(0, n_chunks, body, 0)

    @pl.when(jnp.logical_not(found))
    def _():
        def to_key(s):
            bits = pltpu.bitcast(jnp.where(s == 0.0, 0.0, s), jnp.int32)
            return bits ^ ((bits >> 31) & jnp.int32(0x7FFFFFFF))

        def bit_step(b, thr):
            bit = jnp.left_shift(jnp.int32(1), 31 - b)
            cand = jnp.where(b == 0, jnp.zeros_like(thr), thr | bit)
            return jnp.where(count_where(lambda s: to_key(s) >= cand) >= k_t, cand, thr)

        thr = lax.fori_loop(0, 32, bit_step, jnp.full((1, tq), -2 ** 31, jnp.int32))
        need = k_t - count_where(lambda s: to_key(s) > thr)

        def body(c, seen):
            off = pl.multiple_of(c * KC, KC)
            ks = to_key(score_scr[pl.ds(off, KC), :])
            eq = ks == thr
            eqf = jnp.where(eq, 1.0, 0.0)
            before = jnp.dot(ltri_ref[...], eqf.astype(jnp.bfloat16),
                             preferred_element_type=jnp.float32) + seen
            sel = jnp.logical_or(ks > thr, jnp.logical_and(eq, before < need))
            sel = jnp.logical_and(sel, krow + off <= qpos)
            bias_scr[pl.ds(off, KC), :] = jnp.where(sel, 0.0, NEG_BIG)
            return seen + jnp.sum(eqf, axis=0, keepdims=True)
        lax.fori_loop(0, n_chunks, body, zeros1)

    def qk(c, half, s_ref):
        off = pl.multiple_of(c * KC + half * HALF, HALF)
        s = jnp.dot(k_ref[pl.ds(off, HALF), :], qt_ref[0], preferred_element_type=jnp.float32)
        return _store_logits(s_ref, s + jnp.concatenate([bias_scr[pl.ds(off, HALF), :]] * nh, axis=1))

    def pv(c, half, p_ref):
        return jnp.dot(vt_ref[0, c, :, half * HALF:(half + 1) * HALF], p_ref[...],
                       preferred_element_type=jnp.float32)

    acc_scr[...] = jnp.zeros_like(acc_scr)
    pb_scr[...] = jnp.zeros_like(pb_scr)
    mx_first = qk(0, 0, sa_scr)

    def chunk(c, carry):
        m_prev, mx_a = carry
        mx_b = qk(c, 1, sb_scr)
        acc = acc_scr[...] + pv(jnp.maximum(c - 1, 0), 1, pb_scr)
        m_a, alpha_a = _softmax_update(sa_scr, pa_scr, m_prev, mx_a)
        acc = acc * alpha_a
        mx_next = qk(jnp.minimum(c + 1, n_chunks - 1), 0, sa_scr)
        acc = acc + pv(c, 0, pa_scr)
        m_b, alpha_b = _softmax_update(sb_scr, pb_scr, m_a, mx_b)
        acc_scr[...] = acc * alpha_b
        return m_b, mx_next

    lax.fori_loop(0, n_chunks, chunk, (jnp.full((1, width), NEG_BIG, jnp.float32), mx_first))
    acc = acc_scr[...] + pv(n_chunks - 1, 1, pb_scr)
    out_t = acc[:DSA_DIM, :] / acc[DSA_DIM:DSA_DIM + 1, :]
    for hp in range(nh // 2):
        pair = jnp.concatenate([out_t[:, 2 * hp * tq:(2 * hp + 1) * tq],
                                out_t[:, (2 * hp + 1) * tq:(2 * hp + 2) * tq]], axis=0)
        o_ref[:, hp * LANES:(hp + 1) * LANES] = pair.T.astype(o_ref.dtype)


def _dsa(dk, vt, qt, iqt, iwt, ltri_strict, batch, seq):
    nq = seq // TQ_DSA
    assert vt.shape[3] == KC and ltri_strict.shape[0] == KC and seq % KC == 0
    topk = min(DSA_MAX_TOPK, seq // 4)
    width = DSA_HEADS * TQ_DSA
    return pl.pallas_call(
        functools.partial(_dsa_kernel, topk=topk),
        grid=(batch, nq),
        in_specs=[pl.BlockSpec((seq, LANES), lambda b, qi: (b, 0)),
                  pl.BlockSpec((1,) + vt.shape[1:], lambda b, qi: (b, 0, 0, 0)),
                  pl.BlockSpec((1, LANES, width), lambda b, qi: (b * nq + qi, 0, 0)),
                  pl.BlockSpec((1, LANES, IDX_HEADS * TQ_DSA), lambda b, qi: (b * nq + qi, 0, 0)),
                  pl.BlockSpec((1, IDX_HEADS, TQ_DSA), lambda b, qi: (b, 0, qi)),
                  pl.BlockSpec(ltri_strict.shape, lambda b, qi: (0, 0))],
        out_specs=pl.BlockSpec((TQ_DSA, DSA_W), lambda b, qi: (b * nq + qi, 0)),
        out_shape=jax.ShapeDtypeStruct((batch * seq, DSA_W), jnp.bfloat16),
        scratch_shapes=[pltpu.VMEM((seq, TQ_DSA), jnp.float32),
                        pltpu.VMEM((seq, TQ_DSA), jnp.float32),
                        pltpu.VMEM((HALF, width), jnp.float32), pltpu.VMEM((HALF, width), jnp.float32),
                        pltpu.VMEM((HALF, width), jnp.bfloat16), pltpu.VMEM((HALF, width), jnp.bfloat16),
                        pltpu.VMEM((DSA_VROWS, width), jnp.float32)],
        compiler_params=_cparams(("parallel", "arbitrary")), name="dsa",
    )(dk, vt, qt, iqt, iwt, ltri_strict)


def _ret_kernel(q_ref, k_ref, kt_ref, v_ref, g_ref, din_ref, dq_ref, dk_ref, dc_ref, gain_ref,
                o_ref, state_scr):
    t = pl.program_id(1)

    @pl.when(t == 0)
    def _():
        state_scr[...] = jnp.zeros_like(state_scr)

    for hd in range(RET_HEADS):
        q = q_ref[:, hd * RET_QK_DIM:(hd + 1) * RET_QK_DIM]
        k = k_ref[:, hd * RET_QK_DIM:(hd + 1) * RET_QK_DIM]
        v = v_ref[:, hd * RET_V_DIM:(hd + 1) * RET_V_DIM]
        state = state_scr[hd]
        attn = lax.dot_general(q, k, (((1,), (1,)), ((), ())),
                               preferred_element_type=jnp.float32) * din_ref[hd]
        inner = jnp.dot(attn.astype(jnp.bfloat16), v, preferred_element_type=jnp.float32)
        cross = jnp.dot(q, state.astype(jnp.bfloat16),
                        preferred_element_type=jnp.float32) * dq_ref[hd]
        ktd = (kt_ref[0, hd * RET_QK_DIM:(hd + 1) * RET_QK_DIM, :] * dk_ref[hd]).astype(jnp.bfloat16)
        state_scr[hd] = dc_ref[hd] * state + jnp.dot(ktd, v, preferred_element_type=jnp.float32)
        y = inner + cross
        yn = y * lax.rsqrt(jnp.mean(y * y, axis=-1, keepdims=True) + NORM_EPS) * gain_ref[hd]
        gate = g_ref[:, hd * RET_V_DIM:(hd + 1) * RET_V_DIM]
        o_ref[:, hd * RET_V_DIM:(hd + 1) * RET_V_DIM] = (
            yn * (gate * jax.nn.sigmoid(gate))).astype(o_ref.dtype)


def _ret(rq, rk, rkt, rv, rg, consts, gain, batch, seq):
    c = RET_CHUNK
    n = seq // c

    def tok(width):
        return pl.BlockSpec((c, width), lambda b, t: (b * n + t, 0))

    def full(a):
        return pl.BlockSpec(a.shape, lambda b, t: (0,) * a.ndim)

    return pl.pallas_call(
        _ret_kernel, grid=(batch, n),
        in_specs=[tok(RET_QK_W), tok(RET_QK_W),
                  pl.BlockSpec((1, RET_QK_W, c), lambda b, t: (b, 0, t)),
                  tok(RET_V_W), tok(RET_V_W),
                  full(consts["ret_din"]), full(consts["ret_dq"]), full(consts["ret_dk"]),
                  full(consts["ret_dc"]), full(gain)],
        out_specs=tok(RET_V_W),
        out_shape=jax.ShapeDtypeStruct((batch * seq, RET_V_W), jnp.bfloat16),
        scratch_shapes=[pltpu.VMEM((RET_HEADS, RET_QK_DIM, RET_V_DIM), jnp.float32)],
        compiler_params=_cparams(("parallel", "arbitrary")), name="ret",
    )(rq, rk, rkt, rv, rg, consts["ret_din"], consts["ret_dq"], consts["ret_dk"], consts["ret_dc"], gain)


def _merge_kernel(x_ref, g_ref, ya_ref, yb_ref, yc_ref, wzg_ref, wa_ref, wb_ref, wc_ref, wo_ref, o_ref):
    x = x_ref[...]
    ms = jnp.mean(x * x, axis=-1, keepdims=True)
    h = (x * lax.rsqrt(ms + NORM_EPS) * g_ref[...]).astype(jnp.bfloat16)
    merged = None
    for j, (y_ref, w_ref) in enumerate(((ya_ref, wa_ref), (yb_ref, wb_ref), (yc_ref, wc_ref))):
        gate = jax.nn.sigmoid(jnp.dot(h, wzg_ref[:, j * D_MODEL:(j + 1) * D_MODEL],
                                      preferred_element_type=jnp.float32))
        term = gate * jnp.dot(y_ref[...], w_ref[...], preferred_element_type=jnp.float32)
        merged = term if merged is None else merged + term
    o_ref[...] = x + jnp.dot(merged.astype(jnp.bfloat16), wo_ref[...],
                             preferred_element_type=jnp.float32)


def _ffn_kernel(x_ref, g_ref, wg_ref, wu_ref, wd_ref, o_ref, *, chunk):
    x = x_ref[...]
    ms = jnp.mean(x * x, axis=-1, keepdims=True)
    h = (x * lax.rsqrt(ms + NORM_EPS) * g_ref[...]).astype(jnp.bfloat16)
    acc = x
    for lo in range(0, FFN_HIDDEN, chunk):
        gt = jnp.dot(h, wg_ref[:, lo:lo + chunk], preferred_element_type=jnp.float32)
        up = jnp.dot(h, wu_ref[:, lo:lo + chunk], preferred_element_type=jnp.float32)
        act = (gt * jax.nn.sigmoid(gt) * up).astype(jnp.bfloat16)
        acc = acc + jnp.dot(act, wd_ref[lo:lo + chunk, :], preferred_element_type=jnp.float32)
    o_ref[...] = acc


def _row_call(kernel, name, order):
    n = order[0][1].shape[0]
    tm = min(TM_POST, n)
    arrays, specs = [], []
    for kind, a in order:
        arrays.append(a)
        if kind == "row":
            specs.append(pl.BlockSpec((tm, a.shape[1]), lambda i: (i, 0)))
        else:
            specs.append(pl.BlockSpec(a.shape, lambda i, nd=a.ndim: (0,) * nd))
    return pl.pallas_call(
        kernel, grid=(n // tm,), in_specs=specs,
        out_specs=pl.BlockSpec((tm, D_MODEL), lambda i: (i, 0)),
        out_shape=jax.ShapeDtypeStruct((n, D_MODEL), jnp.float32),
        compiler_params=_cparams(("parallel",)), name=name,
    )(*arrays)


def _merge(x2d, ya, yb, yc, lw):
    order = [("row", x2d), ("full", lw["ln1_g"]), ("row", ya), ("row", yb), ("row", yc),
             ("full", lw["w_zg"]), ("full", lw["w_a"]), ("full", lw["w_b"]), ("full", lw["w_c"]),
             ("full", lw["w_o"])]
    return _row_call(_merge_kernel, "merge", order)


def _ffn(x2d, lw):
    order = [("row", x2d), ("full", lw["ln2_g"]), ("full", lw["w_g"]), ("full", lw["w_u"]),
             ("full", lw["w_d"])]
    return _row_call(functools.partial(_ffn_kernel, chunk=256), "ffn", order)


def _rope_cs(seq, dim):
    half = dim // 2
    inv_freq = ROPE_THETA ** (-jnp.arange(half, dtype=jnp.float32) / half)
    ang = jnp.arange(seq, dtype=jnp.float32)[:, None] * inv_freq[None, :]
    return jnp.cos(ang), jnp.sin(ang)


def _constants(seq):
    tm = min(TM_IN, seq)
    bf = jnp.bfloat16
    cos64, sin64 = _rope_cs(seq, 64)
    cos32, sin32 = _rope_cs(seq, 32)
    z32 = jnp.zeros((seq, 32), jnp.float32)
    c64 = jnp.concatenate([cos64, cos64], axis=1)
    s64 = jnp.concatenate([-sin64, sin64], axis=1)
    c32 = jnp.concatenate([cos32, cos32], axis=1)
    s32 = jnp.concatenate([-sin32, sin32], axis=1)
    consts = {
        "kc": jnp.concatenate([c64, c32, z32], axis=1), "ks": jnp.concatenate([s64, s32, z32], axis=1),
        "rc": jnp.concatenate([c64, c64], axis=1), "rs": jnp.concatenate([s64, s64], axis=1),
        "c64t": c64.T, "s64t": s64.T,
        "iqa": c32.T, "iqb": s32.T, "rka": c64.T, "rkb": s64.T,
        "ltri": jnp.tril(jnp.ones((tm, tm), jnp.float32)).astype(bf),
        "ltri_strict": jnp.tril(jnp.ones((KC, KC), jnp.float32), -1).astype(bf),
    }
    eq = np.zeros((3 * LANES, FOX_HEADS * LANES), np.float32)
    ek = np.zeros((3 * LANES, FOX_HEADS * LANES), np.float32)
    oneq = np.zeros((1, FOX_HEADS * LANES), np.float32)
    onek = np.zeros((1, FOX_HEADS * LANES), np.float32)
    for hd in range(FOX_HEADS):
        base = hd * LANES + FOX_DIM
        for part in range(3):
            eq[part * LANES + hd, base + part] = 1.0
            ek[part * LANES + hd, base + 3 + part] = -1.0
            oneq[0, base + 3 + part] = 1.0
            onek[0, base + part] = 1.0
    consts.update(eq=jnp.asarray(eq, bf), ek=jnp.asarray(ek, bf), oneq=jnp.asarray(oneq), onek=jnp.asarray(onek))
    log_g = jnp.log1p(-(2.0 ** (-5.0 - jnp.arange(RET_HEADS, dtype=jnp.float32))))
    pos = jnp.arange(RET_CHUNK, dtype=jnp.float32)
    diff = pos[:, None] - pos[None, :]
    din = jnp.where(diff >= 0, jnp.exp(jnp.maximum(diff, 0.0)[None] * log_g[:, None, None]), 0.0)
    dq = jnp.exp((pos + 1.0)[None] * log_g[:, None])
    dk = jnp.exp((RET_CHUNK - 1.0 - pos)[None] * log_g[:, None])
    dc = jnp.exp(RET_CHUNK * log_g)
    consts.update(ret_din=din, ret_dq=dq[:, :, None], ret_dk=dk[:, None, :],
                  ret_dc=jnp.broadcast_to(dc[:, None, None], (RET_HEADS, 1, LANES)))
    return consts


def _pad_heads(w, heads, dim):
    k = w.shape[0]
    w3 = w.reshape(k, heads, dim)
    return jnp.pad(w3, ((0, 0), (0, 0), (0, LANES - dim))).reshape(k, heads * LANES)


def _layer_weights(p, consts):
    bf = jnp.bfloat16
    w_in = p["w_in"]

    def cols(off, size):
        return w_in[:, off:off + size]

    zeros = lambda n: jnp.zeros((D_MODEL, n), jnp.float32)
    w_tm = jnp.concatenate([
        _pad_heads(cols(O_FQ, FOX_W), FOX_HEADS, FOX_DIM),
        _pad_heads(cols(O_FK, FOX_W), FOX_HEADS, FOX_DIM),
        cols(O_FF, FOX_HEADS), zeros(LANES - FOX_HEADS),
        cols(O_DK, DSA_DIM), cols(O_IK, IDX_DIM), zeros(LANES - DSA_DIM - IDX_DIM),
        cols(O_RQ, RET_QK_W), cols(O_RK, RET_QK_W), cols(O_RV, RET_V_W), cols(O_RG, RET_V_W),
    ], axis=1).astype(bf)
    w_fm = jnp.concatenate([
        cols(O_DQ, DSA_W), cols(O_IQ, IDX_W), cols(O_DV, DSA_DIM),
        cols(O_IW, IDX_HEADS), zeros(16 - IDX_HEADS), cols(O_RK, RET_QK_W), cols(O_FV, FOX_W),
    ], axis=1).T.astype(bf)

    def lane_pad(v, fill=0.0):
        return jnp.concatenate([v, jnp.full((LANES - v.shape[0],), fill, jnp.float32)])[None, :]

    g = p["dsa_q_norm"]
    g_sw = jnp.concatenate([g[DSA_DIM // 2:], g[:DSA_DIM // 2]])
    scale = DSA_DIM ** -0.5 * LOG2E
    return {
        "ln1_g": p["ln1_g"][None, :], "ln2_g": p["ln2_g"][None, :],
        "w_tm": w_tm, "w_fm": w_fm,
        "fox_b": lane_pad(p["fox_b_f"]),
        "fq_gain": lane_pad(p["fox_q_norm"]), "fk_gain": lane_pad(p["fox_k_norm"]),
        "dk_gain": jnp.concatenate([p["dsa_k_norm"], jnp.ones((IDX_DIM,), jnp.float32),
                                    jnp.zeros((LANES - DSA_DIM - IDX_DIM,), jnp.float32)])[None, :],
        "dqa": consts["c64t"] * (g * scale)[:, None], "dqb": consts["s64t"] * (g_sw * scale)[:, None],
        "ret_gain": p["ret_out_norm"][:, None, :],
        "w_zg": cols(O_ZG, N_BRANCH * D_MODEL).astype(bf),
        "w_a": p["w_fox_out"].astype(bf), "w_b": p["w_dsa_out"].astype(bf), "w_c": p["w_ret_out"].astype(bf),
        "w_o": p["w_o"].astype(bf),
        "w_g": p["w_ffn_in"][:, :FFN_HIDDEN].astype(bf), "w_u": p["w_ffn_in"][:, FFN_HIDDEN:].astype(bf),
        "w_d": p["w_ffn_out"].astype(bf),
    }


def _layer(x2d, lw, consts, batch, seq):
    (fq, fk, fvt, dk, rq, rk, rv, rg, qt, iqt, vt, iwt, rkt) = _inproj(x2d, lw, consts, batch, seq)
    ya = _fox(fq, fk, fvt, batch, seq)
    yb = _dsa(dk, vt, qt, iqt, iwt, consts["ltri_strict"], batch, seq)
    yc = _ret(rq, rk, rkt, rv, rg, consts, lw["ret_gain"], batch, seq)
    x2d = _merge(x2d, ya, yb, yc, lw)
    return _ffn(x2d, lw)


def kernel(x, ln1_g, w_in, fox_b_f, fox_q_norm, fox_k_norm, dsa_q_norm, dsa_k_norm, ret_out_norm,
           w_fox_out, w_dsa_out, w_ret_out, w_o, ln2_g, w_ffn_in, w_ffn_out):
    batch, seq, _ = x.shape
    depth = w_in.shape[0]
    consts = _constants(seq)
    params = dict(ln1_g=ln1_g, w_in=w_in, fox_b_f=fox_b_f, fox_q_norm=fox_q_norm, fox_k_norm=fox_k_norm,
                  dsa_q_norm=dsa_q_norm, dsa_k_norm=dsa_k_norm, ret_out_norm=ret_out_norm,
                  w_fox_out=w_fox_out, w_dsa_out=w_dsa_out, w_ret_out=w_ret_out, w_o=w_o,
                  ln2_g=ln2_g, w_ffn_in=w_ffn_in, w_ffn_out=w_ffn_out)
    x2d = x.reshape(batch * seq, D_MODEL)
    for layer in range(depth):
        lw = _layer_weights({k: v[layer] for k, v in params.items()}, consts)
        x2d = _layer(x2d, lw, consts, batch, seq)
    return x2d.reshape(batch, seq, D_MODEL)
```

```python
import functools
import math

import jax
import jax.numpy as jnp
import numpy as np
from jax import lax
from jax.experimental import pallas as pl
from jax.experimental.pallas import tpu as pltpu

D_MODEL = 1024
FOX_HEADS = 8
FOX_DIM = 64
DSA_HEADS = 8
DSA_DIM = 64
IDX_HEADS = 8
IDX_DIM = 32
DSA_MAX_TOPK = 256
RET_HEADS = 4
RET_QK_DIM = 64
RET_V_DIM = 128
RET_CHUNK = 128
FFN_HIDDEN = 2816
ROPE_THETA = 10000.0
NORM_EPS = 1e-6
N_BRANCH = 3

FOX_W = FOX_HEADS * FOX_DIM
DSA_W = DSA_HEADS * DSA_DIM
IDX_W = IDX_HEADS * IDX_DIM
RET_QK_W = RET_HEADS * RET_QK_DIM
RET_V_W = RET_HEADS * RET_V_DIM
IN_SIZES = (FOX_W, FOX_W, FOX_W, FOX_HEADS,
            DSA_W, DSA_DIM, DSA_DIM, IDX_W, IDX_DIM, IDX_HEADS,
            RET_QK_W, RET_QK_W, RET_V_W, RET_V_W,
            N_BRANCH * D_MODEL)
IN_OFFS = tuple(int(v) for v in np.cumsum((0,) + IN_SIZES))
(O_FQ, O_FK, O_FV, O_FF, O_DQ, O_DK, O_DV, O_IQ, O_IK, O_IW,
 O_RQ, O_RK, O_RV, O_RG, O_ZG, _) = IN_OFFS

LANES = 128
SUBLANES = 8
BF16_ROWS = 16
VMEM_LIMIT = 56 * 1024 * 1024
NEG_BIG = -1e30
LOG2E = math.log2(math.e)

T_FQ = 0
T_FK = T_FQ + FOX_HEADS * LANES
T_FF = T_FK + FOX_HEADS * LANES
T_DK = T_FF + LANES
T_RQ = T_DK + LANES
T_RK = T_RQ + RET_QK_W
T_RV = T_RK + RET_QK_W
T_RG = T_RV + RET_V_W
T_COLS = T_RG + RET_V_W
F_DQ = 0
F_IQ = F_DQ + DSA_W
F_DV = F_IQ + IDX_W
F_IW = F_DV + DSA_DIM
F_RK = F_IW + 16
F_FV = F_RK + RET_QK_W
F_ROWS = F_FV + FOX_W

TM_IN = 512
KC = TM_IN
HALF = KC // 2
RB = 32
TQ_FOX = 256
FOX_VROWS = 2 * FOX_DIM + BF16_ROWS
DSA_VROWS = DSA_DIM + BF16_ROWS
TQ_DSA = 128
TM_POST = 512


def _cparams(sem):
    return pltpu.CompilerParams(dimension_semantics=sem, vmem_limit_bytes=VMEM_LIMIT)


def _split3(v):
    hi = v.astype(jnp.bfloat16)
    r1 = v - hi.astype(jnp.float32)
    mid = r1.astype(jnp.bfloat16)
    lo = (r1 - mid.astype(jnp.float32)).astype(jnp.bfloat16)
    return hi, mid, lo


def _col_reduce(v, op):
    return op(v.reshape(v.shape[0] // SUBLANES, SUBLANES, v.shape[1]), axis=0)


def _store_logits(s_ref, t):
    s_ref[...] = t
    return _col_reduce(t, jnp.max)


def _softmax_update(s_ref, p_ref, m_prev, mx):
    m_next = jnp.maximum(m_prev, jnp.max(mx, axis=0, keepdims=True))
    for r0 in range(0, s_ref.shape[0], RB):
        p_ref[r0:r0 + RB, :] = jnp.exp2(s_ref[r0:r0 + RB, :] - m_next).astype(p_ref.dtype)
    return m_next, jnp.exp2(m_prev - m_next)


def _inproj_kernel(x_ref, g_ref, wtm_ref, wfm_ref, fb_ref, fqg_ref, fkg_ref, dkg_ref,
                   ltri_ref, eq_ref, ek_ref, oneq_ref, onek_ref,
                   kc_ref, ks_ref, rc_ref, rs_ref,
                   dqa_ref, dqb_ref, iqa_ref, iqb_ref, rka_ref, rkb_ref,
                   fq_out, fk_out, fvt_out, dk_out, rq_out, rk_out, rv_out, rg_out,
                   qt_out, iqt_out, vt_out, iw_out, rkt_out,
                   carry_ref, *, tiles_per_seq):
    tm = x_ref.shape[0]
    i = pl.program_id(0)

    @pl.when(i % tiles_per_seq == 0)
    def _():
        carry_ref[...] = jnp.zeros_like(carry_ref)

    x = x_ref[...]
    ms = jnp.mean(x * x, axis=-1, keepdims=True)
    h = (x * lax.rsqrt(ms + NORM_EPS) * g_ref[...]).astype(jnp.bfloat16)

    def tm_dot(lo, width):
        return jnp.dot(h, wtm_ref[:, lo:lo + width], preferred_element_type=jnp.float32)

    lane = lax.broadcasted_iota(jnp.int32, (tm, LANES), 1)

    ffb = tm_dot(T_FF, LANES) + fb_ref[...]
    lf = (jnp.minimum(ffb, 0.0) - jnp.log1p(jnp.exp(-jnp.abs(ffb)))) * LOG2E
    parts = jnp.concatenate(_split3(lf), axis=1)
    cs = jnp.dot(ltri_ref[...], parts, preferred_element_type=jnp.float32)
    c = cs[:, :LANES] + cs[:, LANES:2 * LANES] + cs[:, 2 * LANES:] + carry_ref[...]
    carry_ref[...] = c[tm - 1:tm, :]
    cparts = jnp.concatenate(_split3(c), axis=1)
    scat_q = jnp.dot(cparts, eq_ref[...], preferred_element_type=jnp.float32) + oneq_ref[...]
    scat_k = jnp.dot(cparts, ek_ref[...], preferred_element_type=jnp.float32) + onek_ref[...]

    for (lo, gain_ref, scat, out, scale) in ((T_FQ, fqg_ref, scat_q, fq_out, FOX_DIM ** -0.5 * LOG2E),
                                             (T_FK, fkg_ref, scat_k, fk_out, 1.0)):
        z = tm_dot(lo, FOX_HEADS * LANES)
        for hd in range(FOX_HEADS):
            blk = z[:, hd * LANES:(hd + 1) * LANES]
            ss = jnp.sum(blk * blk, axis=-1, keepdims=True) * (1.0 / FOX_DIM)
            nb = blk * lax.rsqrt(ss + NORM_EPS) * (gain_ref[...] * scale)
            out[:, hd * LANES:(hd + 1) * LANES] = (
                nb + scat[:, hd * LANES:(hd + 1) * LANES]).astype(out.dtype)

    zk = tm_dot(T_DK, LANES)
    ssk = jnp.sum(jnp.where(lane < DSA_DIM, zk * zk, 0.0), axis=-1, keepdims=True) * (1.0 / DSA_DIM)
    nk = zk * jnp.where(lane < DSA_DIM, lax.rsqrt(ssk + NORM_EPS), 1.0) * dkg_ref[...]
    partner = jnp.where(
        lane < 32, pltpu.roll(nk, LANES - 32, 1),
        jnp.where(lane < 64, pltpu.roll(nk, 32, 1),
                  jnp.where(lane < 80, pltpu.roll(nk, LANES - 16, 1), pltpu.roll(nk, 16, 1))))
    dk_out[...] = (nk * kc_ref[...] + partner * ks_ref[...]).astype(dk_out.dtype)

    first_half = (lane % RET_QK_DIM) < (RET_QK_DIM // 2)
    for (lo, out, scale) in ((T_RQ, rq_out, RET_QK_DIM ** -0.5), (T_RK, rk_out, 1.0)):
        z = tm_dot(lo, RET_QK_W)
        for j in range(RET_QK_W // LANES):
            blk = z[:, j * LANES:(j + 1) * LANES]
            pr = jnp.where(first_half, pltpu.roll(blk, LANES - 32, 1), pltpu.roll(blk, 32, 1))
            out[:, j * LANES:(j + 1) * LANES] = (
                (blk * rc_ref[...] + pr * rs_ref[...]) * scale).astype(out.dtype)

    rv_out[...] = tm_dot(T_RV, RET_V_W).astype(rv_out.dtype)
    rg_out[...] = tm_dot(T_RG, RET_V_W)

    zt = lax.dot_general(wfm_ref[...], h, (((1,), (1,)), ((), ())),
                         preferred_element_type=jnp.float32)
    nq = tm // TQ_DSA

    def swap_halves(v):
        half = v.shape[0] // 2
        return jnp.concatenate([v[half:], v[:half]], axis=0)

    def ones_row_block(rows, dtype):
        first = lax.broadcasted_iota(jnp.int32, (rows, tm), 0) == 0
        return jnp.where(first, 1.0, 0.0).astype(dtype)

    zeros_q = jnp.zeros((LANES - DSA_DIM, DSA_HEADS * TQ_DSA), qt_out.dtype)
    zeros_i0 = jnp.zeros((DSA_DIM, IDX_HEADS * TQ_DSA), iqt_out.dtype)
    zeros_i1 = jnp.zeros((LANES - DSA_DIM - IDX_DIM, IDX_HEADS * TQ_DSA), iqt_out.dtype)
    for j in range(nq):
        qt_out[j, DSA_DIM:, :] = zeros_q
        iqt_out[j, :DSA_DIM, :] = zeros_i0
        iqt_out[j, DSA_DIM + IDX_DIM:, :] = zeros_i1
    for hd in range(DSA_HEADS):
        xh = zt[F_DQ + hd * DSA_DIM:F_DQ + (hd + 1) * DSA_DIM, :]
        r = lax.rsqrt(jnp.sum(xh * xh, axis=0, keepdims=True) * (1.0 / DSA_DIM) + NORM_EPS)
        o = ((xh * dqa_ref[...] + swap_halves(xh) * dqb_ref[...]) * r).astype(qt_out.dtype)
        for j in range(nq):
            qt_out[j, :DSA_DIM, hd * TQ_DSA:(hd + 1) * TQ_DSA] = o[:, j * TQ_DSA:(j + 1) * TQ_DSA]
    for hd in range(IDX_HEADS):
        xh = zt[F_IQ + hd * IDX_DIM:F_IQ + (hd + 1) * IDX_DIM, :]
        o = (xh * iqa_ref[...] + swap_halves(xh) * iqb_ref[...]).astype(iqt_out.dtype)
        for j in range(nq):
            iqt_out[j, DSA_DIM:DSA_DIM + IDX_DIM, hd * TQ_DSA:(hd + 1) * TQ_DSA] = (
                o[:, j * TQ_DSA:(j + 1) * TQ_DSA])
    vt_out[0, 0, :DSA_DIM, :] = zt[F_DV:F_DV + DSA_DIM, :].astype(vt_out.dtype)
    vt_out[0, 0, DSA_DIM:, :] = ones_row_block(BF16_ROWS, vt_out.dtype)
    iw_out[0] = zt[F_IW:F_IW + IDX_HEADS, :] * ((IDX_DIM * IDX_HEADS) ** -0.5)
    for hd in range(RET_HEADS):
        xh = zt[F_RK + hd * RET_QK_DIM:F_RK + (hd + 1) * RET_QK_DIM, :]
        rkt_out[0, hd * RET_QK_DIM:(hd + 1) * RET_QK_DIM, :] = (
            xh * rka_ref[...] + swap_halves(xh) * rkb_ref[...])
    for hp in range(FOX_HEADS // 2):
        fvt_out[0, hp, 0, :2 * FOX_DIM, :] = (
            zt[F_FV + hp * 2 * FOX_DIM:F_FV + (hp + 1) * 2 * FOX_DIM, :].astype(fvt_out.dtype))
        fvt_out[0, hp, 0, 2 * FOX_DIM:, :] = ones_row_block(BF16_ROWS, fvt_out.dtype)


def _inproj(x2d, lw, consts, batch, seq):
    n = x2d.shape[0]
    tm = min(TM_IN, seq)
    tps = seq // tm
    nqt = tm // TQ_DSA
    grid = (n // tm,)
    bf = jnp.bfloat16

    def full(a):
        return pl.BlockSpec(a.shape, lambda i: (0,) * a.ndim)

    def tok(width):
        return pl.BlockSpec((tm, width), lambda i: (i, 0))

    def pos_tm(width):
        return pl.BlockSpec((tm, width), lambda i: (i % tps, 0))

    def pos_fm(rows):
        return pl.BlockSpec((rows, tm), lambda i: (0, i % tps))

    def fm_out(rows):
        return pl.BlockSpec((1, rows, tm), lambda i: (i // tps, 0, i % tps))

    in_arrays = [x2d, lw["ln1_g"], lw["w_tm"], lw["w_fm"], lw["fox_b"], lw["fq_gain"], lw["fk_gain"],
                 lw["dk_gain"], consts["ltri"], consts["eq"], consts["ek"], consts["oneq"], consts["onek"],
                 consts["kc"], consts["ks"], consts["rc"], consts["rs"],
                 lw["dqa"], lw["dqb"], consts["iqa"], consts["iqb"], consts["rka"], consts["rkb"]]
    in_specs = [tok(D_MODEL), full(lw["ln1_g"]), full(lw["w_tm"]), full(lw["w_fm"]), full(lw["fox_b"]),
                full(lw["fq_gain"]), full(lw["fk_gain"]), full(lw["dk_gain"]),
                full(consts["ltri"]), full(consts["eq"]), full(consts["ek"]),
                full(consts["oneq"]), full(consts["onek"]),
                pos_tm(LANES), pos_tm(LANES), pos_tm(LANES), pos_tm(LANES),
                pos_fm(DSA_DIM), pos_fm(DSA_DIM), pos_fm(IDX_DIM), pos_fm(IDX_DIM),
                pos_fm(RET_QK_DIM), pos_fm(RET_QK_DIM)]
    out_shape = [
        jax.ShapeDtypeStruct((n, FOX_HEADS * LANES), bf),
        jax.ShapeDtypeStruct((n, FOX_HEADS * LANES), bf),
        jax.ShapeDtypeStruct((batch, FOX_HEADS // 2, tps, FOX_VROWS, tm), bf),
        jax.ShapeDtypeStruct((n, LANES), bf),
        jax.ShapeDtypeStruct((n, RET_QK_W), bf),
        jax.ShapeDtypeStruct((n, RET_QK_W), bf),
        jax.ShapeDtypeStruct((n, RET_V_W), bf),
        jax.ShapeDtypeStruct((n, RET_V_W), jnp.float32),
        jax.ShapeDtypeStruct((n // TQ_DSA, LANES, DSA_HEADS * TQ_DSA), bf),
        jax.ShapeDtypeStruct((n // TQ_DSA, LANES, IDX_HEADS * TQ_DSA), bf),
        jax.ShapeDtypeStruct((batch, tps, DSA_VROWS, tm), bf),
        jax.ShapeDtypeStruct((batch, IDX_HEADS, seq), jnp.float32),
        jax.ShapeDtypeStruct((batch, RET_QK_W, seq), jnp.float32),
    ]
    out_specs = [tok(FOX_HEADS * LANES), tok(FOX_HEADS * LANES),
                 pl.BlockSpec((1, FOX_HEADS // 2, 1, FOX_VROWS, tm), lambda i: (i // tps, 0, i % tps, 0, 0)),
                 tok(LANES), tok(RET_QK_W), tok(RET_QK_W), tok(RET_V_W), tok(RET_V_W),
                 pl.BlockSpec((nqt, LANES, DSA_HEADS * TQ_DSA), lambda i: (i, 0, 0)),
                 pl.BlockSpec((nqt, LANES, IDX_HEADS * TQ_DSA), lambda i: (i, 0, 0)),
                 pl.BlockSpec((1, 1, DSA_VROWS, tm), lambda i: (i // tps, i % tps, 0, 0)),
                 fm_out(IDX_HEADS), fm_out(RET_QK_W)]
    return pl.pallas_call(
        functools.partial(_inproj_kernel, tiles_per_seq=tps),
        grid=grid, in_specs=in_specs, out_specs=out_specs, out_shape=out_shape,
        scratch_shapes=[pltpu.VMEM((1, LANES), jnp.float32)],
        compiler_params=_cparams(("arbitrary",)), name="inproj",
    )(*in_arrays)


def _fox_kernel(q_ref, k_ref, vt_ref, o_ref, sa_scr, sb_scr, pa_scr, pb_scr, acc_scr):
    tq = q_ref.shape[0]
    qi = pl.program_id(2)
    n_chunks = (qi * tq + tq + KC - 1) // KC
    width = 2 * tq
    nt = (((1,), (1,)), ((), ()))

    qcol = qi * tq + lax.broadcasted_iota(jnp.int32, (HALF, tq), 1)
    krow = lax.broadcasted_iota(jnp.int32, (HALF, tq), 0)

    def qk(c, half, s_ref):
        off = pl.multiple_of(c * KC + half * HALF, HALF)
        is_causal = krow + off <= qcol
        mx = []
        for hh in range(2):
            s = lax.dot_general(k_ref[pl.ds(off, HALF), hh * LANES:(hh + 1) * LANES],
                                q_ref[:, hh * LANES:(hh + 1) * LANES], nt,
                                preferred_element_type=jnp.float32)
            mx.append(_store_logits(s_ref.at[:, hh * tq:(hh + 1) * tq], jnp.where(is_causal, s, NEG_BIG)))
        return jnp.concatenate(mx, axis=1)

    def pv(c, half, p_ref):
        vt = vt_ref[0, 0, c, :, half * HALF:(half + 1) * HALF]
        return jnp.concatenate(
            [jnp.dot(vt, p_ref[:, hh * tq:(hh + 1) * tq], preferred_element_type=jnp.float32)
             for hh in range(2)], axis=1)

    acc_scr[...] = jnp.zeros_like(acc_scr)
    pb_scr[...] = jnp.zeros_like(pb_scr)
    mx_first = qk(0, 0, sa_scr)

    def chunk(c, carry):
        m_prev, mx_a = carry
        mx_b = qk(c, 1, sb_scr)
        acc = acc_scr[...] + pv(jnp.maximum(c - 1, 0), 1, pb_scr)
        m_a, alpha_a = _softmax_update(sa_scr, pa_scr, m_prev, mx_a)
        acc = acc * alpha_a
        mx_next = qk(jnp.minimum(c + 1, n_chunks - 1), 0, sa_scr)
        acc = acc + pv(c, 0, pa_scr)
        m_b, alpha_b = _softmax_update(sb_scr, pb_scr, m_a, mx_b)
        acc_scr[...] = acc * alpha_b
        return m_b, mx_next

    lax.fori_loop(0, n_chunks, chunk, (jnp.full((1, width), NEG_BIG, jnp.float32), mx_first))
    acc = acc_scr[...] + pv(n_chunks - 1, 1, pb_scr)
    den = acc[2 * FOX_DIM:2 * FOX_DIM + 1, :]
    out_t = jnp.concatenate([acc[:FOX_DIM, :tq] / den[:, :tq],
                             acc[FOX_DIM:2 * FOX_DIM, tq:] / den[:, tq:]], axis=0)
    for j in range(tq // LANES):
        o_ref[j * LANES:(j + 1) * LANES, :] = out_t[:, j * LANES:(j + 1) * LANES].T.astype(o_ref.dtype)


def _fox(fq, fk, fvt, batch, seq):
    tq = min(TQ_FOX, seq)
    nq = seq // tq
    width = 2 * tq
    return pl.pallas_call(
        _fox_kernel, grid=(batch, FOX_HEADS // 2, nq),
        in_specs=[pl.BlockSpec((tq, 2 * LANES), lambda b, hp, qi: (b * nq + qi, hp)),
                  pl.BlockSpec((seq, 2 * LANES), lambda b, hp, qi: (b, hp)),
                  pl.BlockSpec((1, 1) + fvt.shape[2:], lambda b, hp, qi: (b, hp, 0, 0, 0))],
        out_specs=pl.BlockSpec((tq, LANES), lambda b, hp, qi: (b * nq + qi, hp)),
        out_shape=jax.ShapeDtypeStruct((batch * seq, FOX_W), jnp.bfloat16),
        scratch_shapes=[pltpu.VMEM((HALF, width), jnp.float32), pltpu.VMEM((HALF, width), jnp.float32),
                        pltpu.VMEM((HALF, width), jnp.bfloat16), pltpu.VMEM((HALF, width), jnp.bfloat16),
                        pltpu.VMEM((FOX_VROWS, width), jnp.float32)],
        compiler_params=_cparams(("parallel", "parallel", "arbitrary")), name="fox",
    )(fq, fk, fvt)


def _dsa_kernel(k_ref, vt_ref, qt_ref, iqt_ref, iw_ref, ltri_ref, o_ref,
                key_scr, bias_scr, sa_scr, sb_scr, pa_scr, pb_scr, acc_scr, *, topk):
    tq = TQ_DSA
    nh = DSA_HEADS
    width = nh * tq
    qi = pl.program_id(1)
    n_chunks = (qi * tq + tq + KC - 1) // KC

    qpos = qi * tq + lax.broadcasted_iota(jnp.int32, (KC, tq), 1)
    krow = lax.broadcasted_iota(jnp.int32, (KC, tq), 0)

    def score_chunk(c, carry):
        off = pl.multiple_of(c * KC, KC)
        rel = jnp.dot(k_ref[pl.ds(off, KC), :], iqt_ref[0], preferred_element_type=jnp.float32)
        score = jnp.maximum(rel[:, :tq], 0.0) * iw_ref[0, 0:1, :]
        for hd in range(1, IDX_HEADS):
            score = score + jnp.maximum(rel[:, hd * tq:(hd + 1) * tq], 0.0) * iw_ref[0, hd:hd + 1, :]
        score = jnp.where(score == 0.0, 0.0, score)
        score = jnp.where(krow + off <= qpos, score, -jnp.inf)
        bits = pltpu.bitcast(score, jnp.int32)
        key_scr[pl.ds(off, KC), :] = bits ^ ((bits >> 31) & jnp.int32(0x7FFFFFFF))
        return carry

    lax.fori_loop(0, n_chunks, score_chunk, 0)

    def count_ge(thr):
        def body(c, acc):
            off = pl.multiple_of(c * KC, KC)
            hit = jnp.where(key_scr[pl.ds(off, KC), :] >= thr, 1, 0).astype(jnp.int32)
            return acc + _col_reduce(hit, jnp.sum)
        acc = lax.fori_loop(0, n_chunks, body, jnp.zeros((SUBLANES, tq), jnp.int32))
        return jnp.sum(acc, axis=0, keepdims=True)

    def bit_step(b, thr):
        bit = jnp.left_shift(jnp.int32(1), 31 - b)
        cand = jnp.where(b == 0, jnp.zeros_like(thr), thr | bit)
        return jnp.where(count_ge(cand) >= topk, cand, thr)

    thr = lax.fori_loop(0, 32, bit_step, jnp.full((1, tq), -2 ** 31, jnp.int32))
    has_ties = jnp.max(count_ge(thr)) > topk

    @pl.when(jnp.logical_not(has_ties))
    def _():
        def body(c, carry):
            off = pl.multiple_of(c * KC, KC)
            bias_scr[pl.ds(off, KC), :] = jnp.where(key_scr[pl.ds(off, KC), :] >= thr, 0.0, NEG_BIG)
            return carry
        lax.fori_loop(0, n_chunks, body, 0)

    @pl.when(has_ties)
    def _():
        int_max = jnp.int32(2 ** 31 - 1)
        n_gt = jnp.where(thr == int_max, 0, count_ge(jnp.where(thr == int_max, thr, thr + 1)))
        need = (topk - n_gt).astype(jnp.float32)

        def body(c, seen):
            off = pl.multiple_of(c * KC, KC)
            ks = key_scr[pl.ds(off, KC), :]
            eq = ks == thr
            eqf = jnp.where(eq, 1.0, 0.0)
            before = jnp.dot(ltri_ref[...], eqf.astype(jnp.bfloat16),
                             preferred_element_type=jnp.float32) + seen
            sel = jnp.logical_or(ks > thr, jnp.logical_and(eq, before < need))
            sel = jnp.logical_and(sel, krow + off <= qpos)
            bias_scr[pl.ds(off, KC), :] = jnp.where(sel, 0.0, NEG_BIG)
            return seen + jnp.sum(eqf, axis=0, keepdims=True)
        lax.fori_loop(0, n_chunks, body, jnp.zeros((1, tq), jnp.float32))

    def qk(c, half, s_ref):
        off = pl.multiple_of(c * KC + half * HALF, HALF)
        s = jnp.dot(k_ref[pl.ds(off, HALF), :], qt_ref[0], preferred_element_type=jnp.float32)
        return _store_logits(s_ref, s + jnp.concatenate([bias_scr[pl.ds(off, HALF), :]] * nh, axis=1))

    def pv(c, half, p_ref):
        return jnp.dot(vt_ref[0, c, :, half * HALF:(half + 1) * HALF], p_ref[...],
                       preferred_element_type=jnp.float32)

    acc_scr[...] = jnp.zeros_like(acc_scr)
    pb_scr[...] = jnp.zeros_like(pb_scr)
    mx_first = qk(0, 0, sa_scr)

    def chunk(c, carry):
        m_prev, mx_a = carry
        mx_b = qk(c, 1, sb_scr)
        acc = acc_scr[...] + pv(jnp.maximum(c - 1, 0), 1, pb_scr)
        m_a, alpha_a = _softmax_update(sa_scr, pa_scr, m_prev, mx_a)
        acc = acc * alpha_a
        mx_next = qk(jnp.minimum(c + 1, n_chunks - 1), 0, sa_scr)
        acc = acc + pv(c, 0, pa_scr)
        m_b, alpha_b = _softmax_update(sb_scr, pb_scr, m_a, mx_b)
        acc_scr[...] = acc * alpha_b
        return m_b, mx_next

    lax.fori_loop(0, n_chunks, chunk, (jnp.full((1, width), NEG_BIG, jnp.float32), mx_first))
    acc = acc_scr[...] + pv(n_chunks - 1, 1, pb_scr)
    out_t = acc[:DSA_DIM, :] / acc[DSA_DIM:DSA_DIM + 1, :]
    for hp in range(nh // 2):
        pair = jnp.concatenate([out_t[:, 2 * hp * tq:(2 * hp + 1) * tq],
                                out_t[:, (2 * hp + 1) * tq:(2 * hp + 2) * tq]], axis=0)
        o_ref[:, hp * LANES:(hp + 1) * LANES] = pair.T.astype(o_ref.dtype)


def _dsa(dk, vt, qt, iqt, iwt, ltri_strict, batch, seq):
    nq = seq // TQ_DSA
    assert vt.shape[3] == KC and ltri_strict.shape[0] == KC and seq % KC == 0
    topk = min(DSA_MAX_TOPK, seq // 4)
    width = DSA_HEADS * TQ_DSA
    return pl.pallas_call(
        functools.partial(_dsa_kernel, topk=topk),
        grid=(batch, nq),
        in_specs=[pl.BlockSpec((seq, LANES), lambda b, qi: (b, 0)),
                  pl.BlockSpec((1,) + vt.shape[1:], lambda b, qi: (b, 0, 0, 0)),
                  pl.BlockSpec((1, LANES, width), lambda b, qi: (b * nq + qi, 0, 0)),
                  pl.BlockSpec((1, LANES, IDX_HEADS * TQ_DSA), lambda b, qi: (b * nq + qi, 0, 0)),
                  pl.BlockSpec((1, IDX_HEADS, TQ_DSA), lambda b, qi: (b, 0, qi)),
                  pl.BlockSpec(ltri_strict.shape, lambda b, qi: (0, 0))],
        out_specs=pl.BlockSpec((TQ_DSA, DSA_W), lambda b, qi: (b * nq + qi, 0)),
        out_shape=jax.ShapeDtypeStruct((batch * seq, DSA_W), jnp.bfloat16),
        scratch_shapes=[pltpu.VMEM((seq, TQ_DSA), jnp.int32),
                        pltpu.VMEM((seq, TQ_DSA), jnp.float32),
                        pltpu.VMEM((HALF, width), jnp.float32), pltpu.VMEM((HALF, width), jnp.float32),
                        pltpu.VMEM((HALF, width), jnp.bfloat16), pltpu.VMEM((HALF, width), jnp.bfloat16),
                        pltpu.VMEM((DSA_VROWS, width), jnp.float32)],
        compiler_params=_cparams(("parallel", "arbitrary")), name="dsa",
    )(dk, vt, qt, iqt, iwt, ltri_strict)


def _ret_kernel(q_ref, k_ref, kt_ref, v_ref, g_ref, din_ref, dq_ref, dk_ref, dc_ref, gain_ref,
                o_ref, state_scr):
    t = pl.program_id(1)

    @pl.when(t == 0)
    def _():
        state_scr[...] = jnp.zeros_like(state_scr)

    for hd in range(RET_HEADS):
        q = q_ref[:, hd * RET_QK_DIM:(hd + 1) * RET_QK_DIM]
        k = k_ref[:, hd * RET_QK_DIM:(hd + 1) * RET_QK_DIM]
        v = v_ref[:, hd * RET_V_DIM:(hd + 1) * RET_V_DIM]
        state = state_scr[hd]
        attn = lax.dot_general(q, k, (((1,), (1,)), ((), ())),
                               preferred_element_type=jnp.float32) * din_ref[hd]
        inner = jnp.dot(attn.astype(jnp.bfloat16), v, preferred_element_type=jnp.float32)
        cross = jnp.dot(q, state.astype(jnp.bfloat16),
                        preferred_element_type=jnp.float32) * dq_ref[hd]
        ktd = (kt_ref[0, hd * RET_QK_DIM:(hd + 1) * RET_QK_DIM, :] * dk_ref[hd]).astype(jnp.bfloat16)
        state_scr[hd] = dc_ref[hd] * state + jnp.dot(ktd, v, preferred_element_type=jnp.float32)
        y = inner + cross
        yn = y * lax.rsqrt(jnp.mean(y * y, axis=-1, keepdims=True) + NORM_EPS) * gain_ref[hd]
        gate = g_ref[:, hd * RET_V_DIM:(hd + 1) * RET_V_DIM]
        o_ref[:, hd * RET_V_DIM:(hd + 1) * RET_V_DIM] = (
            yn * (gate * jax.nn.sigmoid(gate))).astype(o_ref.dtype)


def _ret(rq, rk, rkt, rv, rg, consts, gain, batch, seq):
    c = RET_CHUNK
    n = seq // c

    def tok(width):
        return pl.BlockSpec((c, width), lambda b, t: (b * n + t, 0))

    def full(a):
        return pl.BlockSpec(a.shape, lambda b, t: (0,) * a.ndim)

    return pl.pallas_call(
        _ret_kernel, grid=(batch, n),
        in_specs=[tok(RET_QK_W), tok(RET_QK_W),
                  pl.BlockSpec((1, RET_QK_W, c), lambda b, t: (b, 0, t)),
                  tok(RET_V_W), tok(RET_V_W),
                  full(consts["ret_din"]), full(consts["ret_dq"]), full(consts["ret_dk"]),
                  full(consts["ret_dc"]), full(gain)],
        out_specs=tok(RET_V_W),
        out_shape=jax.ShapeDtypeStruct((batch * seq, RET_V_W), jnp.bfloat16),
        scratch_shapes=[pltpu.VMEM((RET_HEADS, RET_QK_DIM, RET_V_DIM), jnp.float32)],
        compiler_params=_cparams(("parallel", "arbitrary")), name="ret",
    )(rq, rk, rkt, rv, rg, consts["ret_din"], consts["ret_dq"], consts["ret_dk"], consts["ret_dc"], gain)


def _merge_kernel(x_ref, g_ref, ya_ref, yb_ref, yc_ref, wzg_ref, wa_ref, wb_ref, wc_ref, wo_ref, o_ref):
    x = x_ref[...]
    ms = jnp.mean(x * x, axis=-1, keepdims=True)
    h = (x * lax.rsqrt(ms + NORM_EPS) * g_ref[...]).astype(jnp.bfloat16)
    merged = None
    for j, (y_ref, w_ref) in enumerate(((ya_ref, wa_ref), (yb_ref, wb_ref), (yc_ref, wc_ref))):
        gate = jax.nn.sigmoid(jnp.dot(h, wzg_ref[:, j * D_MODEL:(j + 1) * D_MODEL],
                                      preferred_element_type=jnp.float32))
        term = gate * jnp.dot(y_ref[...], w_ref[...], preferred_element_type=jnp.float32)
        merged = term if merged is None else merged + term
    o_ref[...] = x + jnp.dot(merged.astype(jnp.bfloat16), wo_ref[...],
                             preferred_element_type=jnp.float32)


def _ffn_kernel(x_ref, g_ref, wg_ref, wu_ref, wd_ref, o_ref, *, chunk):
    x = x_ref[...]
    ms = jnp.mean(x * x, axis=-1, keepdims=True)
    h = (x * lax.rsqrt(ms + NORM_EPS) * g_ref[...]).astype(jnp.bfloat16)
    acc = x
    for lo in range(0, FFN_HIDDEN, chunk):
        gt = jnp.dot(h, wg_ref[:, lo:lo + chunk], preferred_element_type=jnp.float32)
        up = jnp.dot(h, wu_ref[:, lo:lo + chunk], preferred_element_type=jnp.float32)
        act = (gt * jax.nn.sigmoid(gt) * up).astype(jnp.bfloat16)
        acc = acc + jnp.dot(act, wd_ref[lo:lo + chunk, :], preferred_element_type=jnp.float32)
    o_ref[...] = acc


def _row_call(kernel, name, order):
    n = order[0][1].shape[0]
    tm = min(TM_POST, n)
    arrays, specs = [], []
    for kind, a in order:
        arrays.append(a)
        if kind == "row":
            specs.append(pl.BlockSpec((tm, a.shape[1]), lambda i: (i, 0)))
        else:
            specs.append(pl.BlockSpec(a.shape, lambda i, nd=a.ndim: (0,) * nd))
    return pl.pallas_call(
        kernel, grid=(n // tm,), in_specs=specs,
        out_specs=pl.BlockSpec((tm, D_MODEL), lambda i: (i, 0)),
        out_shape=jax.ShapeDtypeStruct((n, D_MODEL), jnp.float32),
        compiler_params=_cparams(("parallel",)), name=name,
    )(*arrays)


def _merge(x2d, ya, yb, yc, lw):
    order = [("row", x2d), ("full", lw["ln1_g"]), ("row", ya), ("row", yb), ("row", yc),
             ("full", lw["w_zg"]), ("full", lw["w_a"]), ("full", lw["w_b"]), ("full", lw["w_c"]),
             ("full", lw["w_o"])]
    return _row_call(_merge_kernel, "merge", order)


def _ffn(x2d, lw):
    order = [("row", x2d), ("full", lw["ln2_g"]), ("full", lw["w_g"]), ("full", lw["w_u"]),
             ("full", lw["w_d"])]
    return _row_call(functools.partial(_ffn_kernel, chunk=256), "ffn", order)


def _rope_cs(seq, dim):
    half = dim // 2
    inv_freq = ROPE_THETA ** (-jnp.arange(half, dtype=jnp.float32) / half)
    ang = jnp.arange(seq, dtype=jnp.float32)[:, None] * inv_freq[None, :]
    return jnp.cos(ang), jnp.sin(ang)


def _constants(seq):
    tm = min(TM_IN, seq)
    bf = jnp.bfloat16
    cos64, sin64 = _rope_cs(seq, 64)
    cos32, sin32 = _rope_cs(seq, 32)
    z32 = jnp.zeros((seq, 32), jnp.float32)
    c64 = jnp.concatenate([cos64, cos64], axis=1)
    s64 = jnp.concatenate([-sin64, sin64], axis=1)
    c32 = jnp.concatenate([cos32, cos32], axis=1)
    s32 = jnp.concatenate([-sin32, sin32], axis=1)
    consts = {
        "kc": jnp.concatenate([c64, c32, z32], axis=1), "ks": jnp.concatenate([s64, s32, z32], axis=1),
        "rc": jnp.concatenate([c64, c64], axis=1), "rs": jnp.concatenate([s64, s64], axis=1),
        "c64t": c64.T, "s64t": s64.T,
        "iqa": c32.T, "iqb": s32.T, "rka": c64.T, "rkb": s64.T,
        "ltri": jnp.tril(jnp.ones((tm, tm), jnp.float32)).astype(bf),
        "ltri_strict": jnp.tril(jnp.ones((KC, KC), jnp.float32), -1).astype(bf),
    }
    eq = np.zeros((3 * LANES, FOX_HEADS * LANES), np.float32)
    ek = np.zeros((3 * LANES, FOX_HEADS * LANES), np.float32)
    oneq = np.zeros((1, FOX_HEADS * LANES), np.float32)
    onek = np.zeros((1, FOX_HEADS * LANES), np.float32)
    for hd in range(FOX_HEADS):
        base = hd * LANES + FOX_DIM
        for part in range(3):
            eq[part * LANES + hd, base + part] = 1.0
            ek[part * LANES + hd, base + 3 + part] = -1.0
            oneq[0, base + 3 + part] = 1.0
            onek[0, base + part] = 1.0
    consts.update(eq=jnp.asarray(eq, bf), ek=jnp.asarray(ek, bf), oneq=jnp.asarray(oneq), onek=jnp.asarray(onek))
    log_g = jnp.log1p(-(2.0 ** (-5.0 - jnp.arange(RET_HEADS, dtype=jnp.float32))))
    pos = jnp.arange(RET_CHUNK, dtype=jnp.float32)
    diff = pos[:, None] - pos[None, :]
    din = jnp.where(diff >= 0, jnp.exp(jnp.maximum(diff, 0.0)[None] * log_g[:, None, None]), 0.0)
    dq = jnp.exp((pos + 1.0)[None] * log_g[:, None])
    dk = jnp.exp((RET_CHUNK - 1.0 - pos)[None] * log_g[:, None])
    dc = jnp.exp(RET_CHUNK * log_g)
    consts.update(ret_din=din, ret_dq=dq[:, :, None], ret_dk=dk[:, None, :],
                  ret_dc=jnp.broadcast_to(dc[:, None, None], (RET_HEADS, 1, LANES)))
    return consts


def _pad_heads(w, heads, dim):
    k = w.shape[0]
    w3 = w.reshape(k, heads, dim)
    return jnp.pad(w3, ((0, 0), (0, 0), (0, LANES - dim))).reshape(k, heads * LANES)


def _layer_weights(p, consts):
    bf = jnp.bfloat16
    w_in = p["w_in"]

    def cols(off, size):
        return w_in[:, off:off + size]

    zeros = lambda n: jnp.zeros((D_MODEL, n), jnp.float32)
    w_tm = jnp.concatenate([
        _pad_heads(cols(O_FQ, FOX_W), FOX_HEADS, FOX_DIM),
        _pad_heads(cols(O_FK, FOX_W), FOX_HEADS, FOX_DIM),
        cols(O_FF, FOX_HEADS), zeros(LANES - FOX_HEADS),
        cols(O_DK, DSA_DIM), cols(O_IK, IDX_DIM), zeros(LANES - DSA_DIM - IDX_DIM),
        cols(O_RQ, RET_QK_W), cols(O_RK, RET_QK_W), cols(O_RV, RET_V_W), cols(O_RG, RET_V_W),
    ], axis=1).astype(bf)
    w_fm = jnp.concatenate([
        cols(O_DQ, DSA_W), cols(O_IQ, IDX_W), cols(O_DV, DSA_DIM),
        cols(O_IW, IDX_HEADS), zeros(16 - IDX_HEADS), cols(O_RK, RET_QK_W), cols(O_FV, FOX_W),
    ], axis=1).T.astype(bf)

    def lane_pad(v, fill=0.0):
        return jnp.concatenate([v, jnp.full((LANES - v.shape[0],), fill, jnp.float32)])[None, :]

    g = p["dsa_q_norm"]
    g_sw = jnp.concatenate([g[DSA_DIM // 2:], g[:DSA_DIM // 2]])
    scale = DSA_DIM ** -0.5 * LOG2E
    return {
        "ln1_g": p["ln1_g"][None, :], "ln2_g": p["ln2_g"][None, :],
        "w_tm": w_tm, "w_fm": w_fm,
        "fox_b": lane_pad(p["fox_b_f"]),
        "fq_gain": lane_pad(p["fox_q_norm"]), "fk_gain": lane_pad(p["fox_k_norm"]),
        "dk_gain": jnp.concatenate([p["dsa_k_norm"], jnp.ones((IDX_DIM,), jnp.float32),
                                    jnp.zeros((LANES - DSA_DIM - IDX_DIM,), jnp.float32)])[None, :],
        "dqa": consts["c64t"] * (g * scale)[:, None], "dqb": consts["s64t"] * (g_sw * scale)[:, None],
        "ret_gain": p["ret_out_norm"][:, None, :],
        "w_zg": cols(O_ZG, N_BRANCH * D_MODEL).astype(bf),
        "w_a": p["w_fox_out"].astype(bf), "w_b": p["w_dsa_out"].astype(bf), "w_c": p["w_ret_out"].astype(bf),
        "w_o": p["w_o"].astype(bf),
        "w_g": p["w_ffn_in"][:, :FFN_HIDDEN].astype(bf), "w_u": p["w_ffn_in"][:, FFN_HIDDEN:].astype(bf),
        "w_d": p["w_ffn_out"].astype(bf),
    }


def _layer(x2d, lw, consts, batch, seq):
    (fq, fk, fvt, dk, rq, rk, rv, rg, qt, iqt, vt, iwt, rkt) = _inproj(x2d, lw, consts, batch, seq)
    ya = _fox(fq, fk, fvt, batch, seq)
    yb = _dsa(dk, vt, qt, iqt, iwt, consts["ltri_strict"], batch, seq)
    yc = _ret(rq, rk, rkt, rv, rg, consts, lw["ret_gain"], batch, seq)
    x2d = _merge(x2d, ya, yb, yc, lw)
    return _ffn(x2d, lw)


def kernel(x, ln1_g, w_in, fox_b_f, fox_q_norm, fox_k_norm, dsa_q_norm, dsa_k_norm, ret_out_norm,
           w_fox_out, w_dsa_out, w_ret_out, w_o, ln2_g, w_ffn_in, w_ffn_out):
    batch, seq, _ = x.shape
    depth = w_in.shape[0]
    consts = _constants(seq)
    params = dict(ln1_g=ln1_g, w_in=w_in, fox_b_f=fox_b_f, fox_q_norm=fox_q_norm, fox_k_norm=fox_k_norm,
                  dsa_q_norm=dsa_q_norm, dsa_k_norm=dsa_k_norm, ret_out_norm=ret_out_norm,
                  w_fox_out=w_fox_out, w_dsa_out=w_dsa_out, w_ret_out=w_ret_out, w_o=w_o,
                  ln2_g=ln2_g, w_ffn_in=w_ffn_in, w_ffn_out=w_ffn_out)
    x2d = x.reshape(batch * seq, D_MODEL)
    for layer in range(depth):
        lw = _layer_weights({k: v[layer] for k, v in params.items()}, consts)
        x2d = _layer(x2d, lw, consts, batch, seq)
    return x2d.reshape(batch, seq, D_MODEL)
```

```python
import functools
import math

import jax
import jax.numpy as jnp
import numpy as np
from jax import lax
from jax.experimental import pallas as pl
from jax.experimental.pallas import tpu as pltpu

D_MODEL = 1024
FOX_HEADS = 8
FOX_DIM = 64
DSA_HEADS = 8
DSA_DIM = 64
IDX_HEADS = 8
IDX_DIM = 32
DSA_MAX_TOPK = 256
RET_HEADS = 4
RET_QK_DIM = 64
RET_V_DIM = 128
RET_CHUNK = 128
FFN_HIDDEN = 2816
ROPE_THETA = 10000.0
NORM_EPS = 1e-6
N_BRANCH = 3

FOX_W = FOX_HEADS * FOX_DIM
DSA_W = DSA_HEADS * DSA_DIM
IDX_W = IDX_HEADS * IDX_DIM
RET_QK_W = RET_HEADS * RET_QK_DIM
RET_V_W = RET_HEADS * RET_V_DIM
IN_SIZES = (FOX_W, FOX_W, FOX_W, FOX_HEADS,
            DSA_W, DSA_DIM, DSA_DIM, IDX_W, IDX_DIM, IDX_HEADS,
            RET_QK_W, RET_QK_W, RET_V_W, RET_V_W,
            N_BRANCH * D_MODEL)
IN_OFFS = tuple(int(v) for v in np.cumsum((0,) + IN_SIZES))
(O_FQ, O_FK, O_FV, O_FF, O_DQ, O_DK, O_DV, O_IQ, O_IK, O_IW,
 O_RQ, O_RK, O_RV, O_RG, O_ZG, _) = IN_OFFS

LANES = 128
SUBLANES = 8
BF16_ROWS = 16
VMEM_LIMIT = 56 * 1024 * 1024
NEG_BIG = -1e30
LOG2E = math.log2(math.e)
BOUND_SLACK = 1.02
MIN_DENOM = 2.0 ** -100

T_FQ = 0
T_FK = T_FQ + FOX_HEADS * LANES
T_FF = T_FK + FOX_HEADS * LANES
T_DK = T_FF + LANES
T_RQ = T_DK + LANES
T_RK = T_RQ + RET_QK_W
T_RV = T_RK + RET_QK_W
T_RG = T_RV + RET_V_W
T_COLS = T_RG + RET_V_W
F_DQ = 0
F_IQ = F_DQ + DSA_W
F_DV = F_IQ + IDX_W
F_IW = F_DV + DSA_DIM
F_RK = F_IW + 16
F_FV = F_RK + RET_QK_W
F_ROWS = F_FV + FOX_W

TM_IN = 512
KC = TM_IN
HALF = KC // 2
RB = 32
TQ_FOX = 256
FOX_VROWS = 2 * FOX_DIM + BF16_ROWS
DSA_VROWS = DSA_DIM + BF16_ROWS
TQ_DSA = 128
TM_POST = 512


def _cparams(sem):
    return pltpu.CompilerParams(dimension_semantics=sem, vmem_limit_bytes=VMEM_LIMIT)


def _split3(v):
    hi = v.astype(jnp.bfloat16)
    r1 = v - hi.astype(jnp.float32)
    mid = r1.astype(jnp.bfloat16)
    lo = (r1 - mid.astype(jnp.float32)).astype(jnp.bfloat16)
    return hi, mid, lo


def _col_reduce(v, op):
    return op(v.reshape(v.shape[0] // SUBLANES, SUBLANES, v.shape[1]), axis=0)


def _attend(n_chunks, qk, mask, pv, sa_scr, sb_scr, pa_scr, pb_scr, acc_scr, den_row):
    def store_logits(s_ref, c, half):
        lo = 0
        for g in qk(c, half):
            s_ref[:, lo:lo + g.shape[1]] = g
            lo += g.shape[1]

    def probs(s_ref, c, half, p_ref, shift):
        for r0 in range(0, s_ref.shape[0], RB):
            t = mask(s_ref[r0:r0 + RB, :], c, half, r0)
            if shift is not None:
                t = t - shift
            p_ref[r0:r0 + RB, :] = jnp.exp2(t).astype(p_ref.dtype)

    def run(shift):
        acc_scr[...] = jnp.zeros_like(acc_scr)
        pb_scr[...] = jnp.zeros_like(pb_scr)
        store_logits(sa_scr, 0, 0)

        def chunk(c, carry):
            store_logits(sb_scr, c, 1)
            acc_scr[...] += pv(jnp.maximum(c - 1, 0), 1, pb_scr)
            probs(sa_scr, c, 0, pa_scr, shift)
            store_logits(sa_scr, jnp.minimum(c + 1, n_chunks - 1), 0)
            acc_scr[...] += pv(c, 0, pa_scr)
            probs(sb_scr, c, 1, pb_scr, shift)
            return carry

        lax.fori_loop(0, n_chunks, chunk, 0)
        acc_scr[...] += pv(n_chunks - 1, 1, pb_scr)

    run(None)

    @pl.when(jnp.logical_not(jnp.min(acc_scr[den_row:den_row + 1, :]) >= MIN_DENOM))
    def _():
        def col_max(c, mx):
            for half in range(2):
                store_logits(sa_scr, c, half)
                for r0 in range(0, sa_scr.shape[0], RB):
                    mx = jnp.maximum(mx, _col_reduce(mask(sa_scr[r0:r0 + RB, :], c, half, r0), jnp.max))
            return mx
        mx = lax.fori_loop(0, n_chunks, col_max,
                           jnp.full((SUBLANES, acc_scr.shape[1]), NEG_BIG, jnp.float32))
        run(jnp.max(mx, axis=0, keepdims=True))


def _inproj_kernel(bound_ref, x_ref, g_ref, wtm_ref, wfm_ref, fb_ref, fqg_ref, fkg_ref, dkg_ref,
                   ltri_ref, eq_ref, ek_ref, oneq_ref, onek_ref,
                   kc_ref, ks_ref, rc_ref, rs_ref,
                   dqa_ref, dqb_ref, iqa_ref, iqb_ref, rka_ref, rkb_ref,
                   fq_out, fk_out, fvt_out, dk_out, rq_out, rk_out, rv_out, rg_out,
                   qt_out, iqt_out, vt_out, iw_out, rkt_out,
                   carry_ref, *, tiles_per_seq):
    tm = x_ref.shape[0]
    i = pl.program_id(0)

    @pl.when(i % tiles_per_seq == 0)
    def _():
        carry_ref[...] = jnp.zeros_like(carry_ref)

    x = x_ref[...]
    ms = jnp.mean(x * x, axis=-1, keepdims=True)
    h = (x * lax.rsqrt(ms + NORM_EPS) * g_ref[...]).astype(jnp.bfloat16)

    def tm_dot(lo, width):
        return jnp.dot(h, wtm_ref[:, lo:lo + width], preferred_element_type=jnp.float32)

    lane = lax.broadcasted_iota(jnp.int32, (tm, LANES), 1)

    ffb = tm_dot(T_FF, LANES) + fb_ref[...]
    lf = (jnp.minimum(ffb, 0.0) - jnp.log1p(jnp.exp(-jnp.abs(ffb)))) * LOG2E
    parts = jnp.concatenate(_split3(lf), axis=1)
    cs = jnp.dot(ltri_ref[...], parts, preferred_element_type=jnp.float32)
    c = cs[:, :LANES] + cs[:, LANES:2 * LANES] + cs[:, 2 * LANES:] + carry_ref[...]
    carry_ref[...] = c[tm - 1:tm, :]
    cparts_q = jnp.concatenate(_split3(c - bound_ref[0, 0]), axis=1)
    cparts_k = jnp.concatenate(_split3(c), axis=1)
    scat_q = jnp.dot(cparts_q, eq_ref[...], preferred_element_type=jnp.float32) + oneq_ref[...]
    scat_k = jnp.dot(cparts_k, ek_ref[...], preferred_element_type=jnp.float32) + onek_ref[...]

    for (lo, gain_ref, scat, out, scale) in ((T_FQ, fqg_ref, scat_q, fq_out, FOX_DIM ** -0.5 * LOG2E),
                                             (T_FK, fkg_ref, scat_k, fk_out, 1.0)):
        z = tm_dot(lo, FOX_HEADS * LANES)
        for hd in range(FOX_HEADS):
            blk = z[:, hd * LANES:(hd + 1) * LANES]
            ss = jnp.sum(blk * blk, axis=-1, keepdims=True) * (1.0 / FOX_DIM)
            nb = blk * lax.rsqrt(ss + NORM_EPS) * (gain_ref[...] * scale)
            out[:, hd * LANES:(hd + 1) * LANES] = (
                nb + scat[:, hd * LANES:(hd + 1) * LANES]).astype(out.dtype)

    zk = tm_dot(T_DK, LANES)
    ssk = jnp.sum(jnp.where(lane < DSA_DIM, zk * zk, 0.0), axis=-1, keepdims=True) * (1.0 / DSA_DIM)
    nk = zk * jnp.where(lane < DSA_DIM, lax.rsqrt(ssk + NORM_EPS), 1.0) * dkg_ref[...]
    partner = jnp.where(
        lane < 32, pltpu.roll(nk, LANES - 32, 1),
        jnp.where(lane < 64, pltpu.roll(nk, 32, 1),
                  jnp.where(lane < 80, pltpu.roll(nk, LANES - 16, 1), pltpu.roll(nk, 16, 1))))
    dk_out[...] = (nk * kc_ref[...] + partner * ks_ref[...]).astype(dk_out.dtype)

    first_half = (lane % RET_QK_DIM) < (RET_QK_DIM // 2)
    for (lo, out, scale) in ((T_RQ, rq_out, RET_QK_DIM ** -0.5), (T_RK, rk_out, 1.0)):
        z = tm_dot(lo, RET_QK_W)
        for j in range(RET_QK_W // LANES):
            blk = z[:, j * LANES:(j + 1) * LANES]
            pr = jnp.where(first_half, pltpu.roll(blk, LANES - 32, 1), pltpu.roll(blk, 32, 1))
            out[:, j * LANES:(j + 1) * LANES] = (
                (blk * rc_ref[...] + pr * rs_ref[...]) * scale).astype(out.dtype)

    rv_out[...] = tm_dot(T_RV, RET_V_W).astype(rv_out.dtype)
    rg_out[...] = tm_dot(T_RG, RET_V_W)

    zt = lax.dot_general(wfm_ref[...], h, (((1,), (1,)), ((), ())),
                         preferred_element_type=jnp.float32)
    nq = tm // TQ_DSA

    def swap_halves(v):
        half = v.shape[0] // 2
        return jnp.concatenate([v[half:], v[:half]], axis=0)

    def ones_row_block(rows, dtype):
        first = lax.broadcasted_iota(jnp.int32, (rows, tm), 0) == 0
        return jnp.where(first, 1.0, 0.0).astype(dtype)

    zeros_q = jnp.zeros((LANES - DSA_DIM, DSA_HEADS * TQ_DSA), qt_out.dtype)
    zeros_i0 = jnp.zeros((DSA_DIM, IDX_HEADS * TQ_DSA), iqt_out.dtype)
    zeros_i1 = jnp.zeros((LANES - DSA_DIM - IDX_DIM, IDX_HEADS * TQ_DSA), iqt_out.dtype)
    for j in range(nq):
        qt_out[j, DSA_DIM:, :] = zeros_q
        iqt_out[j, :DSA_DIM, :] = zeros_i0
        iqt_out[j, DSA_DIM + IDX_DIM:, :] = zeros_i1
    for hd in range(DSA_HEADS):
        xh = zt[F_DQ + hd * DSA_DIM:F_DQ + (hd + 1) * DSA_DIM, :]
        r = lax.rsqrt(jnp.sum(xh * xh, axis=0, keepdims=True) * (1.0 / DSA_DIM) + NORM_EPS)
        o = ((xh * dqa_ref[...] + swap_halves(xh) * dqb_ref[...]) * r).astype(qt_out.dtype)
        for j in range(nq):
            qt_out[j, :DSA_DIM, hd * TQ_DSA:(hd + 1) * TQ_DSA] = o[:, j * TQ_DSA:(j + 1) * TQ_DSA]
    for hd in range(IDX_HEADS):
        xh = zt[F_IQ + hd * IDX_DIM:F_IQ + (hd + 1) * IDX_DIM, :]
        o = (xh * iqa_ref[...] + swap_halves(xh) * iqb_ref[...]).astype(iqt_out.dtype)
        for j in range(nq):
            iqt_out[j, DSA_DIM:DSA_DIM + IDX_DIM, hd * TQ_DSA:(hd + 1) * TQ_DSA] = (
                o[:, j * TQ_DSA:(j + 1) * TQ_DSA])
    vt_out[0, 0, :DSA_DIM, :] = zt[F_DV:F_DV + DSA_DIM, :].astype(vt_out.dtype)
    vt_out[0, 0, DSA_DIM:, :] = ones_row_block(BF16_ROWS, vt_out.dtype)
    iw_out[0] = zt[F_IW:F_IW + IDX_HEADS, :] * ((IDX_DIM * IDX_HEADS) ** -0.5)
    for hd in range(RET_HEADS):
        xh = zt[F_RK + hd * RET_QK_DIM:F_RK + (hd + 1) * RET_QK_DIM, :]
        rkt_out[0, hd * RET_QK_DIM:(hd + 1) * RET_QK_DIM, :] = (
            xh * rka_ref[...] + swap_halves(xh) * rkb_ref[...])
    for hp in range(FOX_HEADS // 2):
        fvt_out[0, hp, 0, :2 * FOX_DIM, :] = (
            zt[F_FV + hp * 2 * FOX_DIM:F_FV + (hp + 1) * 2 * FOX_DIM, :].astype(fvt_out.dtype))
        fvt_out[0, hp, 0, 2 * FOX_DIM:, :] = ones_row_block(BF16_ROWS, fvt_out.dtype)


def _inproj(x2d, lw, consts, batch, seq):
    n = x2d.shape[0]
    tm = min(TM_IN, seq)
    tps = seq // tm
    nqt = tm // TQ_DSA
    grid = (n // tm,)
    bf = jnp.bfloat16

    def full(a):
        return pl.BlockSpec(a.shape, lambda i: (0,) * a.ndim)

    def tok(width):
        return pl.BlockSpec((tm, width), lambda i: (i, 0))

    def pos_tm(width):
        return pl.BlockSpec((tm, width), lambda i: (i % tps, 0))

    def pos_fm(rows):
        return pl.BlockSpec((rows, tm), lambda i: (0, i % tps))

    def fm_out(rows):
        return pl.BlockSpec((1, rows, tm), lambda i: (i // tps, 0, i % tps))

    in_arrays = [lw["fox_bound"], x2d, lw["ln1_g"], lw["w_tm"], lw["w_fm"], lw["fox_b"], lw["fq_gain"], lw["fk_gain"],
                 lw["dk_gain"], consts["ltri"], consts["eq"], consts["ek"], consts["oneq"], consts["onek"],
                 consts["kc"], consts["ks"], consts["rc"], consts["rs"],
                 lw["dqa"], lw["dqb"], consts["iqa"], consts["iqb"], consts["rka"], consts["rkb"]]
    in_specs = [pl.BlockSpec(memory_space=pltpu.SMEM),
                tok(D_MODEL), full(lw["ln1_g"]), full(lw["w_tm"]), full(lw["w_fm"]), full(lw["fox_b"]),
                full(lw["fq_gain"]), full(lw["fk_gain"]), full(lw["dk_gain"]),
                full(consts["ltri"]), full(consts["eq"]), full(consts["ek"]),
                full(consts["oneq"]), full(consts["onek"]),
                pos_tm(LANES), pos_tm(LANES), pos_tm(LANES), pos_tm(LANES),
                pos_fm(DSA_DIM), pos_fm(DSA_DIM), pos_fm(IDX_DIM), pos_fm(IDX_DIM),
                pos_fm(RET_QK_DIM), pos_fm(RET_QK_DIM)]
    out_shape = [
        jax.ShapeDtypeStruct((n, FOX_HEADS * LANES), bf),
        jax.ShapeDtypeStruct((n, FOX_HEADS * LANES), bf),
        jax.ShapeDtypeStruct((batch, FOX_HEADS // 2, tps, FOX_VROWS, tm), bf),
        jax.ShapeDtypeStruct((n, LANES), bf),
        jax.ShapeDtypeStruct((n, RET_QK_W), bf),
        jax.ShapeDtypeStruct((n, RET_QK_W), bf),
        jax.ShapeDtypeStruct((n, RET_V_W), bf),
        jax.ShapeDtypeStruct((n, RET_V_W), jnp.float32),
        jax.ShapeDtypeStruct((n // TQ_DSA, LANES, DSA_HEADS * TQ_DSA), bf),
        jax.ShapeDtypeStruct((n // TQ_DSA, LANES, IDX_HEADS * TQ_DSA), bf),
        jax.ShapeDtypeStruct((batch, tps, DSA_VROWS, tm), bf),
        jax.ShapeDtypeStruct((batch, IDX_HEADS, seq), jnp.float32),
        jax.ShapeDtypeStruct((batch, RET_QK_W, seq), jnp.float32),
    ]
    out_specs = [tok(FOX_HEADS * LANES), tok(FOX_HEADS * LANES),
                 pl.BlockSpec((1, FOX_HEADS // 2, 1, FOX_VROWS, tm), lambda i: (i // tps, 0, i % tps, 0, 0)),
                 tok(LANES), tok(RET_QK_W), tok(RET_QK_W), tok(RET_V_W), tok(RET_V_W),
                 pl.BlockSpec((nqt, LANES, DSA_HEADS * TQ_DSA), lambda i: (i, 0, 0)),
                 pl.BlockSpec((nqt, LANES, IDX_HEADS * TQ_DSA), lambda i: (i, 0, 0)),
                 pl.BlockSpec((1, 1, DSA_VROWS, tm), lambda i: (i // tps, i % tps, 0, 0)),
                 fm_out(IDX_HEADS), fm_out(RET_QK_W)]
    return pl.pallas_call(
        functools.partial(_inproj_kernel, tiles_per_seq=tps),
        grid=grid, in_specs=in_specs, out_specs=out_specs, out_shape=out_shape,
        scratch_shapes=[pltpu.VMEM((1, LANES), jnp.float32)],
        compiler_params=_cparams(("arbitrary",)), name="inproj",
    )(*in_arrays)


def _fox_kernel(q_ref, k_ref, vt_ref, o_ref, sa_scr, sb_scr, pa_scr, pb_scr, acc_scr):
    tq = q_ref.shape[0]
    qi = pl.program_id(2)
    n_chunks = (qi * tq + tq + KC - 1) // KC
    nt = (((1,), (1,)), ((), ()))

    qcol = qi * tq + lax.broadcasted_iota(jnp.int32, (RB, tq), 1)
    qcol = jnp.concatenate([qcol, qcol], axis=1)
    krow = lax.broadcasted_iota(jnp.int32, (RB, 2 * tq), 0)

    def qk(c, half):
        off = pl.multiple_of(c * KC + half * HALF, HALF)
        return [lax.dot_general(k_ref[pl.ds(off, HALF), hh * LANES:(hh + 1) * LANES],
                                q_ref[:, hh * LANES:(hh + 1) * LANES], nt,
                                preferred_element_type=jnp.float32) for hh in range(2)]

    def mask(t, c, half, r0):
        return jnp.where(krow + (c * KC + half * HALF + r0) <= qcol, t, NEG_BIG)

    def pv(c, half, p_ref):
        vt = vt_ref[0, 0, c, :, half * HALF:(half + 1) * HALF]
        return jnp.concatenate(
            [jnp.dot(vt, p_ref[:, hh * tq:(hh + 1) * tq], preferred_element_type=jnp.float32)
             for hh in range(2)], axis=1)

    _attend(n_chunks, qk, mask, pv, sa_scr, sb_scr, pa_scr, pb_scr, acc_scr, 2 * FOX_DIM)
    acc = acc_scr[...]
    den = acc[2 * FOX_DIM:2 * FOX_DIM + 1, :]
    out_t = jnp.concatenate([acc[:FOX_DIM, :tq] / den[:, :tq],
                             acc[FOX_DIM:2 * FOX_DIM, tq:] / den[:, tq:]], axis=0)
    for j in range(tq // LANES):
        o_ref[j * LANES:(j + 1) * LANES, :] = out_t[:, j * LANES:(j + 1) * LANES].T.astype(o_ref.dtype)


def _fox(fq, fk, fvt, batch, seq):
    tq = min(TQ_FOX, seq)
    nq = seq // tq
    width = 2 * tq
    return pl.pallas_call(
        _fox_kernel, grid=(batch, FOX_HEADS // 2, nq),
        in_specs=[pl.BlockSpec((tq, 2 * LANES), lambda b, hp, qi: (b * nq + qi, hp)),
                  pl.BlockSpec((seq, 2 * LANES), lambda b, hp, qi: (b, hp)),
                  pl.BlockSpec((1, 1) + fvt.shape[2:], lambda b, hp, qi: (b, hp, 0, 0, 0))],
        out_specs=pl.BlockSpec((tq, LANES), lambda b, hp, qi: (b * nq + qi, hp)),
        out_shape=jax.ShapeDtypeStruct((batch * seq, FOX_W), jnp.bfloat16),
        scratch_shapes=[pltpu.VMEM((HALF, width), jnp.float32), pltpu.VMEM((HALF, width), jnp.float32),
                        pltpu.VMEM((HALF, width), jnp.bfloat16), pltpu.VMEM((HALF, width), jnp.bfloat16),
                        pltpu.VMEM((FOX_VROWS, width), jnp.float32)],
        compiler_params=_cparams(("parallel", "parallel", "arbitrary")), name="fox",
    )(fq, fk, fvt)


def _dsa_kernel(bound_ref, k_ref, vt_ref, qt_ref, iqt_ref, iw_ref, ltri_ref, o_ref,
                key_scr, bias_scr, sa_scr, sb_scr, pa_scr, pb_scr, acc_scr, *, topk):
    tq = TQ_DSA
    nh = DSA_HEADS
    selected_bias = -bound_ref[0, 0]
    qi = pl.program_id(1)
    n_chunks = (qi * tq + tq + KC - 1) // KC

    qpos = qi * tq + lax.broadcasted_iota(jnp.int32, (KC, tq), 1)
    krow = lax.broadcasted_iota(jnp.int32, (KC, tq), 0)

    def score_chunk(c, carry):
        off = pl.multiple_of(c * KC, KC)
        rel = jnp.dot(k_ref[pl.ds(off, KC), :], iqt_ref[0], preferred_element_type=jnp.float32)
        score = jnp.maximum(rel[:, :tq], 0.0) * iw_ref[0, 0:1, :]
        for hd in range(1, IDX_HEADS):
            score = score + jnp.maximum(rel[:, hd * tq:(hd + 1) * tq], 0.0) * iw_ref[0, hd:hd + 1, :]
        score = jnp.where(score == 0.0, 0.0, score)
        score = jnp.where(krow + off <= qpos, score, -jnp.inf)
        bits = pltpu.bitcast(score, jnp.int32)
        key_scr[pl.ds(off, KC), :] = bits ^ ((bits >> 31) & jnp.int32(0x7FFFFFFF))
        return carry

    lax.fori_loop(0, n_chunks, score_chunk, 0)

    def count_ge(thr):
        def body(c, acc):
            off = pl.multiple_of(c * KC, KC)
            hit = jnp.where(key_scr[pl.ds(off, KC), :] >= thr, 1, 0).astype(jnp.int32)
            return acc + _col_reduce(hit, jnp.sum)
        acc = lax.fori_loop(0, n_chunks, body, jnp.zeros((SUBLANES, tq), jnp.int32))
        return jnp.sum(acc, axis=0, keepdims=True)

    def bit_step(b, thr):
        bit = jnp.left_shift(jnp.int32(1), 31 - b)
        cand = jnp.where(b == 0, jnp.zeros_like(thr), thr | bit)
        return jnp.where(count_ge(cand) >= topk, cand, thr)

    thr = lax.fori_loop(0, 32, bit_step, jnp.full((1, tq), -2 ** 31, jnp.int32))
    has_ties = jnp.max(count_ge(thr)) > topk

    @pl.when(jnp.logical_not(has_ties))
    def _():
        def body(c, carry):
            off = pl.multiple_of(c * KC, KC)
            bias_scr[pl.ds(off, KC), :] = jnp.where(key_scr[pl.ds(off, KC), :] >= thr, selected_bias, NEG_BIG)
            return carry
        lax.fori_loop(0, n_chunks, body, 0)

    @pl.when(has_ties)
    def _():
        int_max = jnp.int32(2 ** 31 - 1)
        n_gt = jnp.where(thr == int_max, 0, count_ge(jnp.where(thr == int_max, thr, thr + 1)))
        need = (topk - n_gt).astype(jnp.float32)

        def body(c, seen):
            off = pl.multiple_of(c * KC, KC)
            ks = key_scr[pl.ds(off, KC), :]
            eq = ks == thr
            eqf = jnp.where(eq, 1.0, 0.0)
            before = jnp.dot(ltri_ref[...], eqf.astype(jnp.bfloat16),
                             preferred_element_type=jnp.float32) + seen
            sel = jnp.logical_or(ks > thr, jnp.logical_and(eq, before < need))
            sel = jnp.logical_and(sel, krow + off <= qpos)
            bias_scr[pl.ds(off, KC), :] = jnp.where(sel, selected_bias, NEG_BIG)
            return seen + jnp.sum(eqf, axis=0, keepdims=True)
        lax.fori_loop(0, n_chunks, body, jnp.zeros((1, tq), jnp.float32))

    def qk(c, half):
        off = pl.multiple_of(c * KC + half * HALF, HALF)
        return [jnp.dot(k_ref[pl.ds(off, HALF), :], qt_ref[0], preferred_element_type=jnp.float32)]

    def mask(t, c, half, r0):
        b = bias_scr[pl.ds(pl.multiple_of(c * KC + half * HALF + r0, RB), RB), :]
        return t + jnp.concatenate([b] * nh, axis=1)

    def pv(c, half, p_ref):
        return jnp.dot(vt_ref[0, c, :, half * HALF:(half + 1) * HALF], p_ref[...],
                       preferred_element_type=jnp.float32)

    _attend(n_chunks, qk, mask, pv, sa_scr, sb_scr, pa_scr, pb_scr, acc_scr, DSA_DIM)
    acc = acc_scr[...]
    out_t = acc[:DSA_DIM, :] / acc[DSA_DIM:DSA_DIM + 1, :]
    for hp in range(nh // 2):
        pair = jnp.concatenate([out_t[:, 2 * hp * tq:(2 * hp + 1) * tq],
                                out_t[:, (2 * hp + 1) * tq:(2 * hp + 2) * tq]], axis=0)
        o_ref[:, hp * LANES:(hp + 1) * LANES] = pair.T.astype(o_ref.dtype)


def _dsa(bound, dk, vt, qt, iqt, iwt, ltri_strict, batch, seq):
    nq = seq // TQ_DSA
    assert vt.shape[3] == KC and ltri_strict.shape[0] == KC and seq % KC == 0
    topk = min(DSA_MAX_TOPK, seq // 4)
    width = DSA_HEADS * TQ_DSA
    return pl.pallas_call(
        functools.partial(_dsa_kernel, topk=topk),
        grid=(batch, nq),
        in_specs=[pl.BlockSpec(memory_space=pltpu.SMEM),
                  pl.BlockSpec((seq, LANES), lambda b, qi: (b, 0)),
                  pl.BlockSpec((1,) + vt.shape[1:], lambda b, qi: (b, 0, 0, 0)),
                  pl.BlockSpec((1, LANES, width), lambda b, qi: (b * nq + qi, 0, 0)),
                  pl.BlockSpec((1, LANES, IDX_HEADS * TQ_DSA), lambda b, qi: (b * nq + qi, 0, 0)),
                  pl.BlockSpec((1, IDX_HEADS, TQ_DSA), lambda b, qi: (b, 0, qi)),
                  pl.BlockSpec(ltri_strict.shape, lambda b, qi: (0, 0))],
        out_specs=pl.BlockSpec((TQ_DSA, DSA_W), lambda b, qi: (b * nq + qi, 0)),
        out_shape=jax.ShapeDtypeStruct((batch * seq, DSA_W), jnp.bfloat16),
        scratch_shapes=[pltpu.VMEM((seq, TQ_DSA), jnp.int32),
                        pltpu.VMEM((seq, TQ_DSA), jnp.float32),
                        pltpu.VMEM((HALF, width), jnp.float32), pltpu.VMEM((HALF, width), jnp.float32),
                        pltpu.VMEM((HALF, width), jnp.bfloat16), pltpu.VMEM((HALF, width), jnp.bfloat16),
                        pltpu.VMEM((DSA_VROWS, width), jnp.float32)],
        compiler_params=_cparams(("parallel", "arbitrary")), name="dsa",
    )(bound, dk, vt, qt, iqt, iwt, ltri_strict)


def _ret_kernel(q_ref, k_ref, kt_ref, v_ref, g_ref, din_ref, dq_ref, dk_ref, dc_ref, gain_ref,
                o_ref, state_scr):
    t = pl.program_id(1)

    @pl.when(t == 0)
    def _():
        state_scr[...] = jnp.zeros_like(state_scr)

    for hd in range(RET_HEADS):
        q = q_ref[:, hd * RET_QK_DIM:(hd + 1) * RET_QK_DIM]
        k = k_ref[:, hd * RET_QK_DIM:(hd + 1) * RET_QK_DIM]
        v = v_ref[:, hd * RET_V_DIM:(hd + 1) * RET_V_DIM]
        state = state_scr[hd]
        attn = lax.dot_general(q, k, (((1,), (1,)), ((), ())),
                               preferred_element_type=jnp.float32) * din_ref[hd]
        inner = jnp.dot(attn.astype(jnp.bfloat16), v, preferred_element_type=jnp.float32)
        cross = jnp.dot(q, state.astype(jnp.bfloat16),
                        preferred_element_type=jnp.float32) * dq_ref[hd]
        ktd = (kt_ref[0, hd * RET_QK_DIM:(hd + 1) * RET_QK_DIM, :] * dk_ref[hd]).astype(jnp.bfloat16)
        state_scr[hd] = dc_ref[hd] * state + jnp.dot(ktd, v, preferred_element_type=jnp.float32)
        y = inner + cross
        yn = y * lax.rsqrt(jnp.mean(y * y, axis=-1, keepdims=True) + NORM_EPS) * gain_ref[hd]
        gate = g_ref[:, hd * RET_V_DIM:(hd + 1) * RET_V_DIM]
        o_ref[:, hd * RET_V_DIM:(hd + 1) * RET_V_DIM] = (
            yn * (gate * jax.nn.sigmoid(gate))).astype(o_ref.dtype)


def _ret(rq, rk, rkt, rv, rg, consts, gain, batch, seq):
    c = RET_CHUNK
    n = seq // c

    def tok(width):
        return pl.BlockSpec((c, width), lambda b, t: (b * n + t, 0))

    def full(a):
        return pl.BlockSpec(a.shape, lambda b, t: (0,) * a.ndim)

    return pl.pallas_call(
        _ret_kernel, grid=(batch, n),
        in_specs=[tok(RET_QK_W), tok(RET_QK_W),
                  pl.BlockSpec((1, RET_QK_W, c), lambda b, t: (b, 0, t)),
                  tok(RET_V_W), tok(RET_V_W),
                  full(consts["ret_din"]), full(consts["ret_dq"]), full(consts["ret_dk"]),
                  full(consts["ret_dc"]), full(gain)],
        out_specs=tok(RET_V_W),
        out_shape=jax.ShapeDtypeStruct((batch * seq, RET_V_W), jnp.bfloat16),
        scratch_shapes=[pltpu.VMEM((RET_HEADS, RET_QK_DIM, RET_V_DIM), jnp.float32)],
        compiler_params=_cparams(("parallel", "arbitrary")), name="ret",
    )(rq, rk, rkt, rv, rg, consts["ret_din"], consts["ret_dq"], consts["ret_dk"], consts["ret_dc"], gain)


def _merge_kernel(x_ref, g_ref, ya_ref, yb_ref, yc_ref, wzg_ref, wa_ref, wb_ref, wc_ref, wo_ref, o_ref):
    x = x_ref[...]
    ms = jnp.mean(x * x, axis=-1, keepdims=True)
    h = (x * lax.rsqrt(ms + NORM_EPS) * g_ref[...]).astype(jnp.bfloat16)
    merged = None
    for j, (y_ref, w_ref) in enumerate(((ya_ref, wa_ref), (yb_ref, wb_ref), (yc_ref, wc_ref))):
        gate = jax.nn.sigmoid(jnp.dot(h, wzg_ref[:, j * D_MODEL:(j + 1) * D_MODEL],
                                      preferred_element_type=jnp.float32))
        term = gate * jnp.dot(y_ref[...], w_ref[...], preferred_element_type=jnp.float32)
        merged = term if merged is None else merged + term
    o_ref[...] = x + jnp.dot(merged.astype(jnp.bfloat16), wo_ref[...],
                             preferred_element_type=jnp.float32)


def _ffn_kernel(x_ref, g_ref, wg_ref, wu_ref, wd_ref, o_ref, *, chunk):
    x = x_ref[...]
    ms = jnp.mean(x * x, axis=-1, keepdims=True)
    h = (x * lax.rsqrt(ms + NORM_EPS) * g_ref[...]).astype(jnp.bfloat16)
    acc = x
    for lo in range(0, FFN_HIDDEN, chunk):
        gt = jnp.dot(h, wg_ref[:, lo:lo + chunk], preferred_element_type=jnp.float32)
        up = jnp.dot(h, wu_ref[:, lo:lo + chunk], preferred_element_type=jnp.float32)
        act = (gt * jax.nn.sigmoid(gt) * up).astype(jnp.bfloat16)
        acc = acc + jnp.dot(act, wd_ref[lo:lo + chunk, :], preferred_element_type=jnp.float32)
    o_ref[...] = acc


def _row_call(kernel, name, order):
    n = order[0][1].shape[0]
    tm = min(TM_POST, n)
    arrays, specs = [], []
    for kind, a in order:
        arrays.append(a)
        if kind == "row":
            specs.append(pl.BlockSpec((tm, a.shape[1]), lambda i: (i, 0)))
        else:
            specs.append(pl.BlockSpec(a.shape, lambda i, nd=a.ndim: (0,) * nd))
    return pl.pallas_call(
        kernel, grid=(n // tm,), in_specs=specs,
        out_specs=pl.BlockSpec((tm, D_MODEL), lambda i: (i, 0)),
        out_shape=jax.ShapeDtypeStruct((n, D_MODEL), jnp.float32),
        compiler_params=_cparams(("parallel",)), name=name,
    )(*arrays)


def _merge(x2d, ya, yb, yc, lw):
    order = [("row", x2d), ("full", lw["ln1_g"]), ("row", ya), ("row", yb), ("row", yc),
             ("full", lw["w_zg"]), ("full", lw["w_a"]), ("full", lw["w_b"]), ("full", lw["w_c"]),
             ("full", lw["w_o"])]
    return _row_call(_merge_kernel, "merge", order)


def _ffn(x2d, lw):
    order = [("row", x2d), ("full", lw["ln2_g"]), ("full", lw["w_g"]), ("full", lw["w_u"]),
             ("full", lw["w_d"])]
    return _row_call(functools.partial(_ffn_kernel, chunk=256), "ffn", order)


def _rope_cs(seq, dim):
    half = dim // 2
    inv_freq = ROPE_THETA ** (-jnp.arange(half, dtype=jnp.float32) / half)
    ang = jnp.arange(seq, dtype=jnp.float32)[:, None] * inv_freq[None, :]
    return jnp.cos(ang), jnp.sin(ang)


def _constants(seq):
    tm = min(TM_IN, seq)
    bf = jnp.bfloat16
    cos64, sin64 = _rope_cs(seq, 64)
    cos32, sin32 = _rope_cs(seq, 32)
    z32 = jnp.zeros((seq, 32), jnp.float32)
    c64 = jnp.concatenate([cos64, cos64], axis=1)
    s64 = jnp.concatenate([-sin64, sin64], axis=1)
    c32 = jnp.concatenate([cos32, cos32], axis=1)
    s32 = jnp.concatenate([-sin32, sin32], axis=1)
    consts = {
        "kc": jnp.concatenate([c64, c32, z32], axis=1), "ks": jnp.concatenate([s64, s32, z32], axis=1),
        "rc": jnp.concatenate([c64, c64], axis=1), "rs": jnp.concatenate([s64, s64], axis=1),
        "c64t": c64.T, "s64t": s64.T,
        "iqa": c32.T, "iqb": s32.T, "rka": c64.T, "rkb": s64.T,
        "ltri": jnp.tril(jnp.ones((tm, tm), jnp.float32)).astype(bf),
        "ltri_strict": jnp.tril(jnp.ones((KC, KC), jnp.float32), -1).astype(bf),
    }
    eq = np.zeros((3 * LANES, FOX_HEADS * LANES), np.float32)
    ek = np.zeros((3 * LANES, FOX_HEADS * LANES), np.float32)
    oneq = np.zeros((1, FOX_HEADS * LANES), np.float32)
    onek = np.zeros((1, FOX_HEADS * LANES), np.float32)
    for hd in range(FOX_HEADS):
        base = hd * LANES + FOX_DIM
        for part in range(3):
            eq[part * LANES + hd, base + part] = 1.0
            ek[part * LANES + hd, base + 3 + part] = -1.0
            oneq[0, base + 3 + part] = 1.0
            onek[0, base + part] = 1.0
    consts.update(eq=jnp.asarray(eq, bf), ek=jnp.asarray(ek, bf), oneq=jnp.asarray(oneq), onek=jnp.asarray(onek))
    log_g = jnp.log1p(-(2.0 ** (-5.0 - jnp.arange(RET_HEADS, dtype=jnp.float32))))
    pos = jnp.arange(RET_CHUNK, dtype=jnp.float32)
    diff = pos[:, None] - pos[None, :]
    din = jnp.where(diff >= 0, jnp.exp(jnp.maximum(diff, 0.0)[None] * log_g[:, None, None]), 0.0)
    dq = jnp.exp((pos + 1.0)[None] * log_g[:, None])
    dk = jnp.exp((RET_CHUNK - 1.0 - pos)[None] * log_g[:, None])
    dc = jnp.exp(RET_CHUNK * log_g)
    consts.update(ret_din=din, ret_dq=dq[:, :, None], ret_dk=dk[:, None, :],
                  ret_dc=jnp.broadcast_to(dc[:, None, None], (RET_HEADS, 1, LANES)))
    return consts


def _pad_heads(w, heads, dim):
    k = w.shape[0]
    w3 = w.reshape(k, heads, dim)
    return jnp.pad(w3, ((0, 0), (0, 0), (0, LANES - dim))).reshape(k, heads * LANES)


def _layer_weights(p, consts):
    bf = jnp.bfloat16
    w_in = p["w_in"]

    def cols(off, size):
        return w_in[:, off:off + size]

    zeros = lambda n: jnp.zeros((D_MODEL, n), jnp.float32)
    w_tm = jnp.concatenate([
        _pad_heads(cols(O_FQ, FOX_W), FOX_HEADS, FOX_DIM),
        _pad_heads(cols(O_FK, FOX_W), FOX_HEADS, FOX_DIM),
        cols(O_FF, FOX_HEADS), zeros(LANES - FOX_HEADS),
        cols(O_DK, DSA_DIM), cols(O_IK, IDX_DIM), zeros(LANES - DSA_DIM - IDX_DIM),
        cols(O_RQ, RET_QK_W), cols(O_RK, RET_QK_W), cols(O_RV, RET_V_W), cols(O_RG, RET_V_W),
    ], axis=1).astype(bf)
    w_fm = jnp.concatenate([
        cols(O_DQ, DSA_W), cols(O_IQ, IDX_W), cols(O_DV, DSA_DIM),
        cols(O_IW, IDX_HEADS), zeros(16 - IDX_HEADS), cols(O_RK, RET_QK_W), cols(O_FV, FOX_W),
    ], axis=1).T.astype(bf)

    def lane_pad(v, fill=0.0):
        return jnp.concatenate([v, jnp.full((LANES - v.shape[0],), fill, jnp.float32)])[None, :]

    g = p["dsa_q_norm"]
    g_sw = jnp.concatenate([g[DSA_DIM // 2:], g[:DSA_DIM // 2]])
    scale = DSA_DIM ** -0.5 * LOG2E

    def logit_bound(gq, gk, dim):
        b = BOUND_SLACK * dim * jnp.max(jnp.abs(gq)) * jnp.max(jnp.abs(gk)) * (dim ** -0.5 * LOG2E)
        return b.reshape(1, 1).astype(jnp.float32)

    return {
        "fox_bound": logit_bound(p["fox_q_norm"], p["fox_k_norm"], FOX_DIM),
        "dsa_bound": logit_bound(p["dsa_q_norm"], p["dsa_k_norm"], DSA_DIM),
        "ln1_g": p["ln1_g"][None, :], "ln2_g": p["ln2_g"][None, :],
        "w_tm": w_tm, "w_fm": w_fm,
        "fox_b": lane_pad(p["fox_b_f"]),
        "fq_gain": lane_pad(p["fox_q_norm"]), "fk_gain": lane_pad(p["fox_k_norm"]),
        "dk_gain": jnp.concatenate([p["dsa_k_norm"], jnp.ones((IDX_DIM,), jnp.float32),
                                    jnp.zeros((LANES - DSA_DIM - IDX_DIM,), jnp.float32)])[None, :],
        "dqa": consts["c64t"] * (g * scale)[:, None], "dqb": consts["s64t"] * (g_sw * scale)[:, None],
        "ret_gain": p["ret_out_norm"][:, None, :],
        "w_zg": cols(O_ZG, N_BRANCH * D_MODEL).astype(bf),
        "w_a": p["w_fox_out"].astype(bf), "w_b": p["w_dsa_out"].astype(bf), "w_c": p["w_ret_out"].astype(bf),
        "w_o": p["w_o"].astype(bf),
        "w_g": p["w_ffn_in"][:, :FFN_HIDDEN].astype(bf), "w_u": p["w_ffn_in"][:, FFN_HIDDEN:].astype(bf),
        "w_d": p["w_ffn_out"].astype(bf),
    }


def _layer(x2d, lw, consts, batch, seq):
    (fq, fk, fvt, dk, rq, rk, rv, rg, qt, iqt, vt, iwt, rkt) = _inproj(x2d, lw, consts, batch, seq)
    ya = _fox(fq, fk, fvt, batch, seq)
    yb = _dsa(lw["dsa_bound"], dk, vt, qt, iqt, iwt, consts["ltri_strict"], batch, seq)
    yc = _ret(rq, rk, rkt, rv, rg, consts, lw["ret_gain"], batch, seq)
    x2d = _merge(x2d, ya, yb, yc, lw)
    return _ffn(x2d, lw)


def kernel(x, ln1_g, w_in, fox_b_f, fox_q_norm, fox_k_norm, dsa_q_norm, dsa_k_norm, ret_out_norm,
           w_fox_out, w_dsa_out, w_ret_out, w_o, ln2_g, w_ffn_in, w_ffn_out):
    batch, seq, _ = x.shape
    depth = w_in.shape[0]
    consts = _constants(seq)
    params = dict(ln1_g=ln1_g, w_in=w_in, fox_b_f=fox_b_f, fox_q_norm=fox_q_norm, fox_k_norm=fox_k_norm,
                  dsa_q_norm=dsa_q_norm, dsa_k_norm=dsa_k_norm, ret_out_norm=ret_out_norm,
                  w_fox_out=w_fox_out, w_dsa_out=w_dsa_out, w_ret_out=w_ret_out, w_o=w_o,
                  ln2_g=ln2_g, w_ffn_in=w_ffn_in, w_ffn_out=w_ffn_out)
    x2d = x.reshape(batch * seq, D_MODEL)
    for layer in range(depth):
        lw = _layer_weights({k: v[layer] for k, v in params.items()}, consts)
        x2d = _layer(x2d, lw, consts, batch, seq)
    return x2d.reshape(batch, seq, D_MODEL)
```

```python
import functools
import math

import jax
import jax.numpy as jnp
import numpy as np
from jax import lax
from jax.experimental import pallas as pl
from jax.experimental.pallas import tpu as pltpu

D_MODEL = 1024
FOX_HEADS = 8
FOX_DIM = 64
DSA_HEADS = 8
DSA_DIM = 64
IDX_HEADS = 8
IDX_DIM = 32
DSA_MAX_TOPK = 256
RET_HEADS = 4
RET_QK_DIM = 64
RET_V_DIM = 128
RET_CHUNK = 128
FFN_HIDDEN = 2816
ROPE_THETA = 10000.0
NORM_EPS = 1e-6
N_BRANCH = 3

FOX_W = FOX_HEADS * FOX_DIM
DSA_W = DSA_HEADS * DSA_DIM
IDX_W = IDX_HEADS * IDX_DIM
RET_QK_W = RET_HEADS * RET_QK_DIM
RET_V_W = RET_HEADS * RET_V_DIM
IN_SIZES = (FOX_W, FOX_W, FOX_W, FOX_HEADS,
            DSA_W, DSA_DIM, DSA_DIM, IDX_W, IDX_DIM, IDX_HEADS,
            RET_QK_W, RET_QK_W, RET_V_W, RET_V_W,
            N_BRANCH * D_MODEL)
IN_OFFS = tuple(int(v) for v in np.cumsum((0,) + IN_SIZES))
(O_FQ, O_FK, O_FV, O_FF, O_DQ, O_DK, O_DV, O_IQ, O_IK, O_IW,
 O_RQ, O_RK, O_RV, O_RG, O_ZG, _) = IN_OFFS

LANES = 128
SUBLANES = 8
BF16_ROWS = 16
VMEM_LIMIT = 56 * 1024 * 1024
NEG_BIG = -1e30
LOG2E = math.log2(math.e)
BOUND_SLACK = 1.02
MIN_DENOM = 2.0 ** -100

T_FQ = 0
T_FK = T_FQ + FOX_HEADS * LANES
T_FF = T_FK + FOX_HEADS * LANES
T_DK = T_FF + LANES
T_RQ = T_DK + LANES
T_RK = T_RQ + RET_QK_W
T_RV = T_RK + RET_QK_W
T_RG = T_RV + RET_V_W
T_COLS = T_RG + RET_V_W
F_DQ = 0
F_IQ = F_DQ + DSA_W
F_DV = F_IQ + IDX_W
F_IW = F_DV + DSA_DIM
F_RK = F_IW + 16
F_FV = F_RK + RET_QK_W
F_ROWS = F_FV + FOX_W

TM_IN = 512
KC = TM_IN
HALF = KC // 2
RB = 32
TQ_FOX = 512
FOX_VROWS = 2 * FOX_DIM + BF16_ROWS
DSA_VROWS = DSA_DIM + BF16_ROWS
TQ_DSA = 256
TM_POST = 512


def _cparams(sem):
    return pltpu.CompilerParams(dimension_semantics=sem, vmem_limit_bytes=VMEM_LIMIT)


def _split3(v):
    hi = v.astype(jnp.bfloat16)
    r1 = v - hi.astype(jnp.float32)
    mid = r1.astype(jnp.bfloat16)
    lo = (r1 - mid.astype(jnp.float32)).astype(jnp.bfloat16)
    return hi, mid, lo


def _col_reduce(v, op):
    return op(v.reshape(v.shape[0] // SUBLANES, SUBLANES, v.shape[1]), axis=0)


def _attend(n_chunks, qk, mask, pv, sa_scr, sb_scr, pa_scr, pb_scr, acc_scr, den_row):
    def store_logits(s_ref, c, half):
        lo = 0
        for g in qk(c, half):
            s_ref[:, lo:lo + g.shape[1]] = g
            lo += g.shape[1]

    def probs(s_ref, c, half, p_ref, shift):
        for r0 in range(0, s_ref.shape[0], RB):
            t = mask(s_ref[r0:r0 + RB, :], c, half, r0)
            if shift is not None:
                t = t - shift
            p_ref[r0:r0 + RB, :] = jnp.exp2(t).astype(p_ref.dtype)

    def run(shift):
        acc_scr[...] = jnp.zeros_like(acc_scr)
        pb_scr[...] = jnp.zeros_like(pb_scr)
        store_logits(sa_scr, 0, 0)

        def chunk(c, carry):
            store_logits(sb_scr, c, 1)
            acc_scr[...] += pv(jnp.maximum(c - 1, 0), 1, pb_scr)
            probs(sa_scr, c, 0, pa_scr, shift)
            store_logits(sa_scr, jnp.minimum(c + 1, n_chunks - 1), 0)
            acc_scr[...] += pv(c, 0, pa_scr)
            probs(sb_scr, c, 1, pb_scr, shift)
            return carry

        lax.fori_loop(0, n_chunks, chunk, 0)
        acc_scr[...] += pv(n_chunks - 1, 1, pb_scr)

    run(None)

    @pl.when(jnp.logical_not(jnp.min(acc_scr[den_row:den_row + 1, :]) >= MIN_DENOM))
    def _():
        def col_max(c, mx):
            for half in range(2):
                store_logits(sa_scr, c, half)
                for r0 in range(0, sa_scr.shape[0], RB):
                    mx = jnp.maximum(mx, _col_reduce(mask(sa_scr[r0:r0 + RB, :], c, half, r0), jnp.max))
            return mx
        mx = lax.fori_loop(0, n_chunks, col_max,
                           jnp.full((SUBLANES, acc_scr.shape[1]), NEG_BIG, jnp.float32))
        run(jnp.max(mx, axis=0, keepdims=True))


def _inproj_kernel(bound_ref, x_ref, g_ref, wtm_ref, wfm_ref, fb_ref, fqg_ref, fkg_ref, dkg_ref,
                   ltri_ref, eq_ref, ek_ref, oneq_ref, onek_ref,
                   kc_ref, ks_ref, rc_ref, rs_ref,
                   dqa_ref, dqb_ref, iqa_ref, iqb_ref, rka_ref, rkb_ref,
                   fq_out, fk_out, fvt_out, dk_out, rq_out, rk_out, rv_out, rg_out,
                   qt_out, iqt_out, vt_out, iw_out, rkt_out,
                   carry_ref, *, tiles_per_seq):
    tm = x_ref.shape[0]
    i = pl.program_id(0)

    @pl.when(i % tiles_per_seq == 0)
    def _():
        carry_ref[...] = jnp.zeros_like(carry_ref)

    x = x_ref[...]
    ms = jnp.mean(x * x, axis=-1, keepdims=True)
    h = (x * lax.rsqrt(ms + NORM_EPS) * g_ref[...]).astype(jnp.bfloat16)

    def tm_dot(lo, width):
        return jnp.dot(h, wtm_ref[:, lo:lo + width], preferred_element_type=jnp.float32)

    lane = lax.broadcasted_iota(jnp.int32, (tm, LANES), 1)

    ffb = tm_dot(T_FF, LANES) + fb_ref[...]
    lf = (jnp.minimum(ffb, 0.0) - jnp.log1p(jnp.exp(-jnp.abs(ffb)))) * LOG2E
    parts = jnp.concatenate(_split3(lf), axis=1)
    cs = jnp.dot(ltri_ref[...], parts, preferred_element_type=jnp.float32)
    c = cs[:, :LANES] + cs[:, LANES:2 * LANES] + cs[:, 2 * LANES:] + carry_ref[...]
    carry_ref[...] = c[tm - 1:tm, :]
    cparts_q = jnp.concatenate(_split3(c - bound_ref[0, 0]), axis=1)
    cparts_k = jnp.concatenate(_split3(c), axis=1)
    scat_q = jnp.dot(cparts_q, eq_ref[...], preferred_element_type=jnp.float32) + oneq_ref[...]
    scat_k = jnp.dot(cparts_k, ek_ref[...], preferred_element_type=jnp.float32) + onek_ref[...]

    for (lo, gain_ref, scat, out, scale) in ((T_FQ, fqg_ref, scat_q, fq_out, FOX_DIM ** -0.5 * LOG2E),
                                             (T_FK, fkg_ref, scat_k, fk_out, 1.0)):
        z = tm_dot(lo, FOX_HEADS * LANES)
        for hd in range(FOX_HEADS):
            blk = z[:, hd * LANES:(hd + 1) * LANES]
            ss = jnp.sum(blk * blk, axis=-1, keepdims=True) * (1.0 / FOX_DIM)
            nb = blk * lax.rsqrt(ss + NORM_EPS) * (gain_ref[...] * scale)
            out[:, hd * LANES:(hd + 1) * LANES] = (
                nb + scat[:, hd * LANES:(hd + 1) * LANES]).astype(out.dtype)

    zk = tm_dot(T_DK, LANES)
    ssk = jnp.sum(jnp.where(lane < DSA_DIM, zk * zk, 0.0), axis=-1, keepdims=True) * (1.0 / DSA_DIM)
    nk = zk * jnp.where(lane < DSA_DIM, lax.rsqrt(ssk + NORM_EPS), 1.0) * dkg_ref[...]
    partner = jnp.where(
        lane < 32, pltpu.roll(nk, LANES - 32, 1),
        jnp.where(lane < 64, pltpu.roll(nk, 32, 1),
                  jnp.where(lane < 80, pltpu.roll(nk, LANES - 16, 1), pltpu.roll(nk, 16, 1))))
    dk_out[...] = (nk * kc_ref[...] + partner * ks_ref[...]).astype(dk_out.dtype)

    first_half = (lane % RET_QK_DIM) < (RET_QK_DIM // 2)
    for (lo, out, scale) in ((T_RQ, rq_out, RET_QK_DIM ** -0.5), (T_RK, rk_out, 1.0)):
        z = tm_dot(lo, RET_QK_W)
        for j in range(RET_QK_W // LANES):
            blk = z[:, j * LANES:(j + 1) * LANES]
            pr = jnp.where(first_half, pltpu.roll(blk, LANES - 32, 1), pltpu.roll(blk, 32, 1))
            out[:, j * LANES:(j + 1) * LANES] = (
                (blk * rc_ref[...] + pr * rs_ref[...]) * scale).astype(out.dtype)

    rv_out[...] = tm_dot(T_RV, RET_V_W).astype(rv_out.dtype)
    rg_out[...] = tm_dot(T_RG, RET_V_W)

    zt = lax.dot_general(wfm_ref[...], h, (((1,), (1,)), ((), ())),
                         preferred_element_type=jnp.float32)
    nq = tm // TQ_DSA

    def swap_halves(v):
        half = v.shape[0] // 2
        return jnp.concatenate([v[half:], v[:half]], axis=0)

    def ones_row_block(rows, dtype):
        first = lax.broadcasted_iota(jnp.int32, (rows, tm), 0) == 0
        return jnp.where(first, 1.0, 0.0).astype(dtype)

    zeros_q = jnp.zeros((LANES - DSA_DIM, DSA_HEADS * TQ_DSA), qt_out.dtype)
    zeros_i0 = jnp.zeros((DSA_DIM, IDX_HEADS * TQ_DSA), iqt_out.dtype)
    zeros_i1 = jnp.zeros((LANES - DSA_DIM - IDX_DIM, IDX_HEADS * TQ_DSA), iqt_out.dtype)
    for j in range(nq):
        qt_out[j, DSA_DIM:, :] = zeros_q
        iqt_out[j, :DSA_DIM, :] = zeros_i0
        iqt_out[j, DSA_DIM + IDX_DIM:, :] = zeros_i1
    for hd in range(DSA_HEADS):
        xh = zt[F_DQ + hd * DSA_DIM:F_DQ + (hd + 1) * DSA_DIM, :]
        r = lax.rsqrt(jnp.sum(xh * xh, axis=0, keepdims=True) * (1.0 / DSA_DIM) + NORM_EPS)
        o = ((xh * dqa_ref[...] + swap_halves(xh) * dqb_ref[...]) * r).astype(qt_out.dtype)
        for j in range(nq):
            qt_out[j, :DSA_DIM, hd * TQ_DSA:(hd + 1) * TQ_DSA] = o[:, j * TQ_DSA:(j + 1) * TQ_DSA]
    for hd in range(IDX_HEADS):
        xh = zt[F_IQ + hd * IDX_DIM:F_IQ + (hd + 1) * IDX_DIM, :]
        o = (xh * iqa_ref[...] + swap_halves(xh) * iqb_ref[...]).astype(iqt_out.dtype)
        for j in range(nq):
            iqt_out[j, DSA_DIM:DSA_DIM + IDX_DIM, hd * TQ_DSA:(hd + 1) * TQ_DSA] = (
                o[:, j * TQ_DSA:(j + 1) * TQ_DSA])
    vt_out[0, 0, :DSA_DIM, :] = zt[F_DV:F_DV + DSA_DIM, :].astype(vt_out.dtype)
    vt_out[0, 0, DSA_DIM:, :] = ones_row_block(BF16_ROWS, vt_out.dtype)
    iw_out[0] = zt[F_IW:F_IW + IDX_HEADS, :] * ((IDX_DIM * IDX_HEADS) ** -0.5)
    for hd in range(RET_HEADS):
        xh = zt[F_RK + hd * RET_QK_DIM:F_RK + (hd + 1) * RET_QK_DIM, :]
        rkt_out[0, hd * RET_QK_DIM:(hd + 1) * RET_QK_DIM, :] = (
            xh * rka_ref[...] + swap_halves(xh) * rkb_ref[...])
    for hp in range(FOX_HEADS // 2):
        fvt_out[0, hp, 0, :2 * FOX_DIM, :] = (
            zt[F_FV + hp * 2 * FOX_DIM:F_FV + (hp + 1) * 2 * FOX_DIM, :].astype(fvt_out.dtype))
        fvt_out[0, hp, 0, 2 * FOX_DIM:, :] = ones_row_block(BF16_ROWS, fvt_out.dtype)


def _inproj(x2d, lw, consts, batch, seq):
    n = x2d.shape[0]
    tm = min(TM_IN, seq)
    tps = seq // tm
    nqt = tm // TQ_DSA
    grid = (n // tm,)
    bf = jnp.bfloat16

    def full(a):
        return pl.BlockSpec(a.shape, lambda i: (0,) * a.ndim)

    def tok(width):
        return pl.BlockSpec((tm, width), lambda i: (i, 0))

    def pos_tm(width):
        return pl.BlockSpec((tm, width), lambda i: (i % tps, 0))

    def pos_fm(rows):
        return pl.BlockSpec((rows, tm), lambda i: (0, i % tps))

    def fm_out(rows):
        return pl.BlockSpec((1, rows, tm), lambda i: (i // tps, 0, i % tps))

    in_arrays = [lw["fox_bound"], x2d, lw["ln1_g"], lw["w_tm"], lw["w_fm"], lw["fox_b"], lw["fq_gain"], lw["fk_gain"],
                 lw["dk_gain"], consts["ltri"], consts["eq"], consts["ek"], consts["oneq"], consts["onek"],
                 consts["kc"], consts["ks"], consts["rc"], consts["rs"],
                 lw["dqa"], lw["dqb"], consts["iqa"], consts["iqb"], consts["rka"], consts["rkb"]]
    in_specs = [pl.BlockSpec(memory_space=pltpu.SMEM),
                tok(D_MODEL), full(lw["ln1_g"]), full(lw["w_tm"]), full(lw["w_fm"]), full(lw["fox_b"]),
                full(lw["fq_gain"]), full(lw["fk_gain"]), full(lw["dk_gain"]),
                full(consts["ltri"]), full(consts["eq"]), full(consts["ek"]),
                full(consts["oneq"]), full(consts["onek"]),
                pos_tm(LANES), pos_tm(LANES), pos_tm(LANES), pos_tm(LANES),
                pos_fm(DSA_DIM), pos_fm(DSA_DIM), pos_fm(IDX_DIM), pos_fm(IDX_DIM),
                pos_fm(RET_QK_DIM), pos_fm(RET_QK_DIM)]
    out_shape = [
        jax.ShapeDtypeStruct((n, FOX_HEADS * LANES), bf),
        jax.ShapeDtypeStruct((n, FOX_HEADS * LANES), bf),
        jax.ShapeDtypeStruct((batch, FOX_HEADS // 2, tps, FOX_VROWS, tm), bf),
        jax.ShapeDtypeStruct((n, LANES), bf),
        jax.ShapeDtypeStruct((n, RET_QK_W), bf),
        jax.ShapeDtypeStruct((n, RET_QK_W), bf),
        jax.ShapeDtypeStruct((n, RET_V_W), bf),
        jax.ShapeDtypeStruct((n, RET_V_W), jnp.float32),
        jax.ShapeDtypeStruct((n // TQ_DSA, LANES, DSA_HEADS * TQ_DSA), bf),
        jax.ShapeDtypeStruct((n // TQ_DSA, LANES, IDX_HEADS * TQ_DSA), bf),
        jax.ShapeDtypeStruct((batch, tps, DSA_VROWS, tm), bf),
        jax.ShapeDtypeStruct((batch, IDX_HEADS, seq), jnp.float32),
        jax.ShapeDtypeStruct((batch, RET_QK_W, seq), jnp.float32),
    ]
    out_specs = [tok(FOX_HEADS * LANES), tok(FOX_HEADS * LANES),
                 pl.BlockSpec((1, FOX_HEADS // 2, 1, FOX_VROWS, tm), lambda i: (i // tps, 0, i % tps, 0, 0)),
                 tok(LANES), tok(RET_QK_W), tok(RET_QK_W), tok(RET_V_W), tok(RET_V_W),
                 pl.BlockSpec((nqt, LANES, DSA_HEADS * TQ_DSA), lambda i: (i, 0, 0)),
                 pl.BlockSpec((nqt, LANES, IDX_HEADS * TQ_DSA), lambda i: (i, 0, 0)),
                 pl.BlockSpec((1, 1, DSA_VROWS, tm), lambda i: (i // tps, i % tps, 0, 0)),
                 fm_out(IDX_HEADS), fm_out(RET_QK_W)]
    return pl.pallas_call(
        functools.partial(_inproj_kernel, tiles_per_seq=tps),
        grid=grid, in_specs=in_specs, out_specs=out_specs, out_shape=out_shape,
        scratch_shapes=[pltpu.VMEM((1, LANES), jnp.float32)],
        compiler_params=_cparams(("arbitrary",)), name="inproj",
    )(*in_arrays)


def _fox_kernel(q_ref, k_ref, vt_ref, o_ref, sa_scr, sb_scr, pa_scr, pb_scr, acc_scr):
    tq = q_ref.shape[0]
    qi = pl.program_id(2)
    n_chunks = (qi * tq + tq + KC - 1) // KC
    nt = (((1,), (1,)), ((), ()))

    qcol = qi * tq + lax.broadcasted_iota(jnp.int32, (RB, tq), 1)
    qcol = jnp.concatenate([qcol, qcol], axis=1)
    krow = lax.broadcasted_iota(jnp.int32, (RB, 2 * tq), 0)

    def qk(c, half):
        off = pl.multiple_of(c * KC + half * HALF, HALF)
        return [lax.dot_general(k_ref[pl.ds(off, HALF), hh * LANES:(hh + 1) * LANES],
                                q_ref[:, hh * LANES:(hh + 1) * LANES], nt,
                                preferred_element_type=jnp.float32) for hh in range(2)]

    def mask(t, c, half, r0):
        return jnp.where(krow + (c * KC + half * HALF + r0) <= qcol, t, NEG_BIG)

    def pv(c, half, p_ref):
        vt = vt_ref[0, 0, c, :, half * HALF:(half + 1) * HALF]
        return jnp.concatenate(
            [jnp.dot(vt, p_ref[:, hh * tq:(hh + 1) * tq], preferred_element_type=jnp.float32)
             for hh in range(2)], axis=1)

    _attend(n_chunks, qk, mask, pv, sa_scr, sb_scr, pa_scr, pb_scr, acc_scr, 2 * FOX_DIM)
    acc = acc_scr[...]
    den = acc[2 * FOX_DIM:2 * FOX_DIM + 1, :]
    out_t = jnp.concatenate([acc[:FOX_DIM, :tq] / den[:, :tq],
                             acc[FOX_DIM:2 * FOX_DIM, tq:] / den[:, tq:]], axis=0)
    for j in range(tq // LANES):
        o_ref[j * LANES:(j + 1) * LANES, :] = out_t[:, j * LANES:(j + 1) * LANES].T.astype(o_ref.dtype)


def _fox(fq, fk, fvt, batch, seq):
    tq = min(TQ_FOX, seq)
    nq = seq // tq
    width = 2 * tq
    return pl.pallas_call(
        _fox_kernel, grid=(batch, FOX_HEADS // 2, nq),
        in_specs=[pl.BlockSpec((tq, 2 * LANES), lambda b, hp, qi: (b * nq + qi, hp)),
                  pl.BlockSpec((seq, 2 * LANES), lambda b, hp, qi: (b, hp)),
                  pl.BlockSpec((1, 1) + fvt.shape[2:], lambda b, hp, qi: (b, hp, 0, 0, 0))],
        out_specs=pl.BlockSpec((tq, LANES), lambda b, hp, qi: (b * nq + qi, hp)),
        out_shape=jax.ShapeDtypeStruct((batch * seq, FOX_W), jnp.bfloat16),
        scratch_shapes=[pltpu.VMEM((HALF, width), jnp.float32), pltpu.VMEM((HALF, width), jnp.float32),
                        pltpu.VMEM((HALF, width), jnp.bfloat16), pltpu.VMEM((HALF, width), jnp.bfloat16),
                        pltpu.VMEM((FOX_VROWS, width), jnp.float32)],
        compiler_params=_cparams(("parallel", "parallel", "arbitrary")), name="fox",
    )(fq, fk, fvt)


def _dsa_kernel(bound_ref, k_ref, vt_ref, qt_ref, iqt_ref, iw_ref, ltri_ref, o_ref,
                key_scr, bias_scr, sa_scr, sb_scr, pa_scr, pb_scr, acc_scr, *, topk):
    tq = TQ_DSA
    nh = DSA_HEADS
    selected_bias = -bound_ref[0, 0]
    qi = pl.program_id(1)
    n_chunks = (qi * tq + tq + KC - 1) // KC

    qpos = qi * tq + lax.broadcasted_iota(jnp.int32, (KC, tq), 1)
    krow = lax.broadcasted_iota(jnp.int32, (KC, tq), 0)

    def score_chunk(c, carry):
        off = pl.multiple_of(c * KC, KC)
        rel = jnp.dot(k_ref[pl.ds(off, KC), :], iqt_ref[0], preferred_element_type=jnp.float32)
        score = jnp.maximum(rel[:, :tq], 0.0) * iw_ref[0, 0:1, :]
        for hd in range(1, IDX_HEADS):
            score = score + jnp.maximum(rel[:, hd * tq:(hd + 1) * tq], 0.0) * iw_ref[0, hd:hd + 1, :]
        score = jnp.where(score == 0.0, 0.0, score)
        score = jnp.where(krow + off <= qpos, score, -jnp.inf)
        bits = pltpu.bitcast(score, jnp.int32)
        key_scr[pl.ds(off, KC), :] = bits ^ ((bits >> 31) & jnp.int32(0x7FFFFFFF))
        return carry

    lax.fori_loop(0, n_chunks, score_chunk, 0)

    def count_ge(thr):
        def body(c, acc):
            off = pl.multiple_of(c * KC, KC)
            hit = jnp.where(key_scr[pl.ds(off, KC), :] >= thr, 1, 0).astype(jnp.int32)
            return acc + _col_reduce(hit, jnp.sum)
        acc = lax.fori_loop(0, n_chunks, body, jnp.zeros((SUBLANES, tq), jnp.int32))
        return jnp.sum(acc, axis=0, keepdims=True)

    def bit_step(b, thr):
        bit = jnp.left_shift(jnp.int32(1), 31 - b)
        cand = jnp.where(b == 0, jnp.zeros_like(thr), thr | bit)
        return jnp.where(count_ge(cand) >= topk, cand, thr)

    thr = lax.fori_loop(0, 32, bit_step, jnp.full((1, tq), -2 ** 31, jnp.int32))
    has_ties = jnp.max(count_ge(thr)) > topk

    @pl.when(jnp.logical_not(has_ties))
    def _():
        def body(c, carry):
            off = pl.multiple_of(c * KC, KC)
            bias_scr[pl.ds(off, KC), :] = jnp.where(key_scr[pl.ds(off, KC), :] >= thr, selected_bias, NEG_BIG)
            return carry
        lax.fori_loop(0, n_chunks, body, 0)

    @pl.when(has_ties)
    def _():
        int_max = jnp.int32(2 ** 31 - 1)
        n_gt = jnp.where(thr == int_max, 0, count_ge(jnp.where(thr == int_max, thr, thr + 1)))
        need = (topk - n_gt).astype(jnp.float32)

        def body(c, seen):
            off = pl.multiple_of(c * KC, KC)
            ks = key_scr[pl.ds(off, KC), :]
            eq = ks == thr
            eqf = jnp.where(eq, 1.0, 0.0)
            before = jnp.dot(ltri_ref[...], eqf.astype(jnp.bfloat16),
                             preferred_element_type=jnp.float32) + seen
            sel = jnp.logical_or(ks > thr, jnp.logical_and(eq, before < need))
            sel = jnp.logical_and(sel, krow + off <= qpos)
            bias_scr[pl.ds(off, KC), :] = jnp.where(sel, selected_bias, NEG_BIG)
            return seen + jnp.sum(eqf, axis=0, keepdims=True)
        lax.fori_loop(0, n_chunks, body, jnp.zeros((1, tq), jnp.float32))

    def qk(c, half):
        off = pl.multiple_of(c * KC + half * HALF, HALF)
        return [jnp.dot(k_ref[pl.ds(off, HALF), :], qt_ref[0], preferred_element_type=jnp.float32)]

    def mask(t, c, half, r0):
        b = bias_scr[pl.ds(pl.multiple_of(c * KC + half * HALF + r0, RB), RB), :]
        return t + jnp.concatenate([b] * nh, axis=1)

    def pv(c, half, p_ref):
        return jnp.dot(vt_ref[0, c, :, half * HALF:(half + 1) * HALF], p_ref[...],
                       preferred_element_type=jnp.float32)

    _attend(n_chunks, qk, mask, pv, sa_scr, sb_scr, pa_scr, pb_scr, acc_scr, DSA_DIM)
    acc = acc_scr[...]
    out_t = acc[:DSA_DIM, :] / acc[DSA_DIM:DSA_DIM + 1, :]
    for hp in range(nh // 2):
        pair = jnp.concatenate([out_t[:, 2 * hp * tq:(2 * hp + 1) * tq],
                                out_t[:, (2 * hp + 1) * tq:(2 * hp + 2) * tq]], axis=0)
        o_ref[:, hp * LANES:(hp + 1) * LANES] = pair.T.astype(o_ref.dtype)


def _dsa(bound, dk, vt, qt, iqt, iwt, ltri_strict, batch, seq):
    nq = seq // TQ_DSA
    assert vt.shape[3] == KC and ltri_strict.shape[0] == KC and seq % KC == 0
    topk = min(DSA_MAX_TOPK, seq // 4)
    width = DSA_HEADS * TQ_DSA
    return pl.pallas_call(
        functools.partial(_dsa_kernel, topk=topk),
        grid=(batch, nq),
        in_specs=[pl.BlockSpec(memory_space=pltpu.SMEM),
                  pl.BlockSpec((seq, LANES), lambda b, qi: (b, 0)),
                  pl.BlockSpec((1,) + vt.shape[1:], lambda b, qi: (b, 0, 0, 0)),
                  pl.BlockSpec((1, LANES, width), lambda b, qi: (b * nq + qi, 0, 0)),
                  pl.BlockSpec((1, LANES, IDX_HEADS * TQ_DSA), lambda b, qi: (b * nq + qi, 0, 0)),
                  pl.BlockSpec((1, IDX_HEADS, TQ_DSA), lambda b, qi: (b, 0, qi)),
                  pl.BlockSpec(ltri_strict.shape, lambda b, qi: (0, 0))],
        out_specs=pl.BlockSpec((TQ_DSA, DSA_W), lambda b, qi: (b * nq + qi, 0)),
        out_shape=jax.ShapeDtypeStruct((batch * seq, DSA_W), jnp.bfloat16),
        scratch_shapes=[pltpu.VMEM((seq, TQ_DSA), jnp.int32),
                        pltpu.VMEM((seq, TQ_DSA), jnp.float32),
                        pltpu.VMEM((HALF, width), jnp.float32), pltpu.VMEM((HALF, width), jnp.float32),
                        pltpu.VMEM((HALF, width), jnp.bfloat16), pltpu.VMEM((HALF, width), jnp.bfloat16),
                        pltpu.VMEM((DSA_VROWS, width), jnp.float32)],
        compiler_params=_cparams(("parallel", "arbitrary")), name="dsa",
    )(bound, dk, vt, qt, iqt, iwt, ltri_strict)


def _ret_kernel(q_ref, k_ref, kt_ref, v_ref, g_ref, din_ref, dq_ref, dk_ref, dc_ref, gain_ref,
                o_ref, state_scr):
    t = pl.program_id(1)

    @pl.when(t == 0)
    def _():
        state_scr[...] = jnp.zeros_like(state_scr)

    for hd in range(RET_HEADS):
        q = q_ref[:, hd * RET_QK_DIM:(hd + 1) * RET_QK_DIM]
        k = k_ref[:, hd * RET_QK_DIM:(hd + 1) * RET_QK_DIM]
        v = v_ref[:, hd * RET_V_DIM:(hd + 1) * RET_V_DIM]
        state = state_scr[hd]
        attn = lax.dot_general(q, k, (((1,), (1,)), ((), ())),
                               preferred_element_type=jnp.float32) * din_ref[hd]
        inner = jnp.dot(attn.astype(jnp.bfloat16), v, preferred_element_type=jnp.float32)
        cross = jnp.dot(q, state.astype(jnp.bfloat16),
                        preferred_element_type=jnp.float32) * dq_ref[hd]
        ktd = (kt_ref[0, hd * RET_QK_DIM:(hd + 1) * RET_QK_DIM, :] * dk_ref[hd]).astype(jnp.bfloat16)
        state_scr[hd] = dc_ref[hd] * state + jnp.dot(ktd, v, preferred_element_type=jnp.float32)
        y = inner + cross
        yn = y * lax.rsqrt(jnp.mean(y * y, axis=-1, keepdims=True) + NORM_EPS) * gain_ref[hd]
        gate = g_ref[:, hd * RET_V_DIM:(hd + 1) * RET_V_DIM]
        o_ref[:, hd * RET_V_DIM:(hd + 1) * RET_V_DIM] = (
            yn * (gate * jax.nn.sigmoid(gate))).astype(o_ref.dtype)


def _ret(rq, rk, rkt, rv, rg, consts, gain, batch, seq):
    c = RET_CHUNK
    n = seq // c

    def tok(width):
        return pl.BlockSpec((c, width), lambda b, t: (b * n + t, 0))

    def full(a):
        return pl.BlockSpec(a.shape, lambda b, t: (0,) * a.ndim)

    return pl.pallas_call(
        _ret_kernel, grid=(batch, n),
        in_specs=[tok(RET_QK_W), tok(RET_QK_W),
                  pl.BlockSpec((1, RET_QK_W, c), lambda b, t: (b, 0, t)),
                  tok(RET_V_W), tok(RET_V_W),
                  full(consts["ret_din"]), full(consts["ret_dq"]), full(consts["ret_dk"]),
                  full(consts["ret_dc"]), full(gain)],
        out_specs=tok(RET_V_W),
        out_shape=jax.ShapeDtypeStruct((batch * seq, RET_V_W), jnp.bfloat16),
        scratch_shapes=[pltpu.VMEM((RET_HEADS, RET_QK_DIM, RET_V_DIM), jnp.float32)],
        compiler_params=_cparams(("parallel", "arbitrary")), name="ret",
    )(rq, rk, rkt, rv, rg, consts["ret_din"], consts["ret_dq"], consts["ret_dk"], consts["ret_dc"], gain)


def _merge_kernel(x_ref, g_ref, ya_ref, yb_ref, yc_ref, wzg_ref, wa_ref, wb_ref, wc_ref, wo_ref, o_ref):
    x = x_ref[...]
    ms = jnp.mean(x * x, axis=-1, keepdims=True)
    h = (x * lax.rsqrt(ms + NORM_EPS) * g_ref[...]).astype(jnp.bfloat16)
    merged = None
    for j, (y_ref, w_ref) in enumerate(((ya_ref, wa_ref), (yb_ref, wb_ref), (yc_ref, wc_ref))):
        gate = jax.nn.sigmoid(jnp.dot(h, wzg_ref[:, j * D_MODEL:(j + 1) * D_MODEL],
                                      preferred_element_type=jnp.float32))
        term = gate * jnp.dot(y_ref[...], w_ref[...], preferred_element_type=jnp.float32)
        merged = term if merged is None else merged + term
    o_ref[...] = x + jnp.dot(merged.astype(jnp.bfloat16), wo_ref[...],
                             preferred_element_type=jnp.float32)


def _ffn_kernel(x_ref, g_ref, wg_ref, wu_ref, wd_ref, o_ref, *, chunk):
    x = x_ref[...]
    ms = jnp.mean(x * x, axis=-1, keepdims=True)
    h = (x * lax.rsqrt(ms + NORM_EPS) * g_ref[...]).astype(jnp.bfloat16)
    acc = x
    for lo in range(0, FFN_HIDDEN, chunk):
        gt = jnp.dot(h, wg_ref[:, lo:lo + chunk], preferred_element_type=jnp.float32)
        up = jnp.dot(h, wu_ref[:, lo:lo + chunk], preferred_element_type=jnp.float32)
        act = (gt * jax.nn.sigmoid(gt) * up).astype(jnp.bfloat16)
        acc = acc + jnp.dot(act, wd_ref[lo:lo + chunk, :], preferred_element_type=jnp.float32)
    o_ref[...] = acc


def _row_call(kernel, name, order):
    n = order[0][1].shape[0]
    tm = min(TM_POST, n)
    arrays, specs = [], []
    for kind, a in order:
        arrays.append(a)
        if kind == "row":
            specs.append(pl.BlockSpec((tm, a.shape[1]), lambda i: (i, 0)))
        else:
            specs.append(pl.BlockSpec(a.shape, lambda i, nd=a.ndim: (0,) * nd))
    return pl.pallas_call(
        kernel, grid=(n // tm,), in_specs=specs,
        out_specs=pl.BlockSpec((tm, D_MODEL), lambda i: (i, 0)),
        out_shape=jax.ShapeDtypeStruct((n, D_MODEL), jnp.float32),
        compiler_params=_cparams(("parallel",)), name=name,
    )(*arrays)


def _merge(x2d, ya, yb, yc, lw):
    order = [("row", x2d), ("full", lw["ln1_g"]), ("row", ya), ("row", yb), ("row", yc),
             ("full", lw["w_zg"]), ("full", lw["w_a"]), ("full", lw["w_b"]), ("full", lw["w_c"]),
             ("full", lw["w_o"])]
    return _row_call(_merge_kernel, "merge", order)


def _ffn(x2d, lw):
    order = [("row", x2d), ("full", lw["ln2_g"]), ("full", lw["w_g"]), ("full", lw["w_u"]),
             ("full", lw["w_d"])]
    return _row_call(functools.partial(_ffn_kernel, chunk=256), "ffn", order)


def _rope_cs(seq, dim):
    half = dim // 2
    inv_freq = ROPE_THETA ** (-jnp.arange(half, dtype=jnp.float32) / half)
    ang = jnp.arange(seq, dtype=jnp.float32)[:, None] * inv_freq[None, :]
    return jnp.cos(ang), jnp.sin(ang)


def _constants(seq):
    tm = min(TM_IN, seq)
    bf = jnp.bfloat16
    cos64, sin64 = _rope_cs(seq, 64)
    cos32, sin32 = _rope_cs(seq, 32)
    z32 = jnp.zeros((seq, 32), jnp.float32)
    c64 = jnp.concatenate([cos64, cos64], axis=1)
    s64 = jnp.concatenate([-sin64, sin64], axis=1)
    c32 = jnp.concatenate([cos32, cos32], axis=1)
    s32 = jnp.concatenate([-sin32, sin32], axis=1)
    consts = {
        "kc": jnp.concatenate([c64, c32, z32], axis=1), "ks": jnp.concatenate([s64, s32, z32], axis=1),
        "rc": jnp.concatenate([c64, c64], axis=1), "rs": jnp.concatenate([s64, s64], axis=1),
        "c64t": c64.T, "s64t": s64.T,
        "iqa": c32.T, "iqb": s32.T, "rka": c64.T, "rkb": s64.T,
        "ltri": jnp.tril(jnp.ones((tm, tm), jnp.float32)).astype(bf),
        "ltri_strict": jnp.tril(jnp.ones((KC, KC), jnp.float32), -1).astype(bf),
    }
    eq = np.zeros((3 * LANES, FOX_HEADS * LANES), np.float32)
    ek = np.zeros((3 * LANES, FOX_HEADS * LANES), np.float32)
    oneq = np.zeros((1, FOX_HEADS * LANES), np.float32)
    onek = np.zeros((1, FOX_HEADS * LANES), np.float32)
    for hd in range(FOX_HEADS):
        base = hd * LANES + FOX_DIM
        for part in range(3):
            eq[part * LANES + hd, base + part] = 1.0
            ek[part * LANES + hd, base + 3 + part] = -1.0
            oneq[0, base + 3 + part] = 1.0
            onek[0, base + part] = 1.0
    consts.update(eq=jnp.asarray(eq, bf), ek=jnp.asarray(ek, bf), oneq=jnp.asarray(oneq), onek=jnp.asarray(onek))
    log_g = jnp.log1p(-(2.0 ** (-5.0 - jnp.arange(RET_HEADS, dtype=jnp.float32))))
    pos = jnp.arange(RET_CHUNK, dtype=jnp.float32)
    diff = pos[:, None] - pos[None, :]
    din = jnp.where(diff >= 0, jnp.exp(jnp.maximum(diff, 0.0)[None] * log_g[:, None, None]), 0.0)
    dq = jnp.exp((pos + 1.0)[None] * log_g[:, None])
    dk = jnp.exp((RET_CHUNK - 1.0 - pos)[None] * log_g[:, None])
    dc = jnp.exp(RET_CHUNK * log_g)
    consts.update(ret_din=din, ret_dq=dq[:, :, None], ret_dk=dk[:, None, :],
                  ret_dc=jnp.broadcast_to(dc[:, None, None], (RET_HEADS, 1, LANES)))
    return consts


def _pad_heads(w, heads, dim):
    k = w.shape[0]
    w3 = w.reshape(k, heads, dim)
    return jnp.pad(w3, ((0, 0), (0, 0), (0, LANES - dim))).reshape(k, heads * LANES)


def _layer_weights(p, consts):
    bf = jnp.bfloat16
    w_in = p["w_in"]

    def cols(off, size):
        return w_in[:, off:off + size]

    zeros = lambda n: jnp.zeros((D_MODEL, n), jnp.float32)
    w_tm = jnp.concatenate([
        _pad_heads(cols(O_FQ, FOX_W), FOX_HEADS, FOX_DIM),
        _pad_heads(cols(O_FK, FOX_W), FOX_HEADS, FOX_DIM),
        cols(O_FF, FOX_HEADS), zeros(LANES - FOX_HEADS),
        cols(O_DK, DSA_DIM), cols(O_IK, IDX_DIM), zeros(LANES - DSA_DIM - IDX_DIM),
        cols(O_RQ, RET_QK_W), cols(O_RK, RET_QK_W), cols(O_RV, RET_V_W), cols(O_RG, RET_V_W),
    ], axis=1).astype(bf)
    w_fm = jnp.concatenate([
        cols(O_DQ, DSA_W), cols(O_IQ, IDX_W), cols(O_DV, DSA_DIM),
        cols(O_IW, IDX_HEADS), zeros(16 - IDX_HEADS), cols(O_RK, RET_QK_W), cols(O_FV, FOX_W),
    ], axis=1).T.astype(bf)

    def lane_pad(v, fill=0.0):
        return jnp.concatenate([v, jnp.full((LANES - v.shape[0],), fill, jnp.float32)])[None, :]

    g = p["dsa_q_norm"]
    g_sw = jnp.concatenate([g[DSA_DIM // 2:], g[:DSA_DIM // 2]])
    scale = DSA_DIM ** -0.5 * LOG2E

    def logit_bound(gq, gk, dim):
        b = BOUND_SLACK * dim * jnp.max(jnp.abs(gq)) * jnp.max(jnp.abs(gk)) * (dim ** -0.5 * LOG2E)
        return b.reshape(1, 1).astype(jnp.float32)

    return {
        "fox_bound": logit_bound(p["fox_q_norm"], p["fox_k_norm"], FOX_DIM),
        "dsa_bound": logit_bound(p["dsa_q_norm"], p["dsa_k_norm"], DSA_DIM),
        "ln1_g": p["ln1_g"][None, :], "ln2_g": p["ln2_g"][None, :],
        "w_tm": w_tm, "w_fm": w_fm,
        "fox_b": lane_pad(p["fox_b_f"]),
        "fq_gain": lane_pad(p["fox_q_norm"]), "fk_gain": lane_pad(p["fox_k_norm"]),
        "dk_gain": jnp.concatenate([p["dsa_k_norm"], jnp.ones((IDX_DIM,), jnp.float32),
                                    jnp.zeros((LANES - DSA_DIM - IDX_DIM,), jnp.float32)])[None, :],
        "dqa": consts["c64t"] * (g * scale)[:, None], "dqb": consts["s64t"] * (g_sw * scale)[:, None],
        "ret_gain": p["ret_out_norm"][:, None, :],
        "w_zg": cols(O_ZG, N_BRANCH * D_MODEL).astype(bf),
        "w_a": p["w_fox_out"].astype(bf), "w_b": p["w_dsa_out"].astype(bf), "w_c": p["w_ret_out"].astype(bf),
        "w_o": p["w_o"].astype(bf),
        "w_g": p["w_ffn_in"][:, :FFN_HIDDEN].astype(bf), "w_u": p["w_ffn_in"][:, FFN_HIDDEN:].astype(bf),
        "w_d": p["w_ffn_out"].astype(bf),
    }


def _layer(x2d, lw, consts, batch, seq):
    (fq, fk, fvt, dk, rq, rk, rv, rg, qt, iqt, vt, iwt, rkt) = _inproj(x2d, lw, consts, batch, seq)
    ya = _fox(fq, fk, fvt, batch, seq)
    yb = _dsa(lw["dsa_bound"], dk, vt, qt, iqt, iwt, consts["ltri_strict"], batch, seq)
    yc = _ret(rq, rk, rkt, rv, rg, consts, lw["ret_gain"], batch, seq)
    x2d = _merge(x2d, ya, yb, yc, lw)
    return _ffn(x2d, lw)


def kernel(x, ln1_g, w_in, fox_b_f, fox_q_norm, fox_k_norm, dsa_q_norm, dsa_k_norm, ret_out_norm,
           w_fox_out, w_dsa_out, w_ret_out, w_o, ln2_g, w_ffn_in, w_ffn_out):
    batch, seq, _ = x.shape
    depth = w_in.shape[0]
    consts = _constants(seq)
    params = dict(ln1_g=ln1_g, w_in=w_in, fox_b_f=fox_b_f, fox_q_norm=fox_q_norm, fox_k_norm=fox_k_norm,
                  dsa_q_norm=dsa_q_norm, dsa_k_norm=dsa_k_norm, ret_out_norm=ret_out_norm,
                  w_fox_out=w_fox_out, w_dsa_out=w_dsa_out, w_ret_out=w_ret_out, w_o=w_o,
                  ln2_g=ln2_g, w_ffn_in=w_ffn_in, w_ffn_out=w_ffn_out)
    x2d = x.reshape(batch * seq, D_MODEL)
    for layer in range(depth):
        lw = _layer_weights({k: v[layer] for k, v in params.items()}, consts)
        x2d = _layer(x2d, lw, consts, batch, seq)
    return x2d.reshape(batch, seq, D_MODEL)
```

```python
import functools
import math

import jax
import jax.numpy as jnp
import numpy as np
from jax import lax
from jax.experimental import pallas as pl
from jax.experimental.pallas import tpu as pltpu

D_MODEL = 1024
FOX_HEADS = 8
FOX_DIM = 64
DSA_HEADS = 8
DSA_DIM = 64
IDX_HEADS = 8
IDX_DIM = 32
DSA_MAX_TOPK = 256
RET_HEADS = 4
RET_QK_DIM = 64
RET_V_DIM = 128
RET_CHUNK = 128
FFN_HIDDEN = 2816
ROPE_THETA = 10000.0
NORM_EPS = 1e-6
N_BRANCH = 3

FOX_W = FOX_HEADS * FOX_DIM
DSA_W = DSA_HEADS * DSA_DIM
IDX_W = IDX_HEADS * IDX_DIM
RET_QK_W = RET_HEADS * RET_QK_DIM
RET_V_W = RET_HEADS * RET_V_DIM
IN_SIZES = (FOX_W, FOX_W, FOX_W, FOX_HEADS,
            DSA_W, DSA_DIM, DSA_DIM, IDX_W, IDX_DIM, IDX_HEADS,
            RET_QK_W, RET_QK_W, RET_V_W, RET_V_W,
            N_BRANCH * D_MODEL)
IN_OFFS = tuple(int(v) for v in np.cumsum((0,) + IN_SIZES))
(O_FQ, O_FK, O_FV, O_FF, O_DQ, O_DK, O_DV, O_IQ, O_IK, O_IW,
 O_RQ, O_RK, O_RV, O_RG, O_ZG, _) = IN_OFFS

LANES = 128
SUBLANES = 8
BF16_ROWS = 16
VMEM_LIMIT = 56 * 1024 * 1024
NEG_BIG = -1e30
LOG2E = math.log2(math.e)
BOUND_SLACK = 1.02
MIN_DENOM = 2.0 ** -100
HALF_RANGE = 2 ** 15

T_FQ = 0
T_FK = T_FQ + FOX_HEADS * LANES
T_FF = T_FK + FOX_HEADS * LANES
T_DK = T_FF + LANES
T_RQ = T_DK + LANES
T_RK = T_RQ + RET_QK_W
T_RV = T_RK + RET_QK_W
T_RG = T_RV + RET_V_W
T_COLS = T_RG + RET_V_W
F_DQ = 0
F_IQ = F_DQ + DSA_W
F_DV = F_IQ + IDX_W
F_IW = F_DV + DSA_DIM
F_RK = F_IW + 16
F_FV = F_RK + RET_QK_W
F_ROWS = F_FV + FOX_W

TM_IN = 512
KC = TM_IN
HALF = KC // 2
RB = 32
TQ_FOX = 512
FOX_VROWS = 2 * FOX_DIM + BF16_ROWS
DSA_VROWS = DSA_DIM + BF16_ROWS
TQ_DSA = 256
TM_POST = 512


def _cparams(sem):
    return pltpu.CompilerParams(dimension_semantics=sem, vmem_limit_bytes=VMEM_LIMIT)


def _split3(v):
    hi = v.astype(jnp.bfloat16)
    r1 = v - hi.astype(jnp.float32)
    mid = r1.astype(jnp.bfloat16)
    lo = (r1 - mid.astype(jnp.float32)).astype(jnp.bfloat16)
    return hi, mid, lo


def _col_reduce(v, op):
    return op(v.reshape(v.shape[0] // SUBLANES, SUBLANES, v.shape[1]), axis=0)


def _attend(n_chunks, qk, mask, pv, sa_scr, sb_scr, pa_scr, pb_scr, acc_scr, den_row):
    def store_logits(s_ref, c, half):
        lo = 0
        for g in qk(c, half):
            s_ref[:, lo:lo + g.shape[1]] = g
            lo += g.shape[1]

    def probs(s_ref, c, half, p_ref, shift):
        for r0 in range(0, s_ref.shape[0], RB):
            t = mask(s_ref[r0:r0 + RB, :], c, half, r0)
            if shift is not None:
                t = t - shift
            p_ref[r0:r0 + RB, :] = jnp.exp2(t).astype(p_ref.dtype)

    def run(shift):
        acc_scr[...] = jnp.zeros_like(acc_scr)
        pb_scr[...] = jnp.zeros_like(pb_scr)
        store_logits(sa_scr, 0, 0)

        def chunk(c, carry):
            store_logits(sb_scr, c, 1)
            acc_scr[...] += pv(jnp.maximum(c - 1, 0), 1, pb_scr)
            probs(sa_scr, c, 0, pa_scr, shift)
            store_logits(sa_scr, jnp.minimum(c + 1, n_chunks - 1), 0)
            acc_scr[...] += pv(c, 0, pa_scr)
            probs(sb_scr, c, 1, pb_scr, shift)
            return carry

        lax.fori_loop(0, n_chunks, chunk, 0)
        acc_scr[...] += pv(n_chunks - 1, 1, pb_scr)

    run(None)

    @pl.when(jnp.logical_not(jnp.min(acc_scr[den_row:den_row + 1, :]) >= MIN_DENOM))
    def _():
        def col_max(c, mx):
            for half in range(2):
                store_logits(sa_scr, c, half)
                for r0 in range(0, sa_scr.shape[0], RB):
                    mx = jnp.maximum(mx, _col_reduce(mask(sa_scr[r0:r0 + RB, :], c, half, r0), jnp.max))
            return mx
        mx = lax.fori_loop(0, n_chunks, col_max,
                           jnp.full((SUBLANES, acc_scr.shape[1]), NEG_BIG, jnp.float32))
        run(jnp.max(mx, axis=0, keepdims=True))


def _inproj_kernel(bound_ref, x_ref, g_ref, wtm_ref, wfm_ref, fb_ref, fqg_ref, fkg_ref, dkg_ref,
                   ltri_ref, eq_ref, ek_ref, oneq_ref, onek_ref,
                   kc_ref, ks_ref, rc_ref, rs_ref,
                   dqa_ref, dqb_ref, iqa_ref, iqb_ref, rka_ref, rkb_ref,
                   fq_out, fk_out, fvt_out, dk_out, rq_out, rk_out, rv_out, rg_out,
                   qt_out, iqt_out, vt_out, iw_out, rkt_out,
                   carry_ref, *, tiles_per_seq):
    tm = x_ref.shape[0]
    i = pl.program_id(0)

    @pl.when(i % tiles_per_seq == 0)
    def _():
        carry_ref[...] = jnp.zeros_like(carry_ref)

    x = x_ref[...]
    ms = jnp.mean(x * x, axis=-1, keepdims=True)
    h = (x * lax.rsqrt(ms + NORM_EPS) * g_ref[...]).astype(jnp.bfloat16)

    def tm_dot(lo, width):
        return jnp.dot(h, wtm_ref[:, lo:lo + width], preferred_element_type=jnp.float32)

    lane = lax.broadcasted_iota(jnp.int32, (tm, LANES), 1)

    ffb = tm_dot(T_FF, LANES) + fb_ref[...]
    lf = (jnp.minimum(ffb, 0.0) - jnp.log1p(jnp.exp(-jnp.abs(ffb)))) * LOG2E
    parts = jnp.concatenate(_split3(lf), axis=1)
    cs = jnp.dot(ltri_ref[...], parts, preferred_element_type=jnp.float32)
    c = cs[:, :LANES] + cs[:, LANES:2 * LANES] + cs[:, 2 * LANES:] + carry_ref[...]
    carry_ref[...] = c[tm - 1:tm, :]
    cparts_q = jnp.concatenate(_split3(c - bound_ref[0, 0]), axis=1)
    cparts_k = jnp.concatenate(_split3(c), axis=1)
    scat_q = jnp.dot(cparts_q, eq_ref[...], preferred_element_type=jnp.float32) + oneq_ref[...]
    scat_k = jnp.dot(cparts_k, ek_ref[...], preferred_element_type=jnp.float32) + onek_ref[...]

    for (lo, gain_ref, scat, out, scale) in ((T_FQ, fqg_ref, scat_q, fq_out, FOX_DIM ** -0.5 * LOG2E),
                                             (T_FK, fkg_ref, scat_k, fk_out, 1.0)):
        z = tm_dot(lo, FOX_HEADS * LANES)
        for hd in range(FOX_HEADS):
            blk = z[:, hd * LANES:(hd + 1) * LANES]
            ss = jnp.sum(blk * blk, axis=-1, keepdims=True) * (1.0 / FOX_DIM)
            nb = blk * lax.rsqrt(ss + NORM_EPS) * (gain_ref[...] * scale)
            out[:, hd * LANES:(hd + 1) * LANES] = (
                nb + scat[:, hd * LANES:(hd + 1) * LANES]).astype(out.dtype)

    zk = tm_dot(T_DK, LANES)
    ssk = jnp.sum(jnp.where(lane < DSA_DIM, zk * zk, 0.0), axis=-1, keepdims=True) * (1.0 / DSA_DIM)
    nk = zk * jnp.where(lane < DSA_DIM, lax.rsqrt(ssk + NORM_EPS), 1.0) * dkg_ref[...]
    partner = jnp.where(
        lane < 32, pltpu.roll(nk, LANES - 32, 1),
        jnp.where(lane < 64, pltpu.roll(nk, 32, 1),
                  jnp.where(lane < 80, pltpu.roll(nk, LANES - 16, 1), pltpu.roll(nk, 16, 1))))
    dk_out[...] = (nk * kc_ref[...] + partner * ks_ref[...]).astype(dk_out.dtype)

    first_half = (lane % RET_QK_DIM) < (RET_QK_DIM // 2)
    for (lo, out, scale) in ((T_RQ, rq_out, RET_QK_DIM ** -0.5), (T_RK, rk_out, 1.0)):
        z = tm_dot(lo, RET_QK_W)
        for j in range(RET_QK_W // LANES):
            blk = z[:, j * LANES:(j + 1) * LANES]
            pr = jnp.where(first_half, pltpu.roll(blk, LANES - 32, 1), pltpu.roll(blk, 32, 1))
            out[:, j * LANES:(j + 1) * LANES] = (
                (blk * rc_ref[...] + pr * rs_ref[...]) * scale).astype(out.dtype)

    rv_out[...] = tm_dot(T_RV, RET_V_W).astype(rv_out.dtype)
    rg_out[...] = tm_dot(T_RG, RET_V_W)

    zt = lax.dot_general(wfm_ref[...], h, (((1,), (1,)), ((), ())),
                         preferred_element_type=jnp.float32)
    nq = tm // TQ_DSA

    def swap_halves(v):
        half = v.shape[0] // 2
        return jnp.concatenate([v[half:], v[:half]], axis=0)

    def ones_row_block(rows, dtype):
        first = lax.broadcasted_iota(jnp.int32, (rows, tm), 0) == 0
        return jnp.where(first, 1.0, 0.0).astype(dtype)

    zeros_q = jnp.zeros((LANES - DSA_DIM, DSA_HEADS * TQ_DSA), qt_out.dtype)
    zeros_i0 = jnp.zeros((DSA_DIM, IDX_HEADS * TQ_DSA), iqt_out.dtype)
    zeros_i1 = jnp.zeros((LANES - DSA_DIM - IDX_DIM, IDX_HEADS * TQ_DSA), iqt_out.dtype)
    for j in range(nq):
        qt_out[j, DSA_DIM:, :] = zeros_q
        iqt_out[j, :DSA_DIM, :] = zeros_i0
        iqt_out[j, DSA_DIM + IDX_DIM:, :] = zeros_i1
    for hd in range(DSA_HEADS):
        xh = zt[F_DQ + hd * DSA_DIM:F_DQ + (hd + 1) * DSA_DIM, :]
        r = lax.rsqrt(jnp.sum(xh * xh, axis=0, keepdims=True) * (1.0 / DSA_DIM) + NORM_EPS)
        o = ((xh * dqa_ref[...] + swap_halves(xh) * dqb_ref[...]) * r).astype(qt_out.dtype)
        for j in range(nq):
            qt_out[j, :DSA_DIM, hd * TQ_DSA:(hd + 1) * TQ_DSA] = o[:, j * TQ_DSA:(j + 1) * TQ_DSA]
    for hd in range(IDX_HEADS):
        xh = zt[F_IQ + hd * IDX_DIM:F_IQ + (hd + 1) * IDX_DIM, :]
        o = (xh * iqa_ref[...] + swap_halves(xh) * iqb_ref[...]).astype(iqt_out.dtype)
        for j in range(nq):
            iqt_out[j, DSA_DIM:DSA_DIM + IDX_DIM, hd * TQ_DSA:(hd + 1) * TQ_DSA] = (
                o[:, j * TQ_DSA:(j + 1) * TQ_DSA])
    vt_out[0, 0, :DSA_DIM, :] = zt[F_DV:F_DV + DSA_DIM, :].astype(vt_out.dtype)
    vt_out[0, 0, DSA_DIM:, :] = ones_row_block(BF16_ROWS, vt_out.dtype)
    iw_out[0] = zt[F_IW:F_IW + IDX_HEADS, :] * ((IDX_DIM * IDX_HEADS) ** -0.5)
    for hd in range(RET_HEADS):
        xh = zt[F_RK + hd * RET_QK_DIM:F_RK + (hd + 1) * RET_QK_DIM, :]
        rkt_out[0, hd * RET_QK_DIM:(hd + 1) * RET_QK_DIM, :] = (
            xh * rka_ref[...] + swap_halves(xh) * rkb_ref[...])
    for hp in range(FOX_HEADS // 2):
        fvt_out[0, hp, 0, :2 * FOX_DIM, :] = (
            zt[F_FV + hp * 2 * FOX_DIM:F_FV + (hp + 1) * 2 * FOX_DIM, :].astype(fvt_out.dtype))
        fvt_out[0, hp, 0, 2 * FOX_DIM:, :] = ones_row_block(BF16_ROWS, fvt_out.dtype)


def _inproj(x2d, lw, consts, batch, seq):
    n = x2d.shape[0]
    tm = min(TM_IN, seq)
    tps = seq // tm
    nqt = tm // TQ_DSA
    grid = (n // tm,)
    bf = jnp.bfloat16

    def full(a):
        return pl.BlockSpec(a.shape, lambda i: (0,) * a.ndim)

    def tok(width):
        return pl.BlockSpec((tm, width), lambda i: (i, 0))

    def pos_tm(width):
        return pl.BlockSpec((tm, width), lambda i: (i % tps, 0))

    def pos_fm(rows):
        return pl.BlockSpec((rows, tm), lambda i: (0, i % tps))

    def fm_out(rows):
        return pl.BlockSpec((1, rows, tm), lambda i: (i // tps, 0, i % tps))

    in_arrays = [lw["fox_bound"], x2d, lw["ln1_g"], lw["w_tm"], lw["w_fm"], lw["fox_b"], lw["fq_gain"], lw["fk_gain"],
                 lw["dk_gain"], consts["ltri"], consts["eq"], consts["ek"], consts["oneq"], consts["onek"],
                 consts["kc"], consts["ks"], consts["rc"], consts["rs"],
                 lw["dqa"], lw["dqb"], consts["iqa"], consts["iqb"], consts["rka"], consts["rkb"]]
    in_specs = [pl.BlockSpec(memory_space=pltpu.SMEM),
                tok(D_MODEL), full(lw["ln1_g"]), full(lw["w_tm"]), full(lw["w_fm"]), full(lw["fox_b"]),
                full(lw["fq_gain"]), full(lw["fk_gain"]), full(lw["dk_gain"]),
                full(consts["ltri"]), full(consts["eq"]), full(consts["ek"]),
                full(consts["oneq"]), full(consts["onek"]),
                pos_tm(LANES), pos_tm(LANES), pos_tm(LANES), pos_tm(LANES),
                pos_fm(DSA_DIM), pos_fm(DSA_DIM), pos_fm(IDX_DIM), pos_fm(IDX_DIM),
                pos_fm(RET_QK_DIM), pos_fm(RET_QK_DIM)]
    out_shape = [
        jax.ShapeDtypeStruct((n, FOX_HEADS * LANES), bf),
        jax.ShapeDtypeStruct((n, FOX_HEADS * LANES), bf),
        jax.ShapeDtypeStruct((batch, FOX_HEADS // 2, tps, FOX_VROWS, tm), bf),
        jax.ShapeDtypeStruct((n, LANES), bf),
        jax.ShapeDtypeStruct((n, RET_QK_W), bf),
        jax.ShapeDtypeStruct((n, RET_QK_W), bf),
        jax.ShapeDtypeStruct((n, RET_V_W), bf),
        jax.ShapeDtypeStruct((n, RET_V_W), jnp.float32),
        jax.ShapeDtypeStruct((n // TQ_DSA, LANES, DSA_HEADS * TQ_DSA), bf),
        jax.ShapeDtypeStruct((n // TQ_DSA, LANES, IDX_HEADS * TQ_DSA), bf),
        jax.ShapeDtypeStruct((batch, tps, DSA_VROWS, tm), bf),
        jax.ShapeDtypeStruct((batch, IDX_HEADS, seq), jnp.float32),
        jax.ShapeDtypeStruct((batch, RET_QK_W, seq), jnp.float32),
    ]
    out_specs = [tok(FOX_HEADS * LANES), tok(FOX_HEADS * LANES),
                 pl.BlockSpec((1, FOX_HEADS // 2, 1, FOX_VROWS, tm), lambda i: (i // tps, 0, i % tps, 0, 0)),
                 tok(LANES), tok(RET_QK_W), tok(RET_QK_W), tok(RET_V_W), tok(RET_V_W),
                 pl.BlockSpec((nqt, LANES, DSA_HEADS * TQ_DSA), lambda i: (i, 0, 0)),
                 pl.BlockSpec((nqt, LANES, IDX_HEADS * TQ_DSA), lambda i: (i, 0, 0)),
                 pl.BlockSpec((1, 1, DSA_VROWS, tm), lambda i: (i // tps, i % tps, 0, 0)),
                 fm_out(IDX_HEADS), fm_out(RET_QK_W)]
    return pl.pallas_call(
        functools.partial(_inproj_kernel, tiles_per_seq=tps),
        grid=grid, in_specs=in_specs, out_specs=out_specs, out_shape=out_shape,
        scratch_shapes=[pltpu.VMEM((1, LANES), jnp.float32)],
        compiler_params=_cparams(("arbitrary",)), name="inproj",
    )(*in_arrays)


def _fox_kernel(q_ref, k_ref, vt_ref, o_ref, sa_scr, sb_scr, pa_scr, pb_scr, acc_scr):
    tq = q_ref.shape[0]
    qi = pl.program_id(2)
    n_chunks = (qi * tq + tq + KC - 1) // KC
    nt = (((1,), (1,)), ((), ()))

    qcol = qi * tq + lax.broadcasted_iota(jnp.int32, (RB, tq), 1)
    qcol = jnp.concatenate([qcol, qcol], axis=1)
    krow = lax.broadcasted_iota(jnp.int32, (RB, 2 * tq), 0)

    def qk(c, half):
        off = pl.multiple_of(c * KC + half * HALF, HALF)
        return [lax.dot_general(k_ref[pl.ds(off, HALF), hh * LANES:(hh + 1) * LANES],
                                q_ref[:, hh * LANES:(hh + 1) * LANES], nt,
                                preferred_element_type=jnp.float32) for hh in range(2)]

    def mask(t, c, half, r0):
        return jnp.where(krow + (c * KC + half * HALF + r0) <= qcol, t, NEG_BIG)

    def pv(c, half, p_ref):
        vt = vt_ref[0, 0, c, :, half * HALF:(half + 1) * HALF]
        return jnp.concatenate(
            [jnp.dot(vt, p_ref[:, hh * tq:(hh + 1) * tq], preferred_element_type=jnp.float32)
             for hh in range(2)], axis=1)

    _attend(n_chunks, qk, mask, pv, sa_scr, sb_scr, pa_scr, pb_scr, acc_scr, 2 * FOX_DIM)
    acc = acc_scr[...]
    den = acc[2 * FOX_DIM:2 * FOX_DIM + 1, :]
    out_t = jnp.concatenate([acc[:FOX_DIM, :tq] / den[:, :tq],
                             acc[FOX_DIM:2 * FOX_DIM, tq:] / den[:, tq:]], axis=0)
    for j in range(tq // LANES):
        o_ref[j * LANES:(j + 1) * LANES, :] = out_t[:, j * LANES:(j + 1) * LANES].T.astype(o_ref.dtype)


def _fox(fq, fk, fvt, batch, seq):
    tq = min(TQ_FOX, seq)
    nq = seq // tq
    width = 2 * tq
    return pl.pallas_call(
        _fox_kernel, grid=(batch, FOX_HEADS // 2, nq),
        in_specs=[pl.BlockSpec((tq, 2 * LANES), lambda b, hp, qi: (b * nq + qi, hp)),
                  pl.BlockSpec((seq, 2 * LANES), lambda b, hp, qi: (b, hp)),
                  pl.BlockSpec((1, 1) + fvt.shape[2:], lambda b, hp, qi: (b, hp, 0, 0, 0))],
        out_specs=pl.BlockSpec((tq, LANES), lambda b, hp, qi: (b * nq + qi, hp)),
        out_shape=jax.ShapeDtypeStruct((batch * seq, FOX_W), jnp.bfloat16),
        scratch_shapes=[pltpu.VMEM((HALF, width), jnp.float32), pltpu.VMEM((HALF, width), jnp.float32),
                        pltpu.VMEM((HALF, width), jnp.bfloat16), pltpu.VMEM((HALF, width), jnp.bfloat16),
                        pltpu.VMEM((FOX_VROWS, width), jnp.float32)],
        compiler_params=_cparams(("parallel", "parallel", "arbitrary")), name="fox",
    )(fq, fk, fvt)


def _dsa_kernel(bound_ref, k_ref, vt_ref, qt_ref, iqt_ref, iw_ref, ltri_ref, o_ref,
                key_scr, hi_scr, lo_scr, bias_scr, sa_scr, sb_scr, pa_scr, pb_scr, acc_scr, *, topk):
    tq = TQ_DSA
    nh = DSA_HEADS
    selected_bias = -bound_ref[0, 0]
    qi = pl.program_id(1)
    n_chunks = (qi * tq + tq + KC - 1) // KC

    qpos = qi * tq + lax.broadcasted_iota(jnp.int32, (KC, tq), 1)
    krow = lax.broadcasted_iota(jnp.int32, (KC, tq), 0)

    def score_chunk(c, carry):
        off = pl.multiple_of(c * KC, KC)
        rel = jnp.dot(k_ref[pl.ds(off, KC), :], iqt_ref[0], preferred_element_type=jnp.float32)
        score = jnp.maximum(rel[:, :tq], 0.0) * iw_ref[0, 0:1, :]
        for hd in range(1, IDX_HEADS):
            score = score + jnp.maximum(rel[:, hd * tq:(hd + 1) * tq], 0.0) * iw_ref[0, hd:hd + 1, :]
        score = jnp.where(score == 0.0, 0.0, score)
        score = jnp.where(krow + off <= qpos, score, -jnp.inf)
        bits = pltpu.bitcast(score, jnp.int32)
        key = bits ^ ((bits >> 31) & jnp.int32(0x7FFFFFFF))
        key_scr[pl.ds(off, KC), :] = key
        hi_scr[pl.ds(off, KC), :] = (key >> 16).astype(jnp.int16)
        lo_scr[pl.ds(off, KC), :] = ((key & jnp.int32(0xFFFF)) - HALF_RANGE).astype(jnp.int16)
        return carry

    lax.fori_loop(0, n_chunks, score_chunk, 0)

    def count16_ge(plane_scr, thr):
        thr16 = thr.astype(jnp.int16)

        def body(c, acc):
            off = pl.multiple_of(c * KC, KC)
            hit = jnp.where(plane_scr[pl.ds(off, KC), :] >= thr16, jnp.int16(1), jnp.int16(0))
            parts = [hit[r0:r0 + BF16_ROWS] for r0 in range(0, KC, BF16_ROWS)]
            while len(parts) > 1:
                parts = [parts[i] + parts[i + 1] for i in range(0, len(parts), 2)]
            return acc + parts[0]
        acc = lax.fori_loop(0, n_chunks, body, jnp.zeros((BF16_ROWS, tq), jnp.int16))
        return jnp.sum(acc.astype(jnp.int32), axis=0, keepdims=True)

    def kth_largest16(plane_scr, kth):
        def bit_step(b, thr):
            bit = jnp.left_shift(jnp.int32(1), 15 - b)
            cand = jnp.where(b == 0, jnp.zeros_like(thr), thr | bit)
            return jnp.where(count16_ge(plane_scr, cand) >= kth, cand, thr)
        return lax.fori_loop(0, 16, bit_step, jnp.full((1, tq), -HALF_RANGE, jnp.int32))

    def count_ge(thr):
        def body(c, acc):
            off = pl.multiple_of(c * KC, KC)
            hit = jnp.where(key_scr[pl.ds(off, KC), :] >= thr, 1, 0).astype(jnp.int32)
            return acc + _col_reduce(hit, jnp.sum)
        acc = lax.fori_loop(0, n_chunks, body, jnp.zeros((SUBLANES, tq), jnp.int32))
        return jnp.sum(acc, axis=0, keepdims=True)

    thr_hi = kth_largest16(hi_scr, topk)
    top_hi = HALF_RANGE - 1
    n_above = jnp.where(thr_hi == top_hi, 0, count16_ge(hi_scr, jnp.minimum(thr_hi + 1, top_hi)))
    thr_hi16 = thr_hi.astype(jnp.int16)

    def keep_candidates(c, carry):
        off = pl.multiple_of(c * KC, KC)
        lo_scr[pl.ds(off, KC), :] = jnp.where(hi_scr[pl.ds(off, KC), :] == thr_hi16,
                                              lo_scr[pl.ds(off, KC), :], jnp.int16(-HALF_RANGE))
        return carry

    lax.fori_loop(0, n_chunks, keep_candidates, 0)
    thr_lo = kth_largest16(lo_scr, topk - n_above)
    thr = thr_hi * (2 * HALF_RANGE) + (thr_lo + HALF_RANGE)
    has_ties = jnp.max(count_ge(thr)) > topk

    @pl.when(jnp.logical_not(has_ties))
    def _():
        def body(c, carry):
            off = pl.multiple_of(c * KC, KC)
            bias_scr[pl.ds(off, KC), :] = jnp.where(key_scr[pl.ds(off, KC), :] >= thr, selected_bias, NEG_BIG)
            return carry
        lax.fori_loop(0, n_chunks, body, 0)

    @pl.when(has_ties)
    def _():
        int_max = jnp.int32(2 ** 31 - 1)
        n_gt = jnp.where(thr == int_max, 0, count_ge(jnp.where(thr == int_max, thr, thr + 1)))
        need = (topk - n_gt).astype(jnp.float32)

        def body(c, seen):
            off = pl.multiple_of(c * KC, KC)
            ks = key_scr[pl.ds(off, KC), :]
            eq = ks == thr
            eqf = jnp.where(eq, 1.0, 0.0)
            before = jnp.dot(ltri_ref[...], eqf.astype(jnp.bfloat16),
                             preferred_element_type=jnp.float32) + seen
            sel = jnp.logical_or(ks > thr, jnp.logical_and(eq, before < need))
            sel = jnp.logical_and(sel, krow + off <= qpos)
            bias_scr[pl.ds(off, KC), :] = jnp.where(sel, selected_bias, NEG_BIG)
            return seen + jnp.sum(eqf, axis=0, keepdims=True)
        lax.fori_loop(0, n_chunks, body, jnp.zeros((1, tq), jnp.float32))

    def qk(c, half):
        off = pl.multiple_of(c * KC + half * HALF, HALF)
        return [jnp.dot(k_ref[pl.ds(off, HALF), :], qt_ref[0], preferred_element_type=jnp.float32)]

    def mask(t, c, half, r0):
        b = bias_scr[pl.ds(pl.multiple_of(c * KC + half * HALF + r0, RB), RB), :]
        return t + jnp.concatenate([b] * nh, axis=1)

    def pv(c, half, p_ref):
        return jnp.dot(vt_ref[0, c, :, half * HALF:(half + 1) * HALF], p_ref[...],
                       preferred_element_type=jnp.float32)

    _attend(n_chunks, qk, mask, pv, sa_scr, sb_scr, pa_scr, pb_scr, acc_scr, DSA_DIM)
    acc = acc_scr[...]
    out_t = acc[:DSA_DIM, :] / acc[DSA_DIM:DSA_DIM + 1, :]
    for hp in range(nh // 2):
        pair = jnp.concatenate([out_t[:, 2 * hp * tq:(2 * hp + 1) * tq],
                                out_t[:, (2 * hp + 1) * tq:(2 * hp + 2) * tq]], axis=0)
        o_ref[:, hp * LANES:(hp + 1) * LANES] = pair.T.astype(o_ref.dtype)


def _dsa(bound, dk, vt, qt, iqt, iwt, ltri_strict, batch, seq):
    nq = seq // TQ_DSA
    assert vt.shape[3] == KC and ltri_strict.shape[0] == KC and seq % KC == 0
    topk = min(DSA_MAX_TOPK, seq // 4)
    width = DSA_HEADS * TQ_DSA
    return pl.pallas_call(
        functools.partial(_dsa_kernel, topk=topk),
        grid=(batch, nq),
        in_specs=[pl.BlockSpec(memory_space=pltpu.SMEM),
                  pl.BlockSpec((seq, LANES), lambda b, qi: (b, 0)),
                  pl.BlockSpec((1,) + vt.shape[1:], lambda b, qi: (b, 0, 0, 0)),
                  pl.BlockSpec((1, LANES, width), lambda b, qi: (b * nq + qi, 0, 0)),
                  pl.BlockSpec((1, LANES, IDX_HEADS * TQ_DSA), lambda b, qi: (b * nq + qi, 0, 0)),
                  pl.BlockSpec((1, IDX_HEADS, TQ_DSA), lambda b, qi: (b, 0, qi)),
                  pl.BlockSpec(ltri_strict.shape, lambda b, qi: (0, 0))],
        out_specs=pl.BlockSpec((TQ_DSA, DSA_W), lambda b, qi: (b * nq + qi, 0)),
        out_shape=jax.ShapeDtypeStruct((batch * seq, DSA_W), jnp.bfloat16),
        scratch_shapes=[pltpu.VMEM((seq, TQ_DSA), jnp.int32),
                        pltpu.VMEM((seq, TQ_DSA), jnp.int16),
                        pltpu.VMEM((seq, TQ_DSA), jnp.int16),
                        pltpu.VMEM((seq, TQ_DSA), jnp.float32),
                        pltpu.VMEM((HALF, width), jnp.float32), pltpu.VMEM((HALF, width), jnp.float32),
                        pltpu.VMEM((HALF, width), jnp.bfloat16), pltpu.VMEM((HALF, width), jnp.bfloat16),
                        pltpu.VMEM((DSA_VROWS, width), jnp.float32)],
        compiler_params=_cparams(("parallel", "arbitrary")), name="dsa",
    )(bound, dk, vt, qt, iqt, iwt, ltri_strict)


def _ret_kernel(q_ref, k_ref, kt_ref, v_ref, g_ref, din_ref, dq_ref, dk_ref, dc_ref, gain_ref,
                o_ref, state_scr):
    t = pl.program_id(1)

    @pl.when(t == 0)
    def _():
        state_scr[...] = jnp.zeros_like(state_scr)

    for hd in range(RET_HEADS):
        q = q_ref[:, hd * RET_QK_DIM:(hd + 1) * RET_QK_DIM]
        k = k_ref[:, hd * RET_QK_DIM:(hd + 1) * RET_QK_DIM]
        v = v_ref[:, hd * RET_V_DIM:(hd + 1) * RET_V_DIM]
        state = state_scr[hd]
        attn = lax.dot_general(q, k, (((1,), (1,)), ((), ())),
                               preferred_element_type=jnp.float32) * din_ref[hd]
        inner = jnp.dot(attn.astype(jnp.bfloat16), v, preferred_element_type=jnp.float32)
        cross = jnp.dot(q, state.astype(jnp.bfloat16),
                        preferred_element_type=jnp.float32) * dq_ref[hd]
        ktd = (kt_ref[0, hd * RET_QK_DIM:(hd + 1) * RET_QK_DIM, :] * dk_ref[hd]).astype(jnp.bfloat16)
        state_scr[hd] = dc_ref[hd] * state + jnp.dot(ktd, v, preferred_element_type=jnp.float32)
        y = inner + cross
        yn = y * lax.rsqrt(jnp.mean(y * y, axis=-1, keepdims=True) + NORM_EPS) * gain_ref[hd]
        gate = g_ref[:, hd * RET_V_DIM:(hd + 1) * RET_V_DIM]
        o_ref[:, hd * RET_V_DIM:(hd + 1) * RET_V_DIM] = (
            yn * (gate * jax.nn.sigmoid(gate))).astype(o_ref.dtype)


def _ret(rq, rk, rkt, rv, rg, consts, gain, batch, seq):
    c = RET_CHUNK
    n = seq // c

    def tok(width):
        return pl.BlockSpec((c, width), lambda b, t: (b * n + t, 0))

    def full(a):
        return pl.BlockSpec(a.shape, lambda b, t: (0,) * a.ndim)

    return pl.pallas_call(
        _ret_kernel, grid=(batch, n),
        in_specs=[tok(RET_QK_W), tok(RET_QK_W),
                  pl.BlockSpec((1, RET_QK_W, c), lambda b, t: (b, 0, t)),
                  tok(RET_V_W), tok(RET_V_W),
                  full(consts["ret_din"]), full(consts["ret_dq"]), full(consts["ret_dk"]),
                  full(consts["ret_dc"]), full(gain)],
        out_specs=tok(RET_V_W),
        out_shape=jax.ShapeDtypeStruct((batch * seq, RET_V_W), jnp.bfloat16),
        scratch_shapes=[pltpu.VMEM((RET_HEADS, RET_QK_DIM, RET_V_DIM), jnp.float32)],
        compiler_params=_cparams(("parallel", "arbitrary")), name="ret",
    )(rq, rk, rkt, rv, rg, consts["ret_din"], consts["ret_dq"], consts["ret_dk"], consts["ret_dc"], gain)


def _merge_kernel(x_ref, g_ref, ya_ref, yb_ref, yc_ref, wzg_ref, wa_ref, wb_ref, wc_ref, wo_ref, o_ref):
    x = x_ref[...]
    ms = jnp.mean(x * x, axis=-1, keepdims=True)
    h = (x * lax.rsqrt(ms + NORM_EPS) * g_ref[...]).astype(jnp.bfloat16)
    merged = None
    for j, (y_ref, w_ref) in enumerate(((ya_ref, wa_ref), (yb_ref, wb_ref), (yc_ref, wc_ref))):
        gate = jax.nn.sigmoid(jnp.dot(h, wzg_ref[:, j * D_MODEL:(j + 1) * D_MODEL],
                                      preferred_element_type=jnp.float32))
        term = gate * jnp.dot(y_ref[...], w_ref[...], preferred_element_type=jnp.float32)
        merged = term if merged is None else merged + term
    o_ref[...] = x + jnp.dot(merged.astype(jnp.bfloat16), wo_ref[...],
                             preferred_element_type=jnp.float32)


def _ffn_kernel(x_ref, g_ref, wg_ref, wu_ref, wd_ref, o_ref, *, chunk):
    x = x_ref[...]
    ms = jnp.mean(x * x, axis=-1, keepdims=True)
    h = (x * lax.rsqrt(ms + NORM_EPS) * g_ref[...]).astype(jnp.bfloat16)
    acc = x
    for lo in range(0, FFN_HIDDEN, chunk):
        gt = jnp.dot(h, wg_ref[:, lo:lo + chunk], preferred_element_type=jnp.float32)
        up = jnp.dot(h, wu_ref[:, lo:lo + chunk], preferred_element_type=jnp.float32)
        act = (gt * jax.nn.sigmoid(gt) * up).astype(jnp.bfloat16)
        acc = acc + jnp.dot(act, wd_ref[lo:lo + chunk, :], preferred_element_type=jnp.float32)
    o_ref[...] = acc


def _row_call(kernel, name, order):
    n = order[0][1].shape[0]
    tm = min(TM_POST, n)
    arrays, specs = [], []
    for kind, a in order:
        arrays.append(a)
        if kind == "row":
            specs.append(pl.BlockSpec((tm, a.shape[1]), lambda i: (i, 0)))
        else:
            specs.append(pl.BlockSpec(a.shape, lambda i, nd=a.ndim: (0,) * nd))
    return pl.pallas_call(
        kernel, grid=(n // tm,), in_specs=specs,
        out_specs=pl.BlockSpec((tm, D_MODEL), lambda i: (i, 0)),
        out_shape=jax.ShapeDtypeStruct((n, D_MODEL), jnp.float32),
        compiler_params=_cparams(("parallel",)), name=name,
    )(*arrays)


def _merge(x2d, ya, yb, yc, lw):
    order = [("row", x2d), ("full", lw["ln1_g"]), ("row", ya), ("row", yb), ("row", yc),
             ("full", lw["w_zg"]), ("full", lw["w_a"]), ("full", lw["w_b"]), ("full", lw["w_c"]),
             ("full", lw["w_o"])]
    return _row_call(_merge_kernel, "merge", order)


def _ffn(x2d, lw):
    order = [("row", x2d), ("full", lw["ln2_g"]), ("full", lw["w_g"]), ("full", lw["w_u"]),
             ("full", lw["w_d"])]
    return _row_call(functools.partial(_ffn_kernel, chunk=256), "ffn", order)


def _rope_cs(seq, dim):
    half = dim // 2
    inv_freq = ROPE_THETA ** (-jnp.arange(half, dtype=jnp.float32) / half)
    ang = jnp.arange(seq, dtype=jnp.float32)[:, None] * inv_freq[None, :]
    return jnp.cos(ang), jnp.sin(ang)


def _constants(seq):
    tm = min(TM_IN, seq)
    bf = jnp.bfloat16
    cos64, sin64 = _rope_cs(seq, 64)
    cos32, sin32 = _rope_cs(seq, 32)
    z32 = jnp.zeros((seq, 32), jnp.float32)
    c64 = jnp.concatenate([cos64, cos64], axis=1)
    s64 = jnp.concatenate([-sin64, sin64], axis=1)
    c32 = jnp.concatenate([cos32, cos32], axis=1)
    s32 = jnp.concatenate([-sin32, sin32], axis=1)
    consts = {
        "kc": jnp.concatenate([c64, c32, z32], axis=1), "ks": jnp.concatenate([s64, s32, z32], axis=1),
        "rc": jnp.concatenate([c64, c64], axis=1), "rs": jnp.concatenate([s64, s64], axis=1),
        "c64t": c64.T, "s64t": s64.T,
        "iqa": c32.T, "iqb": s32.T, "rka": c64.T, "rkb": s64.T,
        "ltri": jnp.tril(jnp.ones((tm, tm), jnp.float32)).astype(bf),
        "ltri_strict": jnp.tril(jnp.ones((KC, KC), jnp.float32), -1).astype(bf),
    }
    eq = np.zeros((3 * LANES, FOX_HEADS * LANES), np.float32)
    ek = np.zeros((3 * LANES, FOX_HEADS * LANES), np.float32)
    oneq = np.zeros((1, FOX_HEADS * LANES), np.float32)
    onek = np.zeros((1, FOX_HEADS * LANES), np.float32)
    for hd in range(FOX_HEADS):
        base = hd * LANES + FOX_DIM
        for part in range(3):
            eq[part * LANES + hd, base + part] = 1.0
            ek[part * LANES + hd, base + 3 + part] = -1.0
            oneq[0, base + 3 + part] = 1.0
            onek[0, base + part] = 1.0
    consts.update(eq=jnp.asarray(eq, bf), ek=jnp.asarray(ek, bf), oneq=jnp.asarray(oneq), onek=jnp.asarray(onek))
    log_g = jnp.log1p(-(2.0 ** (-5.0 - jnp.arange(RET_HEADS, dtype=jnp.float32))))
    pos = jnp.arange(RET_CHUNK, dtype=jnp.float32)
    diff = pos[:, None] - pos[None, :]
    din = jnp.where(diff >= 0, jnp.exp(jnp.maximum(diff, 0.0)[None] * log_g[:, None, None]), 0.0)
    dq = jnp.exp((pos + 1.0)[None] * log_g[:, None])
    dk = jnp.exp((RET_CHUNK - 1.0 - pos)[None] * log_g[:, None])
    dc = jnp.exp(RET_CHUNK * log_g)
    consts.update(ret_din=din, ret_dq=dq[:, :, None], ret_dk=dk[:, None, :],
                  ret_dc=jnp.broadcast_to(dc[:, None, None], (RET_HEADS, 1, LANES)))
    return consts


def _pad_heads(w, heads, dim):
    k = w.shape[0]
    w3 = w.reshape(k, heads, dim)
    return jnp.pad(w3, ((0, 0), (0, 0), (0, LANES - dim))).reshape(k, heads * LANES)


def _layer_weights(p, consts):
    bf = jnp.bfloat16
    w_in = p["w_in"]

    def cols(off, size):
        return w_in[:, off:off + size]

    zeros = lambda n: jnp.zeros((D_MODEL, n), jnp.float32)
    w_tm = jnp.concatenate([
        _pad_heads(cols(O_FQ, FOX_W), FOX_HEADS, FOX_DIM),
        _pad_heads(cols(O_FK, FOX_W), FOX_HEADS, FOX_DIM),
        cols(O_FF, FOX_HEADS), zeros(LANES - FOX_HEADS),
        cols(O_DK, DSA_DIM), cols(O_IK, IDX_DIM), zeros(LANES - DSA_DIM - IDX_DIM),
        cols(O_RQ, RET_QK_W), cols(O_RK, RET_QK_W), cols(O_RV, RET_V_W), cols(O_RG, RET_V_W),
    ], axis=1).astype(bf)
    w_fm = jnp.concatenate([
        cols(O_DQ, DSA_W), cols(O_IQ, IDX_W), cols(O_DV, DSA_DIM),
        cols(O_IW, IDX_HEADS), zeros(16 - IDX_HEADS), cols(O_RK, RET_QK_W), cols(O_FV, FOX_W),
    ], axis=1).T.astype(bf)

    def lane_pad(v, fill=0.0):
        return jnp.concatenate([v, jnp.full((LANES - v.shape[0],), fill, jnp.float32)])[None, :]

    g = p["dsa_q_norm"]
    g_sw = jnp.concatenate([g[DSA_DIM // 2:], g[:DSA_DIM // 2]])
    scale = DSA_DIM ** -0.5 * LOG2E

    def logit_bound(gq, gk, dim):
        b = BOUND_SLACK * dim * jnp.max(jnp.abs(gq)) * jnp.max(jnp.abs(gk)) * (dim ** -0.5 * LOG2E)
        return b.reshape(1, 1).astype(jnp.float32)

    return {
        "fox_bound": logit_bound(p["fox_q_norm"], p["fox_k_norm"], FOX_DIM),
        "dsa_bound": logit_bound(p["dsa_q_norm"], p["dsa_k_norm"], DSA_DIM),
        "ln1_g": p["ln1_g"][None, :], "ln2_g": p["ln2_g"][None, :],
        "w_tm": w_tm, "w_fm": w_fm,
        "fox_b": lane_pad(p["fox_b_f"]),
        "fq_gain": lane_pad(p["fox_q_norm"]), "fk_gain": lane_pad(p["fox_k_norm"]),
        "dk_gain": jnp.concatenate([p["dsa_k_norm"], jnp.ones((IDX_DIM,), jnp.float32),
                                    jnp.zeros((LANES - DSA_DIM - IDX_DIM,), jnp.float32)])[None, :],
        "dqa": consts["c64t"] * (g * scale)[:, None], "dqb": consts["s64t"] * (g_sw * scale)[:, None],
        "ret_gain": p["ret_out_norm"][:, None, :],
        "w_zg": cols(O_ZG, N_BRANCH * D_MODEL).astype(bf),
        "w_a": p["w_fox_out"].astype(bf), "w_b": p["w_dsa_out"].astype(bf), "w_c": p["w_ret_out"].astype(bf),
        "w_o": p["w_o"].astype(bf),
        "w_g": p["w_ffn_in"][:, :FFN_HIDDEN].astype(bf), "w_u": p["w_ffn_in"][:, FFN_HIDDEN:].astype(bf),
        "w_d": p["w_ffn_out"].astype(bf),
    }


def _layer(x2d, lw, consts, batch, seq):
    (fq, fk, fvt, dk, rq, rk, rv, rg, qt, iqt, vt, iwt, rkt) = _inproj(x2d, lw, consts, batch, seq)
    ya = _fox(fq, fk, fvt, batch, seq)
    yb = _dsa(lw["dsa_bound"], dk, vt, qt, iqt, iwt, consts["ltri_strict"], batch, seq)
    yc = _ret(rq, rk, rkt, rv, rg, consts, lw["ret_gain"], batch, seq)
    x2d = _merge(x2d, ya, yb, yc, lw)
    return _ffn(x2d, lw)


def kernel(x, ln1_g, w_in, fox_b_f, fox_q_norm, fox_k_norm, dsa_q_norm, dsa_k_norm, ret_out_norm,
           w_fox_out, w_dsa_out, w_ret_out, w_o, ln2_g, w_ffn_in, w_ffn_out):
    batch, seq, _ = x.shape
    depth = w_in.shape[0]
    consts = _constants(seq)
    params = dict(ln1_g=ln1_g, w_in=w_in, fox_b_f=fox_b_f, fox_q_norm=fox_q_norm, fox_k_norm=fox_k_norm,
                  dsa_q_norm=dsa_q_norm, dsa_k_norm=dsa_k_norm, ret_out_norm=ret_out_norm,
                  w_fox_out=w_fox_out, w_dsa_out=w_dsa_out, w_ret_out=w_ret_out, w_o=w_o,
                  ln2_g=ln2_g, w_ffn_in=w_ffn_in, w_ffn_out=w_ffn_out)
    x2d = x.reshape(batch * seq, D_MODEL)
    for layer in range(depth):
        lw = _layer_weights({k: v[layer] for k, v in params.items()}, consts)
        x2d = _layer(x2d, lw, consts, batch, seq)
    return x2d.reshape(batch, seq, D_MODEL)
```

```python
import functools
import math

import jax
import jax.numpy as jnp
import numpy as np
from jax import lax
from jax.experimental import pallas as pl
from jax.experimental.pallas import tpu as pltpu

D_MODEL = 1024
FOX_HEADS = 8
FOX_DIM = 64
DSA_HEADS = 8
DSA_DIM = 64
IDX_HEADS = 8
IDX_DIM = 32
DSA_MAX_TOPK = 256
RET_HEADS = 4
RET_QK_DIM = 64
RET_V_DIM = 128
RET_CHUNK = 128
FFN_HIDDEN = 2816
ROPE_THETA = 10000.0
NORM_EPS = 1e-6
N_BRANCH = 3

FOX_W = FOX_HEADS * FOX_DIM
DSA_W = DSA_HEADS * DSA_DIM
IDX_W = IDX_HEADS * IDX_DIM
RET_QK_W = RET_HEADS * RET_QK_DIM
RET_V_W = RET_HEADS * RET_V_DIM
IN_SIZES = (FOX_W, FOX_W, FOX_W, FOX_HEADS,
            DSA_W, DSA_DIM, DSA_DIM, IDX_W, IDX_DIM, IDX_HEADS,
            RET_QK_W, RET_QK_W, RET_V_W, RET_V_W,
            N_BRANCH * D_MODEL)
IN_OFFS = tuple(int(v) for v in np.cumsum((0,) + IN_SIZES))
(O_FQ, O_FK, O_FV, O_FF, O_DQ, O_DK, O_DV, O_IQ, O_IK, O_IW,
 O_RQ, O_RK, O_RV, O_RG, O_ZG, _) = IN_OFFS

LANES = 128
SUBLANES = 8
BF16_ROWS = 16
VMEM_LIMIT = 56 * 1024 * 1024
NEG_BIG = -1e30
LOG2E = math.log2(math.e)
BOUND_SLACK = 1.02
MIN_DENOM = 2.0 ** -100
HALF_RANGE = 2 ** 15

T_FQ = 0
T_FK = T_FQ + FOX_W
T_FF = T_FK + FOX_W
T_DK = T_FF + LANES
T_RQ = T_DK + LANES
T_RK = T_RQ + RET_QK_W
T_RV = T_RK + RET_QK_W
T_RG = T_RV + RET_V_W
T_COLS = T_RG + RET_V_W
F_DQ = 0
F_IQ = F_DQ + DSA_W
F_DV = F_IQ + IDX_W
F_IW = F_DV + DSA_DIM
F_RK = F_IW + 16
F_FV = F_RK + RET_QK_W
F_ROWS = F_FV + FOX_W

TM_IN = 512
KC = TM_IN
HALF = KC // 2
RB = 32
TQ_FOX = 512
FOX_VROWS = 2 * FOX_DIM + BF16_ROWS
DSA_VROWS = DSA_DIM + BF16_ROWS
TQ_DSA = 256
RET_STEP = 4 * RET_CHUNK
TM_POST = 512


def _cparams(sem):
    return pltpu.CompilerParams(dimension_semantics=sem, vmem_limit_bytes=VMEM_LIMIT)


def _split3(v):
    hi = v.astype(jnp.bfloat16)
    r1 = v - hi.astype(jnp.float32)
    mid = r1.astype(jnp.bfloat16)
    lo = (r1 - mid.astype(jnp.float32)).astype(jnp.bfloat16)
    return hi, mid, lo


def _col_reduce(v, op):
    return op(v.reshape(v.shape[0] // SUBLANES, SUBLANES, v.shape[1]), axis=0)


def _attend(n_chunks, qk, mask, pv, sa_scr, sb_scr, pa_scr, pb_scr, acc_scr, den_row):
    def store_logits(s_ref, c, half):
        lo = 0
        for g in qk(c, half):
            s_ref[:, lo:lo + g.shape[1]] = g
            lo += g.shape[1]

    def probs(s_ref, c, half, p_ref, shift):
        for r0 in range(0, s_ref.shape[0], RB):
            t = mask(s_ref[r0:r0 + RB, :], c, half, r0)
            if shift is not None:
                t = t - shift
            p_ref[r0:r0 + RB, :] = jnp.exp2(t).astype(p_ref.dtype)

    def run(shift):
        acc_scr[...] = jnp.zeros_like(acc_scr)
        pb_scr[...] = jnp.zeros_like(pb_scr)
        store_logits(sa_scr, 0, 0)

        def chunk(c, carry):
            store_logits(sb_scr, c, 1)
            acc_scr[...] += pv(jnp.maximum(c - 1, 0), 1, pb_scr)
            probs(sa_scr, c, 0, pa_scr, shift)
            store_logits(sa_scr, jnp.minimum(c + 1, n_chunks - 1), 0)
            acc_scr[...] += pv(c, 0, pa_scr)
            probs(sb_scr, c, 1, pb_scr, shift)
            return carry

        lax.fori_loop(0, n_chunks, chunk, 0)
        acc_scr[...] += pv(n_chunks - 1, 1, pb_scr)

    run(None)

    @pl.when(jnp.logical_not(jnp.min(acc_scr[den_row:den_row + 1, :]) >= MIN_DENOM))
    def _():
        def col_max(c, mx):
            for half in range(2):
                store_logits(sa_scr, c, half)
                for r0 in range(0, sa_scr.shape[0], RB):
                    mx = jnp.maximum(mx, _col_reduce(mask(sa_scr[r0:r0 + RB, :], c, half, r0), jnp.max))
            return mx
        mx = lax.fori_loop(0, n_chunks, col_max,
                           jnp.full((SUBLANES, acc_scr.shape[1]), NEG_BIG, jnp.float32))
        run(jnp.max(mx, axis=0, keepdims=True))


def _inproj_kernel(bound_ref, x_ref, g_ref, wtm_ref, wfm_ref, fb_ref, fqg_ref, fkg_ref, dkg_ref,
                   ltri_ref, eq_ref, ek_ref, oneq_ref, onek_ref,
                   kc_ref, ks_ref, rc_ref, rs_ref,
                   dqa_ref, dqb_ref, iqa_ref, iqb_ref, rka_ref, rkb_ref,
                   fq_out, fk_out, fvt_out, dk_out, rq_out, rk_out, rv_out, rg_out,
                   qt_out, iqt_out, vt_out, iw_out, rkt_out,
                   carry_ref, *, tiles_per_seq):
    tm = x_ref.shape[0]
    i = pl.program_id(0)

    @pl.when(i % tiles_per_seq == 0)
    def _():
        carry_ref[...] = jnp.zeros_like(carry_ref)

    x = x_ref[...]
    ms = jnp.mean(x * x, axis=-1, keepdims=True)
    h = (x * lax.rsqrt(ms + NORM_EPS) * g_ref[...]).astype(jnp.bfloat16)

    def tm_dot(lo, width):
        return jnp.dot(h, wtm_ref[:, lo:lo + width], preferred_element_type=jnp.float32)

    lane = lax.broadcasted_iota(jnp.int32, (tm, LANES), 1)

    ffb = tm_dot(T_FF, LANES) + fb_ref[...]
    lf = (jnp.minimum(ffb, 0.0) - jnp.log1p(jnp.exp(-jnp.abs(ffb)))) * LOG2E
    parts = jnp.concatenate(_split3(lf), axis=1)
    cs = jnp.dot(ltri_ref[...], parts, preferred_element_type=jnp.float32)
    c = cs[:, :LANES] + cs[:, LANES:2 * LANES] + cs[:, 2 * LANES:] + carry_ref[...]
    carry_ref[...] = c[tm - 1:tm, :]
    cparts_q = jnp.concatenate(_split3(c - bound_ref[0, 0]), axis=1)
    cparts_k = jnp.concatenate(_split3(c), axis=1)
    scat_q = jnp.dot(cparts_q, eq_ref[...], preferred_element_type=jnp.float32) + oneq_ref[...]
    scat_k = jnp.dot(cparts_k, ek_ref[...], preferred_element_type=jnp.float32) + onek_ref[...]

    for (lo, gain_ref, scat, out, scale) in ((T_FQ, fqg_ref, scat_q, fq_out, FOX_DIM ** -0.5 * LOG2E),
                                             (T_FK, fkg_ref, scat_k, fk_out, 1.0)):
        z = tm_dot(lo, FOX_W)
        for hd in range(FOX_HEADS):
            blk = z[:, (hd // 2) * LANES:(hd // 2 + 1) * LANES]
            if hd % 2:
                blk = pltpu.roll(blk, FOX_DIM, 1)
            ss = jnp.sum(jnp.where(lane < FOX_DIM, blk * blk, 0.0), axis=-1, keepdims=True) * (1.0 / FOX_DIM)
            nb = blk * lax.rsqrt(ss + NORM_EPS) * (gain_ref[...] * scale)
            out[:, hd * LANES:(hd + 1) * LANES] = (
                nb + scat[:, hd * LANES:(hd + 1) * LANES]).astype(out.dtype)

    zk = tm_dot(T_DK, LANES)
    ssk = jnp.sum(jnp.where(lane < DSA_DIM, zk * zk, 0.0), axis=-1, keepdims=True) * (1.0 / DSA_DIM)
    nk = zk * jnp.where(lane < DSA_DIM, lax.rsqrt(ssk + NORM_EPS), 1.0) * dkg_ref[...]
    partner = jnp.where(
        lane < 32, pltpu.roll(nk, LANES - 32, 1),
        jnp.where(lane < 64, pltpu.roll(nk, 32, 1),
                  jnp.where(lane < 80, pltpu.roll(nk, LANES - 16, 1), pltpu.roll(nk, 16, 1))))
    dk_out[...] = (nk * kc_ref[...] + partner * ks_ref[...]).astype(dk_out.dtype)

    first_half = (lane % RET_QK_DIM) < (RET_QK_DIM // 2)
    for (lo, out, scale) in ((T_RQ, rq_out, RET_QK_DIM ** -0.5), (T_RK, rk_out, 1.0)):
        z = tm_dot(lo, RET_QK_W)
        for j in range(RET_QK_W // LANES):
            blk = z[:, j * LANES:(j + 1) * LANES]
            pr = jnp.where(first_half, pltpu.roll(blk, LANES - 32, 1), pltpu.roll(blk, 32, 1))
            out[:, j * LANES:(j + 1) * LANES] = (
                (blk * rc_ref[...] + pr * rs_ref[...]) * scale).astype(out.dtype)

    rv_out[...] = tm_dot(T_RV, RET_V_W).astype(rv_out.dtype)
    rg_out[...] = tm_dot(T_RG, RET_V_W)

    zt = lax.dot_general(wfm_ref[...], h, (((1,), (1,)), ((), ())),
                         preferred_element_type=jnp.float32)
    nq = tm // TQ_DSA

    def swap_halves(v):
        half = v.shape[0] // 2
        return jnp.concatenate([v[half:], v[:half]], axis=0)

    def ones_row_block(rows, dtype):
        first = lax.broadcasted_iota(jnp.int32, (rows, tm), 0) == 0
        return jnp.where(first, 1.0, 0.0).astype(dtype)

    zeros_q = jnp.zeros((LANES - DSA_DIM, DSA_HEADS * TQ_DSA), qt_out.dtype)
    zeros_i0 = jnp.zeros((DSA_DIM, IDX_HEADS * TQ_DSA), iqt_out.dtype)
    zeros_i1 = jnp.zeros((LANES - DSA_DIM - IDX_DIM, IDX_HEADS * TQ_DSA), iqt_out.dtype)
    for j in range(nq):
        qt_out[j, DSA_DIM:, :] = zeros_q
        iqt_out[j, :DSA_DIM, :] = zeros_i0
        iqt_out[j, DSA_DIM + IDX_DIM:, :] = zeros_i1
    for hd in range(DSA_HEADS):
        xh = zt[F_DQ + hd * DSA_DIM:F_DQ + (hd + 1) * DSA_DIM, :]
        r = lax.rsqrt(jnp.sum(xh * xh, axis=0, keepdims=True) * (1.0 / DSA_DIM) + NORM_EPS)
        o = ((xh * dqa_ref[...] + swap_halves(xh) * dqb_ref[...]) * r).astype(qt_out.dtype)
        for j in range(nq):
            qt_out[j, :DSA_DIM, hd * TQ_DSA:(hd + 1) * TQ_DSA] = o[:, j * TQ_DSA:(j + 1) * TQ_DSA]
    for hd in range(IDX_HEADS):
        xh = zt[F_IQ + hd * IDX_DIM:F_IQ + (hd + 1) * IDX_DIM, :]
        o = (xh * iqa_ref[...] + swap_halves(xh) * iqb_ref[...]).astype(iqt_out.dtype)
        for j in range(nq):
            iqt_out[j, DSA_DIM:DSA_DIM + IDX_DIM, hd * TQ_DSA:(hd + 1) * TQ_DSA] = (
                o[:, j * TQ_DSA:(j + 1) * TQ_DSA])
    vt_out[0, 0, :DSA_DIM, :] = zt[F_DV:F_DV + DSA_DIM, :].astype(vt_out.dtype)
    vt_out[0, 0, DSA_DIM:, :] = ones_row_block(BF16_ROWS, vt_out.dtype)
    iw_out[0] = zt[F_IW:F_IW + IDX_HEADS, :] * ((IDX_DIM * IDX_HEADS) ** -0.5)
    for hd in range(RET_HEADS):
        xh = zt[F_RK + hd * RET_QK_DIM:F_RK + (hd + 1) * RET_QK_DIM, :]
        rkt_out[0, hd * RET_QK_DIM:(hd + 1) * RET_QK_DIM, :] = (
            xh * rka_ref[...] + swap_halves(xh) * rkb_ref[...])
    for hp in range(FOX_HEADS // 2):
        fvt_out[0, hp, 0, :2 * FOX_DIM, :] = (
            zt[F_FV + hp * 2 * FOX_DIM:F_FV + (hp + 1) * 2 * FOX_DIM, :].astype(fvt_out.dtype))
        fvt_out[0, hp, 0, 2 * FOX_DIM:, :] = ones_row_block(BF16_ROWS, fvt_out.dtype)


def _inproj(x2d, lw, consts, batch, seq):
    n = x2d.shape[0]
    tm = min(TM_IN, seq)
    tps = seq // tm
    nqt = tm // TQ_DSA
    grid = (n // tm,)
    bf = jnp.bfloat16

    def full(a):
        return pl.BlockSpec(a.shape, lambda i: (0,) * a.ndim)

    def tok(width):
        return pl.BlockSpec((tm, width), lambda i: (i, 0))

    def pos_tm(width):
        return pl.BlockSpec((tm, width), lambda i: (i % tps, 0))

    def pos_fm(rows):
        return pl.BlockSpec((rows, tm), lambda i: (0, i % tps))

    def fm_out(rows):
        return pl.BlockSpec((1, rows, tm), lambda i: (i // tps, 0, i % tps))

    in_arrays = [lw["fox_bound"], x2d, lw["ln1_g"], lw["w_tm"], lw["w_fm"], lw["fox_b"], lw["fq_gain"], lw["fk_gain"],
                 lw["dk_gain"], consts["ltri"], consts["eq"], consts["ek"], consts["oneq"], consts["onek"],
                 consts["kc"], consts["ks"], consts["rc"], consts["rs"],
                 lw["dqa"], lw["dqb"], consts["iqa"], consts["iqb"], consts["rka"], consts["rkb"]]
    in_specs = [pl.BlockSpec(memory_space=pltpu.SMEM),
                tok(D_MODEL), full(lw["ln1_g"]), full(lw["w_tm"]), full(lw["w_fm"]), full(lw["fox_b"]),
                full(lw["fq_gain"]), full(lw["fk_gain"]), full(lw["dk_gain"]),
                full(consts["ltri"]), full(consts["eq"]), full(consts["ek"]),
                full(consts["oneq"]), full(consts["onek"]),
                pos_tm(LANES), pos_tm(LANES), pos_tm(LANES), pos_tm(LANES),
                pos_fm(DSA_DIM), pos_fm(DSA_DIM), pos_fm(IDX_DIM), pos_fm(IDX_DIM),
                pos_fm(RET_QK_DIM), pos_fm(RET_QK_DIM)]
    out_shape = [
        jax.ShapeDtypeStruct((n, FOX_HEADS * LANES), bf),
        jax.ShapeDtypeStruct((n, FOX_HEADS * LANES), bf),
        jax.ShapeDtypeStruct((batch, FOX_HEADS // 2, tps, FOX_VROWS, tm), bf),
        jax.ShapeDtypeStruct((n, LANES), bf),
        jax.ShapeDtypeStruct((n, RET_QK_W), bf),
        jax.ShapeDtypeStruct((n, RET_QK_W), bf),
        jax.ShapeDtypeStruct((n, RET_V_W), bf),
        jax.ShapeDtypeStruct((n, RET_V_W), jnp.float32),
        jax.ShapeDtypeStruct((n // TQ_DSA, LANES, DSA_HEADS * TQ_DSA), bf),
        jax.ShapeDtypeStruct((n // TQ_DSA, LANES, IDX_HEADS * TQ_DSA), bf),
        jax.ShapeDtypeStruct((batch, tps, DSA_VROWS, tm), bf),
        jax.ShapeDtypeStruct((batch, IDX_HEADS, seq), jnp.float32),
        jax.ShapeDtypeStruct((batch, RET_QK_W, seq), jnp.float32),
    ]
    out_specs = [tok(FOX_HEADS * LANES), tok(FOX_HEADS * LANES),
                 pl.BlockSpec((1, FOX_HEADS // 2, 1, FOX_VROWS, tm), lambda i: (i // tps, 0, i % tps, 0, 0)),
                 tok(LANES), tok(RET_QK_W), tok(RET_QK_W), tok(RET_V_W), tok(RET_V_W),
                 pl.BlockSpec((nqt, LANES, DSA_HEADS * TQ_DSA), lambda i: (i, 0, 0)),
                 pl.BlockSpec((nqt, LANES, IDX_HEADS * TQ_DSA), lambda i: (i, 0, 0)),
                 pl.BlockSpec((1, 1, DSA_VROWS, tm), lambda i: (i // tps, i % tps, 0, 0)),
                 fm_out(IDX_HEADS), fm_out(RET_QK_W)]
    return pl.pallas_call(
        functools.partial(_inproj_kernel, tiles_per_seq=tps),
        grid=grid, in_specs=in_specs, out_specs=out_specs, out_shape=out_shape,
        scratch_shapes=[pltpu.VMEM((1, LANES), jnp.float32)],
        compiler_params=_cparams(("arbitrary",)), name="inproj",
    )(*in_arrays)


def _fox_kernel(q_ref, k_ref, vt_ref, o_ref, sa_scr, sb_scr, pa_scr, pb_scr, acc_scr):
    tq = q_ref.shape[0]
    qi = pl.program_id(2)
    n_chunks = (qi * tq + tq + KC - 1) // KC
    nt = (((1,), (1,)), ((), ()))

    qcol = qi * tq + lax.broadcasted_iota(jnp.int32, (RB, tq), 1)
    qcol = jnp.concatenate([qcol, qcol], axis=1)
    krow = lax.broadcasted_iota(jnp.int32, (RB, 2 * tq), 0)

    def qk(c, half):
        off = pl.multiple_of(c * KC + half * HALF, HALF)
        return [lax.dot_general(k_ref[pl.ds(off, HALF), hh * LANES:(hh + 1) * LANES],
                                q_ref[:, hh * LANES:(hh + 1) * LANES], nt,
                                preferred_element_type=jnp.float32) for hh in range(2)]

    def mask(t, c, half, r0):
        return jnp.where(krow + (c * KC + half * HALF + r0) <= qcol, t, NEG_BIG)

    def pv(c, half, p_ref):
        vt = vt_ref[0, 0, c, :, half * HALF:(half + 1) * HALF]
        return jnp.concatenate(
            [jnp.dot(vt, p_ref[:, hh * tq:(hh + 1) * tq], preferred_element_type=jnp.float32)
             for hh in range(2)], axis=1)

    _attend(n_chunks, qk, mask, pv, sa_scr, sb_scr, pa_scr, pb_scr, acc_scr, 2 * FOX_DIM)
    acc = acc_scr[...]
    den = acc[2 * FOX_DIM:2 * FOX_DIM + 1, :]
    out_t = jnp.concatenate([acc[:FOX_DIM, :tq] / den[:, :tq],
                             acc[FOX_DIM:2 * FOX_DIM, tq:] / den[:, tq:]], axis=0)
    for j in range(tq // LANES):
        o_ref[j * LANES:(j + 1) * LANES, :] = out_t[:, j * LANES:(j + 1) * LANES].T.astype(o_ref.dtype)


def _fox(fq, fk, fvt, batch, seq):
    tq = min(TQ_FOX, seq)
    nq = seq // tq
    width = 2 * tq
    return pl.pallas_call(
        _fox_kernel, grid=(batch, FOX_HEADS // 2, nq),
        in_specs=[pl.BlockSpec((tq, 2 * LANES), lambda b, hp, qi: (b * nq + qi, hp)),
                  pl.BlockSpec((seq, 2 * LANES), lambda b, hp, qi: (b, hp)),
                  pl.BlockSpec((1, 1) + fvt.shape[2:], lambda b, hp, qi: (b, hp, 0, 0, 0))],
        out_specs=pl.BlockSpec((tq, LANES), lambda b, hp, qi: (b * nq + qi, hp)),
        out_shape=jax.ShapeDtypeStruct((batch * seq, FOX_W), jnp.bfloat16),
        scratch_shapes=[pltpu.VMEM((HALF, width), jnp.float32), pltpu.VMEM((HALF, width), jnp.float32),
                        pltpu.VMEM((HALF, width), jnp.bfloat16), pltpu.VMEM((HALF, width), jnp.bfloat16),
                        pltpu.VMEM((FOX_VROWS, width), jnp.float32)],
        compiler_params=_cparams(("parallel", "parallel", "arbitrary")), name="fox",
    )(fq, fk, fvt)


def _dsa_kernel(bound_ref, k_ref, vt_ref, qt_ref, iqt_ref, iw_ref, ltri_ref, o_ref,
                key_scr, hi_scr, lo_scr, bias_scr, sa_scr, sb_scr, pa_scr, pb_scr, acc_scr, *, topk):
    tq = TQ_DSA
    nh = DSA_HEADS
    selected_bias = -bound_ref[0, 0]
    qi = pl.program_id(1)
    n_chunks = (qi * tq + tq + KC - 1) // KC

    qpos = qi * tq + lax.broadcasted_iota(jnp.int32, (KC, tq), 1)
    krow = lax.broadcasted_iota(jnp.int32, (KC, tq), 0)

    def score_chunk(c, carry):
        off = pl.multiple_of(c * KC, KC)
        rel = jnp.dot(k_ref[pl.ds(off, KC), :], iqt_ref[0], preferred_element_type=jnp.float32)
        score = jnp.maximum(rel[:, :tq], 0.0) * iw_ref[0, 0:1, :]
        for hd in range(1, IDX_HEADS):
            score = score + jnp.maximum(rel[:, hd * tq:(hd + 1) * tq], 0.0) * iw_ref[0, hd:hd + 1, :]
        score = jnp.where(score == 0.0, 0.0, score)
        score = jnp.where(krow + off <= qpos, score, -jnp.inf)
        bits = pltpu.bitcast(score, jnp.int32)
        key = bits ^ ((bits >> 31) & jnp.int32(0x7FFFFFFF))
        key_scr[pl.ds(off, KC), :] = key
        hi_scr[pl.ds(off, KC), :] = (key >> 16).astype(jnp.int16)
        lo_scr[pl.ds(off, KC), :] = ((key & jnp.int32(0xFFFF)) - HALF_RANGE).astype(jnp.int16)
        return carry

    lax.fori_loop(0, n_chunks, score_chunk, 0)

    def count16_ge(plane_scr, thr):
        thr16 = thr.astype(jnp.int16)

        def body(c, acc):
            off = pl.multiple_of(c * KC, KC)
            hit = jnp.where(plane_scr[pl.ds(off, KC), :] >= thr16, jnp.int16(1), jnp.int16(0))
            parts = [hit[r0:r0 + BF16_ROWS] for r0 in range(0, KC, BF16_ROWS)]
            while len(parts) > 1:
                parts = [parts[i] + parts[i + 1] for i in range(0, len(parts), 2)]
            return acc + parts[0]
        acc = lax.fori_loop(0, n_chunks, body, jnp.zeros((BF16_ROWS, tq), jnp.int16))
        return jnp.sum(acc.astype(jnp.int32), axis=0, keepdims=True)

    def kth_largest16(plane_scr, kth):
        def bit_step(b, thr):
            bit = jnp.left_shift(jnp.int32(1), 15 - b)
            cand = jnp.where(b == 0, jnp.zeros_like(thr), thr | bit)
            return jnp.where(count16_ge(plane_scr, cand) >= kth, cand, thr)
        return lax.fori_loop(0, 16, bit_step, jnp.full((1, tq), -HALF_RANGE, jnp.int32))

    def count_ge(thr):
        def body(c, acc):
            off = pl.multiple_of(c * KC, KC)
            hit = jnp.where(key_scr[pl.ds(off, KC), :] >= thr, 1, 0).astype(jnp.int32)
            return acc + _col_reduce(hit, jnp.sum)
        acc = lax.fori_loop(0, n_chunks, body, jnp.zeros((SUBLANES, tq), jnp.int32))
        return jnp.sum(acc, axis=0, keepdims=True)

    thr_hi = kth_largest16(hi_scr, topk)
    top_hi = HALF_RANGE - 1
    n_above = jnp.where(thr_hi == top_hi, 0, count16_ge(hi_scr, jnp.minimum(thr_hi + 1, top_hi)))
    thr_hi16 = thr_hi.astype(jnp.int16)

    def keep_candidates(c, carry):
        off = pl.multiple_of(c * KC, KC)
        lo_scr[pl.ds(off, KC), :] = jnp.where(hi_scr[pl.ds(off, KC), :] == thr_hi16,
                                              lo_scr[pl.ds(off, KC), :], jnp.int16(-HALF_RANGE))
        return carry

    lax.fori_loop(0, n_chunks, keep_candidates, 0)
    thr_lo = kth_largest16(lo_scr, topk - n_above)
    thr = thr_hi * (2 * HALF_RANGE) + (thr_lo + HALF_RANGE)
    has_ties = jnp.max(count_ge(thr)) > topk

    @pl.when(jnp.logical_not(has_ties))
    def _():
        def body(c, carry):
            off = pl.multiple_of(c * KC, KC)
            bias_scr[pl.ds(off, KC), :] = jnp.where(key_scr[pl.ds(off, KC), :] >= thr, selected_bias, NEG_BIG)
            return carry
        lax.fori_loop(0, n_chunks, body, 0)

    @pl.when(has_ties)
    def _():
        int_max = jnp.int32(2 ** 31 - 1)
        n_gt = jnp.where(thr == int_max, 0, count_ge(jnp.where(thr == int_max, thr, thr + 1)))
        need = (topk - n_gt).astype(jnp.float32)

        def body(c, seen):
            off = pl.multiple_of(c * KC, KC)
            ks = key_scr[pl.ds(off, KC), :]
            eq = ks == thr
            eqf = jnp.where(eq, 1.0, 0.0)
            before = jnp.dot(ltri_ref[...], eqf.astype(jnp.bfloat16),
                             preferred_element_type=jnp.float32) + seen
            sel = jnp.logical_or(ks > thr, jnp.logical_and(eq, before < need))
            sel = jnp.logical_and(sel, krow + off <= qpos)
            bias_scr[pl.ds(off, KC), :] = jnp.where(sel, selected_bias, NEG_BIG)
            return seen + jnp.sum(eqf, axis=0, keepdims=True)
        lax.fori_loop(0, n_chunks, body, jnp.zeros((1, tq), jnp.float32))

    def qk(c, half):
        off = pl.multiple_of(c * KC + half * HALF, HALF)
        return [jnp.dot(k_ref[pl.ds(off, HALF), :], qt_ref[0], preferred_element_type=jnp.float32)]

    def mask(t, c, half, r0):
        b = bias_scr[pl.ds(pl.multiple_of(c * KC + half * HALF + r0, RB), RB), :]
        return t + jnp.concatenate([b] * nh, axis=1)

    def pv(c, half, p_ref):
        return jnp.dot(vt_ref[0, c, :, half * HALF:(half + 1) * HALF], p_ref[...],
                       preferred_element_type=jnp.float32)

    _attend(n_chunks, qk, mask, pv, sa_scr, sb_scr, pa_scr, pb_scr, acc_scr, DSA_DIM)
    acc = acc_scr[...]
    out_t = acc[:DSA_DIM, :] / acc[DSA_DIM:DSA_DIM + 1, :]
    for hp in range(nh // 2):
        pair = jnp.concatenate([out_t[:, 2 * hp * tq:(2 * hp + 1) * tq],
                                out_t[:, (2 * hp + 1) * tq:(2 * hp + 2) * tq]], axis=0)
        o_ref[:, hp * LANES:(hp + 1) * LANES] = pair.T.astype(o_ref.dtype)


def _dsa(bound, dk, vt, qt, iqt, iwt, ltri_strict, batch, seq):
    nq = seq // TQ_DSA
    assert vt.shape[3] == KC and ltri_strict.shape[0] == KC and seq % KC == 0
    topk = min(DSA_MAX_TOPK, seq // 4)
    width = DSA_HEADS * TQ_DSA
    return pl.pallas_call(
        functools.partial(_dsa_kernel, topk=topk),
        grid=(batch, nq),
        in_specs=[pl.BlockSpec(memory_space=pltpu.SMEM),
                  pl.BlockSpec((seq, LANES), lambda b, qi: (b, 0)),
                  pl.BlockSpec((1,) + vt.shape[1:], lambda b, qi: (b, 0, 0, 0)),
                  pl.BlockSpec((1, LANES, width), lambda b, qi: (b * nq + qi, 0, 0)),
                  pl.BlockSpec((1, LANES, IDX_HEADS * TQ_DSA), lambda b, qi: (b * nq + qi, 0, 0)),
                  pl.BlockSpec((1, IDX_HEADS, TQ_DSA), lambda b, qi: (b, 0, qi)),
                  pl.BlockSpec(ltri_strict.shape, lambda b, qi: (0, 0))],
        out_specs=pl.BlockSpec((TQ_DSA, DSA_W), lambda b, qi: (b * nq + qi, 0)),
        out_shape=jax.ShapeDtypeStruct((batch * seq, DSA_W), jnp.bfloat16),
        scratch_shapes=[pltpu.VMEM((seq, TQ_DSA), jnp.int32),
                        pltpu.VMEM((seq, TQ_DSA), jnp.int16),
                        pltpu.VMEM((seq, TQ_DSA), jnp.int16),
                        pltpu.VMEM((seq, TQ_DSA), jnp.float32),
                        pltpu.VMEM((HALF, width), jnp.float32), pltpu.VMEM((HALF, width), jnp.float32),
                        pltpu.VMEM((HALF, width), jnp.bfloat16), pltpu.VMEM((HALF, width), jnp.bfloat16),
                        pltpu.VMEM((DSA_VROWS, width), jnp.float32)],
        compiler_params=_cparams(("parallel", "arbitrary")), name="dsa",
    )(bound, dk, vt, qt, iqt, iwt, ltri_strict)


def _ret_kernel(q_ref, k_ref, kt_ref, v_ref, g_ref, din_ref, dq_ref, dk_ref, dc_ref, gain_ref,
                o_ref, state_scr):
    t = pl.program_id(1)
    c = RET_CHUNK

    @pl.when(t == 0)
    def _():
        state_scr[...] = jnp.zeros_like(state_scr)

    for hd in range(RET_HEADS):
        state = state_scr[hd]
        for j in range(q_ref.shape[0] // c):
            rows = slice(j * c, (j + 1) * c)
            q = q_ref[rows, hd * RET_QK_DIM:(hd + 1) * RET_QK_DIM]
            k = k_ref[rows, hd * RET_QK_DIM:(hd + 1) * RET_QK_DIM]
            v = v_ref[rows, hd * RET_V_DIM:(hd + 1) * RET_V_DIM]
            attn = lax.dot_general(q, k, (((1,), (1,)), ((), ())),
                                   preferred_element_type=jnp.float32) * din_ref[hd]
            inner = jnp.dot(attn.astype(jnp.bfloat16), v, preferred_element_type=jnp.float32)
            cross = jnp.dot(q, state.astype(jnp.bfloat16),
                            preferred_element_type=jnp.float32) * dq_ref[hd]
            ktd = (kt_ref[0, hd * RET_QK_DIM:(hd + 1) * RET_QK_DIM, rows] * dk_ref[hd]).astype(jnp.bfloat16)
            state = dc_ref[hd] * state + jnp.dot(ktd, v, preferred_element_type=jnp.float32)
            y = inner + cross
            yn = y * lax.rsqrt(jnp.mean(y * y, axis=-1, keepdims=True) + NORM_EPS) * gain_ref[hd]
            gate = g_ref[rows, hd * RET_V_DIM:(hd + 1) * RET_V_DIM]
            o_ref[rows, hd * RET_V_DIM:(hd + 1) * RET_V_DIM] = (
                yn * (gate * jax.nn.sigmoid(gate))).astype(o_ref.dtype)
        state_scr[hd] = state


def _ret(rq, rk, rkt, rv, rg, consts, gain, batch, seq):
    c = min(RET_STEP, seq)
    n = seq // c

    def tok(width):
        return pl.BlockSpec((c, width), lambda b, t: (b * n + t, 0))

    def full(a):
        return pl.BlockSpec(a.shape, lambda b, t: (0,) * a.ndim)

    return pl.pallas_call(
        _ret_kernel, grid=(batch, n),
        in_specs=[tok(RET_QK_W), tok(RET_QK_W),
                  pl.BlockSpec((1, RET_QK_W, c), lambda b, t: (b, 0, t)),
                  tok(RET_V_W), tok(RET_V_W),
                  full(consts["ret_din"]), full(consts["ret_dq"]), full(consts["ret_dk"]),
                  full(consts["ret_dc"]), full(gain)],
        out_specs=tok(RET_V_W),
        out_shape=jax.ShapeDtypeStruct((batch * seq, RET_V_W), jnp.bfloat16),
        scratch_shapes=[pltpu.VMEM((RET_HEADS, RET_QK_DIM, RET_V_DIM), jnp.float32)],
        compiler_params=_cparams(("parallel", "arbitrary")), name="ret",
    )(rq, rk, rkt, rv, rg, consts["ret_din"], consts["ret_dq"], consts["ret_dk"], consts["ret_dc"], gain)


def _merge_kernel(x_ref, g_ref, ya_ref, yb_ref, yc_ref, wzg_ref, wa_ref, wb_ref, wc_ref, wo_ref, o_ref):
    x = x_ref[...]
    ms = jnp.mean(x * x, axis=-1, keepdims=True)
    h = (x * lax.rsqrt(ms + NORM_EPS) * g_ref[...]).astype(jnp.bfloat16)
    merged = None
    for j, (y_ref, w_ref) in enumerate(((ya_ref, wa_ref), (yb_ref, wb_ref), (yc_ref, wc_ref))):
        gate = jax.nn.sigmoid(jnp.dot(h, wzg_ref[:, j * D_MODEL:(j + 1) * D_MODEL],
                                      preferred_element_type=jnp.float32))
        term = gate * jnp.dot(y_ref[...], w_ref[...], preferred_element_type=jnp.float32)
        merged = term if merged is None else merged + term
    o_ref[...] = x + jnp.dot(merged.astype(jnp.bfloat16), wo_ref[...],
                             preferred_element_type=jnp.float32)


def _ffn_kernel(x_ref, g_ref, wg_ref, wu_ref, wd_ref, o_ref, *, chunk):
    x = x_ref[...]
    ms = jnp.mean(x * x, axis=-1, keepdims=True)
    h = (x * lax.rsqrt(ms + NORM_EPS) * g_ref[...]).astype(jnp.bfloat16)
    acc = x
    for lo in range(0, FFN_HIDDEN, chunk):
        gt = jnp.dot(h, wg_ref[:, lo:lo + chunk], preferred_element_type=jnp.float32)
        up = jnp.dot(h, wu_ref[:, lo:lo + chunk], preferred_element_type=jnp.float32)
        act = (gt * jax.nn.sigmoid(gt) * up).astype(jnp.bfloat16)
        acc = acc + jnp.dot(act, wd_ref[lo:lo + chunk, :], preferred_element_type=jnp.float32)
    o_ref[...] = acc


def _row_call(kernel, name, order):
    n = order[0][1].shape[0]
    tm = min(TM_POST, n)
    arrays, specs = [], []
    for kind, a in order:
        arrays.append(a)
        if kind == "row":
            specs.append(pl.BlockSpec((tm, a.shape[1]), lambda i: (i, 0)))
        else:
            specs.append(pl.BlockSpec(a.shape, lambda i, nd=a.ndim: (0,) * nd))
    return pl.pallas_call(
        kernel, grid=(n // tm,), in_specs=specs,
        out_specs=pl.BlockSpec((tm, D_MODEL), lambda i: (i, 0)),
        out_shape=jax.ShapeDtypeStruct((n, D_MODEL), jnp.float32),
        compiler_params=_cparams(("parallel",)), name=name,
    )(*arrays)


def _merge(x2d, ya, yb, yc, lw):
    order = [("row", x2d), ("full", lw["ln1_g"]), ("row", ya), ("row", yb), ("row", yc),
             ("full", lw["w_zg"]), ("full", lw["w_a"]), ("full", lw["w_b"]), ("full", lw["w_c"]),
             ("full", lw["w_o"])]
    return _row_call(_merge_kernel, "merge", order)


def _ffn(x2d, lw):
    order = [("row", x2d), ("full", lw["ln2_g"]), ("full", lw["w_g"]), ("full", lw["w_u"]),
             ("full", lw["w_d"])]
    return _row_call(functools.partial(_ffn_kernel, chunk=256), "ffn", order)


def _rope_cs(seq, dim):
    half = dim // 2
    inv_freq = ROPE_THETA ** (-jnp.arange(half, dtype=jnp.float32) / half)
    ang = jnp.arange(seq, dtype=jnp.float32)[:, None] * inv_freq[None, :]
    return jnp.cos(ang), jnp.sin(ang)


def _constants(seq):
    tm = min(TM_IN, seq)
    bf = jnp.bfloat16
    cos64, sin64 = _rope_cs(seq, 64)
    cos32, sin32 = _rope_cs(seq, 32)
    z32 = jnp.zeros((seq, 32), jnp.float32)
    c64 = jnp.concatenate([cos64, cos64], axis=1)
    s64 = jnp.concatenate([-sin64, sin64], axis=1)
    c32 = jnp.concatenate([cos32, cos32], axis=1)
    s32 = jnp.concatenate([-sin32, sin32], axis=1)
    consts = {
        "kc": jnp.concatenate([c64, c32, z32], axis=1), "ks": jnp.concatenate([s64, s32, z32], axis=1),
        "rc": jnp.concatenate([c64, c64], axis=1), "rs": jnp.concatenate([s64, s64], axis=1),
        "c64t": c64.T, "s64t": s64.T,
        "iqa": c32.T, "iqb": s32.T, "rka": c64.T, "rkb": s64.T,
        "ltri": jnp.tril(jnp.ones((tm, tm), jnp.float32)).astype(bf),
        "ltri_strict": jnp.tril(jnp.ones((KC, KC), jnp.float32), -1).astype(bf),
    }
    eq = np.zeros((3 * LANES, FOX_HEADS * LANES), np.float32)
    ek = np.zeros((3 * LANES, FOX_HEADS * LANES), np.float32)
    oneq = np.zeros((1, FOX_HEADS * LANES), np.float32)
    onek = np.zeros((1, FOX_HEADS * LANES), np.float32)
    for hd in range(FOX_HEADS):
        base = hd * LANES + FOX_DIM
        for part in range(3):
            eq[part * LANES + hd, base + part] = 1.0
            ek[part * LANES + hd, base + 3 + part] = -1.0
            oneq[0, base + 3 + part] = 1.0
            onek[0, base + part] = 1.0
    consts.update(eq=jnp.asarray(eq, bf), ek=jnp.asarray(ek, bf), oneq=jnp.asarray(oneq), onek=jnp.asarray(onek))
    log_g = jnp.log1p(-(2.0 ** (-5.0 - jnp.arange(RET_HEADS, dtype=jnp.float32))))
    pos = jnp.arange(RET_CHUNK, dtype=jnp.float32)
    diff = pos[:, None] - pos[None, :]
    din = jnp.where(diff >= 0, jnp.exp(jnp.maximum(diff, 0.0)[None] * log_g[:, None, None]), 0.0)
    dq = jnp.exp((pos + 1.0)[None] * log_g[:, None])
    dk = jnp.exp((RET_CHUNK - 1.0 - pos)[None] * log_g[:, None])
    dc = jnp.exp(RET_CHUNK * log_g)
    consts.update(ret_din=din, ret_dq=dq[:, :, None], ret_dk=dk[:, None, :],
                  ret_dc=jnp.broadcast_to(dc[:, None, None], (RET_HEADS, 1, LANES)))
    return consts


def _layer_weights(p, consts):
    bf = jnp.bfloat16
    w_in = p["w_in"]

    def cols(off, size):
        return w_in[:, off:off + size]

    zeros = lambda n: jnp.zeros((D_MODEL, n), jnp.float32)
    w_tm = jnp.concatenate([
        cols(O_FQ, FOX_W), cols(O_FK, FOX_W),
        cols(O_FF, FOX_HEADS), zeros(LANES - FOX_HEADS),
        cols(O_DK, DSA_DIM), cols(O_IK, IDX_DIM), zeros(LANES - DSA_DIM - IDX_DIM),
        cols(O_RQ, RET_QK_W), cols(O_RK, RET_QK_W), cols(O_RV, RET_V_W), cols(O_RG, RET_V_W),
    ], axis=1).astype(bf)
    w_fm = jnp.concatenate([
        cols(O_DQ, DSA_W), cols(O_IQ, IDX_W), cols(O_DV, DSA_DIM),
        cols(O_IW, IDX_HEADS), zeros(16 - IDX_HEADS), cols(O_RK, RET_QK_W), cols(O_FV, FOX_W),
    ], axis=1).T.astype(bf)

    def lane_pad(v, fill=0.0):
        return jnp.concatenate([v, jnp.full((LANES - v.shape[0],), fill, jnp.float32)])[None, :]

    g = p["dsa_q_norm"]
    g_sw = jnp.concatenate([g[DSA_DIM // 2:], g[:DSA_DIM // 2]])
    scale = DSA_DIM ** -0.5 * LOG2E

    def logit_bound(gq, gk, dim):
        b = BOUND_SLACK * dim * jnp.max(jnp.abs(gq)) * jnp.max(jnp.abs(gk)) * (dim ** -0.5 * LOG2E)
        return b.reshape(1, 1).astype(jnp.float32)

    return {
        "fox_bound": logit_bound(p["fox_q_norm"], p["fox_k_norm"], FOX_DIM),
        "dsa_bound": logit_bound(p["dsa_q_norm"], p["dsa_k_norm"], DSA_DIM),
        "ln1_g": p["ln1_g"][None, :], "ln2_g": p["ln2_g"][None, :],
        "w_tm": w_tm, "w_fm": w_fm,
        "fox_b": lane_pad(p["fox_b_f"]),
        "fq_gain": lane_pad(p["fox_q_norm"]), "fk_gain": lane_pad(p["fox_k_norm"]),
        "dk_gain": jnp.concatenate([p["dsa_k_norm"], jnp.ones((IDX_DIM,), jnp.float32),
                                    jnp.zeros((LANES - DSA_DIM - IDX_DIM,), jnp.float32)])[None, :],
        "dqa": consts["c64t"] * (g * scale)[:, None], "dqb": consts["s64t"] * (g_sw * scale)[:, None],
        "ret_gain": p["ret_out_norm"][:, None, :],
        "w_zg": cols(O_ZG, N_BRANCH * D_MODEL).astype(bf),
        "w_a": p["w_fox_out"].astype(bf), "w_b": p["w_dsa_out"].astype(bf), "w_c": p["w_ret_out"].astype(bf),
        "w_o": p["w_o"].astype(bf),
        "w_g": p["w_ffn_in"][:, :FFN_HIDDEN].astype(bf), "w_u": p["w_ffn_in"][:, FFN_HIDDEN:].astype(bf),
        "w_d": p["w_ffn_out"].astype(bf),
    }


def _layer(x2d, lw, consts, batch, seq):
    (fq, fk, fvt, dk, rq, rk, rv, rg, qt, iqt, vt, iwt, rkt) = _inproj(x2d, lw, consts, batch, seq)
    ya = _fox(fq, fk, fvt, batch, seq)
    yb = _dsa(lw["dsa_bound"], dk, vt, qt, iqt, iwt, consts["ltri_strict"], batch, seq)
    yc = _ret(rq, rk, rkt, rv, rg, consts, lw["ret_gain"], batch, seq)
    x2d = _merge(x2d, ya, yb, yc, lw)
    return _ffn(x2d, lw)


def kernel(x, ln1_g, w_in, fox_b_f, fox_q_norm, fox_k_norm, dsa_q_norm, dsa_k_norm, ret_out_norm,
           w_fox_out, w_dsa_out, w_ret_out, w_o, ln2_g, w_ffn_in, w_ffn_out):
    batch, seq, _ = x.shape
    depth = w_in.shape[0]
    consts = _constants(seq)
    params = dict(ln1_g=ln1_g, w_in=w_in, fox_b_f=fox_b_f, fox_q_norm=fox_q_norm, fox_k_norm=fox_k_norm,
                  dsa_q_norm=dsa_q_norm, dsa_k_norm=dsa_k_norm, ret_out_norm=ret_out_norm,
                  w_fox_out=w_fox_out, w_dsa_out=w_dsa_out, w_ret_out=w_ret_out, w_o=w_o,
                  ln2_g=ln2_g, w_ffn_in=w_ffn_in, w_ffn_out=w_ffn_out)
    x2d = x.reshape(batch * seq, D_MODEL)
    for layer in range(depth):
        lw = _layer_weights({k: v[layer] for k, v in params.items()}, consts)
        x2d = _layer(x2d, lw, consts, batch, seq)
    return x2d.reshape(batch, seq, D_MODEL)
```

```python
import functools
import math

import jax
import jax.numpy as jnp
import numpy as np
from jax import lax
from jax.experimental import pallas as pl
from jax.experimental.pallas import tpu as pltpu

D_MODEL = 1024
FOX_HEADS = 8
FOX_DIM = 64
DSA_HEADS = 8
DSA_DIM = 64
IDX_HEADS = 8
IDX_DIM = 32
DSA_MAX_TOPK = 256
RET_HEADS = 4
RET_QK_DIM = 64
RET_V_DIM = 128
RET_CHUNK = 128
FFN_HIDDEN = 2816
ROPE_THETA = 10000.0
NORM_EPS = 1e-6
N_BRANCH = 3

FOX_W = FOX_HEADS * FOX_DIM
DSA_W = DSA_HEADS * DSA_DIM
IDX_W = IDX_HEADS * IDX_DIM
RET_QK_W = RET_HEADS * RET_QK_DIM
RET_V_W = RET_HEADS * RET_V_DIM
IN_SIZES = (FOX_W, FOX_W, FOX_W, FOX_HEADS,
            DSA_W, DSA_DIM, DSA_DIM, IDX_W, IDX_DIM, IDX_HEADS,
            RET_QK_W, RET_QK_W, RET_V_W, RET_V_W,
            N_BRANCH * D_MODEL)
IN_OFFS = tuple(int(v) for v in np.cumsum((0,) + IN_SIZES))
(O_FQ, O_FK, O_FV, O_FF, O_DQ, O_DK, O_DV, O_IQ, O_IK, O_IW,
 O_RQ, O_RK, O_RV, O_RG, O_ZG, _) = IN_OFFS

LANES = 128
SUBLANES = 8
BF16_ROWS = 16
VMEM_LIMIT = 56 * 1024 * 1024
NEG_BIG = -1e30
LOG2E = math.log2(math.e)
BOUND_SLACK = 1.02
MIN_DENOM = 2.0 ** -100
HALF_RANGE = 2 ** 15

T_FQ = 0
T_FK = T_FQ + FOX_W
T_FF = T_FK + FOX_W
T_DK = T_FF + LANES
T_RQ = T_DK + LANES
T_RK = T_RQ + RET_QK_W
T_RV = T_RK + RET_QK_W
T_RG = T_RV + RET_V_W
T_COLS = T_RG + RET_V_W
F_DQ = 0
F_IQ = F_DQ + DSA_W
F_DV = F_IQ + IDX_W
F_IW = F_DV + DSA_DIM
F_RK = F_IW + 16
F_FV = F_RK + RET_QK_W
F_ROWS = F_FV + FOX_W

TM_IN = 512
KC = TM_IN
HALF = KC // 2
RB = 32
TQ_FOX = 512
FOX_GROUP = 4
FOX_VROWS = 2 * FOX_DIM + BF16_ROWS
DSA_VROWS = DSA_DIM + BF16_ROWS
TQ_DSA = 256
RET_STEP = 4 * RET_CHUNK
TM_POST = 512


def _cparams(sem):
    return pltpu.CompilerParams(dimension_semantics=sem, vmem_limit_bytes=VMEM_LIMIT)


def _split3(v):
    hi = v.astype(jnp.bfloat16)
    r1 = v - hi.astype(jnp.float32)
    mid = r1.astype(jnp.bfloat16)
    lo = (r1 - mid.astype(jnp.float32)).astype(jnp.bfloat16)
    return hi, mid, lo


def _col_reduce(v, op):
    return op(v.reshape(v.shape[0] // SUBLANES, SUBLANES, v.shape[1]), axis=0)


def _attend(n_chunks, qk, mask, pv, sa_scr, sb_scr, pa_scr, pb_scr, acc_scr, den_row):
    def store_logits(s_ref, c, half):
        lo = 0
        for g in qk(c, half):
            s_ref[:, lo:lo + g.shape[1]] = g
            lo += g.shape[1]

    def probs(s_ref, c, half, p_ref, shift):
        for r0 in range(0, s_ref.shape[0], RB):
            t = mask(s_ref[r0:r0 + RB, :], c, half, r0)
            if shift is not None:
                t = t - shift
            p_ref[r0:r0 + RB, :] = jnp.exp2(t).astype(p_ref.dtype)

    def run(shift):
        acc_scr[...] = jnp.zeros_like(acc_scr)
        pb_scr[...] = jnp.zeros_like(pb_scr)
        store_logits(sa_scr, 0, 0)

        def chunk(c, carry):
            store_logits(sb_scr, c, 1)
            acc_scr[...] += pv(jnp.maximum(c - 1, 0), 1, pb_scr)
            probs(sa_scr, c, 0, pa_scr, shift)
            store_logits(sa_scr, jnp.minimum(c + 1, n_chunks - 1), 0)
            acc_scr[...] += pv(c, 0, pa_scr)
            probs(sb_scr, c, 1, pb_scr, shift)
            return carry

        lax.fori_loop(0, n_chunks, chunk, 0)
        acc_scr[...] += pv(n_chunks - 1, 1, pb_scr)

    run(None)

    @pl.when(jnp.logical_not(jnp.min(acc_scr[den_row:den_row + 1, :]) >= MIN_DENOM))
    def _():
        def col_max(c, mx):
            for half in range(2):
                store_logits(sa_scr, c, half)
                for r0 in range(0, sa_scr.shape[0], RB):
                    mx = jnp.maximum(mx, _col_reduce(mask(sa_scr[r0:r0 + RB, :], c, half, r0), jnp.max))
            return mx
        mx = lax.fori_loop(0, n_chunks, col_max,
                           jnp.full((SUBLANES, acc_scr.shape[1]), NEG_BIG, jnp.float32))
        run(jnp.max(mx, axis=0, keepdims=True))


def _inproj_kernel(bound_ref, x_ref, g_ref, wtm_ref, wfm_ref, fb_ref, fqg_ref, fkg_ref, dkg_ref,
                   ltri_ref, eq_ref, ek_ref, oneq_ref, onek_ref,
                   kc_ref, ks_ref, rc_ref, rs_ref,
                   dqa_ref, dqb_ref, iqa_ref, iqb_ref, rka_ref, rkb_ref,
                   fq_out, fk_out, fvt_out, dk_out, rq_out, rk_out, rv_out, rg_out,
                   qt_out, iqt_out, vt_out, iw_out, rkt_out,
                   carry_ref, *, tiles_per_seq):
    tm = x_ref.shape[0]
    i = pl.program_id(0)

    @pl.when(i % tiles_per_seq == 0)
    def _():
        carry_ref[...] = jnp.zeros_like(carry_ref)

    x = x_ref[...]
    ms = jnp.mean(x * x, axis=-1, keepdims=True)
    h = (x * lax.rsqrt(ms + NORM_EPS) * g_ref[...]).astype(jnp.bfloat16)

    def tm_dot(lo, width):
        return jnp.dot(h, wtm_ref[:, lo:lo + width], preferred_element_type=jnp.float32)

    lane = lax.broadcasted_iota(jnp.int32, (tm, LANES), 1)

    ffb = tm_dot(T_FF, LANES) + fb_ref[...]
    lf = (jnp.minimum(ffb, 0.0) - jnp.log1p(jnp.exp(-jnp.abs(ffb)))) * LOG2E
    parts = jnp.concatenate(_split3(lf), axis=1)
    cs = jnp.dot(ltri_ref[...], parts, preferred_element_type=jnp.float32)
    c = cs[:, :LANES] + cs[:, LANES:2 * LANES] + cs[:, 2 * LANES:] + carry_ref[...]
    carry_ref[...] = c[tm - 1:tm, :]
    cparts_q = jnp.concatenate(_split3(c - bound_ref[0, 0]), axis=1)
    cparts_k = jnp.concatenate(_split3(c), axis=1)
    scat_q = jnp.dot(cparts_q, eq_ref[...], preferred_element_type=jnp.float32) + oneq_ref[...]
    scat_k = jnp.dot(cparts_k, ek_ref[...], preferred_element_type=jnp.float32) + onek_ref[...]

    for (lo, gain_ref, scat, out, scale) in ((T_FQ, fqg_ref, scat_q, fq_out, FOX_DIM ** -0.5 * LOG2E),
                                             (T_FK, fkg_ref, scat_k, fk_out, 1.0)):
        z = tm_dot(lo, FOX_W)
        for hd in range(FOX_HEADS):
            blk = z[:, (hd // 2) * LANES:(hd // 2 + 1) * LANES]
            if hd % 2:
                blk = pltpu.roll(blk, FOX_DIM, 1)
            ss = jnp.sum(jnp.where(lane < FOX_DIM, blk * blk, 0.0), axis=-1, keepdims=True) * (1.0 / FOX_DIM)
            nb = blk * lax.rsqrt(ss + NORM_EPS) * (gain_ref[...] * scale)
            out[:, hd * LANES:(hd + 1) * LANES] = (
                nb + scat[:, hd * LANES:(hd + 1) * LANES]).astype(out.dtype)

    zk = tm_dot(T_DK, LANES)
    ssk = jnp.sum(jnp.where(lane < DSA_DIM, zk * zk, 0.0), axis=-1, keepdims=True) * (1.0 / DSA_DIM)
    nk = zk * jnp.where(lane < DSA_DIM, lax.rsqrt(ssk + NORM_EPS), 1.0) * dkg_ref[...]
    partner = jnp.where(
        lane < 32, pltpu.roll(nk, LANES - 32, 1),
        jnp.where(lane < 64, pltpu.roll(nk, 32, 1),
                  jnp.where(lane < 80, pltpu.roll(nk, LANES - 16, 1), pltpu.roll(nk, 16, 1))))
    dk_out[...] = (nk * kc_ref[...] + partner * ks_ref[...]).astype(dk_out.dtype)

    first_half = (lane % RET_QK_DIM) < (RET_QK_DIM // 2)
    for (lo, out, scale) in ((T_RQ, rq_out, RET_QK_DIM ** -0.5), (T_RK, rk_out, 1.0)):
        z = tm_dot(lo, RET_QK_W)
        for j in range(RET_QK_W // LANES):
            blk = z[:, j * LANES:(j + 1) * LANES]
            pr = jnp.where(first_half, pltpu.roll(blk, LANES - 32, 1), pltpu.roll(blk, 32, 1))
            out[:, j * LANES:(j + 1) * LANES] = (
                (blk * rc_ref[...] + pr * rs_ref[...]) * scale).astype(out.dtype)

    rv_out[...] = tm_dot(T_RV, RET_V_W).astype(rv_out.dtype)
    rg_out[...] = tm_dot(T_RG, RET_V_W)

    zt = lax.dot_general(wfm_ref[...], h, (((1,), (1,)), ((), ())),
                         preferred_element_type=jnp.float32)
    nq = tm // TQ_DSA

    def swap_halves(v):
        half = v.shape[0] // 2
        return jnp.concatenate([v[half:], v[:half]], axis=0)

    def ones_row_block(rows, dtype):
        first = lax.broadcasted_iota(jnp.int32, (rows, tm), 0) == 0
        return jnp.where(first, 1.0, 0.0).astype(dtype)

    zeros_q = jnp.zeros((LANES - DSA_DIM, DSA_HEADS * TQ_DSA), qt_out.dtype)
    zeros_i0 = jnp.zeros((DSA_DIM, IDX_HEADS * TQ_DSA), iqt_out.dtype)
    zeros_i1 = jnp.zeros((LANES - DSA_DIM - IDX_DIM, IDX_HEADS * TQ_DSA), iqt_out.dtype)
    for j in range(nq):
        qt_out[j, DSA_DIM:, :] = zeros_q
        iqt_out[j, :DSA_DIM, :] = zeros_i0
        iqt_out[j, DSA_DIM + IDX_DIM:, :] = zeros_i1
    for hd in range(DSA_HEADS):
        xh = zt[F_DQ + hd * DSA_DIM:F_DQ + (hd + 1) * DSA_DIM, :]
        r = lax.rsqrt(jnp.sum(xh * xh, axis=0, keepdims=True) * (1.0 / DSA_DIM) + NORM_EPS)
        o = ((xh * dqa_ref[...] + swap_halves(xh) * dqb_ref[...]) * r).astype(qt_out.dtype)
        for j in range(nq):
            qt_out[j, :DSA_DIM, hd * TQ_DSA:(hd + 1) * TQ_DSA] = o[:, j * TQ_DSA:(j + 1) * TQ_DSA]
    for hd in range(IDX_HEADS):
        xh = zt[F_IQ + hd * IDX_DIM:F_IQ + (hd + 1) * IDX_DIM, :]
        o = (xh * iqa_ref[...] + swap_halves(xh) * iqb_ref[...]).astype(iqt_out.dtype)
        for j in range(nq):
            iqt_out[j, DSA_DIM:DSA_DIM + IDX_DIM, hd * TQ_DSA:(hd + 1) * TQ_DSA] = (
                o[:, j * TQ_DSA:(j + 1) * TQ_DSA])
    vt_out[0, 0, :DSA_DIM, :] = zt[F_DV:F_DV + DSA_DIM, :].astype(vt_out.dtype)
    vt_out[0, 0, DSA_DIM:, :] = ones_row_block(BF16_ROWS, vt_out.dtype)
    iw_out[0] = zt[F_IW:F_IW + IDX_HEADS, :] * ((IDX_DIM * IDX_HEADS) ** -0.5)
    for hd in range(RET_HEADS):
        xh = zt[F_RK + hd * RET_QK_DIM:F_RK + (hd + 1) * RET_QK_DIM, :]
        rkt_out[0, hd * RET_QK_DIM:(hd + 1) * RET_QK_DIM, :] = (
            xh * rka_ref[...] + swap_halves(xh) * rkb_ref[...])
    for hp in range(FOX_HEADS // 2):
        fvt_out[0, hp, 0, :2 * FOX_DIM, :] = (
            zt[F_FV + hp * 2 * FOX_DIM:F_FV + (hp + 1) * 2 * FOX_DIM, :].astype(fvt_out.dtype))
        fvt_out[0, hp, 0, 2 * FOX_DIM:, :] = ones_row_block(BF16_ROWS, fvt_out.dtype)


def _inproj(x2d, lw, consts, batch, seq):
    n = x2d.shape[0]
    tm = min(TM_IN, seq)
    tps = seq // tm
    nqt = tm // TQ_DSA
    grid = (n // tm,)
    bf = jnp.bfloat16

    def full(a):
        return pl.BlockSpec(a.shape, lambda i: (0,) * a.ndim)

    def tok(width):
        return pl.BlockSpec((tm, width), lambda i: (i, 0))

    def pos_tm(width):
        return pl.BlockSpec((tm, width), lambda i: (i % tps, 0))

    def pos_fm(rows):
        return pl.BlockSpec((rows, tm), lambda i: (0, i % tps))

    def fm_out(rows):
        return pl.BlockSpec((1, rows, tm), lambda i: (i // tps, 0, i % tps))

    in_arrays = [lw["fox_bound"], x2d, lw["ln1_g"], lw["w_tm"], lw["w_fm"], lw["fox_b"], lw["fq_gain"], lw["fk_gain"],
                 lw["dk_gain"], consts["ltri"], consts["eq"], consts["ek"], consts["oneq"], consts["onek"],
                 consts["kc"], consts["ks"], consts["rc"], consts["rs"],
                 lw["dqa"], lw["dqb"], consts["iqa"], consts["iqb"], consts["rka"], consts["rkb"]]
    in_specs = [pl.BlockSpec(memory_space=pltpu.SMEM),
                tok(D_MODEL), full(lw["ln1_g"]), full(lw["w_tm"]), full(lw["w_fm"]), full(lw["fox_b"]),
                full(lw["fq_gain"]), full(lw["fk_gain"]), full(lw["dk_gain"]),
                full(consts["ltri"]), full(consts["eq"]), full(consts["ek"]),
                full(consts["oneq"]), full(consts["onek"]),
                pos_tm(LANES), pos_tm(LANES), pos_tm(LANES), pos_tm(LANES),
                pos_fm(DSA_DIM), pos_fm(DSA_DIM), pos_fm(IDX_DIM), pos_fm(IDX_DIM),
                pos_fm(RET_QK_DIM), pos_fm(RET_QK_DIM)]
    out_shape = [
        jax.ShapeDtypeStruct((n, FOX_HEADS * LANES), bf),
        jax.ShapeDtypeStruct((n, FOX_HEADS * LANES), bf),
        jax.ShapeDtypeStruct((batch, FOX_HEADS // 2, tps, FOX_VROWS, tm), bf),
        jax.ShapeDtypeStruct((n, LANES), bf),
        jax.ShapeDtypeStruct((n, RET_QK_W), bf),
        jax.ShapeDtypeStruct((n, RET_QK_W), bf),
        jax.ShapeDtypeStruct((n, RET_V_W), bf),
        jax.ShapeDtypeStruct((n, RET_V_W), jnp.float32),
        jax.ShapeDtypeStruct((n // TQ_DSA, LANES, DSA_HEADS * TQ_DSA), bf),
        jax.ShapeDtypeStruct((n // TQ_DSA, LANES, IDX_HEADS * TQ_DSA), bf),
        jax.ShapeDtypeStruct((batch, tps, DSA_VROWS, tm), bf),
        jax.ShapeDtypeStruct((batch, IDX_HEADS, seq), jnp.float32),
        jax.ShapeDtypeStruct((batch, RET_QK_W, seq), jnp.float32),
    ]
    out_specs = [tok(FOX_HEADS * LANES), tok(FOX_HEADS * LANES),
                 pl.BlockSpec((1, FOX_HEADS // 2, 1, FOX_VROWS, tm), lambda i: (i // tps, 0, i % tps, 0, 0)),
                 tok(LANES), tok(RET_QK_W), tok(RET_QK_W), tok(RET_V_W), tok(RET_V_W),
                 pl.BlockSpec((nqt, LANES, DSA_HEADS * TQ_DSA), lambda i: (i, 0, 0)),
                 pl.BlockSpec((nqt, LANES, IDX_HEADS * TQ_DSA), lambda i: (i, 0, 0)),
                 pl.BlockSpec((1, 1, DSA_VROWS, tm), lambda i: (i // tps, i % tps, 0, 0)),
                 fm_out(IDX_HEADS), fm_out(RET_QK_W)]
    return pl.pallas_call(
        functools.partial(_inproj_kernel, tiles_per_seq=tps),
        grid=grid, in_specs=in_specs, out_specs=out_specs, out_shape=out_shape,
        scratch_shapes=[pltpu.VMEM((1, LANES), jnp.float32)],
        compiler_params=_cparams(("arbitrary",)), name="inproj",
    )(*in_arrays)


def _fox_kernel(q_ref, k_ref, vt_ref, o_ref, sa_scr, sb_scr, pa_scr, pb_scr, acc_scr):
    tq = q_ref.shape[0]
    qi = pl.program_id(2)
    n_chunks = (qi * tq + tq + KC - 1) // KC
    nt = (((1,), (1,)), ((), ()))

    nheads = FOX_GROUP
    qcol = qi * tq + lax.broadcasted_iota(jnp.int32, (RB, tq), 1)
    qcol = jnp.concatenate([qcol] * nheads, axis=1)
    krow = lax.broadcasted_iota(jnp.int32, (RB, nheads * tq), 0)

    def qk(c, half):
        off = pl.multiple_of(c * KC + half * HALF, HALF)
        return [lax.dot_general(k_ref[pl.ds(off, HALF), hh * LANES:(hh + 1) * LANES],
                                q_ref[:, hh * LANES:(hh + 1) * LANES], nt,
                                preferred_element_type=jnp.float32) for hh in range(nheads)]

    def mask(t, c, half, r0):
        return jnp.where(krow + (c * KC + half * HALF + r0) <= qcol, t, NEG_BIG)

    def pv(c, half, p_ref):
        return jnp.concatenate(
            [jnp.dot(vt_ref[0, hh // 2, c, :, half * HALF:(half + 1) * HALF],
                     p_ref[:, hh * tq:(hh + 1) * tq], preferred_element_type=jnp.float32)
             for hh in range(nheads)], axis=1)

    _attend(n_chunks, qk, mask, pv, sa_scr, sb_scr, pa_scr, pb_scr, acc_scr, 2 * FOX_DIM)
    acc = acc_scr[...]
    den = acc[2 * FOX_DIM:2 * FOX_DIM + 1, :]
    for hp in range(nheads // 2):
        ev, od = 2 * hp * tq, (2 * hp + 1) * tq
        out_t = jnp.concatenate([acc[:FOX_DIM, ev:ev + tq] / den[:, ev:ev + tq],
                                 acc[FOX_DIM:2 * FOX_DIM, od:od + tq] / den[:, od:od + tq]], axis=0)
        for j in range(tq // LANES):
            o_ref[j * LANES:(j + 1) * LANES, hp * LANES:(hp + 1) * LANES] = (
                out_t[:, j * LANES:(j + 1) * LANES].T.astype(o_ref.dtype))


def _fox(fq, fk, fvt, batch, seq):
    tq = min(TQ_FOX, seq)
    nq = seq // tq
    g = FOX_GROUP
    width = g * tq
    return pl.pallas_call(
        _fox_kernel, grid=(batch, FOX_HEADS // g, nq),
        in_specs=[pl.BlockSpec((tq, g * LANES), lambda b, hg, qi: (b * nq + qi, hg)),
                  pl.BlockSpec((seq, g * LANES), lambda b, hg, qi: (b, hg)),
                  pl.BlockSpec((1, g // 2) + fvt.shape[2:], lambda b, hg, qi: (b, hg, 0, 0, 0))],
        out_specs=pl.BlockSpec((tq, g * FOX_DIM), lambda b, hg, qi: (b * nq + qi, hg)),
        out_shape=jax.ShapeDtypeStruct((batch * seq, FOX_W), jnp.bfloat16),
        scratch_shapes=[pltpu.VMEM((HALF, width), jnp.float32), pltpu.VMEM((HALF, width), jnp.float32),
                        pltpu.VMEM((HALF, width), jnp.bfloat16), pltpu.VMEM((HALF, width), jnp.bfloat16),
                        pltpu.VMEM((FOX_VROWS, width), jnp.float32)],
        compiler_params=_cparams(("parallel", "parallel", "arbitrary")), name="fox",
    )(fq, fk, fvt)


def _dsa_kernel(bound_ref, k_ref, vt_ref, qt_ref, iqt_ref, iw_ref, ltri_ref, o_ref,
                key_scr, hi_scr, lo_scr, bias_scr, sa_scr, sb_scr, pa_scr, pb_scr, acc_scr, *, topk):
    tq = TQ_DSA
    nh = DSA_HEADS
    selected_bias = -bound_ref[0, 0]
    qi = pl.program_id(1)
    n_chunks = (qi * tq + tq + KC - 1) // KC

    qpos = qi * tq + lax.broadcasted_iota(jnp.int32, (KC, tq), 1)
    krow = lax.broadcasted_iota(jnp.int32, (KC, tq), 0)

    def score_chunk(c, carry):
        off = pl.multiple_of(c * KC, KC)
        rel = jnp.dot(k_ref[pl.ds(off, KC), :], iqt_ref[0], preferred_element_type=jnp.float32)
        score = jnp.maximum(rel[:, :tq], 0.0) * iw_ref[0, 0:1, :]
        for hd in range(1, IDX_HEADS):
            score = score + jnp.maximum(rel[:, hd * tq:(hd + 1) * tq], 0.0) * iw_ref[0, hd:hd + 1, :]
        score = jnp.where(score == 0.0, 0.0, score)
        score = jnp.where(krow + off <= qpos, score, -jnp.inf)
        bits = pltpu.bitcast(score, jnp.int32)
        key = bits ^ ((bits >> 31) & jnp.int32(0x7FFFFFFF))
        key_scr[pl.ds(off, KC), :] = key
        hi_scr[pl.ds(off, KC), :] = (key >> 16).astype(jnp.int16)
        lo_scr[pl.ds(off, KC), :] = ((key & jnp.int32(0xFFFF)) - HALF_RANGE).astype(jnp.int16)
        return carry

    lax.fori_loop(0, n_chunks, score_chunk, 0)

    def count16_ge(plane_scr, thr):
        thr16 = thr.astype(jnp.int16)

        def body(c, acc):
            off = pl.multiple_of(c * KC, KC)
            hit = jnp.where(plane_scr[pl.ds(off, KC), :] >= thr16, jnp.int16(1), jnp.int16(0))
            parts = [hit[r0:r0 + BF16_ROWS] for r0 in range(0, KC, BF16_ROWS)]
            while len(parts) > 1:
                parts = [parts[i] + parts[i + 1] for i in range(0, len(parts), 2)]
            return acc + parts[0]
        acc = lax.fori_loop(0, n_chunks, body, jnp.zeros((BF16_ROWS, tq), jnp.int16))
        return jnp.sum(acc.astype(jnp.int32), axis=0, keepdims=True)

    def kth_largest16(plane_scr, kth):
        def bit_step(b, carry):
            thr, cnt = carry
            bit = jnp.left_shift(jnp.int32(1), 15 - b)
            cand = jnp.where(b == 0, jnp.zeros_like(thr), thr | bit)
            cand_cnt = count16_ge(plane_scr, cand)
            take = cand_cnt >= kth
            return jnp.where(take, cand, thr), jnp.where(take, cand_cnt, cnt)
        return lax.fori_loop(0, 16, bit_step, (jnp.full((1, tq), -HALF_RANGE, jnp.int32),
                                               jnp.full((1, tq), n_chunks * KC, jnp.int32)))

    def count_ge(thr):
        def body(c, acc):
            off = pl.multiple_of(c * KC, KC)
            hit = jnp.where(key_scr[pl.ds(off, KC), :] >= thr, 1, 0).astype(jnp.int32)
            return acc + _col_reduce(hit, jnp.sum)
        acc = lax.fori_loop(0, n_chunks, body, jnp.zeros((SUBLANES, tq), jnp.int32))
        return jnp.sum(acc, axis=0, keepdims=True)

    thr_hi, cnt_hi = kth_largest16(hi_scr, topk)
    top_hi = HALF_RANGE - 1
    n_above = jnp.where(thr_hi == top_hi, 0, count16_ge(hi_scr, jnp.minimum(thr_hi + 1, top_hi)))
    thr_hi16 = thr_hi.astype(jnp.int16)

    def keep_candidates(c, carry):
        off = pl.multiple_of(c * KC, KC)
        lo_scr[pl.ds(off, KC), :] = jnp.where(hi_scr[pl.ds(off, KC), :] == thr_hi16,
                                              lo_scr[pl.ds(off, KC), :], jnp.int16(-HALF_RANGE))
        return carry

    lax.fori_loop(0, n_chunks, keep_candidates, 0)
    thr_lo, cnt_lo = kth_largest16(lo_scr, topk - n_above)
    thr = thr_hi * (2 * HALF_RANGE) + (thr_lo + HALF_RANGE)
    n_ge = jnp.where(thr_lo > -HALF_RANGE, n_above + cnt_lo, cnt_hi)
    has_ties = jnp.max(n_ge) > topk

    @pl.when(jnp.logical_not(has_ties))
    def _():
        def body(c, carry):
            off = pl.multiple_of(c * KC, KC)
            bias_scr[pl.ds(off, KC), :] = jnp.where(key_scr[pl.ds(off, KC), :] >= thr, selected_bias, NEG_BIG)
            return carry
        lax.fori_loop(0, n_chunks, body, 0)

    @pl.when(has_ties)
    def _():
        int_max = jnp.int32(2 ** 31 - 1)
        n_gt = jnp.where(thr == int_max, 0, count_ge(jnp.where(thr == int_max, thr, thr + 1)))
        need = (topk - n_gt).astype(jnp.float32)

        def body(c, seen):
            off = pl.multiple_of(c * KC, KC)
            ks = key_scr[pl.ds(off, KC), :]
            eq = ks == thr
            eqf = jnp.where(eq, 1.0, 0.0)
            before = jnp.dot(ltri_ref[...], eqf.astype(jnp.bfloat16),
                             preferred_element_type=jnp.float32) + seen
            sel = jnp.logical_or(ks > thr, jnp.logical_and(eq, before < need))
            sel = jnp.logical_and(sel, krow + off <= qpos)
            bias_scr[pl.ds(off, KC), :] = jnp.where(sel, selected_bias, NEG_BIG)
            return seen + jnp.sum(eqf, axis=0, keepdims=True)
        lax.fori_loop(0, n_chunks, body, jnp.zeros((1, tq), jnp.float32))

    def qk(c, half):
        off = pl.multiple_of(c * KC + half * HALF, HALF)
        return [jnp.dot(k_ref[pl.ds(off, HALF), :], qt_ref[0], preferred_element_type=jnp.float32)]

    def mask(t, c, half, r0):
        b = bias_scr[pl.ds(pl.multiple_of(c * KC + half * HALF + r0, RB), RB), :]
        return t + jnp.concatenate([b] * nh, axis=1)

    def pv(c, half, p_ref):
        return jnp.dot(vt_ref[0, c, :, half * HALF:(half + 1) * HALF], p_ref[...],
                       preferred_element_type=jnp.float32)

    _attend(n_chunks, qk, mask, pv, sa_scr, sb_scr, pa_scr, pb_scr, acc_scr, DSA_DIM)
    acc = acc_scr[...]
    out_t = acc[:DSA_DIM, :] / acc[DSA_DIM:DSA_DIM + 1, :]
    for hp in range(nh // 2):
        pair = jnp.concatenate([out_t[:, 2 * hp * tq:(2 * hp + 1) * tq],
                                out_t[:, (2 * hp + 1) * tq:(2 * hp + 2) * tq]], axis=0)
        o_ref[:, hp * LANES:(hp + 1) * LANES] = pair.T.astype(o_ref.dtype)


def _dsa(bound, dk, vt, qt, iqt, iwt, ltri_strict, batch, seq):
    nq = seq // TQ_DSA
    assert vt.shape[3] == KC and ltri_strict.shape[0] == KC and seq % KC == 0
    topk = min(DSA_MAX_TOPK, seq // 4)
    width = DSA_HEADS * TQ_DSA
    return pl.pallas_call(
        functools.partial(_dsa_kernel, topk=topk),
        grid=(batch, nq),
        in_specs=[pl.BlockSpec(memory_space=pltpu.SMEM),
                  pl.BlockSpec((seq, LANES), lambda b, qi: (b, 0)),
                  pl.BlockSpec((1,) + vt.shape[1:], lambda b, qi: (b, 0, 0, 0)),
                  pl.BlockSpec((1, LANES, width), lambda b, qi: (b * nq + qi, 0, 0)),
                  pl.BlockSpec((1, LANES, IDX_HEADS * TQ_DSA), lambda b, qi: (b * nq + qi, 0, 0)),
                  pl.BlockSpec((1, IDX_HEADS, TQ_DSA), lambda b, qi: (b, 0, qi)),
                  pl.BlockSpec(ltri_strict.shape, lambda b, qi: (0, 0))],
        out_specs=pl.BlockSpec((TQ_DSA, DSA_W), lambda b, qi: (b * nq + qi, 0)),
        out_shape=jax.ShapeDtypeStruct((batch * seq, DSA_W), jnp.bfloat16),
        scratch_shapes=[pltpu.VMEM((seq, TQ_DSA), jnp.int32),
                        pltpu.VMEM((seq, TQ_DSA), jnp.int16),
                        pltpu.VMEM((seq, TQ_DSA), jnp.int16),
                        pltpu.VMEM((seq, TQ_DSA), jnp.float32),
                        pltpu.VMEM((HALF, width), jnp.float32), pltpu.VMEM((HALF, width), jnp.float32),
                        pltpu.VMEM((HALF, width), jnp.bfloat16), pltpu.VMEM((HALF, width), jnp.bfloat16),
                        pltpu.VMEM((DSA_VROWS, width), jnp.float32)],
        compiler_params=_cparams(("parallel", "arbitrary")), name="dsa",
    )(bound, dk, vt, qt, iqt, iwt, ltri_strict)


def _ret_kernel(q_ref, k_ref, kt_ref, v_ref, g_ref, din_ref, dq_ref, dk_ref, dc_ref, gain_ref,
                o_ref, state_scr):
    t = pl.program_id(1)
    c = RET_CHUNK

    @pl.when(t == 0)
    def _():
        state_scr[...] = jnp.zeros_like(state_scr)

    for hd in range(RET_HEADS):
        state = state_scr[hd]
        for j in range(q_ref.shape[0] // c):
            rows = slice(j * c, (j + 1) * c)
            q = q_ref[rows, hd * RET_QK_DIM:(hd + 1) * RET_QK_DIM]
            k = k_ref[rows, hd * RET_QK_DIM:(hd + 1) * RET_QK_DIM]
            v = v_ref[rows, hd * RET_V_DIM:(hd + 1) * RET_V_DIM]
            attn = lax.dot_general(q, k, (((1,), (1,)), ((), ())),
                                   preferred_element_type=jnp.float32) * din_ref[hd]
            inner = jnp.dot(attn.astype(jnp.bfloat16), v, preferred_element_type=jnp.float32)
            cross = jnp.dot(q, state.astype(jnp.bfloat16),
                            preferred_element_type=jnp.float32) * dq_ref[hd]
            ktd = (kt_ref[0, hd * RET_QK_DIM:(hd + 1) * RET_QK_DIM, rows] * dk_ref[hd]).astype(jnp.bfloat16)
            state = dc_ref[hd] * state + jnp.dot(ktd, v, preferred_element_type=jnp.float32)
            y = inner + cross
            yn = y * lax.rsqrt(jnp.mean(y * y, axis=-1, keepdims=True) + NORM_EPS) * gain_ref[hd]
            gate = g_ref[rows, hd * RET_V_DIM:(hd + 1) * RET_V_DIM]
            o_ref[rows, hd * RET_V_DIM:(hd + 1) * RET_V_DIM] = (
                yn * (gate * jax.nn.sigmoid(gate))).astype(o_ref.dtype)
        state_scr[hd] = state


def _ret(rq, rk, rkt, rv, rg, consts, gain, batch, seq):
    c = min(RET_STEP, seq)
    n = seq // c

    def tok(width):
        return pl.BlockSpec((c, width), lambda b, t: (b * n + t, 0))

    def full(a):
        return pl.BlockSpec(a.shape, lambda b, t: (0,) * a.ndim)

    return pl.pallas_call(
        _ret_kernel, grid=(batch, n),
        in_specs=[tok(RET_QK_W), tok(RET_QK_W),
                  pl.BlockSpec((1, RET_QK_W, c), lambda b, t: (b, 0, t)),
                  tok(RET_V_W), tok(RET_V_W),
                  full(consts["ret_din"]), full(consts["ret_dq"]), full(consts["ret_dk"]),
                  full(consts["ret_dc"]), full(gain)],
        out_specs=tok(RET_V_W),
        out_shape=jax.ShapeDtypeStruct((batch * seq, RET_V_W), jnp.bfloat16),
        scratch_shapes=[pltpu.VMEM((RET_HEADS, RET_QK_DIM, RET_V_DIM), jnp.float32)],
        compiler_params=_cparams(("parallel", "arbitrary")), name="ret",
    )(rq, rk, rkt, rv, rg, consts["ret_din"], consts["ret_dq"], consts["ret_dk"], consts["ret_dc"], gain)


def _merge_kernel(x_ref, g_ref, ya_ref, yb_ref, yc_ref, wzg_ref, wa_ref, wb_ref, wc_ref, wo_ref, o_ref):
    x = x_ref[...]
    ms = jnp.mean(x * x, axis=-1, keepdims=True)
    h = (x * lax.rsqrt(ms + NORM_EPS) * g_ref[...]).astype(jnp.bfloat16)
    merged = None
    for j, (y_ref, w_ref) in enumerate(((ya_ref, wa_ref), (yb_ref, wb_ref), (yc_ref, wc_ref))):
        gate = jax.nn.sigmoid(jnp.dot(h, wzg_ref[:, j * D_MODEL:(j + 1) * D_MODEL],
                                      preferred_element_type=jnp.float32))
        term = gate * jnp.dot(y_ref[...], w_ref[...], preferred_element_type=jnp.float32)
        merged = term if merged is None else merged + term
    o_ref[...] = x + jnp.dot(merged.astype(jnp.bfloat16), wo_ref[...],
                             preferred_element_type=jnp.float32)


def _ffn_kernel(x_ref, g_ref, wg_ref, wu_ref, wd_ref, o_ref, *, chunk):
    x = x_ref[...]
    ms = jnp.mean(x * x, axis=-1, keepdims=True)
    h = (x * lax.rsqrt(ms + NORM_EPS) * g_ref[...]).astype(jnp.bfloat16)
    acc = x
    for lo in range(0, FFN_HIDDEN, chunk):
        gt = jnp.dot(h, wg_ref[:, lo:lo + chunk], preferred_element_type=jnp.float32)
        up = jnp.dot(h, wu_ref[:, lo:lo + chunk], preferred_element_type=jnp.float32)
        act = (gt * jax.nn.sigmoid(gt) * up).astype(jnp.bfloat16)
        acc = acc + jnp.dot(act, wd_ref[lo:lo + chunk, :], preferred_element_type=jnp.float32)
    o_ref[...] = acc


def _row_call(kernel, name, order):
    n = order[0][1].shape[0]
    tm = min(TM_POST, n)
    arrays, specs = [], []
    for kind, a in order:
        arrays.append(a)
        if kind == "row":
            specs.append(pl.BlockSpec((tm, a.shape[1]), lambda i: (i, 0)))
        else:
            specs.append(pl.BlockSpec(a.shape, lambda i, nd=a.ndim: (0,) * nd))
    return pl.pallas_call(
        kernel, grid=(n // tm,), in_specs=specs,
        out_specs=pl.BlockSpec((tm, D_MODEL), lambda i: (i, 0)),
        out_shape=jax.ShapeDtypeStruct((n, D_MODEL), jnp.float32),
        compiler_params=_cparams(("parallel",)), name=name,
    )(*arrays)


def _merge(x2d, ya, yb, yc, lw):
    order = [("row", x2d), ("full", lw["ln1_g"]), ("row", ya), ("row", yb), ("row", yc),
             ("full", lw["w_zg"]), ("full", lw["w_a"]), ("full", lw["w_b"]), ("full", lw["w_c"]),
             ("full", lw["w_o"])]
    return _row_call(_merge_kernel, "merge", order)


def _ffn(x2d, lw):
    order = [("row", x2d), ("full", lw["ln2_g"]), ("full", lw["w_g"]), ("full", lw["w_u"]),
             ("full", lw["w_d"])]
    return _row_call(functools.partial(_ffn_kernel, chunk=256), "ffn", order)


def _rope_cs(seq, dim):
    half = dim // 2
    inv_freq = ROPE_THETA ** (-jnp.arange(half, dtype=jnp.float32) / half)
    ang = jnp.arange(seq, dtype=jnp.float32)[:, None] * inv_freq[None, :]
    return jnp.cos(ang), jnp.sin(ang)


def _constants(seq):
    tm = min(TM_IN, seq)
    bf = jnp.bfloat16
    cos64, sin64 = _rope_cs(seq, 64)
    cos32, sin32 = _rope_cs(seq, 32)
    z32 = jnp.zeros((seq, 32), jnp.float32)
    c64 = jnp.concatenate([cos64, cos64], axis=1)
    s64 = jnp.concatenate([-sin64, sin64], axis=1)
    c32 = jnp.concatenate([cos32, cos32], axis=1)
    s32 = jnp.concatenate([-sin32, sin32], axis=1)
    consts = {
        "kc": jnp.concatenate([c64, c32, z32], axis=1), "ks": jnp.concatenate([s64, s32, z32], axis=1),
        "rc": jnp.concatenate([c64, c64], axis=1), "rs": jnp.concatenate([s64, s64], axis=1),
        "c64t": c64.T, "s64t": s64.T,
        "iqa": c32.T, "iqb": s32.T, "rka": c64.T, "rkb": s64.T,
        "ltri": jnp.tril(jnp.ones((tm, tm), jnp.float32)).astype(bf),
        "ltri_strict": jnp.tril(jnp.ones((KC, KC), jnp.float32), -1).astype(bf),
    }
    eq = np.zeros((3 * LANES, FOX_HEADS * LANES), np.float32)
    ek = np.zeros((3 * LANES, FOX_HEADS * LANES), np.float32)
    oneq = np.zeros((1, FOX_HEADS * LANES), np.float32)
    onek = np.zeros((1, FOX_HEADS * LANES), np.float32)
    for hd in range(FOX_HEADS):
        base = hd * LANES + FOX_DIM
        for part in range(3):
            eq[part * LANES + hd, base + part] = 1.0
            ek[part * LANES + hd, base + 3 + part] = -1.0
            oneq[0, base + 3 + part] = 1.0
            onek[0, base + part] = 1.0
    consts.update(eq=jnp.asarray(eq, bf), ek=jnp.asarray(ek, bf), oneq=jnp.asarray(oneq), onek=jnp.asarray(onek))
    log_g = jnp.log1p(-(2.0 ** (-5.0 - jnp.arange(RET_HEADS, dtype=jnp.float32))))
    pos = jnp.arange(RET_CHUNK, dtype=jnp.float32)
    diff = pos[:, None] - pos[None, :]
    din = jnp.where(diff >= 0, jnp.exp(jnp.maximum(diff, 0.0)[None] * log_g[:, None, None]), 0.0)
    dq = jnp.exp((pos + 1.0)[None] * log_g[:, None])
    dk = jnp.exp((RET_CHUNK - 1.0 - pos)[None] * log_g[:, None])
    dc = jnp.exp(RET_CHUNK * log_g)
    consts.update(ret_din=din, ret_dq=dq[:, :, None], ret_dk=dk[:, None, :],
                  ret_dc=jnp.broadcast_to(dc[:, None, None], (RET_HEADS, 1, LANES)))
    return consts


def _layer_weights(p, consts):
    bf = jnp.bfloat16
    w_in = p["w_in"]

    def cols(off, size):
        return w_in[:, off:off + size]

    zeros = lambda n: jnp.zeros((D_MODEL, n), jnp.float32)
    w_tm = jnp.concatenate([
        cols(O_FQ, FOX_W), cols(O_FK, FOX_W),
        cols(O_FF, FOX_HEADS), zeros(LANES - FOX_HEADS),
        cols(O_DK, DSA_DIM), cols(O_IK, IDX_DIM), zeros(LANES - DSA_DIM - IDX_DIM),
        cols(O_RQ, RET_QK_W), cols(O_RK, RET_QK_W), cols(O_RV, RET_V_W), cols(O_RG, RET_V_W),
    ], axis=1).astype(bf)
    w_fm = jnp.concatenate([
        cols(O_DQ, DSA_W), cols(O_IQ, IDX_W), cols(O_DV, DSA_DIM),
        cols(O_IW, IDX_HEADS), zeros(16 - IDX_HEADS), cols(O_RK, RET_QK_W), cols(O_FV, FOX_W),
    ], axis=1).T.astype(bf)

    def lane_pad(v, fill=0.0):
        return jnp.concatenate([v, jnp.full((LANES - v.shape[0],), fill, jnp.float32)])[None, :]

    g = p["dsa_q_norm"]
    g_sw = jnp.concatenate([g[DSA_DIM // 2:], g[:DSA_DIM // 2]])
    scale = DSA_DIM ** -0.5 * LOG2E

    def logit_bound(gq, gk, dim):
        b = BOUND_SLACK * dim * jnp.max(jnp.abs(gq)) * jnp.max(jnp.abs(gk)) * (dim ** -0.5 * LOG2E)
        return b.reshape(1, 1).astype(jnp.float32)

    return {
        "fox_bound": logit_bound(p["fox_q_norm"], p["fox_k_norm"], FOX_DIM),
        "dsa_bound": logit_bound(p["dsa_q_norm"], p["dsa_k_norm"], DSA_DIM),
        "ln1_g": p["ln1_g"][None, :], "ln2_g": p["ln2_g"][None, :],
        "w_tm": w_tm, "w_fm": w_fm,
        "fox_b": lane_pad(p["fox_b_f"]),
        "fq_gain": lane_pad(p["fox_q_norm"]), "fk_gain": lane_pad(p["fox_k_norm"]),
        "dk_gain": jnp.concatenate([p["dsa_k_norm"], jnp.ones((IDX_DIM,), jnp.float32),
                                    jnp.zeros((LANES - DSA_DIM - IDX_DIM,), jnp.float32)])[None, :],
        "dqa": consts["c64t"] * (g * scale)[:, None], "dqb": consts["s64t"] * (g_sw * scale)[:, None],
        "ret_gain": p["ret_out_norm"][:, None, :],
        "w_zg": cols(O_ZG, N_BRANCH * D_MODEL).astype(bf),
        "w_a": p["w_fox_out"].astype(bf), "w_b": p["w_dsa_out"].astype(bf), "w_c": p["w_ret_out"].astype(bf),
        "w_o": p["w_o"].astype(bf),
        "w_g": p["w_ffn_in"][:, :FFN_HIDDEN].astype(bf), "w_u": p["w_ffn_in"][:, FFN_HIDDEN:].astype(bf),
        "w_d": p["w_ffn_out"].astype(bf),
    }


def _layer(x2d, lw, consts, batch, seq):
    (fq, fk, fvt, dk, rq, rk, rv, rg, qt, iqt, vt, iwt, rkt) = _inproj(x2d, lw, consts, batch, seq)
    ya = _fox(fq, fk, fvt, batch, seq)
    yb = _dsa(lw["dsa_bound"], dk, vt, qt, iqt, iwt, consts["ltri_strict"], batch, seq)
    yc = _ret(rq, rk, rkt, rv, rg, consts, lw["ret_gain"], batch, seq)
    x2d = _merge(x2d, ya, yb, yc, lw)
    return _ffn(x2d, lw)


def kernel(x, ln1_g, w_in, fox_b_f, fox_q_norm, fox_k_norm, dsa_q_norm, dsa_k_norm, ret_out_norm,
           w_fox_out, w_dsa_out, w_ret_out, w_o, ln2_g, w_ffn_in, w_ffn_out):
    batch, seq, _ = x.shape
    depth = w_in.shape[0]
    consts = _constants(seq)
    params = dict(ln1_g=ln1_g, w_in=w_in, fox_b_f=fox_b_f, fox_q_norm=fox_q_norm, fox_k_norm=fox_k_norm,
                  dsa_q_norm=dsa_q_norm, dsa_k_norm=dsa_k_norm, ret_out_norm=ret_out_norm,
                  w_fox_out=w_fox_out, w_dsa_out=w_dsa_out, w_ret_out=w_ret_out, w_o=w_o,
                  ln2_g=ln2_g, w_ffn_in=w_ffn_in, w_ffn_out=w_ffn_out)
    x2d = x.reshape(batch * seq, D_MODEL)
    for layer in range(depth):
        lw = _layer_weights({k: v[layer] for k, v in params.items()}, consts)
        x2d = _layer(x2d, lw, consts, batch, seq)
    return x2d.reshape(batch, seq, D_MODEL)
```

```python
import functools
import math

import jax
import jax.numpy as jnp
import numpy as np
from jax import lax
from jax.experimental import pallas as pl
from jax.experimental.pallas import tpu as pltpu

D_MODEL = 1024
FOX_HEADS = 8
FOX_DIM = 64
DSA_HEADS = 8
DSA_DIM = 64
IDX_HEADS = 8
IDX_DIM = 32
DSA_MAX_TOPK = 256
RET_HEADS = 4
RET_QK_DIM = 64
RET_V_DIM = 128
RET_CHUNK = 128
FFN_HIDDEN = 2816
ROPE_THETA = 10000.0
NORM_EPS = 1e-6
N_BRANCH = 3

FOX_W = FOX_HEADS * FOX_DIM
DSA_W = DSA_HEADS * DSA_DIM
IDX_W = IDX_HEADS * IDX_DIM
RET_QK_W = RET_HEADS * RET_QK_DIM
RET_V_W = RET_HEADS * RET_V_DIM
IN_SIZES = (FOX_W, FOX_W, FOX_W, FOX_HEADS,
            DSA_W, DSA_DIM, DSA_DIM, IDX_W, IDX_DIM, IDX_HEADS,
            RET_QK_W, RET_QK_W, RET_V_W, RET_V_W,
            N_BRANCH * D_MODEL)
IN_OFFS = tuple(int(v) for v in np.cumsum((0,) + IN_SIZES))
(O_FQ, O_FK, O_FV, O_FF, O_DQ, O_DK, O_DV, O_IQ, O_IK, O_IW,
 O_RQ, O_RK, O_RV, O_RG, O_ZG, _) = IN_OFFS

LANES = 128
SUBLANES = 8
BF16_ROWS = 16
VMEM_LIMIT = 56 * 1024 * 1024
NEG_BIG = -1e30
LOG2E = math.log2(math.e)
BOUND_SLACK = 1.02
MIN_DENOM = 2.0 ** -100
HALF_RANGE = 2 ** 15

T_FQ = 0
T_FK = T_FQ + FOX_W
T_FF = T_FK + FOX_W
T_DK = T_FF + LANES
T_RQ = T_DK + LANES
T_RK = T_RQ + RET_QK_W
T_RV = T_RK + RET_QK_W
T_RG = T_RV + RET_V_W
T_COLS = T_RG + RET_V_W
F_DQ = 0
F_IQ = F_DQ + DSA_W
F_DV = F_IQ + IDX_W
F_IW = F_DV + DSA_DIM
F_RK = F_IW + 16
F_FV = F_RK + RET_QK_W
F_ROWS = F_FV + FOX_W

TM_IN = 512
KC = TM_IN
HALF = KC // 2
RB = 32
TQ_FOX = 512
FOX_GROUP = 4
FOX_VROWS = 2 * FOX_DIM + BF16_ROWS
DSA_VROWS = DSA_DIM + BF16_ROWS
TQ_DSA = 256
RET_STEP = 4 * RET_CHUNK
TM_POST = 512


def _cparams(sem):
    return pltpu.CompilerParams(dimension_semantics=sem, vmem_limit_bytes=VMEM_LIMIT)


def _split3(v):
    hi = v.astype(jnp.bfloat16)
    r1 = v - hi.astype(jnp.float32)
    mid = r1.astype(jnp.bfloat16)
    lo = (r1 - mid.astype(jnp.float32)).astype(jnp.bfloat16)
    return hi, mid, lo


def _col_reduce(v, op):
    return op(v.reshape(v.shape[0] // SUBLANES, SUBLANES, v.shape[1]), axis=0)


def _attend(n_chunks, qk, mask, pv, sa_scr, sb_scr, pa_scr, pb_scr, acc_scr, den_row):
    def store_logits(s_ref, c, half):
        lo = 0
        for g in qk(c, half):
            s_ref[:, lo:lo + g.shape[1]] = g
            lo += g.shape[1]

    def probs(s_ref, c, half, p_ref, shift):
        for r0 in range(0, s_ref.shape[0], RB):
            t = mask(s_ref[r0:r0 + RB, :], c, half, r0)
            if shift is not None:
                t = t - shift
            p_ref[r0:r0 + RB, :] = jnp.exp2(t).astype(p_ref.dtype)

    def run(shift):
        acc_scr[...] = jnp.zeros_like(acc_scr)
        pb_scr[...] = jnp.zeros_like(pb_scr)
        store_logits(sa_scr, 0, 0)

        def chunk(c, carry):
            store_logits(sb_scr, c, 1)
            acc_scr[...] += pv(jnp.maximum(c - 1, 0), 1, pb_scr)
            probs(sa_scr, c, 0, pa_scr, shift)
            store_logits(sa_scr, jnp.minimum(c + 1, n_chunks - 1), 0)
            acc_scr[...] += pv(c, 0, pa_scr)
            probs(sb_scr, c, 1, pb_scr, shift)
            return carry

        lax.fori_loop(0, n_chunks, chunk, 0)
        acc_scr[...] += pv(n_chunks - 1, 1, pb_scr)

    run(None)

    @pl.when(jnp.logical_not(jnp.min(acc_scr[den_row:den_row + 1, :]) >= MIN_DENOM))
    def _():
        def col_max(c, mx):
            for half in range(2):
                store_logits(sa_scr, c, half)
                for r0 in range(0, sa_scr.shape[0], RB):
                    mx = jnp.maximum(mx, _col_reduce(mask(sa_scr[r0:r0 + RB, :], c, half, r0), jnp.max))
            return mx
        mx = lax.fori_loop(0, n_chunks, col_max,
                           jnp.full((SUBLANES, acc_scr.shape[1]), NEG_BIG, jnp.float32))
        run(jnp.max(mx, axis=0, keepdims=True))


def _inproj_kernel(bound_ref, x_ref, g_ref, wtm_ref, wfm_ref, fb_ref, fqg_ref, fkg_ref, dkg_ref,
                   ltri_ref, eq_ref, ek_ref, oneq_ref, onek_ref,
                   kc_ref, ks_ref, rc_ref, rs_ref,
                   dqa_ref, dqb_ref, iqa_ref, iqb_ref, rka_ref, rkb_ref,
                   fq_out, fk_out, fvt_out, dk_out, rq_out, rk_out, rv_out, rg_out,
                   qt_out, iqt_out, vt_out, iw_out, rkt_out,
                   carry_ref, *, tiles_per_seq):
    tm = x_ref.shape[0]
    i = pl.program_id(0)

    @pl.when(i % tiles_per_seq == 0)
    def _():
        carry_ref[...] = jnp.zeros_like(carry_ref)

    x = x_ref[...]
    ms = jnp.mean(x * x, axis=-1, keepdims=True)
    h = (x * lax.rsqrt(ms + NORM_EPS) * g_ref[...]).astype(jnp.bfloat16)

    def tm_dot(lo, width):
        return jnp.dot(h, wtm_ref[:, lo:lo + width], preferred_element_type=jnp.float32)

    lane = lax.broadcasted_iota(jnp.int32, (tm, LANES), 1)

    ffb = tm_dot(T_FF, LANES) + fb_ref[...]
    lf = (jnp.minimum(ffb, 0.0) - jnp.log1p(jnp.exp(-jnp.abs(ffb)))) * LOG2E
    parts = jnp.concatenate(_split3(lf), axis=1)
    cs = jnp.dot(ltri_ref[...], parts, preferred_element_type=jnp.float32)
    c = cs[:, :LANES] + cs[:, LANES:2 * LANES] + cs[:, 2 * LANES:] + carry_ref[...]
    carry_ref[...] = c[tm - 1:tm, :]
    cparts_q = jnp.concatenate(_split3(c - bound_ref[0, 0]), axis=1)
    cparts_k = jnp.concatenate(_split3(c), axis=1)
    scat_q = jnp.dot(cparts_q, eq_ref[...], preferred_element_type=jnp.float32) + oneq_ref[...]
    scat_k = jnp.dot(cparts_k, ek_ref[...], preferred_element_type=jnp.float32) + onek_ref[...]

    for (lo, gain_ref, scat, out, scale) in ((T_FQ, fqg_ref, scat_q, fq_out, FOX_DIM ** -0.5 * LOG2E),
                                             (T_FK, fkg_ref, scat_k, fk_out, 1.0)):
        z = tm_dot(lo, FOX_W)
        for hd in range(FOX_HEADS):
            blk = z[:, (hd // 2) * LANES:(hd // 2 + 1) * LANES]
            if hd % 2:
                blk = pltpu.roll(blk, FOX_DIM, 1)
            ss = jnp.sum(jnp.where(lane < FOX_DIM, blk * blk, 0.0), axis=-1, keepdims=True) * (1.0 / FOX_DIM)
            nb = blk * lax.rsqrt(ss + NORM_EPS) * (gain_ref[...] * scale)
            out[:, hd * LANES:(hd + 1) * LANES] = (
                nb + scat[:, hd * LANES:(hd + 1) * LANES]).astype(out.dtype)

    zk = tm_dot(T_DK, LANES)
    ssk = jnp.sum(jnp.where(lane < DSA_DIM, zk * zk, 0.0), axis=-1, keepdims=True) * (1.0 / DSA_DIM)
    nk = zk * jnp.where(lane < DSA_DIM, lax.rsqrt(ssk + NORM_EPS), 1.0) * dkg_ref[...]
    partner = jnp.where(
        lane < 32, pltpu.roll(nk, LANES - 32, 1),
        jnp.where(lane < 64, pltpu.roll(nk, 32, 1),
                  jnp.where(lane < 80, pltpu.roll(nk, LANES - 16, 1), pltpu.roll(nk, 16, 1))))
    dk_out[...] = (nk * kc_ref[...] + partner * ks_ref[...]).astype(dk_out.dtype)

    first_half = (lane % RET_QK_DIM) < (RET_QK_DIM // 2)
    for (lo, out, scale) in ((T_RQ, rq_out, RET_QK_DIM ** -0.5), (T_RK, rk_out, 1.0)):
        z = tm_dot(lo, RET_QK_W)
        for j in range(RET_QK_W // LANES):
            blk = z[:, j * LANES:(j + 1) * LANES]
            pr = jnp.where(first_half, pltpu.roll(blk, LANES - 32, 1), pltpu.roll(blk, 32, 1))
            out[:, j * LANES:(j + 1) * LANES] = (
                (blk * rc_ref[...] + pr * rs_ref[...]) * scale).astype(out.dtype)

    rv_out[...] = tm_dot(T_RV, RET_V_W).astype(rv_out.dtype)
    rg_out[...] = tm_dot(T_RG, RET_V_W)

    zt = lax.dot_general(wfm_ref[...], h, (((1,), (1,)), ((), ())),
                         preferred_element_type=jnp.float32)
    nq = tm // TQ_DSA

    def swap_halves(v):
        half = v.shape[0] // 2
        return jnp.concatenate([v[half:], v[:half]], axis=0)

    def ones_row_block(rows, dtype):
        first = lax.broadcasted_iota(jnp.int32, (rows, tm), 0) == 0
        return jnp.where(first, 1.0, 0.0).astype(dtype)

    zeros_q = jnp.zeros((LANES - DSA_DIM, DSA_HEADS * TQ_DSA), qt_out.dtype)
    zeros_i0 = jnp.zeros((DSA_DIM, IDX_HEADS * TQ_DSA), iqt_out.dtype)
    zeros_i1 = jnp.zeros((LANES - DSA_DIM - IDX_DIM, IDX_HEADS * TQ_DSA), iqt_out.dtype)
    for j in range(nq):
        qt_out[j, DSA_DIM:, :] = zeros_q
        iqt_out[j, :DSA_DIM, :] = zeros_i0
        iqt_out[j, DSA_DIM + IDX_DIM:, :] = zeros_i1
    for hd in range(DSA_HEADS):
        xh = zt[F_DQ + hd * DSA_DIM:F_DQ + (hd + 1) * DSA_DIM, :]
        r = lax.rsqrt(jnp.sum(xh * xh, axis=0, keepdims=True) * (1.0 / DSA_DIM) + NORM_EPS)
        o = ((xh * dqa_ref[...] + swap_halves(xh) * dqb_ref[...]) * r).astype(qt_out.dtype)
        for j in range(nq):
            qt_out[j, :DSA_DIM, hd * TQ_DSA:(hd + 1) * TQ_DSA] = o[:, j * TQ_DSA:(j + 1) * TQ_DSA]
    for hd in range(IDX_HEADS):
        xh = zt[F_IQ + hd * IDX_DIM:F_IQ + (hd + 1) * IDX_DIM, :]
        o = (xh * iqa_ref[...] + swap_halves(xh) * iqb_ref[...]).astype(iqt_out.dtype)
        for j in range(nq):
            iqt_out[j, DSA_DIM:DSA_DIM + IDX_DIM, hd * TQ_DSA:(hd + 1) * TQ_DSA] = (
                o[:, j * TQ_DSA:(j + 1) * TQ_DSA])
    vt_out[0, 0, :DSA_DIM, :] = zt[F_DV:F_DV + DSA_DIM, :].astype(vt_out.dtype)
    vt_out[0, 0, DSA_DIM:, :] = ones_row_block(BF16_ROWS, vt_out.dtype)
    iw_out[0] = zt[F_IW:F_IW + IDX_HEADS, :] * ((IDX_DIM * IDX_HEADS) ** -0.5)
    for hd in range(RET_HEADS):
        xh = zt[F_RK + hd * RET_QK_DIM:F_RK + (hd + 1) * RET_QK_DIM, :]
        rkt_out[0, hd * RET_QK_DIM:(hd + 1) * RET_QK_DIM, :] = (
            xh * rka_ref[...] + swap_halves(xh) * rkb_ref[...])
    for hp in range(FOX_HEADS // 2):
        fvt_out[0, hp, 0, :2 * FOX_DIM, :] = (
            zt[F_FV + hp * 2 * FOX_DIM:F_FV + (hp + 1) * 2 * FOX_DIM, :].astype(fvt_out.dtype))
        fvt_out[0, hp, 0, 2 * FOX_DIM:, :] = ones_row_block(BF16_ROWS, fvt_out.dtype)


def _inproj(x2d, lw, consts, batch, seq):
    n = x2d.shape[0]
    tm = min(TM_IN, seq)
    tps = seq // tm
    nqt = tm // TQ_DSA
    grid = (n // tm,)
    bf = jnp.bfloat16

    def full(a):
        return pl.BlockSpec(a.shape, lambda i: (0,) * a.ndim)

    def tok(width):
        return pl.BlockSpec((tm, width), lambda i: (i, 0))

    def pos_tm(width):
        return pl.BlockSpec((tm, width), lambda i: (i % tps, 0))

    def pos_fm(rows):
        return pl.BlockSpec((rows, tm), lambda i: (0, i % tps))

    def fm_out(rows):
        return pl.BlockSpec((1, rows, tm), lambda i: (i // tps, 0, i % tps))

    in_arrays = [lw["fox_bound"], x2d, lw["ln1_g"], lw["w_tm"], lw["w_fm"], lw["fox_b"], lw["fq_gain"], lw["fk_gain"],
                 lw["dk_gain"], consts["ltri"], consts["eq"], consts["ek"], consts["oneq"], consts["onek"],
                 consts["kc"], consts["ks"], consts["rc"], consts["rs"],
                 lw["dqa"], lw["dqb"], consts["iqa"], consts["iqb"], consts["rka"], consts["rkb"]]
    in_specs = [pl.BlockSpec(memory_space=pltpu.SMEM),
                tok(D_MODEL), full(lw["ln1_g"]), full(lw["w_tm"]), full(lw["w_fm"]), full(lw["fox_b"]),
                full(lw["fq_gain"]), full(lw["fk_gain"]), full(lw["dk_gain"]),
                full(consts["ltri"]), full(consts["eq"]), full(consts["ek"]),
                full(consts["oneq"]), full(consts["onek"]),
                pos_tm(LANES), pos_tm(LANES), pos_tm(LANES), pos_tm(LANES),
                pos_fm(DSA_DIM), pos_fm(DSA_DIM), pos_fm(IDX_DIM), pos_fm(IDX_DIM),
                pos_fm(RET_QK_DIM), pos_fm(RET_QK_DIM)]
    out_shape = [
        jax.ShapeDtypeStruct((n, FOX_HEADS * LANES), bf),
        jax.ShapeDtypeStruct((n, FOX_HEADS * LANES), bf),
        jax.ShapeDtypeStruct((batch, FOX_HEADS // 2, tps, FOX_VROWS, tm), bf),
        jax.ShapeDtypeStruct((n, LANES), bf),
        jax.ShapeDtypeStruct((n, RET_QK_W), bf),
        jax.ShapeDtypeStruct((n, RET_QK_W), bf),
        jax.ShapeDtypeStruct((n, RET_V_W), bf),
        jax.ShapeDtypeStruct((n, RET_V_W), jnp.float32),
        jax.ShapeDtypeStruct((n // TQ_DSA, LANES, DSA_HEADS * TQ_DSA), bf),
        jax.ShapeDtypeStruct((n // TQ_DSA, LANES, IDX_HEADS * TQ_DSA), bf),
        jax.ShapeDtypeStruct((batch, tps, DSA_VROWS, tm), bf),
        jax.ShapeDtypeStruct((batch, IDX_HEADS, seq), jnp.float32),
        jax.ShapeDtypeStruct((batch, RET_QK_W, seq), jnp.float32),
    ]
    out_specs = [tok(FOX_HEADS * LANES), tok(FOX_HEADS * LANES),
                 pl.BlockSpec((1, FOX_HEADS // 2, 1, FOX_VROWS, tm), lambda i: (i // tps, 0, i % tps, 0, 0)),
                 tok(LANES), tok(RET_QK_W), tok(RET_QK_W), tok(RET_V_W), tok(RET_V_W),
                 pl.BlockSpec((nqt, LANES, DSA_HEADS * TQ_DSA), lambda i: (i, 0, 0)),
                 pl.BlockSpec((nqt, LANES, IDX_HEADS * TQ_DSA), lambda i: (i, 0, 0)),
                 pl.BlockSpec((1, 1, DSA_VROWS, tm), lambda i: (i // tps, i % tps, 0, 0)),
                 fm_out(IDX_HEADS), fm_out(RET_QK_W)]
    return pl.pallas_call(
        functools.partial(_inproj_kernel, tiles_per_seq=tps),
        grid=grid, in_specs=in_specs, out_specs=out_specs, out_shape=out_shape,
        scratch_shapes=[pltpu.VMEM((1, LANES), jnp.float32)],
        compiler_params=_cparams(("arbitrary",)), name="inproj",
    )(*in_arrays)


def _fox_kernel(q_ref, k_ref, vt_ref, o_ref, sa_scr, sb_scr, pa_scr, pb_scr, acc_scr):
    tq = q_ref.shape[0]
    qi = pl.program_id(2)
    n_chunks = (qi * tq + tq + KC - 1) // KC
    nt = (((1,), (1,)), ((), ()))

    nheads = FOX_GROUP
    qcol = qi * tq + lax.broadcasted_iota(jnp.int32, (RB, tq), 1)
    qcol = jnp.concatenate([qcol] * nheads, axis=1)
    krow = lax.broadcasted_iota(jnp.int32, (RB, nheads * tq), 0)

    def qk(c, half):
        off = pl.multiple_of(c * KC + half * HALF, HALF)
        return [lax.dot_general(k_ref[pl.ds(off, HALF), hh * LANES:(hh + 1) * LANES],
                                q_ref[:, hh * LANES:(hh + 1) * LANES], nt,
                                preferred_element_type=jnp.float32) for hh in range(nheads)]

    def mask(t, c, half, r0):
        return jnp.where(krow + (c * KC + half * HALF + r0) <= qcol, t, NEG_BIG)

    def pv(c, half, p_ref):
        return jnp.concatenate(
            [jnp.dot(vt_ref[0, hh // 2, c, :, half * HALF:(half + 1) * HALF],
                     p_ref[:, hh * tq:(hh + 1) * tq], preferred_element_type=jnp.float32)
             for hh in range(nheads)], axis=1)

    _attend(n_chunks, qk, mask, pv, sa_scr, sb_scr, pa_scr, pb_scr, acc_scr, 2 * FOX_DIM)
    acc = acc_scr[...]
    den = acc[2 * FOX_DIM:2 * FOX_DIM + 1, :]
    for hp in range(nheads // 2):
        ev, od = 2 * hp * tq, (2 * hp + 1) * tq
        out_t = jnp.concatenate([acc[:FOX_DIM, ev:ev + tq] / den[:, ev:ev + tq],
                                 acc[FOX_DIM:2 * FOX_DIM, od:od + tq] / den[:, od:od + tq]], axis=0)
        for j in range(tq // LANES):
            o_ref[j * LANES:(j + 1) * LANES, hp * LANES:(hp + 1) * LANES] = (
                out_t[:, j * LANES:(j + 1) * LANES].T.astype(o_ref.dtype))


def _fox(fq, fk, fvt, batch, seq):
    tq = min(TQ_FOX, seq)
    nq = seq // tq
    g = FOX_GROUP
    width = g * tq
    return pl.pallas_call(
        _fox_kernel, grid=(batch, FOX_HEADS // g, nq),
        in_specs=[pl.BlockSpec((tq, g * LANES), lambda b, hg, qi: (b * nq + qi, hg)),
                  pl.BlockSpec((seq, g * LANES), lambda b, hg, qi: (b, hg)),
                  pl.BlockSpec((1, g // 2) + fvt.shape[2:], lambda b, hg, qi: (b, hg, 0, 0, 0))],
        out_specs=pl.BlockSpec((tq, g * FOX_DIM), lambda b, hg, qi: (b * nq + qi, hg)),
        out_shape=jax.ShapeDtypeStruct((batch * seq, FOX_W), jnp.bfloat16),
        scratch_shapes=[pltpu.VMEM((HALF, width), jnp.float32), pltpu.VMEM((HALF, width), jnp.float32),
                        pltpu.VMEM((HALF, width), jnp.bfloat16), pltpu.VMEM((HALF, width), jnp.bfloat16),
                        pltpu.VMEM((FOX_VROWS, width), jnp.float32)],
        compiler_params=_cparams(("parallel", "parallel", "arbitrary")), name="fox",
    )(fq, fk, fvt)


def _dsa_kernel(bound_ref, k_ref, vt_ref, qt_ref, iqt_ref, iw_ref, ltri_ref, o_ref,
                key_scr, hi_scr, lo_scr, bias_scr, sa_scr, sb_scr, pa_scr, pb_scr, acc_scr, *, topk):
    tq = TQ_DSA
    nh = DSA_HEADS
    selected_bias = -bound_ref[0, 0]
    qi = pl.program_id(1)
    n_chunks = (qi * tq + tq + KC - 1) // KC

    qpos = qi * tq + lax.broadcasted_iota(jnp.int32, (KC, tq), 1)
    krow = lax.broadcasted_iota(jnp.int32, (KC, tq), 0)

    def score_chunk(c, carry):
        off = pl.multiple_of(c * KC, KC)
        rel = jnp.dot(k_ref[pl.ds(off, KC), :], iqt_ref[0], preferred_element_type=jnp.float32)
        score = jnp.maximum(rel[:, :tq], 0.0) * iw_ref[0, 0:1, :]
        for hd in range(1, IDX_HEADS):
            score = score + jnp.maximum(rel[:, hd * tq:(hd + 1) * tq], 0.0) * iw_ref[0, hd:hd + 1, :]
        score = jnp.where(score == 0.0, 0.0, score)
        score = jnp.where(krow + off <= qpos, score, -jnp.inf)
        bits = pltpu.bitcast(score, jnp.int32)
        key = bits ^ ((bits >> 31) & jnp.int32(0x7FFFFFFF))
        key_scr[pl.ds(off, KC), :] = key
        hi_scr[pl.ds(off, KC), :] = (key >> 16).astype(jnp.int16)
        lo_scr[pl.ds(off, KC), :] = ((key & jnp.int32(0xFFFF)) - HALF_RANGE).astype(jnp.int16)
        return carry

    lax.fori_loop(0, n_chunks, score_chunk, 0)

    def count16_ge(plane_scr, thr):
        thr16 = thr.astype(jnp.int16)

        def body(c, acc):
            off = pl.multiple_of(c * KC, KC)
            hit = jnp.where(plane_scr[pl.ds(off, KC), :] >= thr16, jnp.int16(1), jnp.int16(0))
            parts = [hit[r0:r0 + BF16_ROWS] for r0 in range(0, KC, BF16_ROWS)]
            while len(parts) > 1:
                parts = [parts[i] + parts[i + 1] for i in range(0, len(parts), 2)]
            return acc + parts[0]
        acc = lax.fori_loop(0, n_chunks, body, jnp.zeros((BF16_ROWS, tq), jnp.int16))
        return jnp.sum(acc.astype(jnp.int32), axis=0, keepdims=True)

    def kth_largest16(plane_scr, kth):
        def bit_step(b, carry):
            thr, cnt = carry
            bit = jnp.left_shift(jnp.int32(1), 15 - b)
            cand = jnp.where(b == 0, jnp.zeros_like(thr), thr | bit)
            cand_cnt = count16_ge(plane_scr, cand)
            take = cand_cnt >= kth
            return jnp.where(take, cand, thr), jnp.where(take, cand_cnt, cnt)
        return lax.fori_loop(0, 16, bit_step, (jnp.full((1, tq), -HALF_RANGE, jnp.int32),
                                               jnp.full((1, tq), n_chunks * KC, jnp.int32)))

    def count_ge(thr):
        def body(c, acc):
            off = pl.multiple_of(c * KC, KC)
            hit = jnp.where(key_scr[pl.ds(off, KC), :] >= thr, 1, 0).astype(jnp.int32)
            return acc + _col_reduce(hit, jnp.sum)
        acc = lax.fori_loop(0, n_chunks, body, jnp.zeros((SUBLANES, tq), jnp.int32))
        return jnp.sum(acc, axis=0, keepdims=True)

    thr_hi, cnt_hi = kth_largest16(hi_scr, topk)
    top_hi = HALF_RANGE - 1
    n_above = jnp.where(thr_hi == top_hi, 0, count16_ge(hi_scr, jnp.minimum(thr_hi + 1, top_hi)))
    thr_hi16 = thr_hi.astype(jnp.int16)

    def keep_candidates(c, carry):
        off = pl.multiple_of(c * KC, KC)
        lo_scr[pl.ds(off, KC), :] = jnp.where(hi_scr[pl.ds(off, KC), :] == thr_hi16,
                                              lo_scr[pl.ds(off, KC), :], jnp.int16(-HALF_RANGE))
        return carry

    lax.fori_loop(0, n_chunks, keep_candidates, 0)
    thr_lo, cnt_lo = kth_largest16(lo_scr, topk - n_above)
    thr = thr_hi * (2 * HALF_RANGE) + (thr_lo + HALF_RANGE)
    n_ge = jnp.where(thr_lo > -HALF_RANGE, n_above + cnt_lo, cnt_hi)
    has_ties = jnp.max(n_ge) > topk

    @pl.when(jnp.logical_not(has_ties))
    def _():
        def body(c, carry):
            off = pl.multiple_of(c * KC, KC)
            bias_scr[pl.ds(off, KC), :] = jnp.where(key_scr[pl.ds(off, KC), :] >= thr, selected_bias, NEG_BIG)
            return carry
        lax.fori_loop(0, n_chunks, body, 0)

    @pl.when(has_ties)
    def _():
        int_max = jnp.int32(2 ** 31 - 1)
        n_gt = jnp.where(thr == int_max, 0, count_ge(jnp.where(thr == int_max, thr, thr + 1)))
        need = (topk - n_gt).astype(jnp.float32)

        def body(c, seen):
            off = pl.multiple_of(c * KC, KC)
            ks = key_scr[pl.ds(off, KC), :]
            eq = ks == thr
            eqf = jnp.where(eq, 1.0, 0.0)
            before = jnp.dot(ltri_ref[...], eqf.astype(jnp.bfloat16),
                             preferred_element_type=jnp.float32) + seen
            sel = jnp.logical_or(ks > thr, jnp.logical_and(eq, before < need))
            sel = jnp.logical_and(sel, krow + off <= qpos)
            bias_scr[pl.ds(off, KC), :] = jnp.where(sel, selected_bias, NEG_BIG)
            return seen + jnp.sum(eqf, axis=0, keepdims=True)
        lax.fori_loop(0, n_chunks, body, jnp.zeros((1, tq), jnp.float32))

    def qk(c, half):
        off = pl.multiple_of(c * KC + half * HALF, HALF)
        return [jnp.dot(k_ref[pl.ds(off, HALF), :], qt_ref[0], preferred_element_type=jnp.float32)]

    def mask(t, c, half, r0):
        b = bias_scr[pl.ds(pl.multiple_of(c * KC + half * HALF + r0, RB), RB), :]
        return t + jnp.concatenate([b] * nh, axis=1)

    def pv(c, half, p_ref):
        return jnp.dot(vt_ref[0, c, :, half * HALF:(half + 1) * HALF], p_ref[...],
                       preferred_element_type=jnp.float32)

    _attend(n_chunks, qk, mask, pv, sa_scr, sb_scr, pa_scr, pb_scr, acc_scr, DSA_DIM)
    acc = acc_scr[...]
    out_t = acc[:DSA_DIM, :] / acc[DSA_DIM:DSA_DIM + 1, :]
    for hp in range(nh // 2):
        pair = jnp.concatenate([out_t[:, 2 * hp * tq:(2 * hp + 1) * tq],
                                out_t[:, (2 * hp + 1) * tq:(2 * hp + 2) * tq]], axis=0)
        o_ref[:, hp * LANES:(hp + 1) * LANES] = pair.T.astype(o_ref.dtype)


def _dsa(bound, dk, vt, qt, iqt, iwt, ltri_strict, batch, seq):
    nq = seq // TQ_DSA
    assert vt.shape[3] == KC and ltri_strict.shape[0] == KC and seq % KC == 0
    topk = min(DSA_MAX_TOPK, seq // 4)
    width = DSA_HEADS * TQ_DSA
    return pl.pallas_call(
        functools.partial(_dsa_kernel, topk=topk),
        grid=(batch, nq),
        in_specs=[pl.BlockSpec(memory_space=pltpu.SMEM),
                  pl.BlockSpec((seq, LANES), lambda b, qi: (b, 0)),
                  pl.BlockSpec((1,) + vt.shape[1:], lambda b, qi: (b, 0, 0, 0)),
                  pl.BlockSpec((1, LANES, width), lambda b, qi: (b * nq + qi, 0, 0)),
                  pl.BlockSpec((1, LANES, IDX_HEADS * TQ_DSA), lambda b, qi: (b * nq + qi, 0, 0)),
                  pl.BlockSpec((1, IDX_HEADS, TQ_DSA), lambda b, qi: (b, 0, qi)),
                  pl.BlockSpec(ltri_strict.shape, lambda b, qi: (0, 0))],
        out_specs=pl.BlockSpec((TQ_DSA, DSA_W), lambda b, qi: (b * nq + qi, 0)),
        out_shape=jax.ShapeDtypeStruct((batch * seq, DSA_W), jnp.bfloat16),
        scratch_shapes=[pltpu.VMEM((seq, TQ_DSA), jnp.int32),
                        pltpu.VMEM((seq, TQ_DSA), jnp.int16),
                        pltpu.VMEM((seq, TQ_DSA), jnp.int16),
                        pltpu.VMEM((seq, TQ_DSA), jnp.float32),
                        pltpu.VMEM((HALF, width), jnp.float32), pltpu.VMEM((HALF, width), jnp.float32),
                        pltpu.VMEM((HALF, width), jnp.bfloat16), pltpu.VMEM((HALF, width), jnp.bfloat16),
                        pltpu.VMEM((DSA_VROWS, width), jnp.float32)],
        compiler_params=_cparams(("parallel", "arbitrary")), name="dsa",
    )(bound, dk, vt, qt, iqt, iwt, ltri_strict)


def _ret_kernel(q_ref, k_ref, kt_ref, v_ref, g_ref, din_ref, dq_ref, dk_ref, dc_ref, gain_ref,
                o_ref, state_scr):
    t = pl.program_id(1)
    c = RET_CHUNK

    @pl.when(t == 0)
    def _():
        state_scr[...] = jnp.zeros_like(state_scr)

    for hd in range(RET_HEADS):
        state = state_scr[hd]
        for j in range(q_ref.shape[0] // c):
            rows = slice(j * c, (j + 1) * c)
            q = q_ref[rows, hd * RET_QK_DIM:(hd + 1) * RET_QK_DIM]
            k = k_ref[rows, hd * RET_QK_DIM:(hd + 1) * RET_QK_DIM]
            v = v_ref[rows, hd * RET_V_DIM:(hd + 1) * RET_V_DIM]
            attn = lax.dot_general(q, k, (((1,), (1,)), ((), ())),
                                   preferred_element_type=jnp.float32) * din_ref[hd]
            inner = jnp.dot(attn.astype(jnp.bfloat16), v, preferred_element_type=jnp.float32)
            cross = jnp.dot(q, state.astype(jnp.bfloat16),
                            preferred_element_type=jnp.float32) * dq_ref[hd]
            ktd = (kt_ref[0, hd * RET_QK_DIM:(hd + 1) * RET_QK_DIM, rows] * dk_ref[hd]).astype(jnp.bfloat16)
            state = dc_ref[hd] * state + jnp.dot(ktd, v, preferred_element_type=jnp.float32)
            y = inner + cross
            yn = y * lax.rsqrt(jnp.mean(y * y, axis=-1, keepdims=True) + NORM_EPS) * gain_ref[hd]
            gate = g_ref[rows, hd * RET_V_DIM:(hd + 1) * RET_V_DIM]
            o_ref[rows, hd * RET_V_DIM:(hd + 1) * RET_V_DIM] = (
                yn * (gate * jax.nn.sigmoid(gate))).astype(o_ref.dtype)
        state_scr[hd] = state


def _ret(rq, rk, rkt, rv, rg, consts, gain, batch, seq):
    c = min(RET_STEP, seq)
    n = seq // c

    def tok(width):
        return pl.BlockSpec((c, width), lambda b, t: (b * n + t, 0))

    def full(a):
        return pl.BlockSpec(a.shape, lambda b, t: (0,) * a.ndim)

    return pl.pallas_call(
        _ret_kernel, grid=(batch, n),
        in_specs=[tok(RET_QK_W), tok(RET_QK_W),
                  pl.BlockSpec((1, RET_QK_W, c), lambda b, t: (b, 0, t)),
                  tok(RET_V_W), tok(RET_V_W),
                  full(consts["ret_din"]), full(consts["ret_dq"]), full(consts["ret_dk"]),
                  full(consts["ret_dc"]), full(gain)],
        out_specs=tok(RET_V_W),
        out_shape=jax.ShapeDtypeStruct((batch * seq, RET_V_W), jnp.bfloat16),
        scratch_shapes=[pltpu.VMEM((RET_HEADS, RET_QK_DIM, RET_V_DIM), jnp.float32)],
        compiler_params=_cparams(("parallel", "arbitrary")), name="ret",
    )(rq, rk, rkt, rv, rg, consts["ret_din"], consts["ret_dq"], consts["ret_dk"], consts["ret_dc"], gain)


def _merge_kernel(x_ref, g_ref, ya_ref, yb_ref, yc_ref, wzg_ref, wa_ref, wb_ref, wc_ref, wo_ref, o_ref):
    x = x_ref[...]
    ms = jnp.mean(x * x, axis=-1, keepdims=True)
    h = (x * lax.rsqrt(ms + NORM_EPS) * g_ref[...]).astype(jnp.bfloat16)
    merged = None
    for j, (y_ref, w_ref) in enumerate(((ya_ref, wa_ref), (yb_ref, wb_ref), (yc_ref, wc_ref))):
        gate = jax.nn.sigmoid(jnp.dot(h, wzg_ref[:, j * D_MODEL:(j + 1) * D_MODEL],
                                      preferred_element_type=jnp.float32))
        term = gate * jnp.dot(y_ref[...], w_ref[...], preferred_element_type=jnp.float32)
        merged = term if merged is None else merged + term
    o_ref[...] = x + jnp.dot(merged.astype(jnp.bfloat16), wo_ref[...],
                             preferred_element_type=jnp.float32)


def _ffn_kernel(x_ref, g_ref, wg_ref, wu_ref, wd_ref, o_ref, *, chunk):
    x = x_ref[...]
    ms = jnp.mean(x * x, axis=-1, keepdims=True)
    h = (x * lax.rsqrt(ms + NORM_EPS) * g_ref[...]).astype(jnp.bfloat16)
    acc = x
    for lo in range(0, FFN_HIDDEN, chunk):
        gt = jnp.dot(h, wg_ref[:, lo:lo + chunk], preferred_element_type=jnp.float32)
        up = jnp.dot(h, wu_ref[:, lo:lo + chunk], preferred_element_type=jnp.float32)
        act = (gt * jax.nn.sigmoid(gt) * up).astype(jnp.bfloat16)
        acc = acc + jnp.dot(act, wd_ref[lo:lo + chunk, :], preferred_element_type=jnp.float32)
    o_ref[...] = acc


def _row_call(kernel, name, order):
    n = order[0][1].shape[0]
    tm = min(TM_POST, n)
    arrays, specs = [], []
    for kind, a in order:
        arrays.append(a)
        if kind == "row":
            specs.append(pl.BlockSpec((tm, a.shape[1]), lambda i: (i, 0)))
        else:
            specs.append(pl.BlockSpec(a.shape, lambda i, nd=a.ndim: (0,) * nd))
    return pl.pallas_call(
        kernel, grid=(n // tm,), in_specs=specs,
        out_specs=pl.BlockSpec((tm, D_MODEL), lambda i: (i, 0)),
        out_shape=jax.ShapeDtypeStruct((n, D_MODEL), jnp.float32),
        compiler_params=_cparams(("parallel",)), name=name,
    )(*arrays)


def _merge(x2d, ya, yb, yc, lw):
    order = [("row", x2d), ("full", lw["ln1_g"]), ("row", ya), ("row", yb), ("row", yc),
             ("full", lw["w_zg"]), ("full", lw["w_a"]), ("full", lw["w_b"]), ("full", lw["w_c"]),
             ("full", lw["w_o"])]
    return _row_call(_merge_kernel, "merge", order)


def _ffn(x2d, lw):
    order = [("row", x2d), ("full", lw["ln2_g"]), ("full", lw["w_g"]), ("full", lw["w_u"]),
             ("full", lw["w_d"])]
    return _row_call(functools.partial(_ffn_kernel, chunk=256), "ffn", order)


def _rope_cs(seq, dim):
    half = dim // 2
    inv_freq = ROPE_THETA ** (-jnp.arange(half, dtype=jnp.float32) / half)
    ang = jnp.arange(seq, dtype=jnp.float32)[:, None] * inv_freq[None, :]
    return jnp.cos(ang), jnp.sin(ang)


def _constants(seq):
    tm = min(TM_IN, seq)
    bf = jnp.bfloat16
    cos64, sin64 = _rope_cs(seq, 64)
    cos32, sin32 = _rope_cs(seq, 32)
    z32 = jnp.zeros((seq, 32), jnp.float32)
    c64 = jnp.concatenate([cos64, cos64], axis=1)
    s64 = jnp.concatenate([-sin64, sin64], axis=1)
    c32 = jnp.concatenate([cos32, cos32], axis=1)
    s32 = jnp.concatenate([-sin32, sin32], axis=1)
    consts = {
        "kc": jnp.concatenate([c64, c32, z32], axis=1), "ks": jnp.concatenate([s64, s32, z32], axis=1),
        "rc": jnp.concatenate([c64, c64], axis=1), "rs": jnp.concatenate([s64, s64], axis=1),
        "c64t": c64.T, "s64t": s64.T,
        "iqa": c32.T, "iqb": s32.T, "rka": c64.T, "rkb": s64.T,
        "ltri": jnp.tril(jnp.ones((tm, tm), jnp.float32)).astype(bf),
        "ltri_strict": jnp.tril(jnp.ones((KC, KC), jnp.float32), -1).astype(bf),
    }
    eq = np.zeros((3 * LANES, FOX_HEADS * LANES), np.float32)
    ek = np.zeros((3 * LANES, FOX_HEADS * LANES), np.float32)
    oneq = np.zeros((1, FOX_HEADS * LANES), np.float32)
    onek = np.zeros((1, FOX_HEADS * LANES), np.float32)
    for hd in range(FOX_HEADS):
        base = hd * LANES + FOX_DIM
        for part in range(3):
            eq[part * LANES + hd, base + part] = 1.0
            ek[part * LANES + hd, base + 3 + part] = -1.0
            oneq[0, base + 3 + part] = 1.0
            onek[0, base + part] = 1.0
    consts.update(eq=jnp.asarray(eq, bf), ek=jnp.asarray(ek, bf), oneq=jnp.asarray(oneq), onek=jnp.asarray(onek))
    log_g = jnp.log1p(-(2.0 ** (-5.0 - jnp.arange(RET_HEADS, dtype=jnp.float32))))
    pos = jnp.arange(RET_CHUNK, dtype=jnp.float32)
    diff = pos[:, None] - pos[None, :]
    din = jnp.where(diff >= 0, jnp.exp(jnp.maximum(diff, 0.0)[None] * log_g[:, None, None]), 0.0)
    dq = jnp.exp((pos + 1.0)[None] * log_g[:, None])
    dk = jnp.exp((RET_CHUNK - 1.0 - pos)[None] * log_g[:, None])
    dc = jnp.exp(RET_CHUNK * log_g)
    consts.update(ret_din=din, ret_dq=dq[:, :, None], ret_dk=dk[:, None, :],
                  ret_dc=jnp.broadcast_to(dc[:, None, None], (RET_HEADS, 1, LANES)))
    return consts


def _layer_weights(p, consts):
    bf = jnp.bfloat16
    w_in = p["w_in"]

    def cols(off, size):
        return w_in[:, off:off + size]

    zeros = lambda n: jnp.zeros((D_MODEL, n), jnp.float32)
    w_tm = jnp.concatenate([
        cols(O_FQ, FOX_W), cols(O_FK, FOX_W),
        cols(O_FF, FOX_HEADS), zeros(LANES - FOX_HEADS),
        cols(O_DK, DSA_DIM), cols(O_IK, IDX_DIM), zeros(LANES - DSA_DIM - IDX_DIM),
        cols(O_RQ, RET_QK_W), cols(O_RK, RET_QK_W), cols(O_RV, RET_V_W), cols(O_RG, RET_V_W),
    ], axis=1).astype(bf)
    w_fm = jnp.concatenate([
        cols(O_DQ, DSA_W), cols(O_IQ, IDX_W), cols(O_DV, DSA_DIM),
        cols(O_IW, IDX_HEADS), zeros(16 - IDX_HEADS), cols(O_RK, RET_QK_W), cols(O_FV, FOX_W),
    ], axis=1).T.astype(bf)

    def lane_pad(v, fill=0.0):
        return jnp.concatenate([v, jnp.full((LANES - v.shape[0],), fill, jnp.float32)])[None, :]

    g = p["dsa_q_norm"]
    g_sw = jnp.concatenate([g[DSA_DIM // 2:], g[:DSA_DIM // 2]])
    scale = DSA_DIM ** -0.5 * LOG2E

    def logit_bound(gq, gk, dim):
        b = BOUND_SLACK * dim * jnp.max(jnp.abs(gq)) * jnp.max(jnp.abs(gk)) * (dim ** -0.5 * LOG2E)
        return b.reshape(1, 1).astype(jnp.float32)

    return {
        "fox_bound": logit_bound(p["fox_q_norm"], p["fox_k_norm"], FOX_DIM),
        "dsa_bound": logit_bound(p["dsa_q_norm"], p["dsa_k_norm"], DSA_DIM),
        "ln1_g": p["ln1_g"][None, :], "ln2_g": p["ln2_g"][None, :],
        "w_tm": w_tm, "w_fm": w_fm,
        "fox_b": lane_pad(p["fox_b_f"]),
        "fq_gain": lane_pad(p["fox_q_norm"]), "fk_gain": lane_pad(p["fox_k_norm"]),
        "dk_gain": jnp.concatenate([p["dsa_k_norm"], jnp.ones((IDX_DIM,), jnp.float32),
                                    jnp.zeros((LANES - DSA_DIM - IDX_DIM,), jnp.float32)])[None, :],
        "dqa": consts["c64t"] * (g * scale)[:, None], "dqb": consts["s64t"] * (g_sw * scale)[:, None],
        "ret_gain": p["ret_out_norm"][:, None, :],
        "w_zg": cols(O_ZG, N_BRANCH * D_MODEL).astype(bf),
        "w_a": p["w_fox_out"].astype(bf), "w_b": p["w_dsa_out"].astype(bf), "w_c": p["w_ret_out"].astype(bf),
        "w_o": p["w_o"].astype(bf),
        "w_g": p["w_ffn_in"][:, :FFN_HIDDEN].astype(bf), "w_u": p["w_ffn_in"][:, FFN_HIDDEN:].astype(bf),
        "w_d": p["w_ffn_out"].astype(bf),
    }


def _layer(x2d, lw, consts, batch, seq):
    (fq, fk, fvt, dk, rq, rk, rv, rg, qt, iqt, vt, iwt, rkt) = _inproj(x2d, lw, consts, batch, seq)
    ya = _fox(fq, fk, fvt, batch, seq)
    yb = _dsa(lw["dsa_bound"], dk, vt, qt, iqt, iwt, consts["ltri_strict"], batch, seq)
    yc = _ret(rq, rk, rkt, rv, rg, consts, lw["ret_gain"], batch, seq)
    x2d = _merge(x2d, ya, yb, yc, lw)
    return _ffn(x2d, lw)


def kernel(x, ln1_g, w_in, fox_b_f, fox_q_norm, fox_k_norm, dsa_q_norm, dsa_k_norm, ret_out_norm,
           w_fox_out, w_dsa_out, w_ret_out, w_o, ln2_g, w_ffn_in, w_ffn_out):
    batch, seq, _ = x.shape
    depth = w_in.shape[0]
    consts = _constants(seq)
    params = dict(ln1_g=ln1_g, w_in=w_in, fox_b_f=fox_b_f, fox_q_norm=fox_q_norm, fox_k_norm=fox_k_norm,
                  dsa_q_norm=dsa_q_norm, dsa_k_norm=dsa_k_norm, ret_out_norm=ret_out_norm,
                  w_fox_out=w_fox_out, w_dsa_out=w_dsa_out, w_ret_out=w_ret_out, w_o=w_o,
                  ln2_g=ln2_g, w_ffn_in=w_ffn_in, w_ffn_out=w_ffn_out)
    x2d = x.reshape(batch * seq, D_MODEL)
    stacked = jax.vmap(lambda p: _layer_weights(p, consts))(params)
    for layer in range(depth):
        lw = {k: v[layer] for k, v in stacked.items()}
        x2d = _layer(x2d, lw, consts, batch, seq)
    return x2d.reshape(batch, seq, D_MODEL)
```

```python
import functools
import math

import jax
import jax.numpy as jnp
import numpy as np
from jax import lax
from jax.experimental import pallas as pl
from jax.experimental.pallas import tpu as pltpu

D_MODEL = 1024
FOX_HEADS = 8
FOX_DIM = 64
DSA_HEADS = 8
DSA_DIM = 64
IDX_HEADS = 8
IDX_DIM = 32
DSA_MAX_TOPK = 256
RET_HEADS = 4
RET_QK_DIM = 64
RET_V_DIM = 128
RET_CHUNK = 128
FFN_HIDDEN = 2816
ROPE_THETA = 10000.0
NORM_EPS = 1e-6
N_BRANCH = 3

FOX_W = FOX_HEADS * FOX_DIM
DSA_W = DSA_HEADS * DSA_DIM
IDX_W = IDX_HEADS * IDX_DIM
RET_QK_W = RET_HEADS * RET_QK_DIM
RET_V_W = RET_HEADS * RET_V_DIM
IN_SIZES = (FOX_W, FOX_W, FOX_W, FOX_HEADS,
            DSA_W, DSA_DIM, DSA_DIM, IDX_W, IDX_DIM, IDX_HEADS,
            RET_QK_W, RET_QK_W, RET_V_W, RET_V_W,
            N_BRANCH * D_MODEL)
IN_OFFS = tuple(int(v) for v in np.cumsum((0,) + IN_SIZES))
(O_FQ, O_FK, O_FV, O_FF, O_DQ, O_DK, O_DV, O_IQ, O_IK, O_IW,
 O_RQ, O_RK, O_RV, O_RG, O_ZG, _) = IN_OFFS

LANES = 128
SUBLANES = 8
BF16_ROWS = 16
VMEM_LIMIT = 56 * 1024 * 1024
NEG_BIG = -1e30
LOG2E = math.log2(math.e)
BOUND_SLACK = 1.02
MIN_DENOM = 2.0 ** -100
HALF_RANGE = 2 ** 15

T_FQ = 0
T_FK = T_FQ + FOX_W
T_FF = T_FK + FOX_W
T_DK = T_FF + LANES
T_RQ = T_DK + LANES
T_RK = T_RQ + RET_QK_W
T_RV = T_RK + RET_QK_W
T_RG = T_RV + RET_V_W
T_COLS = T_RG + RET_V_W
F_DQ = 0
F_IQ = F_DQ + DSA_W
F_DV = F_IQ + IDX_W
F_IW = F_DV + DSA_DIM
F_RK = F_IW + 16
F_FV = F_RK + RET_QK_W
F_ROWS = F_FV + FOX_W

TM_IN = 512
KC = TM_IN
HALF = KC // 2
RB = 32
TQ_FOX = 512
FOX_GROUP = 4
FOX_VROWS = 2 * FOX_DIM + BF16_ROWS
DSA_VROWS = DSA_DIM + BF16_ROWS
TQ_DSA = 256
RET_STEP = 4 * RET_CHUNK
TM_POST = 512


def _cparams(sem):
    return pltpu.CompilerParams(dimension_semantics=sem, vmem_limit_bytes=VMEM_LIMIT)


def _split3(v):
    hi = v.astype(jnp.bfloat16)
    r1 = v - hi.astype(jnp.float32)
    mid = r1.astype(jnp.bfloat16)
    lo = (r1 - mid.astype(jnp.float32)).astype(jnp.bfloat16)
    return hi, mid, lo


def _col_reduce(v, op):
    return op(v.reshape(v.shape[0] // SUBLANES, SUBLANES, v.shape[1]), axis=0)


def _fold16(v):
    parts = [v[r0:r0 + BF16_ROWS] for r0 in range(0, v.shape[0], BF16_ROWS)]
    while len(parts) > 1:
        parts = [parts[i] + parts[i + 1] for i in range(0, len(parts), 2)]
    return parts[0]


def _attend(n_chunks, qk, mask, pv, emit, sa_scr, sb_scr, pa_scr, pb_scr, acc_scr, den_row):
    def store_logits(s_ref, c, half):
        lo = 0
        for g in qk(c, half):
            s_ref[:, lo:lo + g.shape[1]] = g
            lo += g.shape[1]

    def probs(s_ref, c, half, p_ref, shift):
        for r0 in range(0, s_ref.shape[0], RB):
            t = mask(s_ref[r0:r0 + RB, :], c, half, r0)
            if shift is not None:
                t = t - shift
            p_ref[r0:r0 + RB, :] = jnp.exp2(t).astype(p_ref.dtype)

    def run(shift):
        acc_scr[...] = jnp.zeros_like(acc_scr)
        pb_scr[...] = jnp.zeros_like(pb_scr)
        store_logits(sa_scr, 0, 0)

        def chunk(c, carry):
            store_logits(sb_scr, c, 1)
            acc_scr[...] += pv(jnp.maximum(c - 1, 0), 1, pb_scr)
            probs(sa_scr, c, 0, pa_scr, shift)
            store_logits(sa_scr, jnp.minimum(c + 1, n_chunks - 1), 0)
            acc_scr[...] += pv(c, 0, pa_scr)
            probs(sb_scr, c, 1, pb_scr, shift)
            return carry

        lax.fori_loop(0, n_chunks, chunk, 0)
        acc_scr[...] += pv(n_chunks - 1, 1, pb_scr)

    run(None)
    emit()

    @pl.when(jnp.logical_not(jnp.min(acc_scr[den_row:den_row + 1, :]) >= MIN_DENOM))
    def _():
        def col_max(c, mx):
            for half in range(2):
                store_logits(sa_scr, c, half)
                for r0 in range(0, sa_scr.shape[0], RB):
                    mx = jnp.maximum(mx, _col_reduce(mask(sa_scr[r0:r0 + RB, :], c, half, r0), jnp.max))
            return mx
        mx = lax.fori_loop(0, n_chunks, col_max,
                           jnp.full((SUBLANES, acc_scr.shape[1]), NEG_BIG, jnp.float32))
        run(jnp.max(mx, axis=0, keepdims=True))
        emit()


def _inproj_kernel(bound_ref, x_ref, g_ref, wtm_ref, wfm_ref, fb_ref, fqg_ref, fkg_ref, dkg_ref,
                   ltri_ref, eq_ref, ek_ref, oneq_ref, onek_ref,
                   kc_ref, ks_ref, rc_ref, rs_ref,
                   dqa_ref, dqb_ref, iqa_ref, iqb_ref, rka_ref, rkb_ref,
                   fq_out, fk_out, fvt_out, dk_out, rq_out, rk_out, rv_out, rg_out,
                   qt_out, iqt_out, vt_out, iw_out, rkt_out,
                   carry_ref, *, tiles_per_seq):
    tm = x_ref.shape[0]
    i = pl.program_id(0)

    @pl.when(i % tiles_per_seq == 0)
    def _():
        carry_ref[...] = jnp.zeros_like(carry_ref)

    x = x_ref[...]
    ms = jnp.mean(x * x, axis=-1, keepdims=True)
    h = (x * lax.rsqrt(ms + NORM_EPS) * g_ref[...]).astype(jnp.bfloat16)

    def tm_dot(lo, width):
        return jnp.dot(h, wtm_ref[:, lo:lo + width], preferred_element_type=jnp.float32)

    lane = lax.broadcasted_iota(jnp.int32, (tm, LANES), 1)

    ffb = tm_dot(T_FF, LANES) + fb_ref[...]
    lf = (jnp.minimum(ffb, 0.0) - jnp.log1p(jnp.exp(-jnp.abs(ffb)))) * LOG2E
    parts = jnp.concatenate(_split3(lf), axis=1)
    cs = jnp.dot(ltri_ref[...], parts, preferred_element_type=jnp.float32)
    c = cs[:, :LANES] + cs[:, LANES:2 * LANES] + cs[:, 2 * LANES:] + carry_ref[...]
    carry_ref[...] = c[tm - 1:tm, :]
    cparts_q = jnp.concatenate(_split3(c - bound_ref[0, 0]), axis=1)
    cparts_k = jnp.concatenate(_split3(c), axis=1)
    scat_q = jnp.dot(cparts_q, eq_ref[...], preferred_element_type=jnp.float32) + oneq_ref[...]
    scat_k = jnp.dot(cparts_k, ek_ref[...], preferred_element_type=jnp.float32) + onek_ref[...]

    for (lo, gain_ref, scat, out, scale) in ((T_FQ, fqg_ref, scat_q, fq_out, FOX_DIM ** -0.5 * LOG2E),
                                             (T_FK, fkg_ref, scat_k, fk_out, 1.0)):
        z = tm_dot(lo, FOX_W)
        for hd in range(FOX_HEADS):
            blk = z[:, (hd // 2) * LANES:(hd // 2 + 1) * LANES]
            if hd % 2:
                blk = pltpu.roll(blk, FOX_DIM, 1)
            ss = jnp.sum(jnp.where(lane < FOX_DIM, blk * blk, 0.0), axis=-1, keepdims=True) * (1.0 / FOX_DIM)
            nb = blk * lax.rsqrt(ss + NORM_EPS) * (gain_ref[...] * scale)
            out[:, hd * LANES:(hd + 1) * LANES] = (
                nb + scat[:, hd * LANES:(hd + 1) * LANES]).astype(out.dtype)

    zk = tm_dot(T_DK, LANES)
    ssk = jnp.sum(jnp.where(lane < DSA_DIM, zk * zk, 0.0), axis=-1, keepdims=True) * (1.0 / DSA_DIM)
    nk = zk * jnp.where(lane < DSA_DIM, lax.rsqrt(ssk + NORM_EPS), 1.0) * dkg_ref[...]
    partner = jnp.where(
        lane < 32, pltpu.roll(nk, LANES - 32, 1),
        jnp.where(lane < 64, pltpu.roll(nk, 32, 1),
                  jnp.where(lane < 80, pltpu.roll(nk, LANES - 16, 1), pltpu.roll(nk, 16, 1))))
    dk_out[...] = (nk * kc_ref[...] + partner * ks_ref[...]).astype(dk_out.dtype)

    first_half = (lane % RET_QK_DIM) < (RET_QK_DIM // 2)
    for (lo, out, scale) in ((T_RQ, rq_out, RET_QK_DIM ** -0.5), (T_RK, rk_out, 1.0)):
        z = tm_dot(lo, RET_QK_W)
        for j in range(RET_QK_W // LANES):
            blk = z[:, j * LANES:(j + 1) * LANES]
            pr = jnp.where(first_half, pltpu.roll(blk, LANES - 32, 1), pltpu.roll(blk, 32, 1))
            out[:, j * LANES:(j + 1) * LANES] = (
                (blk * rc_ref[...] + pr * rs_ref[...]) * scale).astype(out.dtype)

    rv_out[...] = tm_dot(T_RV, RET_V_W).astype(rv_out.dtype)
    rg_out[...] = tm_dot(T_RG, RET_V_W)

    zt = lax.dot_general(wfm_ref[...], h, (((1,), (1,)), ((), ())),
                         preferred_element_type=jnp.float32)
    nq = tm // TQ_DSA

    def swap_halves(v):
        half = v.shape[0] // 2
        return jnp.concatenate([v[half:], v[:half]], axis=0)

    def ones_row_block(rows, dtype):
        first = lax.broadcasted_iota(jnp.int32, (rows, tm), 0) == 0
        return jnp.where(first, 1.0, 0.0).astype(dtype)

    zeros_q = jnp.zeros((LANES - DSA_DIM, DSA_HEADS * TQ_DSA), qt_out.dtype)
    zeros_i0 = jnp.zeros((DSA_DIM, IDX_HEADS * TQ_DSA), iqt_out.dtype)
    zeros_i1 = jnp.zeros((LANES - DSA_DIM - IDX_DIM, IDX_HEADS * TQ_DSA), iqt_out.dtype)
    for j in range(nq):
        qt_out[j, DSA_DIM:, :] = zeros_q
        iqt_out[j, :DSA_DIM, :] = zeros_i0
        iqt_out[j, DSA_DIM + IDX_DIM:, :] = zeros_i1
    for hd in range(DSA_HEADS):
        xh = zt[F_DQ + hd * DSA_DIM:F_DQ + (hd + 1) * DSA_DIM, :]
        r = lax.rsqrt(jnp.sum(xh * xh, axis=0, keepdims=True) * (1.0 / DSA_DIM) + NORM_EPS)
        o = ((xh * dqa_ref[...] + swap_halves(xh) * dqb_ref[...]) * r).astype(qt_out.dtype)
        for j in range(nq):
            qt_out[j, :DSA_DIM, hd * TQ_DSA:(hd + 1) * TQ_DSA] = o[:, j * TQ_DSA:(j + 1) * TQ_DSA]
    for hd in range(IDX_HEADS):
        xh = zt[F_IQ + hd * IDX_DIM:F_IQ + (hd + 1) * IDX_DIM, :]
        o = (xh * iqa_ref[...] + swap_halves(xh) * iqb_ref[...]).astype(iqt_out.dtype)
        for j in range(nq):
            iqt_out[j, DSA_DIM:DSA_DIM + IDX_DIM, hd * TQ_DSA:(hd + 1) * TQ_DSA] = (
                o[:, j * TQ_DSA:(j + 1) * TQ_DSA])
    vt_out[0, 0, :DSA_DIM, :] = zt[F_DV:F_DV + DSA_DIM, :].astype(vt_out.dtype)
    vt_out[0, 0, DSA_DIM:, :] = ones_row_block(BF16_ROWS, vt_out.dtype)
    iw_out[0] = zt[F_IW:F_IW + IDX_HEADS, :] * ((IDX_DIM * IDX_HEADS) ** -0.5)
    for hd in range(RET_HEADS):
        xh = zt[F_RK + hd * RET_QK_DIM:F_RK + (hd + 1) * RET_QK_DIM, :]
        rkt_out[0, hd * RET_QK_DIM:(hd + 1) * RET_QK_DIM, :] = (
            xh * rka_ref[...] + swap_halves(xh) * rkb_ref[...])
    for hp in range(FOX_HEADS // 2):
        fvt_out[0, hp, 0, :2 * FOX_DIM, :] = (
            zt[F_FV + hp * 2 * FOX_DIM:F_FV + (hp + 1) * 2 * FOX_DIM, :].astype(fvt_out.dtype))
        fvt_out[0, hp, 0, 2 * FOX_DIM:, :] = ones_row_block(BF16_ROWS, fvt_out.dtype)


def _inproj(x2d, lw, consts, batch, seq):
    n = x2d.shape[0]
    tm = min(TM_IN, seq)
    tps = seq // tm
    nqt = tm // TQ_DSA
    grid = (n // tm,)
    bf = jnp.bfloat16

    def full(a):
        return pl.BlockSpec(a.shape, lambda i: (0,) * a.ndim)

    def tok(width):
        return pl.BlockSpec((tm, width), lambda i: (i, 0))

    def pos_tm(width):
        return pl.BlockSpec((tm, width), lambda i: (i % tps, 0))

    def pos_fm(rows):
        return pl.BlockSpec((rows, tm), lambda i: (0, i % tps))

    def fm_out(rows):
        return pl.BlockSpec((1, rows, tm), lambda i: (i // tps, 0, i % tps))

    in_arrays = [lw["fox_bound"], x2d, lw["ln1_g"], lw["w_tm"], lw["w_fm"], lw["fox_b"], lw["fq_gain"], lw["fk_gain"],
                 lw["dk_gain"], consts["ltri"], consts["eq"], consts["ek"], consts["oneq"], consts["onek"],
                 consts["kc"], consts["ks"], consts["rc"], consts["rs"],
                 lw["dqa"], lw["dqb"], consts["iqa"], consts["iqb"], consts["rka"], consts["rkb"]]
    in_specs = [pl.BlockSpec(memory_space=pltpu.SMEM),
                tok(D_MODEL), full(lw["ln1_g"]), full(lw["w_tm"]), full(lw["w_fm"]), full(lw["fox_b"]),
                full(lw["fq_gain"]), full(lw["fk_gain"]), full(lw["dk_gain"]),
                full(consts["ltri"]), full(consts["eq"]), full(consts["ek"]),
                full(consts["oneq"]), full(consts["onek"]),
                pos_tm(LANES), pos_tm(LANES), pos_tm(LANES), pos_tm(LANES),
                pos_fm(DSA_DIM), pos_fm(DSA_DIM), pos_fm(IDX_DIM), pos_fm(IDX_DIM),
                pos_fm(RET_QK_DIM), pos_fm(RET_QK_DIM)]
    out_shape = [
        jax.ShapeDtypeStruct((n, FOX_HEADS * LANES), bf),
        jax.ShapeDtypeStruct((n, FOX_HEADS * LANES), bf),
        jax.ShapeDtypeStruct((batch, FOX_HEADS // 2, tps, FOX_VROWS, tm), bf),
        jax.ShapeDtypeStruct((n, LANES), bf),
        jax.ShapeDtypeStruct((n, RET_QK_W), bf),
        jax.ShapeDtypeStruct((n, RET_QK_W), bf),
        jax.ShapeDtypeStruct((n, RET_V_W), bf),
        jax.ShapeDtypeStruct((n, RET_V_W), jnp.float32),
        jax.ShapeDtypeStruct((n // TQ_DSA, LANES, DSA_HEADS * TQ_DSA), bf),
        jax.ShapeDtypeStruct((n // TQ_DSA, LANES, IDX_HEADS * TQ_DSA), bf),
        jax.ShapeDtypeStruct((batch, tps, DSA_VROWS, tm), bf),
        jax.ShapeDtypeStruct((batch, IDX_HEADS, seq), jnp.float32),
        jax.ShapeDtypeStruct((batch, RET_QK_W, seq), jnp.float32),
    ]
    out_specs = [tok(FOX_HEADS * LANES), tok(FOX_HEADS * LANES),
                 pl.BlockSpec((1, FOX_HEADS // 2, 1, FOX_VROWS, tm), lambda i: (i // tps, 0, i % tps, 0, 0)),
                 tok(LANES), tok(RET_QK_W), tok(RET_QK_W), tok(RET_V_W), tok(RET_V_W),
                 pl.BlockSpec((nqt, LANES, DSA_HEADS * TQ_DSA), lambda i: (i, 0, 0)),
                 pl.BlockSpec((nqt, LANES, IDX_HEADS * TQ_DSA), lambda i: (i, 0, 0)),
                 pl.BlockSpec((1, 1, DSA_VROWS, tm), lambda i: (i // tps, i % tps, 0, 0)),
                 fm_out(IDX_HEADS), fm_out(RET_QK_W)]
    return pl.pallas_call(
        functools.partial(_inproj_kernel, tiles_per_seq=tps),
        grid=grid, in_specs=in_specs, out_specs=out_specs, out_shape=out_shape,
        scratch_shapes=[pltpu.VMEM((1, LANES), jnp.float32)],
        compiler_params=_cparams(("arbitrary",)), name="inproj",
    )(*in_arrays)


def _fox_kernel(q_ref, k_ref, vt_ref, o_ref, sa_scr, sb_scr, pa_scr, pb_scr, acc_scr):
    tq = q_ref.shape[0]
    qi = pl.program_id(2)
    n_chunks = (qi * tq + tq + KC - 1) // KC
    nt = (((1,), (1,)), ((), ()))

    nheads = FOX_GROUP
    qcol = qi * tq + lax.broadcasted_iota(jnp.int32, (RB, tq), 1)
    qcol = jnp.concatenate([qcol] * nheads, axis=1)
    krow = lax.broadcasted_iota(jnp.int32, (RB, nheads * tq), 0)

    def qk(c, half):
        off = pl.multiple_of(c * KC + half * HALF, HALF)
        return [lax.dot_general(k_ref[pl.ds(off, HALF), hh * LANES:(hh + 1) * LANES],
                                q_ref[:, hh * LANES:(hh + 1) * LANES], nt,
                                preferred_element_type=jnp.float32) for hh in range(nheads)]

    def mask(t, c, half, r0):
        return jnp.where(krow + (c * KC + half * HALF + r0) <= qcol, t, NEG_BIG)

    def pv(c, half, p_ref):
        return jnp.concatenate(
            [jnp.dot(vt_ref[0, hh // 2, c, :, half * HALF:(half + 1) * HALF],
                     p_ref[:, hh * tq:(hh + 1) * tq], preferred_element_type=jnp.float32)
             for hh in range(nheads)], axis=1)

    def emit():
        acc = acc_scr[...]
        den = acc[2 * FOX_DIM:2 * FOX_DIM + 1, :]
        for hp in range(nheads // 2):
            ev, od = 2 * hp * tq, (2 * hp + 1) * tq
            out_t = jnp.concatenate([acc[:FOX_DIM, ev:ev + tq] / den[:, ev:ev + tq],
                                     acc[FOX_DIM:2 * FOX_DIM, od:od + tq] / den[:, od:od + tq]], axis=0)
            for j in range(tq // LANES):
                o_ref[j * LANES:(j + 1) * LANES, hp * LANES:(hp + 1) * LANES] = (
                    out_t[:, j * LANES:(j + 1) * LANES].T.astype(o_ref.dtype))

    _attend(n_chunks, qk, mask, pv, emit, sa_scr, sb_scr, pa_scr, pb_scr, acc_scr, 2 * FOX_DIM)


def _fox(fq, fk, fvt, batch, seq):
    tq = min(TQ_FOX, seq)
    nq = seq // tq
    g = FOX_GROUP
    width = g * tq
    return pl.pallas_call(
        _fox_kernel, grid=(batch, FOX_HEADS // g, nq),
        in_specs=[pl.BlockSpec((tq, g * LANES), lambda b, hg, qi: (b * nq + qi, hg)),
                  pl.BlockSpec((seq, g * LANES), lambda b, hg, qi: (b, hg)),
                  pl.BlockSpec((1, g // 2) + fvt.shape[2:], lambda b, hg, qi: (b, hg, 0, 0, 0))],
        out_specs=pl.BlockSpec((tq, g * FOX_DIM), lambda b, hg, qi: (b * nq + qi, hg)),
        out_shape=jax.ShapeDtypeStruct((batch * seq, FOX_W), jnp.bfloat16),
        scratch_shapes=[pltpu.VMEM((HALF, width), jnp.float32), pltpu.VMEM((HALF, width), jnp.float32),
                        pltpu.VMEM((HALF, width), jnp.bfloat16), pltpu.VMEM((HALF, width), jnp.bfloat16),
                        pltpu.VMEM((FOX_VROWS, width), jnp.float32)],
        compiler_params=_cparams(("parallel", "parallel", "arbitrary")), name="fox",
    )(fq, fk, fvt)


def _dsa_kernel(bound_ref, k_ref, vt_ref, qt_ref, iqt_ref, iw_ref, ltri_ref, o_ref,
                key_scr, hi_scr, lo_scr, bias_scr, sa_scr, sb_scr, pa_scr, pb_scr, acc_scr, *, topk):
    tq = TQ_DSA
    nh = DSA_HEADS
    selected_bias = -bound_ref[0, 0]
    qi = pl.program_id(1)
    n_chunks = (qi * tq + tq + KC - 1) // KC

    qpos = qi * tq + lax.broadcasted_iota(jnp.int32, (KC, tq), 1)
    krow = lax.broadcasted_iota(jnp.int32, (KC, tq), 0)

    def score_chunk(c, carry):
        off = pl.multiple_of(c * KC, KC)
        rel = jnp.dot(k_ref[pl.ds(off, KC), :], iqt_ref[0], preferred_element_type=jnp.float32)
        score = jnp.maximum(rel[:, :tq], 0.0) * iw_ref[0, 0:1, :]
        for hd in range(1, IDX_HEADS):
            score = score + jnp.maximum(rel[:, hd * tq:(hd + 1) * tq], 0.0) * iw_ref[0, hd:hd + 1, :]
        score = jnp.where(krow + off <= qpos, score, -jnp.inf)
        bits = pltpu.bitcast(score, jnp.int32)
        sign = bits >> 31
        key = (bits ^ (sign & jnp.int32(0x7FFFFFFF))) - sign
        key_scr[pl.ds(off, KC), :] = key
        hi_scr[pl.ds(off, KC), :] = (key >> 16).astype(jnp.int16)
        lo_scr[pl.ds(off, KC), :] = ((key & jnp.int32(0xFFFF)) - HALF_RANGE).astype(jnp.int16)
        return carry

    lax.fori_loop(0, n_chunks, score_chunk, 0)

    def count16_ge(plane_scr, thr):
        thr16 = thr.astype(jnp.int16)

        def body(c, acc):
            off = pl.multiple_of(c * KC, KC)
            hit = jnp.where(plane_scr[pl.ds(off, KC), :] >= thr16, jnp.int16(1), jnp.int16(0))
            return acc + _fold16(hit)
        acc = lax.fori_loop(0, n_chunks, body, jnp.zeros((BF16_ROWS, tq), jnp.int16))
        return jnp.sum(acc.astype(jnp.int32), axis=0, keepdims=True)

    def kth_largest16(plane_scr, kth):
        def bit_step(b, carry):
            thr, cnt = carry
            bit = jnp.left_shift(jnp.int32(1), 15 - b)
            cand = jnp.where(b == 0, jnp.zeros_like(thr), thr | bit)
            cand_cnt = count16_ge(plane_scr, cand)
            take = cand_cnt >= kth
            return jnp.where(take, cand, thr), jnp.where(take, cand_cnt, cnt)
        return lax.fori_loop(0, 16, bit_step, (jnp.full((1, tq), -HALF_RANGE, jnp.int32),
                                               jnp.full((1, tq), n_chunks * KC, jnp.int32)))

    def count_ge(thr):
        def body(c, acc):
            off = pl.multiple_of(c * KC, KC)
            hit = jnp.where(key_scr[pl.ds(off, KC), :] >= thr, 1, 0).astype(jnp.int32)
            return acc + _col_reduce(hit, jnp.sum)
        acc = lax.fori_loop(0, n_chunks, body, jnp.zeros((SUBLANES, tq), jnp.int32))
        return jnp.sum(acc, axis=0, keepdims=True)

    thr_hi, cnt_hi = kth_largest16(hi_scr, topk)
    thr_hi16 = thr_hi.astype(jnp.int16)

    def keep_candidates(c, acc):
        off = pl.multiple_of(c * KC, KC)
        hi = hi_scr[pl.ds(off, KC), :]
        lo_scr[pl.ds(off, KC), :] = jnp.where(hi == thr_hi16, lo_scr[pl.ds(off, KC), :], jnp.int16(-HALF_RANGE))
        return acc + _fold16(jnp.where(hi > thr_hi16, jnp.int16(1), jnp.int16(0)))

    above = lax.fori_loop(0, n_chunks, keep_candidates, jnp.zeros((BF16_ROWS, tq), jnp.int16))
    n_above = jnp.sum(above.astype(jnp.int32), axis=0, keepdims=True)
    thr_lo, cnt_lo = kth_largest16(lo_scr, topk - n_above)
    thr = thr_hi * (2 * HALF_RANGE) + (thr_lo + HALF_RANGE)
    n_ge = jnp.where(thr_lo > -HALF_RANGE, n_above + cnt_lo, cnt_hi)
    has_ties = jnp.max(n_ge) > topk

    @pl.when(jnp.logical_not(has_ties))
    def _():
        def body(c, carry):
            off = pl.multiple_of(c * KC, KC)
            bias_scr[pl.ds(off, KC), :] = jnp.where(key_scr[pl.ds(off, KC), :] >= thr, selected_bias, NEG_BIG)
            return carry
        lax.fori_loop(0, n_chunks, body, 0)

    @pl.when(has_ties)
    def _():
        int_max = jnp.int32(2 ** 31 - 1)
        n_gt = jnp.where(thr == int_max, 0, count_ge(jnp.where(thr == int_max, thr, thr + 1)))
        need = (topk - n_gt).astype(jnp.float32)

        def body(c, seen):
            off = pl.multiple_of(c * KC, KC)
            ks = key_scr[pl.ds(off, KC), :]
            eq = ks == thr
            eqf = jnp.where(eq, 1.0, 0.0)
            before = jnp.dot(ltri_ref[...], eqf.astype(jnp.bfloat16),
                             preferred_element_type=jnp.float32) + seen
            sel = jnp.logical_or(ks > thr, jnp.logical_and(eq, before < need))
            sel = jnp.logical_and(sel, krow + off <= qpos)
            bias_scr[pl.ds(off, KC), :] = jnp.where(sel, selected_bias, NEG_BIG)
            return seen + jnp.sum(eqf, axis=0, keepdims=True)
        lax.fori_loop(0, n_chunks, body, jnp.zeros((1, tq), jnp.float32))

    def qk(c, half):
        off = pl.multiple_of(c * KC + half * HALF, HALF)
        return [jnp.dot(k_ref[pl.ds(off, HALF), :], qt_ref[0], preferred_element_type=jnp.float32)]

    def mask(t, c, half, r0):
        b = bias_scr[pl.ds(pl.multiple_of(c * KC + half * HALF + r0, RB), RB), :]
        return t + jnp.concatenate([b] * nh, axis=1)

    def pv(c, half, p_ref):
        return jnp.dot(vt_ref[0, c, :, half * HALF:(half + 1) * HALF], p_ref[...],
                       preferred_element_type=jnp.float32)

    def emit():
        acc = acc_scr[...]
        out_t = acc[:DSA_DIM, :] / acc[DSA_DIM:DSA_DIM + 1, :]
        for hp in range(nh // 2):
            pair = jnp.concatenate([out_t[:, 2 * hp * tq:(2 * hp + 1) * tq],
                                    out_t[:, (2 * hp + 1) * tq:(2 * hp + 2) * tq]], axis=0)
            o_ref[:, hp * LANES:(hp + 1) * LANES] = pair.T.astype(o_ref.dtype)

    _attend(n_chunks, qk, mask, pv, emit, sa_scr, sb_scr, pa_scr, pb_scr, acc_scr, DSA_DIM)


def _dsa(bound, dk, vt, qt, iqt, iwt, ltri_strict, batch, seq):
    nq = seq // TQ_DSA
    assert vt.shape[3] == KC and ltri_strict.shape[0] == KC and seq % KC == 0
    topk = min(DSA_MAX_TOPK, seq // 4)
    width = DSA_HEADS * TQ_DSA
    return pl.pallas_call(
        functools.partial(_dsa_kernel, topk=topk),
        grid=(batch, nq),
        in_specs=[pl.BlockSpec(memory_space=pltpu.SMEM),
                  pl.BlockSpec((seq, LANES), lambda b, qi: (b, 0)),
                  pl.BlockSpec((1,) + vt.shape[1:], lambda b, qi: (b, 0, 0, 0)),
                  pl.BlockSpec((1, LANES, width), lambda b, qi: (b * nq + qi, 0, 0)),
                  pl.BlockSpec((1, LANES, IDX_HEADS * TQ_DSA), lambda b, qi: (b * nq + qi, 0, 0)),
                  pl.BlockSpec((1, IDX_HEADS, TQ_DSA), lambda b, qi: (b, 0, qi)),
                  pl.BlockSpec(ltri_strict.shape, lambda b, qi: (0, 0))],
        out_specs=pl.BlockSpec((TQ_DSA, DSA_W), lambda b, qi: (b * nq + qi, 0)),
        out_shape=jax.ShapeDtypeStruct((batch * seq, DSA_W), jnp.bfloat16),
        scratch_shapes=[pltpu.VMEM((seq, TQ_DSA), jnp.int32),
                        pltpu.VMEM((seq, TQ_DSA), jnp.int16),
                        pltpu.VMEM((seq, TQ_DSA), jnp.int16),
                        pltpu.VMEM((seq, TQ_DSA), jnp.float32),
                        pltpu.VMEM((HALF, width), jnp.float32), pltpu.VMEM((HALF, width), jnp.float32),
                        pltpu.VMEM((HALF, width), jnp.bfloat16), pltpu.VMEM((HALF, width), jnp.bfloat16),
                        pltpu.VMEM((DSA_VROWS, width), jnp.float32)],
        compiler_params=_cparams(("parallel", "arbitrary")), name="dsa",
    )(bound, dk, vt, qt, iqt, iwt, ltri_strict)


def _ret_kernel(q_ref, k_ref, kt_ref, v_ref, g_ref, din_ref, dq_ref, dk_ref, dc_ref, gain_ref,
                o_ref, state_scr):
    t = pl.program_id(1)
    c = RET_CHUNK

    @pl.when(t == 0)
    def _():
        state_scr[...] = jnp.zeros_like(state_scr)

    for hd in range(RET_HEADS):
        state = state_scr[hd]
        for j in range(q_ref.shape[0] // c):
            rows = slice(j * c, (j + 1) * c)
            q = q_ref[rows, hd * RET_QK_DIM:(hd + 1) * RET_QK_DIM]
            k = k_ref[rows, hd * RET_QK_DIM:(hd + 1) * RET_QK_DIM]
            v = v_ref[rows, hd * RET_V_DIM:(hd + 1) * RET_V_DIM]
            attn = lax.dot_general(q, k, (((1,), (1,)), ((), ())),
                                   preferred_element_type=jnp.float32) * din_ref[hd]
            inner = jnp.dot(attn.astype(jnp.bfloat16), v, preferred_element_type=jnp.float32)
            cross = jnp.dot(q, state.astype(jnp.bfloat16),
                            preferred_element_type=jnp.float32) * dq_ref[hd]
            ktd = (kt_ref[0, hd * RET_QK_DIM:(hd + 1) * RET_QK_DIM, rows] * dk_ref[hd]).astype(jnp.bfloat16)
            state = dc_ref[hd] * state + jnp.dot(ktd, v, preferred_element_type=jnp.float32)
            y = inner + cross
            yn = y * lax.rsqrt(jnp.mean(y * y, axis=-1, keepdims=True) + NORM_EPS) * gain_ref[hd]
            gate = g_ref[rows, hd * RET_V_DIM:(hd + 1) * RET_V_DIM]
            o_ref[rows, hd * RET_V_DIM:(hd + 1) * RET_V_DIM] = (
                yn * (gate * jax.nn.sigmoid(gate))).astype(o_ref.dtype)
        state_scr[hd] = state


def _ret(rq, rk, rkt, rv, rg, consts, gain, batch, seq):
    c = min(RET_STEP, seq)
    n = seq // c

    def tok(width):
        return pl.BlockSpec((c, width), lambda b, t: (b * n + t, 0))

    def full(a):
        return pl.BlockSpec(a.shape, lambda b, t: (0,) * a.ndim)

    return pl.pallas_call(
        _ret_kernel, grid=(batch, n),
        in_specs=[tok(RET_QK_W), tok(RET_QK_W),
                  pl.BlockSpec((1, RET_QK_W, c), lambda b, t: (b, 0, t)),
                  tok(RET_V_W), tok(RET_V_W),
                  full(consts["ret_din"]), full(consts["ret_dq"]), full(consts["ret_dk"]),
                  full(consts["ret_dc"]), full(gain)],
        out_specs=tok(RET_V_W),
        out_shape=jax.ShapeDtypeStruct((batch * seq, RET_V_W), jnp.bfloat16),
        scratch_shapes=[pltpu.VMEM((RET_HEADS, RET_QK_DIM, RET_V_DIM), jnp.float32)],
        compiler_params=_cparams(("parallel", "arbitrary")), name="ret",
    )(rq, rk, rkt, rv, rg, consts["ret_din"], consts["ret_dq"], consts["ret_dk"], consts["ret_dc"], gain)


def _merge_kernel(x_ref, g_ref, ya_ref, yb_ref, yc_ref, wzg_ref, wa_ref, wb_ref, wc_ref, wo_ref, o_ref):
    x = x_ref[...]
    ms = jnp.mean(x * x, axis=-1, keepdims=True)
    h = (x * lax.rsqrt(ms + NORM_EPS) * g_ref[...]).astype(jnp.bfloat16)
    merged = None
    for j, (y_ref, w_ref) in enumerate(((ya_ref, wa_ref), (yb_ref, wb_ref), (yc_ref, wc_ref))):
        gate = jax.nn.sigmoid(jnp.dot(h, wzg_ref[:, j * D_MODEL:(j + 1) * D_MODEL],
                                      preferred_element_type=jnp.float32))
        term = gate * jnp.dot(y_ref[...], w_ref[...], preferred_element_type=jnp.float32)
        merged = term if merged is None else merged + term
    o_ref[...] = x + jnp.dot(merged.astype(jnp.bfloat16), wo_ref[...],
                             preferred_element_type=jnp.float32)


def _ffn_kernel(x_ref, g_ref, wg_ref, wu_ref, wd_ref, o_ref, *, chunk):
    x = x_ref[...]
    ms = jnp.mean(x * x, axis=-1, keepdims=True)
    h = (x * lax.rsqrt(ms + NORM_EPS) * g_ref[...]).astype(jnp.bfloat16)
    acc = x
    for lo in range(0, FFN_HIDDEN, chunk):
        gt = jnp.dot(h, wg_ref[:, lo:lo + chunk], preferred_element_type=jnp.float32)
        up = jnp.dot(h, wu_ref[:, lo:lo + chunk], preferred_element_type=jnp.float32)
        act = (gt * jax.nn.sigmoid(gt) * up).astype(jnp.bfloat16)
        acc = acc + jnp.dot(act, wd_ref[lo:lo + chunk, :], preferred_element_type=jnp.float32)
    o_ref[...] = acc


def _row_call(kernel, name, order):
    n = order[0][1].shape[0]
    tm = min(TM_POST, n)
    arrays, specs = [], []
    for kind, a in order:
        arrays.append(a)
        if kind == "row":
            specs.append(pl.BlockSpec((tm, a.shape[1]), lambda i: (i, 0)))
        else:
            specs.append(pl.BlockSpec(a.shape, lambda i, nd=a.ndim: (0,) * nd))
    return pl.pallas_call(
        kernel, grid=(n // tm,), in_specs=specs,
        out_specs=pl.BlockSpec((tm, D_MODEL), lambda i: (i, 0)),
        out_shape=jax.ShapeDtypeStruct((n, D_MODEL), jnp.float32),
        compiler_params=_cparams(("parallel",)), name=name,
    )(*arrays)


def _merge(x2d, ya, yb, yc, lw):
    order = [("row", x2d), ("full", lw["ln1_g"]), ("row", ya), ("row", yb), ("row", yc),
             ("full", lw["w_zg"]), ("full", lw["w_a"]), ("full", lw["w_b"]), ("full", lw["w_c"]),
             ("full", lw["w_o"])]
    return _row_call(_merge_kernel, "merge", order)


def _ffn(x2d, lw):
    order = [("row", x2d), ("full", lw["ln2_g"]), ("full", lw["w_g"]), ("full", lw["w_u"]),
             ("full", lw["w_d"])]
    return _row_call(functools.partial(_ffn_kernel, chunk=256), "ffn", order)


def _rope_cs(seq, dim):
    half = dim // 2
    inv_freq = ROPE_THETA ** (-jnp.arange(half, dtype=jnp.float32) / half)
    ang = jnp.arange(seq, dtype=jnp.float32)[:, None] * inv_freq[None, :]
    return jnp.cos(ang), jnp.sin(ang)


def _constants(seq):
    tm = min(TM_IN, seq)
    bf = jnp.bfloat16
    cos64, sin64 = _rope_cs(seq, 64)
    cos32, sin32 = _rope_cs(seq, 32)
    z32 = jnp.zeros((seq, 32), jnp.float32)
    c64 = jnp.concatenate([cos64, cos64], axis=1)
    s64 = jnp.concatenate([-sin64, sin64], axis=1)
    c32 = jnp.concatenate([cos32, cos32], axis=1)
    s32 = jnp.concatenate([-sin32, sin32], axis=1)
    consts = {
        "kc": jnp.concatenate([c64, c32, z32], axis=1), "ks": jnp.concatenate([s64, s32, z32], axis=1),
        "rc": jnp.concatenate([c64, c64], axis=1), "rs": jnp.concatenate([s64, s64], axis=1),
        "c64t": c64.T, "s64t": s64.T,
        "iqa": c32.T, "iqb": s32.T, "rka": c64.T, "rkb": s64.T,
        "ltri": jnp.tril(jnp.ones((tm, tm), jnp.float32)).astype(bf),
        "ltri_strict": jnp.tril(jnp.ones((KC, KC), jnp.float32), -1).astype(bf),
    }
    eq = np.zeros((3 * LANES, FOX_HEADS * LANES), np.float32)
    ek = np.zeros((3 * LANES, FOX_HEADS * LANES), np.float32)
    oneq = np.zeros((1, FOX_HEADS * LANES), np.float32)
    onek = np.zeros((1, FOX_HEADS * LANES), np.float32)
    for hd in range(FOX_HEADS):
        base = hd * LANES + FOX_DIM
        for part in range(3):
            eq[part * LANES + hd, base + part] = 1.0
            ek[part * LANES + hd, base + 3 + part] = -1.0
            oneq[0, base + 3 + part] = 1.0
            onek[0, base + part] = 1.0
    consts.update(eq=jnp.asarray(eq, bf), ek=jnp.asarray(ek, bf), oneq=jnp.asarray(oneq), onek=jnp.asarray(onek))
    log_g = jnp.log1p(-(2.0 ** (-5.0 - jnp.arange(RET_HEADS, dtype=jnp.float32))))
    pos = jnp.arange(RET_CHUNK, dtype=jnp.float32)
    diff = pos[:, None] - pos[None, :]
    din = jnp.where(diff >= 0, jnp.exp(jnp.maximum(diff, 0.0)[None] * log_g[:, None, None]), 0.0)
    dq = jnp.exp((pos + 1.0)[None] * log_g[:, None])
    dk = jnp.exp((RET_CHUNK - 1.0 - pos)[None] * log_g[:, None])
    dc = jnp.exp(RET_CHUNK * log_g)
    consts.update(ret_din=din, ret_dq=dq[:, :, None], ret_dk=dk[:, None, :],
                  ret_dc=jnp.broadcast_to(dc[:, None, None], (RET_HEADS, 1, LANES)))
    return consts


def _layer_weights(p, consts):
    bf = jnp.bfloat16
    w_in = p["w_in"]

    def cols(off, size):
        return w_in[:, off:off + size]

    zeros = lambda n: jnp.zeros((D_MODEL, n), jnp.float32)
    w_tm = jnp.concatenate([
        cols(O_FQ, FOX_W), cols(O_FK, FOX_W),
        cols(O_FF, FOX_HEADS), zeros(LANES - FOX_HEADS),
        cols(O_DK, DSA_DIM), cols(O_IK, IDX_DIM), zeros(LANES - DSA_DIM - IDX_DIM),
        cols(O_RQ, RET_QK_W), cols(O_RK, RET_QK_W), cols(O_RV, RET_V_W), cols(O_RG, RET_V_W),
    ], axis=1).astype(bf)
    w_fm = jnp.concatenate([
        cols(O_DQ, DSA_W), cols(O_IQ, IDX_W), cols(O_DV, DSA_DIM),
        cols(O_IW, IDX_HEADS), zeros(16 - IDX_HEADS), cols(O_RK, RET_QK_W), cols(O_FV, FOX_W),
    ], axis=1).T.astype(bf)

    def lane_pad(v, fill=0.0):
        return jnp.concatenate([v, jnp.full((LANES - v.shape[0],), fill, jnp.float32)])[None, :]

    g = p["dsa_q_norm"]
    g_sw = jnp.concatenate([g[DSA_DIM // 2:], g[:DSA_DIM // 2]])
    scale = DSA_DIM ** -0.5 * LOG2E

    def logit_bound(gq, gk, dim):
        b = BOUND_SLACK * dim * jnp.max(jnp.abs(gq)) * jnp.max(jnp.abs(gk)) * (dim ** -0.5 * LOG2E)
        return b.reshape(1, 1).astype(jnp.float32)

    return {
        "fox_bound": logit_bound(p["fox_q_norm"], p["fox_k_norm"], FOX_DIM),
        "dsa_bound": logit_bound(p["dsa_q_norm"], p["dsa_k_norm"], DSA_DIM),
        "ln1_g": p["ln1_g"][None, :], "ln2_g": p["ln2_g"][None, :],
        "w_tm": w_tm, "w_fm": w_fm,
        "fox_b": lane_pad(p["fox_b_f"]),
        "fq_gain": lane_pad(p["fox_q_norm"]), "fk_gain": lane_pad(p["fox_k_norm"]),
        "dk_gain": jnp.concatenate([p["dsa_k_norm"], jnp.ones((IDX_DIM,), jnp.float32),
                                    jnp.zeros((LANES - DSA_DIM - IDX_DIM,), jnp.float32)])[None, :],
        "dqa": consts["c64t"] * (g * scale)[:, None], "dqb": consts["s64t"] * (g_sw * scale)[:, None],
        "ret_gain": p["ret_out_norm"][:, None, :],
        "w_zg": cols(O_ZG, N_BRANCH * D_MODEL).astype(bf),
        "w_a": p["w_fox_out"].astype(bf), "w_b": p["w_dsa_out"].astype(bf), "w_c": p["w_ret_out"].astype(bf),
        "w_o": p["w_o"].astype(bf),
        "w_g": p["w_ffn_in"][:, :FFN_HIDDEN].astype(bf), "w_u": p["w_ffn_in"][:, FFN_HIDDEN:].astype(bf),
        "w_d": p["w_ffn_out"].astype(bf),
    }


def _layer(x2d, lw, consts, batch, seq):
    (fq, fk, fvt, dk, rq, rk, rv, rg, qt, iqt, vt, iwt, rkt) = _inproj(x2d, lw, consts, batch, seq)
    ya = _fox(fq, fk, fvt, batch, seq)
    yb = _dsa(lw["dsa_bound"], dk, vt, qt, iqt, iwt, consts["ltri_strict"], batch, seq)
    yc = _ret(rq, rk, rkt, rv, rg, consts, lw["ret_gain"], batch, seq)
    x2d = _merge(x2d, ya, yb, yc, lw)
    return _ffn(x2d, lw)


def kernel(x, ln1_g, w_in, fox_b_f, fox_q_norm, fox_k_norm, dsa_q_norm, dsa_k_norm, ret_out_norm,
           w_fox_out, w_dsa_out, w_ret_out, w_o, ln2_g, w_ffn_in, w_ffn_out):
    batch, seq, _ = x.shape
    depth = w_in.shape[0]
    consts = _constants(seq)
    params = dict(ln1_g=ln1_g, w_in=w_in, fox_b_f=fox_b_f, fox_q_norm=fox_q_norm, fox_k_norm=fox_k_norm,
                  dsa_q_norm=dsa_q_norm, dsa_k_norm=dsa_k_norm, ret_out_norm=ret_out_norm,
                  w_fox_out=w_fox_out, w_dsa_out=w_dsa_out, w_ret_out=w_ret_out, w_o=w_o,
                  ln2_g=ln2_g, w_ffn_in=w_ffn_in, w_ffn_out=w_ffn_out)
    x2d = x.reshape(batch * seq, D_MODEL)
    stacked = jax.vmap(lambda p: _layer_weights(p, consts))(params)
    for layer in range(depth):
        lw = {k: v[layer] for k, v in stacked.items()}
        x2d = _layer(x2d, lw, consts, batch, seq)
    return x2d.reshape(batch, seq, D_MODEL)
```

```python
import functools
import math

import jax
import jax.numpy as jnp
import numpy as np
from jax import lax
from jax.experimental import pallas as pl
from jax.experimental.pallas import tpu as pltpu

D_MODEL = 1024
FOX_HEADS = 8
FOX_DIM = 64
DSA_HEADS = 8
DSA_DIM = 64
IDX_HEADS = 8
IDX_DIM = 32
DSA_MAX_TOPK = 256
RET_HEADS = 4
RET_QK_DIM = 64
RET_V_DIM = 128
RET_CHUNK = 128
FFN_HIDDEN = 2816
ROPE_THETA = 10000.0
NORM_EPS = 1e-6
N_BRANCH = 3

FOX_W = FOX_HEADS * FOX_DIM
DSA_W = DSA_HEADS * DSA_DIM
IDX_W = IDX_HEADS * IDX_DIM
RET_QK_W = RET_HEADS * RET_QK_DIM
RET_V_W = RET_HEADS * RET_V_DIM
IN_SIZES = (FOX_W, FOX_W, FOX_W, FOX_HEADS,
            DSA_W, DSA_DIM, DSA_DIM, IDX_W, IDX_DIM, IDX_HEADS,
            RET_QK_W, RET_QK_W, RET_V_W, RET_V_W,
            N_BRANCH * D_MODEL)
IN_OFFS = tuple(int(v) for v in np.cumsum((0,) + IN_SIZES))
(O_FQ, O_FK, O_FV, O_FF, O_DQ, O_DK, O_DV, O_IQ, O_IK, O_IW,
 O_RQ, O_RK, O_RV, O_RG, O_ZG, _) = IN_OFFS

LANES = 128
SUBLANES = 8
BF16_ROWS = 16
VMEM_LIMIT = 56 * 1024 * 1024
NEG_BIG = -1e30
LOG2E = math.log2(math.e)
BOUND_SLACK = 1.02
MIN_DENOM = 2.0 ** -100
HALF_RANGE = 2 ** 15

T_FQ = 0
T_FK = T_FQ + FOX_W
T_FF = T_FK + FOX_W
T_DK = T_FF + LANES
T_RQ = T_DK + LANES
T_RK = T_RQ + RET_QK_W
T_RV = T_RK + RET_QK_W
T_RG = T_RV + RET_V_W
T_COLS = T_RG + RET_V_W
F_DQ = 0
F_IQ = F_DQ + DSA_W
F_DV = F_IQ + IDX_W
F_IW = F_DV + DSA_DIM
F_RK = F_IW + 16
F_FV = F_RK + RET_QK_W
F_ROWS = F_FV + FOX_W

TM_IN = 512
KC = TM_IN
HALF = KC // 2
RB = 32
TQ_FOX = 512
FOX_GROUP = 4
FOX_VROWS = 2 * FOX_DIM + BF16_ROWS
DSA_VROWS = DSA_DIM + BF16_ROWS
TQ_DSA = 256
RET_STEP = 4 * RET_CHUNK
TM_POST = 512


def _cparams(sem):
    return pltpu.CompilerParams(dimension_semantics=sem, vmem_limit_bytes=VMEM_LIMIT)


def _split3(v):
    hi = v.astype(jnp.bfloat16)
    r1 = v - hi.astype(jnp.float32)
    mid = r1.astype(jnp.bfloat16)
    lo = (r1 - mid.astype(jnp.float32)).astype(jnp.bfloat16)
    return hi, mid, lo


def _col_reduce(v, op):
    return op(v.reshape(v.shape[0] // SUBLANES, SUBLANES, v.shape[1]), axis=0)


def _fold16(v):
    parts = [v[r0:r0 + BF16_ROWS] for r0 in range(0, v.shape[0], BF16_ROWS)]
    while len(parts) > 1:
        parts = [parts[i] + parts[i + 1] for i in range(0, len(parts), 2)]
    return parts[0]


def _attend(n_chunks, qk, mask, pv, emit, sa_scr, sb_scr, pa_scr, pb_scr, acc_scr, den_row, between=None):
    def store_logits(s_ref, c, half):
        lo = 0
        for g in qk(c, half):
            s_ref[:, lo:lo + g.shape[1]] = g
            lo += g.shape[1]

    def probs(s_ref, c, half, p_ref, shift):
        for r0 in range(0, s_ref.shape[0], RB):
            t = mask(s_ref[r0:r0 + RB, :], c, half, r0)
            if shift is not None:
                t = t - shift
            p_ref[r0:r0 + RB, :] = jnp.exp2(t).astype(p_ref.dtype)

    def run(shift, first=False):
        acc_scr[...] = jnp.zeros_like(acc_scr)
        pb_scr[...] = jnp.zeros_like(pb_scr)
        store_logits(sa_scr, 0, 0)
        if first and between is not None:
            between()

        def chunk(c, carry):
            store_logits(sb_scr, c, 1)
            acc_scr[...] += pv(jnp.maximum(c - 1, 0), 1, pb_scr)
            probs(sa_scr, c, 0, pa_scr, shift)
            store_logits(sa_scr, jnp.minimum(c + 1, n_chunks - 1), 0)
            acc_scr[...] += pv(c, 0, pa_scr)
            probs(sb_scr, c, 1, pb_scr, shift)
            return carry

        lax.fori_loop(0, n_chunks, chunk, 0)
        acc_scr[...] += pv(n_chunks - 1, 1, pb_scr)

    run(None, first=True)
    emit()

    @pl.when(jnp.logical_not(jnp.min(acc_scr[den_row:den_row + 1, :]) >= MIN_DENOM))
    def _():
        def col_max(c, mx):
            for half in range(2):
                store_logits(sa_scr, c, half)
                for r0 in range(0, sa_scr.shape[0], RB):
                    mx = jnp.maximum(mx, _col_reduce(mask(sa_scr[r0:r0 + RB, :], c, half, r0), jnp.max))
            return mx
        mx = lax.fori_loop(0, n_chunks, col_max,
                           jnp.full((SUBLANES, acc_scr.shape[1]), NEG_BIG, jnp.float32))
        run(jnp.max(mx, axis=0, keepdims=True))
        emit()


def _inproj_kernel(bound_ref, x_ref, g_ref, wtm_ref, wfm_ref, fb_ref, fqg_ref, fkg_ref, dkg_ref,
                   ltri_ref, eq_ref, ek_ref, oneq_ref, onek_ref,
                   kc_ref, ks_ref, rc_ref, rs_ref,
                   dqa_ref, dqb_ref, iqa_ref, iqb_ref, rka_ref, rkb_ref,
                   fq_out, fk_out, fvt_out, dk_out, rq_out, rk_out, rv_out, rg_out,
                   qt_out, iqt_out, vt_out, iw_out, rkt_out,
                   carry_ref, *, tiles_per_seq):
    tm = x_ref.shape[0]
    i = pl.program_id(0)

    @pl.when(i % tiles_per_seq == 0)
    def _():
        carry_ref[...] = jnp.zeros_like(carry_ref)

    x = x_ref[...]
    ms = jnp.mean(x * x, axis=-1, keepdims=True)
    h = (x * lax.rsqrt(ms + NORM_EPS) * g_ref[...]).astype(jnp.bfloat16)

    def tm_dot(lo, width):
        return jnp.dot(h, wtm_ref[:, lo:lo + width], preferred_element_type=jnp.float32)

    lane = lax.broadcasted_iota(jnp.int32, (tm, LANES), 1)

    ffb = tm_dot(T_FF, LANES) + fb_ref[...]
    lf = (jnp.minimum(ffb, 0.0) - jnp.log1p(jnp.exp(-jnp.abs(ffb)))) * LOG2E
    parts = jnp.concatenate(_split3(lf), axis=1)
    cs = jnp.dot(ltri_ref[...], parts, preferred_element_type=jnp.float32)
    c = cs[:, :LANES] + cs[:, LANES:2 * LANES] + cs[:, 2 * LANES:] + carry_ref[...]
    carry_ref[...] = c[tm - 1:tm, :]
    cparts_q = jnp.concatenate(_split3(c - bound_ref[0, 0]), axis=1)
    cparts_k = jnp.concatenate(_split3(c), axis=1)
    scat_q = jnp.dot(cparts_q, eq_ref[...], preferred_element_type=jnp.float32) + oneq_ref[...]
    scat_k = jnp.dot(cparts_k, ek_ref[...], preferred_element_type=jnp.float32) + onek_ref[...]

    for (lo, gain_ref, scat, out, scale) in ((T_FQ, fqg_ref, scat_q, fq_out, FOX_DIM ** -0.5 * LOG2E),
                                             (T_FK, fkg_ref, scat_k, fk_out, 1.0)):
        z = tm_dot(lo, FOX_W)
        for hd in range(FOX_HEADS):
            blk = z[:, (hd // 2) * LANES:(hd // 2 + 1) * LANES]
            if hd % 2:
                blk = pltpu.roll(blk, FOX_DIM, 1)
            ss = jnp.sum(jnp.where(lane < FOX_DIM, blk * blk, 0.0), axis=-1, keepdims=True) * (1.0 / FOX_DIM)
            nb = blk * lax.rsqrt(ss + NORM_EPS) * (gain_ref[...] * scale)
            out[:, hd * LANES:(hd + 1) * LANES] = (
                nb + scat[:, hd * LANES:(hd + 1) * LANES]).astype(out.dtype)

    zk = tm_dot(T_DK, LANES)
    ssk = jnp.sum(jnp.where(lane < DSA_DIM, zk * zk, 0.0), axis=-1, keepdims=True) * (1.0 / DSA_DIM)
    nk = zk * jnp.where(lane < DSA_DIM, lax.rsqrt(ssk + NORM_EPS), 1.0) * dkg_ref[...]
    partner = jnp.where(
        lane < 32, pltpu.roll(nk, LANES - 32, 1),
        jnp.where(lane < 64, pltpu.roll(nk, 32, 1),
                  jnp.where(lane < 80, pltpu.roll(nk, LANES - 16, 1), pltpu.roll(nk, 16, 1))))
    dk_out[...] = (nk * kc_ref[...] + partner * ks_ref[...]).astype(dk_out.dtype)

    first_half = (lane % RET_QK_DIM) < (RET_QK_DIM // 2)
    for (lo, out, scale) in ((T_RQ, rq_out, RET_QK_DIM ** -0.5), (T_RK, rk_out, 1.0)):
        z = tm_dot(lo, RET_QK_W)
        for j in range(RET_QK_W // LANES):
            blk = z[:, j * LANES:(j + 1) * LANES]
            pr = jnp.where(first_half, pltpu.roll(blk, LANES - 32, 1), pltpu.roll(blk, 32, 1))
            out[:, j * LANES:(j + 1) * LANES] = (
                (blk * rc_ref[...] + pr * rs_ref[...]) * scale).astype(out.dtype)

    rv_out[...] = tm_dot(T_RV, RET_V_W).astype(rv_out.dtype)
    rg_out[...] = tm_dot(T_RG, RET_V_W)

    zt = lax.dot_general(wfm_ref[...], h, (((1,), (1,)), ((), ())),
                         preferred_element_type=jnp.float32)
    nq = tm // TQ_DSA

    def swap_halves(v):
        half = v.shape[0] // 2
        return jnp.concatenate([v[half:], v[:half]], axis=0)

    def ones_row_block(rows, dtype):
        first = lax.broadcasted_iota(jnp.int32, (rows, tm), 0) == 0
        return jnp.where(first, 1.0, 0.0).astype(dtype)

    zeros_q = jnp.zeros((LANES - DSA_DIM, DSA_HEADS * TQ_DSA), qt_out.dtype)
    zeros_i0 = jnp.zeros((DSA_DIM, IDX_HEADS * TQ_DSA), iqt_out.dtype)
    zeros_i1 = jnp.zeros((LANES - DSA_DIM - IDX_DIM, IDX_HEADS * TQ_DSA), iqt_out.dtype)
    for j in range(nq):
        qt_out[j, DSA_DIM:, :] = zeros_q
        iqt_out[j, :DSA_DIM, :] = zeros_i0
        iqt_out[j, DSA_DIM + IDX_DIM:, :] = zeros_i1
    for hd in range(DSA_HEADS):
        xh = zt[F_DQ + hd * DSA_DIM:F_DQ + (hd + 1) * DSA_DIM, :]
        r = lax.rsqrt(jnp.sum(xh * xh, axis=0, keepdims=True) * (1.0 / DSA_DIM) + NORM_EPS)
        o = ((xh * dqa_ref[...] + swap_halves(xh) * dqb_ref[...]) * r).astype(qt_out.dtype)
        for j in range(nq):
            qt_out[j, :DSA_DIM, hd * TQ_DSA:(hd + 1) * TQ_DSA] = o[:, j * TQ_DSA:(j + 1) * TQ_DSA]
    for hd in range(IDX_HEADS):
        xh = zt[F_IQ + hd * IDX_DIM:F_IQ + (hd + 1) * IDX_DIM, :]
        o = (xh * iqa_ref[...] + swap_halves(xh) * iqb_ref[...]).astype(iqt_out.dtype)
        for j in range(nq):
            iqt_out[j, DSA_DIM:DSA_DIM + IDX_DIM, hd * TQ_DSA:(hd + 1) * TQ_DSA] = (
                o[:, j * TQ_DSA:(j + 1) * TQ_DSA])
    vt_out[0, 0, :DSA_DIM, :] = zt[F_DV:F_DV + DSA_DIM, :].astype(vt_out.dtype)
    vt_out[0, 0, DSA_DIM:, :] = ones_row_block(BF16_ROWS, vt_out.dtype)
    iw_out[0] = zt[F_IW:F_IW + IDX_HEADS, :] * ((IDX_DIM * IDX_HEADS) ** -0.5)
    for hd in range(RET_HEADS):
        xh = zt[F_RK + hd * RET_QK_DIM:F_RK + (hd + 1) * RET_QK_DIM, :]
        rkt_out[0, hd * RET_QK_DIM:(hd + 1) * RET_QK_DIM, :] = (
            xh * rka_ref[...] + swap_halves(xh) * rkb_ref[...])
    for hp in range(FOX_HEADS // 2):
        fvt_out[0, hp, 0, :2 * FOX_DIM, :] = (
            zt[F_FV + hp * 2 * FOX_DIM:F_FV + (hp + 1) * 2 * FOX_DIM, :].astype(fvt_out.dtype))
        fvt_out[0, hp, 0, 2 * FOX_DIM:, :] = ones_row_block(BF16_ROWS, fvt_out.dtype)


def _inproj(x2d, lw, consts, batch, seq):
    n = x2d.shape[0]
    tm = min(TM_IN, seq)
    tps = seq // tm
    nqt = tm // TQ_DSA
    grid = (n // tm,)
    bf = jnp.bfloat16

    def full(a):
        return pl.BlockSpec(a.shape, lambda i: (0,) * a.ndim)

    def tok(width):
        return pl.BlockSpec((tm, width), lambda i: (i, 0))

    def pos_tm(width):
        return pl.BlockSpec((tm, width), lambda i: (i % tps, 0))

    def pos_fm(rows):
        return pl.BlockSpec((rows, tm), lambda i: (0, i % tps))

    def fm_out(rows):
        return pl.BlockSpec((1, rows, tm), lambda i: (i // tps, 0, i % tps))

    in_arrays = [lw["fox_bound"], x2d, lw["ln1_g"], lw["w_tm"], lw["w_fm"], lw["fox_b"], lw["fq_gain"], lw["fk_gain"],
                 lw["dk_gain"], consts["ltri"], consts["eq"], consts["ek"], consts["oneq"], consts["onek"],
                 consts["kc"], consts["ks"], consts["rc"], consts["rs"],
                 lw["dqa"], lw["dqb"], consts["iqa"], consts["iqb"], consts["rka"], consts["rkb"]]
    in_specs = [pl.BlockSpec(memory_space=pltpu.SMEM),
                tok(D_MODEL), full(lw["ln1_g"]), full(lw["w_tm"]), full(lw["w_fm"]), full(lw["fox_b"]),
                full(lw["fq_gain"]), full(lw["fk_gain"]), full(lw["dk_gain"]),
                full(consts["ltri"]), full(consts["eq"]), full(consts["ek"]),
                full(consts["oneq"]), full(consts["onek"]),
                pos_tm(LANES), pos_tm(LANES), pos_tm(LANES), pos_tm(LANES),
                pos_fm(DSA_DIM), pos_fm(DSA_DIM), pos_fm(IDX_DIM), pos_fm(IDX_DIM),
                pos_fm(RET_QK_DIM), pos_fm(RET_QK_DIM)]
    out_shape = [
        jax.ShapeDtypeStruct((n, FOX_HEADS * LANES), bf),
        jax.ShapeDtypeStruct((n, FOX_HEADS * LANES), bf),
        jax.ShapeDtypeStruct((batch, FOX_HEADS // 2, tps, FOX_VROWS, tm), bf),
        jax.ShapeDtypeStruct((n, LANES), bf),
        jax.ShapeDtypeStruct((n, RET_QK_W), bf),
        jax.ShapeDtypeStruct((n, RET_QK_W), bf),
        jax.ShapeDtypeStruct((n, RET_V_W), bf),
        jax.ShapeDtypeStruct((n, RET_V_W), jnp.float32),
        jax.ShapeDtypeStruct((n // TQ_DSA, LANES, DSA_HEADS * TQ_DSA), bf),
        jax.ShapeDtypeStruct((n // TQ_DSA, LANES, IDX_HEADS * TQ_DSA), bf),
        jax.ShapeDtypeStruct((batch, tps, DSA_VROWS, tm), bf),
        jax.ShapeDtypeStruct((batch, IDX_HEADS, seq), jnp.float32),
        jax.ShapeDtypeStruct((batch, RET_QK_W, seq), jnp.float32),
    ]
    out_specs = [tok(FOX_HEADS * LANES), tok(FOX_HEADS * LANES),
                 pl.BlockSpec((1, FOX_HEADS // 2, 1, FOX_VROWS, tm), lambda i: (i // tps, 0, i % tps, 0, 0)),
                 tok(LANES), tok(RET_QK_W), tok(RET_QK_W), tok(RET_V_W), tok(RET_V_W),
                 pl.BlockSpec((nqt, LANES, DSA_HEADS * TQ_DSA), lambda i: (i, 0, 0)),
                 pl.BlockSpec((nqt, LANES, IDX_HEADS * TQ_DSA), lambda i: (i, 0, 0)),
                 pl.BlockSpec((1, 1, DSA_VROWS, tm), lambda i: (i // tps, i % tps, 0, 0)),
                 fm_out(IDX_HEADS), fm_out(RET_QK_W)]
    return pl.pallas_call(
        functools.partial(_inproj_kernel, tiles_per_seq=tps),
        grid=grid, in_specs=in_specs, out_specs=out_specs, out_shape=out_shape,
        scratch_shapes=[pltpu.VMEM((1, LANES), jnp.float32)],
        compiler_params=_cparams(("arbitrary",)), name="inproj",
    )(*in_arrays)


def _fox_kernel(q_ref, k_ref, vt_ref, o_ref, sa_scr, sb_scr, pa_scr, pb_scr, acc_scr):
    tq = q_ref.shape[0]
    qi = pl.program_id(2)
    n_chunks = (qi * tq + tq + KC - 1) // KC
    nt = (((1,), (1,)), ((), ()))

    nheads = FOX_GROUP
    qcol = qi * tq + lax.broadcasted_iota(jnp.int32, (RB, tq), 1)
    qcol = jnp.concatenate([qcol] * nheads, axis=1)
    krow = lax.broadcasted_iota(jnp.int32, (RB, nheads * tq), 0)

    def qk(c, half):
        off = pl.multiple_of(c * KC + half * HALF, HALF)
        return [lax.dot_general(k_ref[pl.ds(off, HALF), hh * LANES:(hh + 1) * LANES],
                                q_ref[:, hh * LANES:(hh + 1) * LANES], nt,
                                preferred_element_type=jnp.float32) for hh in range(nheads)]

    def mask(t, c, half, r0):
        return jnp.where(krow + (c * KC + half * HALF + r0) <= qcol, t, NEG_BIG)

    def pv(c, half, p_ref):
        return jnp.concatenate(
            [jnp.dot(vt_ref[0, hh // 2, c, :, half * HALF:(half + 1) * HALF],
                     p_ref[:, hh * tq:(hh + 1) * tq], preferred_element_type=jnp.float32)
             for hh in range(nheads)], axis=1)

    def emit():
        acc = acc_scr[...]
        den = acc[2 * FOX_DIM:2 * FOX_DIM + 1, :]
        for hp in range(nheads // 2):
            ev, od = 2 * hp * tq, (2 * hp + 1) * tq
            out_t = jnp.concatenate([acc[:FOX_DIM, ev:ev + tq] / den[:, ev:ev + tq],
                                     acc[FOX_DIM:2 * FOX_DIM, od:od + tq] / den[:, od:od + tq]], axis=0)
            for j in range(tq // LANES):
                o_ref[j * LANES:(j + 1) * LANES, hp * LANES:(hp + 1) * LANES] = (
                    out_t[:, j * LANES:(j + 1) * LANES].T.astype(o_ref.dtype))

    _attend(n_chunks, qk, mask, pv, emit, sa_scr, sb_scr, pa_scr, pb_scr, acc_scr, 2 * FOX_DIM)


def _fox(fq, fk, fvt, batch, seq):
    tq = min(TQ_FOX, seq)
    nq = seq // tq
    g = FOX_GROUP
    width = g * tq
    return pl.pallas_call(
        _fox_kernel, grid=(batch, FOX_HEADS // g, nq),
        in_specs=[pl.BlockSpec((tq, g * LANES), lambda b, hg, qi: (b * nq + qi, hg)),
                  pl.BlockSpec((seq, g * LANES), lambda b, hg, qi: (b, hg)),
                  pl.BlockSpec((1, g // 2) + fvt.shape[2:], lambda b, hg, qi: (b, hg, 0, 0, 0))],
        out_specs=pl.BlockSpec((tq, g * FOX_DIM), lambda b, hg, qi: (b * nq + qi, hg)),
        out_shape=jax.ShapeDtypeStruct((batch * seq, FOX_W), jnp.bfloat16),
        scratch_shapes=[pltpu.VMEM((HALF, width), jnp.float32), pltpu.VMEM((HALF, width), jnp.float32),
                        pltpu.VMEM((HALF, width), jnp.bfloat16), pltpu.VMEM((HALF, width), jnp.bfloat16),
                        pltpu.VMEM((FOX_VROWS, width), jnp.float32)],
        compiler_params=_cparams(("parallel", "parallel", "arbitrary")), name="fox",
    )(fq, fk, fvt)


def _dsa_kernel(bound_ref, k_ref, vt_ref, qt_ref, iqt_ref, iw_ref, ltri_ref, o_ref,
                key_scr, hi_scr, lo_scr, bias_scr, sa_scr, sb_scr, pa_scr, pb_scr, acc_scr, *, topk):
    tq = TQ_DSA
    nh = DSA_HEADS
    selected_bias = -bound_ref[0, 0]
    qi = pl.program_id(1)
    n_chunks = (qi * tq + tq + KC - 1) // KC

    qpos = qi * tq + lax.broadcasted_iota(jnp.int32, (KC, tq), 1)
    krow = lax.broadcasted_iota(jnp.int32, (KC, tq), 0)

    def score_chunk(c, carry):
        off = pl.multiple_of(c * KC, KC)
        rel = jnp.dot(k_ref[pl.ds(off, KC), :], iqt_ref[0], preferred_element_type=jnp.float32)
        score = jnp.maximum(rel[:, :tq], 0.0) * iw_ref[0, 0:1, :]
        for hd in range(1, IDX_HEADS):
            score = score + jnp.maximum(rel[:, hd * tq:(hd + 1) * tq], 0.0) * iw_ref[0, hd:hd + 1, :]
        score = jnp.where(krow + off <= qpos, score, -jnp.inf)
        bits = pltpu.bitcast(score, jnp.int32)
        sign = bits >> 31
        key = (bits ^ (sign & jnp.int32(0x7FFFFFFF))) - sign
        key_scr[pl.ds(off, KC), :] = key
        hi_scr[pl.ds(off, KC), :] = (key >> 16).astype(jnp.int16)
        lo_scr[pl.ds(off, KC), :] = ((key & jnp.int32(0xFFFF)) - HALF_RANGE).astype(jnp.int16)
        return carry

    lax.fori_loop(0, n_chunks, score_chunk, 0)

    def count16_ge(plane_scr, thr):
        thr16 = thr.astype(jnp.int16)

        def body(c, acc):
            off = pl.multiple_of(c * KC, KC)
            hit = jnp.where(plane_scr[pl.ds(off, KC), :] >= thr16, jnp.int16(1), jnp.int16(0))
            return acc + _fold16(hit)
        acc = lax.fori_loop(0, n_chunks, body, jnp.zeros((BF16_ROWS, tq), jnp.int16))
        return jnp.sum(acc.astype(jnp.int32), axis=0, keepdims=True)

    def kth_largest16(plane_scr, kth):
        def bit_step(b, carry):
            thr, cnt = carry
            bit = jnp.left_shift(jnp.int32(1), 15 - b)
            cand = jnp.where(b == 0, jnp.zeros_like(thr), thr | bit)
            cand_cnt = count16_ge(plane_scr, cand)
            take = cand_cnt >= kth
            return jnp.where(take, cand, thr), jnp.where(take, cand_cnt, cnt)
        return lax.fori_loop(0, 16, bit_step, (jnp.full((1, tq), -HALF_RANGE, jnp.int32),
                                               jnp.full((1, tq), n_chunks * KC, jnp.int32)))

    def count_ge(thr):
        def body(c, acc):
            off = pl.multiple_of(c * KC, KC)
            hit = jnp.where(key_scr[pl.ds(off, KC), :] >= thr, 1, 0).astype(jnp.int32)
            return acc + _col_reduce(hit, jnp.sum)
        acc = lax.fori_loop(0, n_chunks, body, jnp.zeros((SUBLANES, tq), jnp.int32))
        return jnp.sum(acc, axis=0, keepdims=True)

    thr_hi, cnt_hi = kth_largest16(hi_scr, topk)
    thr_hi16 = thr_hi.astype(jnp.int16)

    def keep_candidates(c, acc):
        off = pl.multiple_of(c * KC, KC)
        hi = hi_scr[pl.ds(off, KC), :]
        lo_scr[pl.ds(off, KC), :] = jnp.where(hi == thr_hi16, lo_scr[pl.ds(off, KC), :], jnp.int16(-HALF_RANGE))
        return acc + _fold16(jnp.where(hi > thr_hi16, jnp.int16(1), jnp.int16(0)))

    above = lax.fori_loop(0, n_chunks, keep_candidates, jnp.zeros((BF16_ROWS, tq), jnp.int16))
    n_above = jnp.sum(above.astype(jnp.int32), axis=0, keepdims=True)
    thr_lo, cnt_lo = kth_largest16(lo_scr, topk - n_above)
    thr = thr_hi * (2 * HALF_RANGE) + (thr_lo + HALF_RANGE)
    n_ge = jnp.where(thr_lo > -HALF_RANGE, n_above + cnt_lo, cnt_hi)

    def select_keys():
        has_ties = jnp.max(n_ge) > topk

        @pl.when(jnp.logical_not(has_ties))
        def _():
            def body(c, carry):
                off = pl.multiple_of(c * KC, KC)
                bias_scr[pl.ds(off, KC), :] = jnp.where(key_scr[pl.ds(off, KC), :] >= thr,
                                                        selected_bias, NEG_BIG)
                return carry
            lax.fori_loop(0, n_chunks, body, 0)

        @pl.when(has_ties)
        def _():
            int_max = jnp.int32(2 ** 31 - 1)
            n_gt = jnp.where(thr == int_max, 0, count_ge(jnp.where(thr == int_max, thr, thr + 1)))
            need = (topk - n_gt).astype(jnp.float32)

            def body(c, seen):
                off = pl.multiple_of(c * KC, KC)
                ks = key_scr[pl.ds(off, KC), :]
                eq = ks == thr
                eqf = jnp.where(eq, 1.0, 0.0)
                before = jnp.dot(ltri_ref[...], eqf.astype(jnp.bfloat16),
                                 preferred_element_type=jnp.float32) + seen
                sel = jnp.logical_or(ks > thr, jnp.logical_and(eq, before < need))
                sel = jnp.logical_and(sel, krow + off <= qpos)
                bias_scr[pl.ds(off, KC), :] = jnp.where(sel, selected_bias, NEG_BIG)
                return seen + jnp.sum(eqf, axis=0, keepdims=True)
            lax.fori_loop(0, n_chunks, body, jnp.zeros((1, tq), jnp.float32))

    def qk(c, half):
        off = pl.multiple_of(c * KC + half * HALF, HALF)
        return [jnp.dot(k_ref[pl.ds(off, HALF), :], qt_ref[0], preferred_element_type=jnp.float32)]

    def mask(t, c, half, r0):
        b = bias_scr[pl.ds(pl.multiple_of(c * KC + half * HALF + r0, RB), RB), :]
        return t + jnp.concatenate([b] * nh, axis=1)

    def pv(c, half, p_ref):
        return jnp.dot(vt_ref[0, c, :, half * HALF:(half + 1) * HALF], p_ref[...],
                       preferred_element_type=jnp.float32)

    def emit():
        acc = acc_scr[...]
        out_t = acc[:DSA_DIM, :] / acc[DSA_DIM:DSA_DIM + 1, :]
        for hp in range(nh // 2):
            pair = jnp.concatenate([out_t[:, 2 * hp * tq:(2 * hp + 1) * tq],
                                    out_t[:, (2 * hp + 1) * tq:(2 * hp + 2) * tq]], axis=0)
            o_ref[:, hp * LANES:(hp + 1) * LANES] = pair.T.astype(o_ref.dtype)

    _attend(n_chunks, qk, mask, pv, emit, sa_scr, sb_scr, pa_scr, pb_scr, acc_scr, DSA_DIM,
            between=select_keys)


def _dsa(bound, dk, vt, qt, iqt, iwt, ltri_strict, batch, seq):
    nq = seq // TQ_DSA
    assert vt.shape[3] == KC and ltri_strict.shape[0] == KC and seq % KC == 0
    topk = min(DSA_MAX_TOPK, seq // 4)
    width = DSA_HEADS * TQ_DSA
    return pl.pallas_call(
        functools.partial(_dsa_kernel, topk=topk),
        grid=(batch, nq),
        in_specs=[pl.BlockSpec(memory_space=pltpu.SMEM),
                  pl.BlockSpec((seq, LANES), lambda b, qi: (b, 0)),
                  pl.BlockSpec((1,) + vt.shape[1:], lambda b, qi: (b, 0, 0, 0)),
                  pl.BlockSpec((1, LANES, width), lambda b, qi: (b * nq + qi, 0, 0)),
                  pl.BlockSpec((1, LANES, IDX_HEADS * TQ_DSA), lambda b, qi: (b * nq + qi, 0, 0)),
                  pl.BlockSpec((1, IDX_HEADS, TQ_DSA), lambda b, qi: (b, 0, qi)),
                  pl.BlockSpec(ltri_strict.shape, lambda b, qi: (0, 0))],
        out_specs=pl.BlockSpec((TQ_DSA, DSA_W), lambda b, qi: (b * nq + qi, 0)),
        out_shape=jax.ShapeDtypeStruct((batch * seq, DSA_W), jnp.bfloat16),
        scratch_shapes=[pltpu.VMEM((seq, TQ_DSA), jnp.int32),
                        pltpu.VMEM((seq, TQ_DSA), jnp.int16),
                        pltpu.VMEM((seq, TQ_DSA), jnp.int16),
                        pltpu.VMEM((seq, TQ_DSA), jnp.float32),
                        pltpu.VMEM((HALF, width), jnp.float32), pltpu.VMEM((HALF, width), jnp.float32),
                        pltpu.VMEM((HALF, width), jnp.bfloat16), pltpu.VMEM((HALF, width), jnp.bfloat16),
                        pltpu.VMEM((DSA_VROWS, width), jnp.float32)],
        compiler_params=_cparams(("parallel", "arbitrary")), name="dsa",
    )(bound, dk, vt, qt, iqt, iwt, ltri_strict)


def _ret_kernel(q_ref, k_ref, kt_ref, v_ref, g_ref, din_ref, dq_ref, dk_ref, dc_ref, gain_ref,
                o_ref, state_scr):
    t = pl.program_id(1)
    c = RET_CHUNK

    @pl.when(t == 0)
    def _():
        state_scr[...] = jnp.zeros_like(state_scr)

    for hd in range(RET_HEADS):
        state = state_scr[hd]
        for j in range(q_ref.shape[0] // c):
            rows = slice(j * c, (j + 1) * c)
            q = q_ref[rows, hd * RET_QK_DIM:(hd + 1) * RET_QK_DIM]
            k = k_ref[rows, hd * RET_QK_DIM:(hd + 1) * RET_QK_DIM]
            v = v_ref[rows, hd * RET_V_DIM:(hd + 1) * RET_V_DIM]
            attn = lax.dot_general(q, k, (((1,), (1,)), ((), ())),
                                   preferred_element_type=jnp.float32) * din_ref[hd]
            inner = jnp.dot(attn.astype(jnp.bfloat16), v, preferred_element_type=jnp.float32)
            cross = jnp.dot(q, state.astype(jnp.bfloat16),
                            preferred_element_type=jnp.float32) * dq_ref[hd]
            ktd = (kt_ref[0, hd * RET_QK_DIM:(hd + 1) * RET_QK_DIM, rows] * dk_ref[hd]).astype(jnp.bfloat16)
            state = dc_ref[hd] * state + jnp.dot(ktd, v, preferred_element_type=jnp.float32)
            y = inner + cross
            yn = y * lax.rsqrt(jnp.mean(y * y, axis=-1, keepdims=True) + NORM_EPS) * gain_ref[hd]
            gate = g_ref[rows, hd * RET_V_DIM:(hd + 1) * RET_V_DIM]
            o_ref[rows, hd * RET_V_DIM:(hd + 1) * RET_V_DIM] = (
                yn * (gate * jax.nn.sigmoid(gate))).astype(o_ref.dtype)
        state_scr[hd] = state


def _ret(rq, rk, rkt, rv, rg, consts, gain, batch, seq):
    c = min(RET_STEP, seq)
    n = seq // c

    def tok(width):
        return pl.BlockSpec((c, width), lambda b, t: (b * n + t, 0))

    def full(a):
        return pl.BlockSpec(a.shape, lambda b, t: (0,) * a.ndim)

    return pl.pallas_call(
        _ret_kernel, grid=(batch, n),
        in_specs=[tok(RET_QK_W), tok(RET_QK_W),
                  pl.BlockSpec((1, RET_QK_W, c), lambda b, t: (b, 0, t)),
                  tok(RET_V_W), tok(RET_V_W),
                  full(consts["ret_din"]), full(consts["ret_dq"]), full(consts["ret_dk"]),
                  full(consts["ret_dc"]), full(gain)],
        out_specs=tok(RET_V_W),
        out_shape=jax.ShapeDtypeStruct((batch * seq, RET_V_W), jnp.bfloat16),
        scratch_shapes=[pltpu.VMEM((RET_HEADS, RET_QK_DIM, RET_V_DIM), jnp.float32)],
        compiler_params=_cparams(("parallel", "arbitrary")), name="ret",
    )(rq, rk, rkt, rv, rg, consts["ret_din"], consts["ret_dq"], consts["ret_dk"], consts["ret_dc"], gain)


def _merge_kernel(x_ref, g_ref, ya_ref, yb_ref, yc_ref, wzg_ref, wa_ref, wb_ref, wc_ref, wo_ref, o_ref):
    x = x_ref[...]
    ms = jnp.mean(x * x, axis=-1, keepdims=True)
    h = (x * lax.rsqrt(ms + NORM_EPS) * g_ref[...]).astype(jnp.bfloat16)
    merged = None
    for j, (y_ref, w_ref) in enumerate(((ya_ref, wa_ref), (yb_ref, wb_ref), (yc_ref, wc_ref))):
        gate = jax.nn.sigmoid(jnp.dot(h, wzg_ref[:, j * D_MODEL:(j + 1) * D_MODEL],
                                      preferred_element_type=jnp.float32))
        term = gate * jnp.dot(y_ref[...], w_ref[...], preferred_element_type=jnp.float32)
        merged = term if merged is None else merged + term
    o_ref[...] = x + jnp.dot(merged.astype(jnp.bfloat16), wo_ref[...],
                             preferred_element_type=jnp.float32)


def _ffn_kernel(x_ref, g_ref, wg_ref, wu_ref, wd_ref, o_ref, *, chunk):
    x = x_ref[...]
    ms = jnp.mean(x * x, axis=-1, keepdims=True)
    h = (x * lax.rsqrt(ms + NORM_EPS) * g_ref[...]).astype(jnp.bfloat16)
    acc = x
    for lo in range(0, FFN_HIDDEN, chunk):
        gt = jnp.dot(h, wg_ref[:, lo:lo + chunk], preferred_element_type=jnp.float32)
        up = jnp.dot(h, wu_ref[:, lo:lo + chunk], preferred_element_type=jnp.float32)
        act = (gt * jax.nn.sigmoid(gt) * up).astype(jnp.bfloat16)
        acc = acc + jnp.dot(act, wd_ref[lo:lo + chunk, :], preferred_element_type=jnp.float32)
    o_ref[...] = acc


def _row_call(kernel, name, order):
    n = order[0][1].shape[0]
    tm = min(TM_POST, n)
    arrays, specs = [], []
    for kind, a in order:
        arrays.append(a)
        if kind == "row":
            specs.append(pl.BlockSpec((tm, a.shape[1]), lambda i: (i, 0)))
        else:
            specs.append(pl.BlockSpec(a.shape, lambda i, nd=a.ndim: (0,) * nd))
    return pl.pallas_call(
        kernel, grid=(n // tm,), in_specs=specs,
        out_specs=pl.BlockSpec((tm, D_MODEL), lambda i: (i, 0)),
        out_shape=jax.ShapeDtypeStruct((n, D_MODEL), jnp.float32),
        compiler_params=_cparams(("parallel",)), name=name,
    )(*arrays)


def _merge(x2d, ya, yb, yc, lw):
    order = [("row", x2d), ("full", lw["ln1_g"]), ("row", ya), ("row", yb), ("row", yc),
             ("full", lw["w_zg"]), ("full", lw["w_a"]), ("full", lw["w_b"]), ("full", lw["w_c"]),
             ("full", lw["w_o"])]
    return _row_call(_merge_kernel, "merge", order)


def _ffn(x2d, lw):
    order = [("row", x2d), ("full", lw["ln2_g"]), ("full", lw["w_g"]), ("full", lw["w_u"]),
             ("full", lw["w_d"])]
    return _row_call(functools.partial(_ffn_kernel, chunk=256), "ffn", order)


def _rope_cs(seq, dim):
    half = dim // 2
    inv_freq = ROPE_THETA ** (-jnp.arange(half, dtype=jnp.float32) / half)
    ang = jnp.arange(seq, dtype=jnp.float32)[:, None] * inv_freq[None, :]
    return jnp.cos(ang), jnp.sin(ang)


def _constants(seq):
    tm = min(TM_IN, seq)
    bf = jnp.bfloat16
    cos64, sin64 = _rope_cs(seq, 64)
    cos32, sin32 = _rope_cs(seq, 32)
    z32 = jnp.zeros((seq, 32), jnp.float32)
    c64 = jnp.concatenate([cos64, cos64], axis=1)
    s64 = jnp.concatenate([-sin64, sin64], axis=1)
    c32 = jnp.concatenate([cos32, cos32], axis=1)
    s32 = jnp.concatenate([-sin32, sin32], axis=1)
    consts = {
        "kc": jnp.concatenate([c64, c32, z32], axis=1), "ks": jnp.concatenate([s64, s32, z32], axis=1),
        "rc": jnp.concatenate([c64, c64], axis=1), "rs": jnp.concatenate([s64, s64], axis=1),
        "c64t": c64.T, "s64t": s64.T,
        "iqa": c32.T, "iqb": s32.T, "rka": c64.T, "rkb": s64.T,
        "ltri": jnp.tril(jnp.ones((tm, tm), jnp.float32)).astype(bf),
        "ltri_strict": jnp.tril(jnp.ones((KC, KC), jnp.float32), -1).astype(bf),
    }
    eq = np.zeros((3 * LANES, FOX_HEADS * LANES), np.float32)
    ek = np.zeros((3 * LANES, FOX_HEADS * LANES), np.float32)
    oneq = np.zeros((1, FOX_HEADS * LANES), np.float32)
    onek = np.zeros((1, FOX_HEADS * LANES), np.float32)
    for hd in range(FOX_HEADS):
        base = hd * LANES + FOX_DIM
        for part in range(3):
            eq[part * LANES + hd, base + part] = 1.0
            ek[part * LANES + hd, base + 3 + part] = -1.0
            oneq[0, base + 3 + part] = 1.0
            onek[0, base + part] = 1.0
    consts.update(eq=jnp.asarray(eq, bf), ek=jnp.asarray(ek, bf), oneq=jnp.asarray(oneq), onek=jnp.asarray(onek))
    log_g = jnp.log1p(-(2.0 ** (-5.0 - jnp.arange(RET_HEADS, dtype=jnp.float32))))
    pos = jnp.arange(RET_CHUNK, dtype=jnp.float32)
    diff = pos[:, None] - pos[None, :]
    din = jnp.where(diff >= 0, jnp.exp(jnp.maximum(diff, 0.0)[None] * log_g[:, None, None]), 0.0)
    dq = jnp.exp((pos + 1.0)[None] * log_g[:, None])
    dk = jnp.exp((RET_CHUNK - 1.0 - pos)[None] * log_g[:, None])
    dc = jnp.exp(RET_CHUNK * log_g)
    consts.update(ret_din=din, ret_dq=dq[:, :, None], ret_dk=dk[:, None, :],
                  ret_dc=jnp.broadcast_to(dc[:, None, None], (RET_HEADS, 1, LANES)))
    return consts


def _layer_weights(p, consts):
    bf = jnp.bfloat16
    w_in = p["w_in"]

    def cols(off, size):
        return w_in[:, off:off + size]

    zeros = lambda n: jnp.zeros((D_MODEL, n), jnp.float32)
    w_tm = jnp.concatenate([
        cols(O_FQ, FOX_W), cols(O_FK, FOX_W),
        cols(O_FF, FOX_HEADS), zeros(LANES - FOX_HEADS),
        cols(O_DK, DSA_DIM), cols(O_IK, IDX_DIM), zeros(LANES - DSA_DIM - IDX_DIM),
        cols(O_RQ, RET_QK_W), cols(O_RK, RET_QK_W), cols(O_RV, RET_V_W), cols(O_RG, RET_V_W),
    ], axis=1).astype(bf)
    w_fm = jnp.concatenate([
        cols(O_DQ, DSA_W), cols(O_IQ, IDX_W), cols(O_DV, DSA_DIM),
        cols(O_IW, IDX_HEADS), zeros(16 - IDX_HEADS), cols(O_RK, RET_QK_W), cols(O_FV, FOX_W),
    ], axis=1).T.astype(bf)

    def lane_pad(v, fill=0.0):
        return jnp.concatenate([v, jnp.full((LANES - v.shape[0],), fill, jnp.float32)])[None, :]

    g = p["dsa_q_norm"]
    g_sw = jnp.concatenate([g[DSA_DIM // 2:], g[:DSA_DIM // 2]])
    scale = DSA_DIM ** -0.5 * LOG2E

    def logit_bound(gq, gk, dim):
        b = BOUND_SLACK * dim * jnp.max(jnp.abs(gq)) * jnp.max(jnp.abs(gk)) * (dim ** -0.5 * LOG2E)
        return b.reshape(1, 1).astype(jnp.float32)

    return {
        "fox_bound": logit_bound(p["fox_q_norm"], p["fox_k_norm"], FOX_DIM),
        "dsa_bound": logit_bound(p["dsa_q_norm"], p["dsa_k_norm"], DSA_DIM),
        "ln1_g": p["ln1_g"][None, :], "ln2_g": p["ln2_g"][None, :],
        "w_tm": w_tm, "w_fm": w_fm,
        "fox_b": lane_pad(p["fox_b_f"]),
        "fq_gain": lane_pad(p["fox_q_norm"]), "fk_gain": lane_pad(p["fox_k_norm"]),
        "dk_gain": jnp.concatenate([p["dsa_k_norm"], jnp.ones((IDX_DIM,), jnp.float32),
                                    jnp.zeros((LANES - DSA_DIM - IDX_DIM,), jnp.float32)])[None, :],
        "dqa": consts["c64t"] * (g * scale)[:, None], "dqb": consts["s64t"] * (g_sw * scale)[:, None],
        "ret_gain": p["ret_out_norm"][:, None, :],
        "w_zg": cols(O_ZG, N_BRANCH * D_MODEL).astype(bf),
        "w_a": p["w_fox_out"].astype(bf), "w_b": p["w_dsa_out"].astype(bf), "w_c": p["w_ret_out"].astype(bf),
        "w_o": p["w_o"].astype(bf),
        "w_g": p["w_ffn_in"][:, :FFN_HIDDEN].astype(bf), "w_u": p["w_ffn_in"][:, FFN_HIDDEN:].astype(bf),
        "w_d": p["w_ffn_out"].astype(bf),
    }


def _layer(x2d, lw, consts, batch, seq):
    (fq, fk, fvt, dk, rq, rk, rv, rg, qt, iqt, vt, iwt, rkt) = _inproj(x2d, lw, consts, batch, seq)
    ya = _fox(fq, fk, fvt, batch, seq)
    yb = _dsa(lw["dsa_bound"], dk, vt, qt, iqt, iwt, consts["ltri_strict"], batch, seq)
    yc = _ret(rq, rk, rkt, rv, rg, consts, lw["ret_gain"], batch, seq)
    x2d = _merge(x2d, ya, yb, yc, lw)
    return _ffn(x2d, lw)


def kernel(x, ln1_g, w_in, fox_b_f, fox_q_norm, fox_k_norm, dsa_q_norm, dsa_k_norm, ret_out_norm,
           w_fox_out, w_dsa_out, w_ret_out, w_o, ln2_g, w_ffn_in, w_ffn_out):
    batch, seq, _ = x.shape
    depth = w_in.shape[0]
    consts = _constants(seq)
    params = dict(ln1_g=ln1_g, w_in=w_in, fox_b_f=fox_b_f, fox_q_norm=fox_q_norm, fox_k_norm=fox_k_norm,
                  dsa_q_norm=dsa_q_norm, dsa_k_norm=dsa_k_norm, ret_out_norm=ret_out_norm,
                  w_fox_out=w_fox_out, w_dsa_out=w_dsa_out, w_ret_out=w_ret_out, w_o=w_o,
                  ln2_g=ln2_g, w_ffn_in=w_ffn_in, w_ffn_out=w_ffn_out)
    x2d = x.reshape(batch * seq, D_MODEL)
    stacked = jax.vmap(lambda p: _layer_weights(p, consts))(params)
    for layer in range(depth):
        lw = {k: v[layer] for k, v in stacked.items()}
        x2d = _layer(x2d, lw, consts, batch, seq)
    return x2d.reshape(batch, seq, D_MODEL)
```

```python
import functools
import math

import jax
import jax.numpy as jnp
import numpy as np
from jax import lax
from jax.experimental import pallas as pl
from jax.experimental.pallas import tpu as pltpu

D_MODEL = 1024
FOX_HEADS = 8
FOX_DIM = 64
DSA_HEADS = 8
DSA_DIM = 64
IDX_HEADS = 8
IDX_DIM = 32
DSA_MAX_TOPK = 256
RET_HEADS = 4
RET_QK_DIM = 64
RET_V_DIM = 128
RET_CHUNK = 128
FFN_HIDDEN = 2816
ROPE_THETA = 10000.0
NORM_EPS = 1e-6
N_BRANCH = 3

FOX_W = FOX_HEADS * FOX_DIM
DSA_W = DSA_HEADS * DSA_DIM
IDX_W = IDX_HEADS * IDX_DIM
RET_QK_W = RET_HEADS * RET_QK_DIM
RET_V_W = RET_HEADS * RET_V_DIM
IN_SIZES = (FOX_W, FOX_W, FOX_W, FOX_HEADS,
            DSA_W, DSA_DIM, DSA_DIM, IDX_W, IDX_DIM, IDX_HEADS,
            RET_QK_W, RET_QK_W, RET_V_W, RET_V_W,
            N_BRANCH * D_MODEL)
IN_OFFS = tuple(int(v) for v in np.cumsum((0,) + IN_SIZES))
(O_FQ, O_FK, O_FV, O_FF, O_DQ, O_DK, O_DV, O_IQ, O_IK, O_IW,
 O_RQ, O_RK, O_RV, O_RG, O_ZG, _) = IN_OFFS

LANES = 128
SUBLANES = 8
BF16_ROWS = 16
VMEM_LIMIT = 56 * 1024 * 1024
NEG_BIG = -1e30
LOG2E = math.log2(math.e)
BOUND_SLACK = 1.02
MIN_DENOM = 2.0 ** -100
HALF_RANGE = 2 ** 15

T_FQ = 0
T_FK = T_FQ + FOX_W
T_FF = T_FK + FOX_W
T_DK = T_FF + LANES
T_RQ = T_DK + LANES
T_RK = T_RQ + RET_QK_W
T_RV = T_RK + RET_QK_W
T_RG = T_RV + RET_V_W
F_DQ = 0
F_IQ = F_DQ + DSA_W
F_DV = F_IQ + IDX_W
F_IW = F_DV + DSA_DIM
F_RK = F_IW + 16
F_FV = F_RK + RET_QK_W

TM_IN = 512
KC = TM_IN
HALF = KC // 2
RB = 32
TQ_FOX = 512
FOX_GROUP = 4
FOX_VROWS = 2 * FOX_DIM + BF16_ROWS
DSA_VROWS = DSA_DIM + BF16_ROWS
TQ_DSA = 256
RET_STEP = 4 * RET_CHUNK
TM_POST = 512


def _cparams(sem):
    return pltpu.CompilerParams(dimension_semantics=sem, vmem_limit_bytes=VMEM_LIMIT)


def _split3(v):
    hi = v.astype(jnp.bfloat16)
    r1 = v - hi.astype(jnp.float32)
    mid = r1.astype(jnp.bfloat16)
    lo = (r1 - mid.astype(jnp.float32)).astype(jnp.bfloat16)
    return hi, mid, lo


def _col_reduce(v, op):
    return op(v.reshape(v.shape[0] // SUBLANES, SUBLANES, v.shape[1]), axis=0)


def _fold16(v):
    parts = [v[r0:r0 + BF16_ROWS] for r0 in range(0, v.shape[0], BF16_ROWS)]
    while len(parts) > 1:
        parts = [parts[i] + parts[i + 1] for i in range(0, len(parts), 2)]
    return parts[0]


def _attend(n_chunks, qk, mask, pv, emit, sa_scr, sb_scr, pa_scr, pb_scr, acc_scr, den_row, between=None):
    def store_logits(s_ref, c, half):
        lo = 0
        for g in qk(c, half):
            s_ref[:, lo:lo + g.shape[1]] = g
            lo += g.shape[1]

    def probs(s_ref, c, half, p_ref, shift):
        for r0 in range(0, s_ref.shape[0], RB):
            t = mask(s_ref[r0:r0 + RB, :], c, half, r0)
            if shift is not None:
                t = t - shift
            p_ref[r0:r0 + RB, :] = jnp.exp2(t).astype(p_ref.dtype)

    def run(shift, first=False):
        acc_scr[...] = jnp.zeros_like(acc_scr)
        pb_scr[...] = jnp.zeros_like(pb_scr)
        store_logits(sa_scr, 0, 0)
        if first and between is not None:
            between()

        def chunk(c, carry):
            store_logits(sb_scr, c, 1)
            acc_scr[...] += pv(jnp.maximum(c - 1, 0), 1, pb_scr)
            probs(sa_scr, c, 0, pa_scr, shift)
            store_logits(sa_scr, jnp.minimum(c + 1, n_chunks - 1), 0)
            acc_scr[...] += pv(c, 0, pa_scr)
            probs(sb_scr, c, 1, pb_scr, shift)
            return carry

        lax.fori_loop(0, n_chunks, chunk, 0)
        acc_scr[...] += pv(n_chunks - 1, 1, pb_scr)

    run(None, first=True)
    emit()

    @pl.when(jnp.logical_not(jnp.min(acc_scr[den_row:den_row + 1, :]) >= MIN_DENOM))
    def _():
        def col_max(c, mx):
            for half in range(2):
                store_logits(sa_scr, c, half)
                for r0 in range(0, sa_scr.shape[0], RB):
                    mx = jnp.maximum(mx, _col_reduce(mask(sa_scr[r0:r0 + RB, :], c, half, r0), jnp.max))
            return mx
        mx = lax.fori_loop(0, n_chunks, col_max,
                           jnp.full((SUBLANES, acc_scr.shape[1]), NEG_BIG, jnp.float32))
        run(jnp.max(mx, axis=0, keepdims=True))
        emit()


def _inproj_kernel(bound_ref, x_ref, g_ref, wtm_ref, wfm_ref, fb_ref, fqg_ref, fkg_ref, dkg_ref,
                   ltri_ref, eq_ref, ek_ref, oneq_ref, onek_ref,
                   kc_ref, ks_ref, rc_ref, rs_ref,
                   dqa_ref, dqb_ref, iqa_ref, iqb_ref, rka_ref, rkb_ref,
                   fq_out, fk_out, fvt_out, dk_out, rq_out, rk_out, rv_out, rg_out,
                   qt_out, iqt_out, vt_out, iw_out, rkt_out,
                   carry_ref, *, tiles_per_seq):
    tm = x_ref.shape[0]
    i = pl.program_id(0)

    @pl.when(i % tiles_per_seq == 0)
    def _():
        carry_ref[...] = jnp.zeros_like(carry_ref)

    x = x_ref[...]
    ms = jnp.mean(x * x, axis=-1, keepdims=True)
    h = (x * lax.rsqrt(ms + NORM_EPS) * g_ref[...]).astype(jnp.bfloat16)

    def tm_dot(lo, width):
        return jnp.dot(h, wtm_ref[:, lo:lo + width], preferred_element_type=jnp.float32)

    lane = lax.broadcasted_iota(jnp.int32, (tm, LANES), 1)

    ffb = tm_dot(T_FF, LANES) + fb_ref[...]
    lf = (jnp.minimum(ffb, 0.0) - jnp.log1p(jnp.exp(-jnp.abs(ffb)))) * LOG2E
    parts = jnp.concatenate(_split3(lf), axis=1)
    cs = jnp.dot(ltri_ref[...], parts, preferred_element_type=jnp.float32)
    c = cs[:, :LANES] + cs[:, LANES:2 * LANES] + cs[:, 2 * LANES:] + carry_ref[...]
    carry_ref[...] = c[tm - 1:tm, :]
    cparts_q = jnp.concatenate(_split3(c - bound_ref[0, 0]), axis=1)
    cparts_k = jnp.concatenate(_split3(c), axis=1)
    scat_q = jnp.dot(cparts_q, eq_ref[...], preferred_element_type=jnp.float32) + oneq_ref[...]
    scat_k = jnp.dot(cparts_k, ek_ref[...], preferred_element_type=jnp.float32) + onek_ref[...]

    for (lo, gain_ref, scat, out, scale) in ((T_FQ, fqg_ref, scat_q, fq_out, FOX_DIM ** -0.5 * LOG2E),
                                             (T_FK, fkg_ref, scat_k, fk_out, 1.0)):
        z = tm_dot(lo, FOX_W)
        for hd in range(FOX_HEADS):
            blk = z[:, (hd // 2) * LANES:(hd // 2 + 1) * LANES]
            if hd % 2:
                blk = pltpu.roll(blk, FOX_DIM, 1)
            ss = jnp.sum(jnp.where(lane < FOX_DIM, blk * blk, 0.0), axis=-1, keepdims=True) * (1.0 / FOX_DIM)
            nb = blk * lax.rsqrt(ss + NORM_EPS) * (gain_ref[...] * scale)
            out[:, hd * LANES:(hd + 1) * LANES] = (
                nb + scat[:, hd * LANES:(hd + 1) * LANES]).astype(out.dtype)

    zk = tm_dot(T_DK, LANES)
    ssk = jnp.sum(jnp.where(lane < DSA_DIM, zk * zk, 0.0), axis=-1, keepdims=True) * (1.0 / DSA_DIM)
    nk = zk * jnp.where(lane < DSA_DIM, lax.rsqrt(ssk + NORM_EPS), 1.0) * dkg_ref[...]
    partner = jnp.where(
        lane < 32, pltpu.roll(nk, LANES - 32, 1),
        jnp.where(lane < 64, pltpu.roll(nk, 32, 1),
                  jnp.where(lane < 80, pltpu.roll(nk, LANES - 16, 1), pltpu.roll(nk, 16, 1))))
    dk_out[...] = (nk * kc_ref[...] + partner * ks_ref[...]).astype(dk_out.dtype)

    first_half = (lane % RET_QK_DIM) < (RET_QK_DIM // 2)
    for (lo, out, scale) in ((T_RQ, rq_out, RET_QK_DIM ** -0.5), (T_RK, rk_out, 1.0)):
        z = tm_dot(lo, RET_QK_W)
        for j in range(RET_QK_W // LANES):
            blk = z[:, j * LANES:(j + 1) * LANES]
            pr = jnp.where(first_half, pltpu.roll(blk, LANES - 32, 1), pltpu.roll(blk, 32, 1))
            out[:, j * LANES:(j + 1) * LANES] = (
                (blk * rc_ref[...] + pr * rs_ref[...]) * scale).astype(out.dtype)

    rv_out[...] = tm_dot(T_RV, RET_V_W).astype(rv_out.dtype)
    rg_out[...] = tm_dot(T_RG, RET_V_W)

    zt = lax.dot_general(wfm_ref[...], h, (((1,), (1,)), ((), ())),
                         preferred_element_type=jnp.float32)
    nq = tm // TQ_DSA

    def swap_halves(v):
        half = v.shape[0] // 2
        return jnp.concatenate([v[half:], v[:half]], axis=0)

    def ones_row_block(rows, dtype):
        first = lax.broadcasted_iota(jnp.int32, (rows, tm), 0) == 0
        return jnp.where(first, 1.0, 0.0).astype(dtype)

    zeros_q = jnp.zeros((LANES - DSA_DIM, DSA_HEADS * TQ_DSA), qt_out.dtype)
    zeros_i0 = jnp.zeros((DSA_DIM, IDX_HEADS * TQ_DSA), iqt_out.dtype)
    zeros_i1 = jnp.zeros((LANES - DSA_DIM - IDX_DIM, IDX_HEADS * TQ_DSA), iqt_out.dtype)
    for j in range(nq):
        qt_out[j, DSA_DIM:, :] = zeros_q
        iqt_out[j, :DSA_DIM, :] = zeros_i0
        iqt_out[j, DSA_DIM + IDX_DIM:, :] = zeros_i1
    for hd in range(DSA_HEADS):
        xh = zt[F_DQ + hd * DSA_DIM:F_DQ + (hd + 1) * DSA_DIM, :]
        r = lax.rsqrt(jnp.sum(xh * xh, axis=0, keepdims=True) * (1.0 / DSA_DIM) + NORM_EPS)
        o = ((xh * dqa_ref[...] + swap_halves(xh) * dqb_ref[...]) * r).astype(qt_out.dtype)
        for j in range(nq):
            qt_out[j, :DSA_DIM, hd * TQ_DSA:(hd + 1) * TQ_DSA] = o[:, j * TQ_DSA:(j + 1) * TQ_DSA]
    for hd in range(IDX_HEADS):
        xh = zt[F_IQ + hd * IDX_DIM:F_IQ + (hd + 1) * IDX_DIM, :]
        o = (xh * iqa_ref[...] + swap_halves(xh) * iqb_ref[...]).astype(iqt_out.dtype)
        for j in range(nq):
            iqt_out[j, DSA_DIM:DSA_DIM + IDX_DIM, hd * TQ_DSA:(hd + 1) * TQ_DSA] = (
                o[:, j * TQ_DSA:(j + 1) * TQ_DSA])
    vt_out[0, 0, :DSA_DIM, :] = zt[F_DV:F_DV + DSA_DIM, :].astype(vt_out.dtype)
    vt_out[0, 0, DSA_DIM:, :] = ones_row_block(BF16_ROWS, vt_out.dtype)
    iw_out[0] = zt[F_IW:F_IW + IDX_HEADS, :] * ((IDX_DIM * IDX_HEADS) ** -0.5)
    for hd in range(RET_HEADS):
        xh = zt[F_RK + hd * RET_QK_DIM:F_RK + (hd + 1) * RET_QK_DIM, :]
        rkt_out[0, hd * RET_QK_DIM:(hd + 1) * RET_QK_DIM, :] = (
            xh * rka_ref[...] + swap_halves(xh) * rkb_ref[...])
    for hp in range(FOX_HEADS // 2):
        fvt_out[0, hp, 0, :2 * FOX_DIM, :] = (
            zt[F_FV + hp * 2 * FOX_DIM:F_FV + (hp + 1) * 2 * FOX_DIM, :].astype(fvt_out.dtype))
        fvt_out[0, hp, 0, 2 * FOX_DIM:, :] = ones_row_block(BF16_ROWS, fvt_out.dtype)


def _inproj(x2d, lw, consts, batch, seq):
    n = x2d.shape[0]
    tm = min(TM_IN, seq)
    tps = seq // tm
    nqt = tm // TQ_DSA
    grid = (n // tm,)
    bf = jnp.bfloat16

    def full(a):
        return pl.BlockSpec(a.shape, lambda i: (0,) * a.ndim)

    def tok(width):
        return pl.BlockSpec((tm, width), lambda i: (i, 0))

    def pos_tm(width):
        return pl.BlockSpec((tm, width), lambda i: (i % tps, 0))

    def pos_fm(rows):
        return pl.BlockSpec((rows, tm), lambda i: (0, i % tps))

    def fm_out(rows):
        return pl.BlockSpec((1, rows, tm), lambda i: (i // tps, 0, i % tps))

    in_arrays = [lw["fox_bound"], x2d, lw["ln1_g"], lw["w_tm"], lw["w_fm"], lw["fox_b"], lw["fq_gain"], lw["fk_gain"],
                 lw["dk_gain"], consts["ltri"], consts["eq"], consts["ek"], consts["oneq"], consts["onek"],
                 consts["kc"], consts["ks"], consts["rc"], consts["rs"],
                 lw["dqa"], lw["dqb"], consts["iqa"], consts["iqb"], consts["rka"], consts["rkb"]]
    in_specs = [pl.BlockSpec(memory_space=pltpu.SMEM),
                tok(D_MODEL), full(lw["ln1_g"]), full(lw["w_tm"]), full(lw["w_fm"]), full(lw["fox_b"]),
                full(lw["fq_gain"]), full(lw["fk_gain"]), full(lw["dk_gain"]),
                full(consts["ltri"]), full(consts["eq"]), full(consts["ek"]),
                full(consts["oneq"]), full(consts["onek"]),
                pos_tm(LANES), pos_tm(LANES), pos_tm(LANES), pos_tm(LANES),
                pos_fm(DSA_DIM), pos_fm(DSA_DIM), pos_fm(IDX_DIM), pos_fm(IDX_DIM),
                pos_fm(RET_QK_DIM), pos_fm(RET_QK_DIM)]
    out_shape = [
        jax.ShapeDtypeStruct((n, FOX_HEADS * LANES), bf),
        jax.ShapeDtypeStruct((n, FOX_HEADS * LANES), bf),
        jax.ShapeDtypeStruct((batch, FOX_HEADS // 2, tps, FOX_VROWS, tm), bf),
        jax.ShapeDtypeStruct((n, LANES), bf),
        jax.ShapeDtypeStruct((n, RET_QK_W), bf),
        jax.ShapeDtypeStruct((n, RET_QK_W), bf),
        jax.ShapeDtypeStruct((n, RET_V_W), bf),
        jax.ShapeDtypeStruct((n, RET_V_W), jnp.float32),
        jax.ShapeDtypeStruct((n // TQ_DSA, LANES, DSA_HEADS * TQ_DSA), bf),
        jax.ShapeDtypeStruct((n // TQ_DSA, LANES, IDX_HEADS * TQ_DSA), bf),
        jax.ShapeDtypeStruct((batch, tps, DSA_VROWS, tm), bf),
        jax.ShapeDtypeStruct((batch, IDX_HEADS, seq), jnp.float32),
        jax.ShapeDtypeStruct((batch, RET_QK_W, seq), jnp.float32),
    ]
    out_specs = [tok(FOX_HEADS * LANES), tok(FOX_HEADS * LANES),
                 pl.BlockSpec((1, FOX_HEADS // 2, 1, FOX_VROWS, tm), lambda i: (i // tps, 0, i % tps, 0, 0)),
                 tok(LANES), tok(RET_QK_W), tok(RET_QK_W), tok(RET_V_W), tok(RET_V_W),
                 pl.BlockSpec((nqt, LANES, DSA_HEADS * TQ_DSA), lambda i: (i, 0, 0)),
                 pl.BlockSpec((nqt, LANES, IDX_HEADS * TQ_DSA), lambda i: (i, 0, 0)),
                 pl.BlockSpec((1, 1, DSA_VROWS, tm), lambda i: (i // tps, i % tps, 0, 0)),
                 fm_out(IDX_HEADS), fm_out(RET_QK_W)]
    return pl.pallas_call(
        functools.partial(_inproj_kernel, tiles_per_seq=tps),
        grid=grid, in_specs=in_specs, out_specs=out_specs, out_shape=out_shape,
        scratch_shapes=[pltpu.VMEM((1, LANES), jnp.float32)],
        compiler_params=_cparams(("arbitrary",)), name="inproj",
    )(*in_arrays)


def _fox_kernel(q_ref, k_ref, vt_ref, o_ref, sa_scr, sb_scr, pa_scr, pb_scr, acc_scr):
    tq = q_ref.shape[0]
    qi = pl.program_id(2)
    n_chunks = (qi * tq + tq + KC - 1) // KC
    nt = (((1,), (1,)), ((), ()))

    nheads = FOX_GROUP
    qcol = qi * tq + lax.broadcasted_iota(jnp.int32, (RB, tq), 1)
    qcol = jnp.concatenate([qcol] * nheads, axis=1)
    krow = lax.broadcasted_iota(jnp.int32, (RB, nheads * tq), 0)

    def qk(c, half):
        off = pl.multiple_of(c * KC + half * HALF, HALF)
        return [lax.dot_general(k_ref[pl.ds(off, HALF), hh * LANES:(hh + 1) * LANES],
                                q_ref[:, hh * LANES:(hh + 1) * LANES], nt,
                                preferred_element_type=jnp.float32) for hh in range(nheads)]

    def mask(t, c, half, r0):
        return jnp.where(krow + (c * KC + half * HALF + r0) <= qcol, t, NEG_BIG)

    def pv(c, half, p_ref):
        return jnp.concatenate(
            [jnp.dot(vt_ref[0, hh // 2, c, :, half * HALF:(half + 1) * HALF],
                     p_ref[:, hh * tq:(hh + 1) * tq], preferred_element_type=jnp.float32)
             for hh in range(nheads)], axis=1)

    def emit():
        acc = acc_scr[...]
        den = acc[2 * FOX_DIM:2 * FOX_DIM + 1, :]
        for hp in range(nheads // 2):
            ev, od = 2 * hp * tq, (2 * hp + 1) * tq
            out_t = jnp.concatenate([acc[:FOX_DIM, ev:ev + tq] / den[:, ev:ev + tq],
                                     acc[FOX_DIM:2 * FOX_DIM, od:od + tq] / den[:, od:od + tq]], axis=0)
            for j in range(tq // LANES):
                o_ref[j * LANES:(j + 1) * LANES, hp * LANES:(hp + 1) * LANES] = (
                    out_t[:, j * LANES:(j + 1) * LANES].T.astype(o_ref.dtype))

    _attend(n_chunks, qk, mask, pv, emit, sa_scr, sb_scr, pa_scr, pb_scr, acc_scr, 2 * FOX_DIM)


def _fox(fq, fk, fvt, batch, seq):
    tq = min(TQ_FOX, seq)
    nq = seq // tq
    g = FOX_GROUP
    width = g * tq
    return pl.pallas_call(
        _fox_kernel, grid=(batch, FOX_HEADS // g, nq),
        in_specs=[pl.BlockSpec((tq, g * LANES), lambda b, hg, qi: (b * nq + qi, hg)),
                  pl.BlockSpec((seq, g * LANES), lambda b, hg, qi: (b, hg)),
                  pl.BlockSpec((1, g // 2) + fvt.shape[2:], lambda b, hg, qi: (b, hg, 0, 0, 0))],
        out_specs=pl.BlockSpec((tq, g * FOX_DIM), lambda b, hg, qi: (b * nq + qi, hg)),
        out_shape=jax.ShapeDtypeStruct((batch * seq, FOX_W), jnp.bfloat16),
        scratch_shapes=[pltpu.VMEM((HALF, width), jnp.float32), pltpu.VMEM((HALF, width), jnp.float32),
                        pltpu.VMEM((HALF, width), jnp.bfloat16), pltpu.VMEM((HALF, width), jnp.bfloat16),
                        pltpu.VMEM((FOX_VROWS, width), jnp.float32)],
        compiler_params=_cparams(("parallel", "parallel", "arbitrary")), name="fox",
    )(fq, fk, fvt)


def _dsa_kernel(bound_ref, k_ref, vt_ref, qt_ref, iqt_ref, iw_ref, ltri_ref, o_ref,
                key_scr, hi_scr, lo_scr, bias_scr, sa_scr, sb_scr, pa_scr, pb_scr, acc_scr, *, topk):
    tq = TQ_DSA
    nh = DSA_HEADS
    selected_bias = -bound_ref[0, 0]
    qi = pl.program_id(1)
    n_chunks = (qi * tq + tq + KC - 1) // KC

    qpos = qi * tq + lax.broadcasted_iota(jnp.int32, (KC, tq), 1)
    krow = lax.broadcasted_iota(jnp.int32, (KC, tq), 0)

    def score_chunk(c, carry):
        off = pl.multiple_of(c * KC, KC)
        rel = jnp.dot(k_ref[pl.ds(off, KC), :], iqt_ref[0], preferred_element_type=jnp.float32)
        score = jnp.maximum(rel[:, :tq], 0.0) * iw_ref[0, 0:1, :]
        for hd in range(1, IDX_HEADS):
            score = score + jnp.maximum(rel[:, hd * tq:(hd + 1) * tq], 0.0) * iw_ref[0, hd:hd + 1, :]
        score = jnp.where(krow + off <= qpos, score, -jnp.inf)
        bits = pltpu.bitcast(score, jnp.int32)
        sign = bits >> 31
        key = (bits ^ (sign & jnp.int32(0x7FFFFFFF))) - sign
        key_scr[pl.ds(off, KC), :] = key
        hi_scr[pl.ds(off, KC), :] = (key >> 16).astype(jnp.int16)
        lo_scr[pl.ds(off, KC), :] = ((key & jnp.int32(0xFFFF)) - HALF_RANGE).astype(jnp.int16)
        return carry

    lax.fori_loop(0, n_chunks, score_chunk, 0)

    def count16_ge(plane_scr, thr):
        thr16 = thr.astype(jnp.int16)

        def body(c, acc):
            off = pl.multiple_of(c * KC, KC)
            hit = jnp.where(plane_scr[pl.ds(off, KC), :] >= thr16, jnp.int16(1), jnp.int16(0))
            return acc + _fold16(hit)
        acc = lax.fori_loop(0, n_chunks, body, jnp.zeros((BF16_ROWS, tq), jnp.int16))
        return jnp.sum(acc.astype(jnp.int32), axis=0, keepdims=True)

    def kth_largest16(plane_scr, kth):
        def bit_step(b, carry):
            thr, cnt = carry
            bit = jnp.left_shift(jnp.int32(1), 15 - b)
            cand = jnp.where(b == 0, jnp.zeros_like(thr), thr | bit)
            cand_cnt = count16_ge(plane_scr, cand)
            take = cand_cnt >= kth
            return jnp.where(take, cand, thr), jnp.where(take, cand_cnt, cnt)
        return lax.fori_loop(0, 16, bit_step, (jnp.full((1, tq), -HALF_RANGE, jnp.int32),
                                               jnp.full((1, tq), n_chunks * KC, jnp.int32)))

    def count_ge(thr):
        def body(c, acc):
            off = pl.multiple_of(c * KC, KC)
            hit = jnp.where(key_scr[pl.ds(off, KC), :] >= thr, 1, 0).astype(jnp.int32)
            return acc + _col_reduce(hit, jnp.sum)
        acc = lax.fori_loop(0, n_chunks, body, jnp.zeros((SUBLANES, tq), jnp.int32))
        return jnp.sum(acc, axis=0, keepdims=True)

    thr_hi, cnt_hi = kth_largest16(hi_scr, topk)
    thr_hi16 = thr_hi.astype(jnp.int16)

    def keep_candidates(c, acc):
        off = pl.multiple_of(c * KC, KC)
        hi = hi_scr[pl.ds(off, KC), :]
        lo_scr[pl.ds(off, KC), :] = jnp.where(hi == thr_hi16, lo_scr[pl.ds(off, KC), :], jnp.int16(-HALF_RANGE))
        return acc + _fold16(jnp.where(hi > thr_hi16, jnp.int16(1), jnp.int16(0)))

    above = lax.fori_loop(0, n_chunks, keep_candidates, jnp.zeros((BF16_ROWS, tq), jnp.int16))
    n_above = jnp.sum(above.astype(jnp.int32), axis=0, keepdims=True)
    thr_lo, cnt_lo = kth_largest16(lo_scr, topk - n_above)
    thr = thr_hi * (2 * HALF_RANGE) + (thr_lo + HALF_RANGE)
    n_ge = jnp.where(thr_lo > -HALF_RANGE, n_above + cnt_lo, cnt_hi)

    def select_keys():
        has_ties = jnp.max(n_ge) > topk

        @pl.when(jnp.logical_not(has_ties))
        def _():
            def body(c, carry):
                off = pl.multiple_of(c * KC, KC)
                bias_scr[pl.ds(off, KC), :] = jnp.where(key_scr[pl.ds(off, KC), :] >= thr,
                                                        selected_bias, NEG_BIG)
                return carry
            lax.fori_loop(0, n_chunks, body, 0)

        @pl.when(has_ties)
        def _():
            int_max = jnp.int32(2 ** 31 - 1)
            n_gt = jnp.where(thr == int_max, 0, count_ge(jnp.where(thr == int_max, thr, thr + 1)))
            need = (topk - n_gt).astype(jnp.float32)

            def body(c, seen):
                off = pl.multiple_of(c * KC, KC)
                ks = key_scr[pl.ds(off, KC), :]
                eq = ks == thr
                eqf = jnp.where(eq, 1.0, 0.0)
                before = jnp.dot(ltri_ref[...], eqf.astype(jnp.bfloat16),
                                 preferred_element_type=jnp.float32) + seen
                sel = jnp.logical_or(ks > thr, jnp.logical_and(eq, before < need))
                sel = jnp.logical_and(sel, krow + off <= qpos)
                bias_scr[pl.ds(off, KC), :] = jnp.where(sel, selected_bias, NEG_BIG)
                return seen + jnp.sum(eqf, axis=0, keepdims=True)
            lax.fori_loop(0, n_chunks, body, jnp.zeros((1, tq), jnp.float32))

    def qk(c, half):
        off = pl.multiple_of(c * KC + half * HALF, HALF)
        return [jnp.dot(k_ref[pl.ds(off, HALF), :], qt_ref[0], preferred_element_type=jnp.float32)]

    def mask(t, c, half, r0):
        b = bias_scr[pl.ds(pl.multiple_of(c * KC + half * HALF + r0, RB), RB), :]
        return t + jnp.concatenate([b] * nh, axis=1)

    def pv(c, half, p_ref):
        return jnp.dot(vt_ref[0, c, :, half * HALF:(half + 1) * HALF], p_ref[...],
                       preferred_element_type=jnp.float32)

    def emit():
        acc = acc_scr[...]
        out_t = acc[:DSA_DIM, :] / acc[DSA_DIM:DSA_DIM + 1, :]
        for hp in range(nh // 2):
            pair = jnp.concatenate([out_t[:, 2 * hp * tq:(2 * hp + 1) * tq],
                                    out_t[:, (2 * hp + 1) * tq:(2 * hp + 2) * tq]], axis=0)
            o_ref[:, hp * LANES:(hp + 1) * LANES] = pair.T.astype(o_ref.dtype)

    _attend(n_chunks, qk, mask, pv, emit, sa_scr, sb_scr, pa_scr, pb_scr, acc_scr, DSA_DIM,
            between=select_keys)


def _dsa(bound, dk, vt, qt, iqt, iwt, ltri_strict, batch, seq):
    nq = seq // TQ_DSA
    assert vt.shape[3] == KC and ltri_strict.shape[0] == KC and seq % KC == 0
    topk = min(DSA_MAX_TOPK, seq // 4)
    width = DSA_HEADS * TQ_DSA
    return pl.pallas_call(
        functools.partial(_dsa_kernel, topk=topk),
        grid=(batch, nq),
        in_specs=[pl.BlockSpec(memory_space=pltpu.SMEM),
                  pl.BlockSpec((seq, LANES), lambda b, qi: (b, 0)),
                  pl.BlockSpec((1,) + vt.shape[1:], lambda b, qi: (b, 0, 0, 0)),
                  pl.BlockSpec((1, LANES, width), lambda b, qi: (b * nq + qi, 0, 0)),
                  pl.BlockSpec((1, LANES, IDX_HEADS * TQ_DSA), lambda b, qi: (b * nq + qi, 0, 0)),
                  pl.BlockSpec((1, IDX_HEADS, TQ_DSA), lambda b, qi: (b, 0, qi)),
                  pl.BlockSpec(ltri_strict.shape, lambda b, qi: (0, 0))],
        out_specs=pl.BlockSpec((TQ_DSA, DSA_W), lambda b, qi: (b * nq + qi, 0)),
        out_shape=jax.ShapeDtypeStruct((batch * seq, DSA_W), jnp.bfloat16),
        scratch_shapes=[pltpu.VMEM((seq, TQ_DSA), jnp.int32),
                        pltpu.VMEM((seq, TQ_DSA), jnp.int16),
                        pltpu.VMEM((seq, TQ_DSA), jnp.int16),
                        pltpu.VMEM((seq, TQ_DSA), jnp.float32),
                        pltpu.VMEM((HALF, width), jnp.float32), pltpu.VMEM((HALF, width), jnp.float32),
                        pltpu.VMEM((HALF, width), jnp.bfloat16), pltpu.VMEM((HALF, width), jnp.bfloat16),
                        pltpu.VMEM((DSA_VROWS, width), jnp.float32)],
        compiler_params=_cparams(("parallel", "arbitrary")), name="dsa",
    )(bound, dk, vt, qt, iqt, iwt, ltri_strict)


def _ret_kernel(q_ref, k_ref, kt_ref, v_ref, g_ref, din_ref, dq_ref, dk_ref, dc_ref, gain_ref,
                o_ref, state_scr):
    t = pl.program_id(1)
    c = RET_CHUNK

    @pl.when(t == 0)
    def _():
        state_scr[...] = jnp.zeros_like(state_scr)

    for hd in range(RET_HEADS):
        state = state_scr[hd]
        for j in range(q_ref.shape[0] // c):
            rows = slice(j * c, (j + 1) * c)
            q = q_ref[rows, hd * RET_QK_DIM:(hd + 1) * RET_QK_DIM]
            k = k_ref[rows, hd * RET_QK_DIM:(hd + 1) * RET_QK_DIM]
            v = v_ref[rows, hd * RET_V_DIM:(hd + 1) * RET_V_DIM]
            attn = lax.dot_general(q, k, (((1,), (1,)), ((), ())),
                                   preferred_element_type=jnp.float32) * din_ref[hd]
            inner = jnp.dot(attn.astype(jnp.bfloat16), v, preferred_element_type=jnp.float32)
            cross = jnp.dot(q, state.astype(jnp.bfloat16),
                            preferred_element_type=jnp.float32) * dq_ref[hd]
            ktd = (kt_ref[0, hd * RET_QK_DIM:(hd + 1) * RET_QK_DIM, rows] * dk_ref[hd]).astype(jnp.bfloat16)
            state = dc_ref[hd] * state + jnp.dot(ktd, v, preferred_element_type=jnp.float32)
            y = inner + cross
            yn = y * lax.rsqrt(jnp.mean(y * y, axis=-1, keepdims=True) + NORM_EPS) * gain_ref[hd]
            gate = g_ref[rows, hd * RET_V_DIM:(hd + 1) * RET_V_DIM]
            o_ref[rows, hd * RET_V_DIM:(hd + 1) * RET_V_DIM] = (
                yn * (gate * jax.nn.sigmoid(gate))).astype(o_ref.dtype)
        state_scr[hd] = state


def _ret(rq, rk, rkt, rv, rg, consts, gain, batch, seq):
    c = min(RET_STEP, seq)
    n = seq // c

    def tok(width):
        return pl.BlockSpec((c, width), lambda b, t: (b * n + t, 0))

    def full(a):
        return pl.BlockSpec(a.shape, lambda b, t: (0,) * a.ndim)

    return pl.pallas_call(
        _ret_kernel, grid=(batch, n),
        in_specs=[tok(RET_QK_W), tok(RET_QK_W),
                  pl.BlockSpec((1, RET_QK_W, c), lambda b, t: (b, 0, t)),
                  tok(RET_V_W), tok(RET_V_W),
                  full(consts["ret_din"]), full(consts["ret_dq"]), full(consts["ret_dk"]),
                  full(consts["ret_dc"]), full(gain)],
        out_specs=tok(RET_V_W),
        out_shape=jax.ShapeDtypeStruct((batch * seq, RET_V_W), jnp.bfloat16),
        scratch_shapes=[pltpu.VMEM((RET_HEADS, RET_QK_DIM, RET_V_DIM), jnp.float32)],
        compiler_params=_cparams(("parallel", "arbitrary")), name="ret",
    )(rq, rk, rkt, rv, rg, consts["ret_din"], consts["ret_dq"], consts["ret_dk"], consts["ret_dc"], gain)


def _merge_kernel(x_ref, g_ref, ya_ref, yb_ref, yc_ref, wzg_ref, wa_ref, wb_ref, wc_ref, wo_ref, o_ref):
    x = x_ref[...]
    ms = jnp.mean(x * x, axis=-1, keepdims=True)
    h = (x * lax.rsqrt(ms + NORM_EPS) * g_ref[...]).astype(jnp.bfloat16)
    merged = None
    for j, (y_ref, w_ref) in enumerate(((ya_ref, wa_ref), (yb_ref, wb_ref), (yc_ref, wc_ref))):
        gate = jax.nn.sigmoid(jnp.dot(h, wzg_ref[:, j * D_MODEL:(j + 1) * D_MODEL],
                                      preferred_element_type=jnp.float32))
        term = gate * jnp.dot(y_ref[...], w_ref[...], preferred_element_type=jnp.float32)
        merged = term if merged is None else merged + term
    o_ref[...] = x + jnp.dot(merged.astype(jnp.bfloat16), wo_ref[...],
                             preferred_element_type=jnp.float32)


def _ffn_kernel(x_ref, g_ref, wgu_ref, wd_ref, o_ref, *, chunk):
    x = x_ref[...]
    ms = jnp.mean(x * x, axis=-1, keepdims=True)
    h = (x * lax.rsqrt(ms + NORM_EPS) * g_ref[...]).astype(jnp.bfloat16)
    acc = x
    for lo in range(0, FFN_HIDDEN, chunk):
        gt = jnp.dot(h, wgu_ref[:, lo:lo + chunk], preferred_element_type=jnp.float32)
        up = jnp.dot(h, wgu_ref[:, FFN_HIDDEN + lo:FFN_HIDDEN + lo + chunk],
                     preferred_element_type=jnp.float32)
        act = (gt * jax.nn.sigmoid(gt) * up).astype(jnp.bfloat16)
        acc = acc + jnp.dot(act, wd_ref[lo:lo + chunk, :], preferred_element_type=jnp.float32)
    o_ref[...] = acc


def _row_call(kernel, name, order):
    n = order[0][1].shape[0]
    tm = min(TM_POST, n)
    arrays, specs = [], []
    for kind, a in order:
        arrays.append(a)
        if kind == "row":
            specs.append(pl.BlockSpec((tm, a.shape[1]), lambda i: (i, 0)))
        else:
            specs.append(pl.BlockSpec(a.shape, lambda i, nd=a.ndim: (0,) * nd))
    return pl.pallas_call(
        kernel, grid=(n // tm,), in_specs=specs,
        out_specs=pl.BlockSpec((tm, D_MODEL), lambda i: (i, 0)),
        out_shape=jax.ShapeDtypeStruct((n, D_MODEL), jnp.float32),
        compiler_params=_cparams(("parallel",)), name=name,
    )(*arrays)


def _merge(x2d, ya, yb, yc, lw):
    order = [("row", x2d), ("full", lw["ln1_g"]), ("row", ya), ("row", yb), ("row", yc),
             ("full", lw["w_zg"]), ("full", lw["w_a"]), ("full", lw["w_b"]), ("full", lw["w_c"]),
             ("full", lw["w_o"])]
    return _row_call(_merge_kernel, "merge", order)


def _ffn(x2d, lw):
    order = [("row", x2d), ("full", lw["ln2_g"]), ("full", lw["w_gu"]), ("full", lw["w_d"])]
    return _row_call(functools.partial(_ffn_kernel, chunk=256), "ffn", order)


def _rope_cs(seq, dim):
    half = dim // 2
    inv_freq = ROPE_THETA ** (-jnp.arange(half, dtype=jnp.float32) / half)
    ang = jnp.arange(seq, dtype=jnp.float32)[:, None] * inv_freq[None, :]
    return jnp.cos(ang), jnp.sin(ang)


def _constants(seq):
    tm = min(TM_IN, seq)
    bf = jnp.bfloat16
    cos64, sin64 = _rope_cs(seq, 64)
    cos32, sin32 = _rope_cs(seq, 32)
    z32 = jnp.zeros((seq, 32), jnp.float32)
    c64 = jnp.concatenate([cos64, cos64], axis=1)
    s64 = jnp.concatenate([-sin64, sin64], axis=1)
    c32 = jnp.concatenate([cos32, cos32], axis=1)
    s32 = jnp.concatenate([-sin32, sin32], axis=1)
    consts = {
        "kc": jnp.concatenate([c64, c32, z32], axis=1), "ks": jnp.concatenate([s64, s32, z32], axis=1),
        "rc": jnp.concatenate([c64, c64], axis=1), "rs": jnp.concatenate([s64, s64], axis=1),
        "c64t": c64.T, "s64t": s64.T,
        "iqa": c32.T, "iqb": s32.T, "rka": c64.T, "rkb": s64.T,
        "ltri": jnp.tril(jnp.ones((tm, tm), jnp.float32)).astype(bf),
        "ltri_strict": jnp.tril(jnp.ones((KC, KC), jnp.float32), -1).astype(bf),
    }
    eq = np.zeros((3 * LANES, FOX_HEADS * LANES), np.float32)
    ek = np.zeros((3 * LANES, FOX_HEADS * LANES), np.float32)
    oneq = np.zeros((1, FOX_HEADS * LANES), np.float32)
    onek = np.zeros((1, FOX_HEADS * LANES), np.float32)
    for hd in range(FOX_HEADS):
        base = hd * LANES + FOX_DIM
        for part in range(3):
            eq[part * LANES + hd, base + part] = 1.0
            ek[part * LANES + hd, base + 3 + part] = -1.0
            oneq[0, base + 3 + part] = 1.0
            onek[0, base + part] = 1.0
    consts.update(eq=jnp.asarray(eq, bf), ek=jnp.asarray(ek, bf), oneq=jnp.asarray(oneq), onek=jnp.asarray(onek))
    log_g = jnp.log1p(-(2.0 ** (-5.0 - jnp.arange(RET_HEADS, dtype=jnp.float32))))
    pos = jnp.arange(RET_CHUNK, dtype=jnp.float32)
    diff = pos[:, None] - pos[None, :]
    din = jnp.where(diff >= 0, jnp.exp(jnp.maximum(diff, 0.0)[None] * log_g[:, None, None]), 0.0)
    dq = jnp.exp((pos + 1.0)[None] * log_g[:, None])
    dk = jnp.exp((RET_CHUNK - 1.0 - pos)[None] * log_g[:, None])
    dc = jnp.exp(RET_CHUNK * log_g)
    consts.update(ret_din=din, ret_dq=dq[:, :, None], ret_dk=dk[:, None, :],
                  ret_dc=jnp.broadcast_to(dc[:, None, None], (RET_HEADS, 1, LANES)))
    return consts


def _layer_weights(p, consts):
    bf = jnp.bfloat16
    w_in = p["w_in"]

    def cols(off, size):
        return w_in[:, off:off + size]

    zeros = lambda n: jnp.zeros((D_MODEL, n), jnp.float32)
    w_tm = jnp.concatenate([
        cols(O_FQ, FOX_W), cols(O_FK, FOX_W),
        cols(O_FF, FOX_HEADS), zeros(LANES - FOX_HEADS),
        cols(O_DK, DSA_DIM), cols(O_IK, IDX_DIM), zeros(LANES - DSA_DIM - IDX_DIM),
        cols(O_RQ, RET_QK_W), cols(O_RK, RET_QK_W), cols(O_RV, RET_V_W), cols(O_RG, RET_V_W),
    ], axis=1).astype(bf)
    w_fm = jnp.concatenate([
        cols(O_DQ, DSA_W), cols(O_IQ, IDX_W), cols(O_DV, DSA_DIM),
        cols(O_IW, IDX_HEADS), zeros(16 - IDX_HEADS), cols(O_RK, RET_QK_W), cols(O_FV, FOX_W),
    ], axis=1).T.astype(bf)

    def lane_pad(v, fill=0.0):
        return jnp.concatenate([v, jnp.full((LANES - v.shape[0],), fill, jnp.float32)])[None, :]

    g = p["dsa_q_norm"]
    g_sw = jnp.concatenate([g[DSA_DIM // 2:], g[:DSA_DIM // 2]])
    scale = DSA_DIM ** -0.5 * LOG2E

    def logit_bound(gq, gk, dim):
        b = BOUND_SLACK * dim * jnp.max(jnp.abs(gq)) * jnp.max(jnp.abs(gk)) * (dim ** -0.5 * LOG2E)
        return b.reshape(1, 1).astype(jnp.float32)

    return {
        "fox_bound": logit_bound(p["fox_q_norm"], p["fox_k_norm"], FOX_DIM),
        "dsa_bound": logit_bound(p["dsa_q_norm"], p["dsa_k_norm"], DSA_DIM),
        "ln1_g": p["ln1_g"][None, :], "ln2_g": p["ln2_g"][None, :],
        "w_tm": w_tm, "w_fm": w_fm,
        "fox_b": lane_pad(p["fox_b_f"]),
        "fq_gain": lane_pad(p["fox_q_norm"]), "fk_gain": lane_pad(p["fox_k_norm"]),
        "dk_gain": jnp.concatenate([p["dsa_k_norm"], jnp.ones((IDX_DIM,), jnp.float32),
                                    jnp.zeros((LANES - DSA_DIM - IDX_DIM,), jnp.float32)])[None, :],
        "dqa": consts["c64t"] * (g * scale)[:, None], "dqb": consts["s64t"] * (g_sw * scale)[:, None],
        "ret_gain": p["ret_out_norm"][:, None, :],
        "w_zg": cols(O_ZG, N_BRANCH * D_MODEL).astype(bf),
        "w_a": p["w_fox_out"].astype(bf), "w_b": p["w_dsa_out"].astype(bf), "w_c": p["w_ret_out"].astype(bf),
        "w_o": p["w_o"].astype(bf),
        "w_gu": p["w_ffn_in"].astype(bf),
        "w_d": p["w_ffn_out"].astype(bf),
    }


def _layer(x2d, lw, consts, batch, seq):
    (fq, fk, fvt, dk, rq, rk, rv, rg, qt, iqt, vt, iwt, rkt) = _inproj(x2d, lw, consts, batch, seq)
    ya = _fox(fq, fk, fvt, batch, seq)
    yb = _dsa(lw["dsa_bound"], dk, vt, qt, iqt, iwt, consts["ltri_strict"], batch, seq)
    yc = _ret(rq, rk, rkt, rv, rg, consts, lw["ret_gain"], batch, seq)
    x2d = _merge(x2d, ya, yb, yc, lw)
    return _ffn(x2d, lw)


def kernel(x, ln1_g, w_in, fox_b_f, fox_q_norm, fox_k_norm, dsa_q_norm, dsa_k_norm, ret_out_norm,
           w_fox_out, w_dsa_out, w_ret_out, w_o, ln2_g, w_ffn_in, w_ffn_out):
    batch, seq, _ = x.shape
    depth = w_in.shape[0]
    consts = _constants(seq)
    params = dict(ln1_g=ln1_g, w_in=w_in, fox_b_f=fox_b_f, fox_q_norm=fox_q_norm, fox_k_norm=fox_k_norm,
                  dsa_q_norm=dsa_q_norm, dsa_k_norm=dsa_k_norm, ret_out_norm=ret_out_norm,
                  w_fox_out=w_fox_out, w_dsa_out=w_dsa_out, w_ret_out=w_ret_out, w_o=w_o,
                  ln2_g=ln2_g, w_ffn_in=w_ffn_in, w_ffn_out=w_ffn_out)
    x2d = x.reshape(batch * seq, D_MODEL)
    stacked = jax.vmap(lambda p: _layer_weights(p, consts))(params)
    for layer in range(depth):
        lw = {k: v[layer] for k, v in stacked.items()}
        x2d = _layer(x2d, lw, consts, batch, seq)
    return x2d.reshape(batch, seq, D_MODEL)
```

```python
import functools
import math

import jax
import jax.numpy as jnp
import numpy as np
from jax import lax
from jax.experimental import pallas as pl
from jax.experimental.pallas import tpu as pltpu

D_MODEL = 1024
FOX_HEADS = 8
FOX_DIM = 64
DSA_HEADS = 8
DSA_DIM = 64
IDX_HEADS = 8
IDX_DIM = 32
DSA_MAX_TOPK = 256
RET_HEADS = 4
RET_QK_DIM = 64
RET_V_DIM = 128
RET_CHUNK = 128
FFN_HIDDEN = 2816
ROPE_THETA = 10000.0
NORM_EPS = 1e-6
N_BRANCH = 3

FOX_W = FOX_HEADS * FOX_DIM
DSA_W = DSA_HEADS * DSA_DIM
IDX_W = IDX_HEADS * IDX_DIM
RET_QK_W = RET_HEADS * RET_QK_DIM
RET_V_W = RET_HEADS * RET_V_DIM
IN_SIZES = (FOX_W, FOX_W, FOX_W, FOX_HEADS,
            DSA_W, DSA_DIM, DSA_DIM, IDX_W, IDX_DIM, IDX_HEADS,
            RET_QK_W, RET_QK_W, RET_V_W, RET_V_W,
            N_BRANCH * D_MODEL)
IN_OFFS = tuple(int(v) for v in np.cumsum((0,) + IN_SIZES))
(O_FQ, O_FK, O_FV, O_FF, O_DQ, O_DK, O_DV, O_IQ, O_IK, O_IW,
 O_RQ, O_RK, O_RV, O_RG, O_ZG, _) = IN_OFFS

LANES = 128
SUBLANES = 8
BF16_ROWS = 16
VMEM_LIMIT = 56 * 1024 * 1024
NEG_BIG = -1e30
LOG2E = math.log2(math.e)
BOUND_SLACK = 1.02
MIN_DENOM = 2.0 ** -100
HALF_RANGE = 2 ** 15

T_FQ = 0
T_FK = T_FQ + FOX_W
T_FF = T_FK + FOX_W
T_DK = T_FF + LANES
T_RQ = T_DK + LANES
T_RK = T_RQ + RET_QK_W
T_RV = T_RK + RET_QK_W
T_RG = T_RV + RET_V_W
F_DQ = 0
F_IQ = F_DQ + DSA_W
F_DV = F_IQ + IDX_W
F_IW = F_DV + DSA_DIM
F_RK = F_IW + 16
F_FV = F_RK + RET_QK_W

TM_IN = 512
KC = TM_IN
HALF = KC // 2
RB = 32
TQ_FOX = 512
FOX_GROUP = 4
FOX_VROWS = 2 * FOX_DIM + BF16_ROWS
DSA_VROWS = DSA_DIM + BF16_ROWS
TQ_DSA = 256
RET_STEP = 4 * RET_CHUNK
TM_POST = 512
STACKED_WEIGHTS = ("w_tm", "w_fm", "w_zg", "w_a", "w_b", "w_c", "w_o", "w_gu", "w_d")


def _cparams(sem):
    return pltpu.CompilerParams(dimension_semantics=sem, vmem_limit_bytes=VMEM_LIMIT)


def _split3(v):
    hi = v.astype(jnp.bfloat16)
    r1 = v - hi.astype(jnp.float32)
    mid = r1.astype(jnp.bfloat16)
    lo = (r1 - mid.astype(jnp.float32)).astype(jnp.bfloat16)
    return hi, mid, lo


def _resident(w):
    if isinstance(w, tuple):
        a, layer = w
        return a, pl.BlockSpec((None,) + a.shape[1:], lambda *_: (layer,) + (0,) * (a.ndim - 1))
    return w, pl.BlockSpec(w.shape, lambda *_: (0,) * w.ndim)


def _col_reduce(v, op):
    return op(v.reshape(v.shape[0] // SUBLANES, SUBLANES, v.shape[1]), axis=0)


def _fold16(v):
    parts = [v[r0:r0 + BF16_ROWS] for r0 in range(0, v.shape[0], BF16_ROWS)]
    while len(parts) > 1:
        parts = [parts[i] + parts[i + 1] for i in range(0, len(parts), 2)]
    return parts[0]


def _attend(n_chunks, qk, mask, pv, emit, sa_scr, sb_scr, pa_scr, pb_scr, acc_scr, den_row, between=None):
    def store_logits(s_ref, c, half):
        lo = 0
        for g in qk(c, half):
            s_ref[:, lo:lo + g.shape[1]] = g
            lo += g.shape[1]

    def probs(s_ref, c, half, p_ref, shift):
        for r0 in range(0, s_ref.shape[0], RB):
            t = mask(s_ref[r0:r0 + RB, :], c, half, r0)
            if shift is not None:
                t = t - shift
            p_ref[r0:r0 + RB, :] = jnp.exp2(t).astype(p_ref.dtype)

    def run(shift, first=False):
        acc_scr[...] = jnp.zeros_like(acc_scr)
        pb_scr[...] = jnp.zeros_like(pb_scr)
        store_logits(sa_scr, 0, 0)
        if first and between is not None:
            between()

        def chunk(c, carry):
            store_logits(sb_scr, c, 1)
            acc_scr[...] += pv(jnp.maximum(c - 1, 0), 1, pb_scr)
            probs(sa_scr, c, 0, pa_scr, shift)
            store_logits(sa_scr, jnp.minimum(c + 1, n_chunks - 1), 0)
            acc_scr[...] += pv(c, 0, pa_scr)
            probs(sb_scr, c, 1, pb_scr, shift)
            return carry

        lax.fori_loop(0, n_chunks, chunk, 0)
        acc_scr[...] += pv(n_chunks - 1, 1, pb_scr)

    run(None, first=True)
    emit()

    @pl.when(jnp.logical_not(jnp.min(acc_scr[den_row:den_row + 1, :]) >= MIN_DENOM))
    def _():
        def col_max(c, mx):
            for half in range(2):
                store_logits(sa_scr, c, half)
                for r0 in range(0, sa_scr.shape[0], RB):
                    mx = jnp.maximum(mx, _col_reduce(mask(sa_scr[r0:r0 + RB, :], c, half, r0), jnp.max))
            return mx
        mx = lax.fori_loop(0, n_chunks, col_max,
                           jnp.full((SUBLANES, acc_scr.shape[1]), NEG_BIG, jnp.float32))
        run(jnp.max(mx, axis=0, keepdims=True))
        emit()


def _inproj_kernel(bound_ref, x_ref, g_ref, wtm_ref, wfm_ref, fb_ref, fqg_ref, fkg_ref, dkg_ref,
                   ltri_ref, eq_ref, ek_ref, oneq_ref, onek_ref,
                   kc_ref, ks_ref, rc_ref, rs_ref,
                   dqa_ref, dqb_ref, iqa_ref, iqb_ref, rka_ref, rkb_ref,
                   fq_out, fk_out, fvt_out, dk_out, rq_out, rk_out, rv_out, rg_out,
                   qt_out, iqt_out, vt_out, iw_out, rkt_out,
                   carry_ref, *, tiles_per_seq):
    tm = x_ref.shape[0]
    i = pl.program_id(0)

    @pl.when(i % tiles_per_seq == 0)
    def _():
        carry_ref[...] = jnp.zeros_like(carry_ref)

    x = x_ref[...]
    ms = jnp.mean(x * x, axis=-1, keepdims=True)
    h = (x * lax.rsqrt(ms + NORM_EPS) * g_ref[...]).astype(jnp.bfloat16)

    def tm_dot(lo, width):
        return jnp.dot(h, wtm_ref[:, lo:lo + width], preferred_element_type=jnp.float32)

    lane = lax.broadcasted_iota(jnp.int32, (tm, LANES), 1)

    ffb = tm_dot(T_FF, LANES) + fb_ref[...]
    lf = (jnp.minimum(ffb, 0.0) - jnp.log1p(jnp.exp(-jnp.abs(ffb)))) * LOG2E
    parts = jnp.concatenate(_split3(lf), axis=1)
    cs = jnp.dot(ltri_ref[...], parts, preferred_element_type=jnp.float32)
    c = cs[:, :LANES] + cs[:, LANES:2 * LANES] + cs[:, 2 * LANES:] + carry_ref[...]
    carry_ref[...] = c[tm - 1:tm, :]
    cparts_q = jnp.concatenate(_split3(c - bound_ref[0, 0]), axis=1)
    cparts_k = jnp.concatenate(_split3(c), axis=1)
    scat_q = jnp.dot(cparts_q, eq_ref[...], preferred_element_type=jnp.float32) + oneq_ref[...]
    scat_k = jnp.dot(cparts_k, ek_ref[...], preferred_element_type=jnp.float32) + onek_ref[...]

    for (lo, gain_ref, scat, out, scale) in ((T_FQ, fqg_ref, scat_q, fq_out, FOX_DIM ** -0.5 * LOG2E),
                                             (T_FK, fkg_ref, scat_k, fk_out, 1.0)):
        z = tm_dot(lo, FOX_W)
        for hd in range(FOX_HEADS):
            blk = z[:, (hd // 2) * LANES:(hd // 2 + 1) * LANES]
            if hd % 2:
                blk = pltpu.roll(blk, FOX_DIM, 1)
            ss = jnp.sum(jnp.where(lane < FOX_DIM, blk * blk, 0.0), axis=-1, keepdims=True) * (1.0 / FOX_DIM)
            nb = blk * lax.rsqrt(ss + NORM_EPS) * (gain_ref[...] * scale)
            out[:, hd * LANES:(hd + 1) * LANES] = (
                nb + scat[:, hd * LANES:(hd + 1) * LANES]).astype(out.dtype)

    zk = tm_dot(T_DK, LANES)
    ssk = jnp.sum(jnp.where(lane < DSA_DIM, zk * zk, 0.0), axis=-1, keepdims=True) * (1.0 / DSA_DIM)
    nk = zk * jnp.where(lane < DSA_DIM, lax.rsqrt(ssk + NORM_EPS), 1.0) * dkg_ref[...]
    partner = jnp.where(
        lane < 32, pltpu.roll(nk, LANES - 32, 1),
        jnp.where(lane < 64, pltpu.roll(nk, 32, 1),
                  jnp.where(lane < 80, pltpu.roll(nk, LANES - 16, 1), pltpu.roll(nk, 16, 1))))
    dk_out[...] = (nk * kc_ref[...] + partner * ks_ref[...]).astype(dk_out.dtype)

    first_half = (lane % RET_QK_DIM) < (RET_QK_DIM // 2)
    for (lo, out, scale) in ((T_RQ, rq_out, RET_QK_DIM ** -0.5), (T_RK, rk_out, 1.0)):
        z = tm_dot(lo, RET_QK_W)
        for j in range(RET_QK_W // LANES):
            blk = z[:, j * LANES:(j + 1) * LANES]
            pr = jnp.where(first_half, pltpu.roll(blk, LANES - 32, 1), pltpu.roll(blk, 32, 1))
            out[:, j * LANES:(j + 1) * LANES] = (
                (blk * rc_ref[...] + pr * rs_ref[...]) * scale).astype(out.dtype)

    rv_out[...] = tm_dot(T_RV, RET_V_W).astype(rv_out.dtype)
    rg_out[...] = tm_dot(T_RG, RET_V_W)

    zt = lax.dot_general(wfm_ref[...], h, (((1,), (1,)), ((), ())),
                         preferred_element_type=jnp.float32)
    nq = tm // TQ_DSA

    def swap_halves(v):
        half = v.shape[0] // 2
        return jnp.concatenate([v[half:], v[:half]], axis=0)

    def ones_row_block(rows, dtype):
        first = lax.broadcasted_iota(jnp.int32, (rows, tm), 0) == 0
        return jnp.where(first, 1.0, 0.0).astype(dtype)

    zeros_q = jnp.zeros((LANES - DSA_DIM, DSA_HEADS * TQ_DSA), qt_out.dtype)
    zeros_i0 = jnp.zeros((DSA_DIM, IDX_HEADS * TQ_DSA), iqt_out.dtype)
    zeros_i1 = jnp.zeros((LANES - DSA_DIM - IDX_DIM, IDX_HEADS * TQ_DSA), iqt_out.dtype)
    for j in range(nq):
        qt_out[j, DSA_DIM:, :] = zeros_q
        iqt_out[j, :DSA_DIM, :] = zeros_i0
        iqt_out[j, DSA_DIM + IDX_DIM:, :] = zeros_i1
    for hd in range(DSA_HEADS):
        xh = zt[F_DQ + hd * DSA_DIM:F_DQ + (hd + 1) * DSA_DIM, :]
        r = lax.rsqrt(jnp.sum(xh * xh, axis=0, keepdims=True) * (1.0 / DSA_DIM) + NORM_EPS)
        o = ((xh * dqa_ref[...] + swap_halves(xh) * dqb_ref[...]) * r).astype(qt_out.dtype)
        for j in range(nq):
            qt_out[j, :DSA_DIM, hd * TQ_DSA:(hd + 1) * TQ_DSA] = o[:, j * TQ_DSA:(j + 1) * TQ_DSA]
    for hd in range(IDX_HEADS):
        xh = zt[F_IQ + hd * IDX_DIM:F_IQ + (hd + 1) * IDX_DIM, :]
        o = (xh * iqa_ref[...] + swap_halves(xh) * iqb_ref[...]).astype(iqt_out.dtype)
        for j in range(nq):
            iqt_out[j, DSA_DIM:DSA_DIM + IDX_DIM, hd * TQ_DSA:(hd + 1) * TQ_DSA] = (
                o[:, j * TQ_DSA:(j + 1) * TQ_DSA])
    vt_out[0, 0, :DSA_DIM, :] = zt[F_DV:F_DV + DSA_DIM, :].astype(vt_out.dtype)
    vt_out[0, 0, DSA_DIM:, :] = ones_row_block(BF16_ROWS, vt_out.dtype)
    iw_out[0] = zt[F_IW:F_IW + IDX_HEADS, :] * ((IDX_DIM * IDX_HEADS) ** -0.5)
    for hd in range(RET_HEADS):
        xh = zt[F_RK + hd * RET_QK_DIM:F_RK + (hd + 1) * RET_QK_DIM, :]
        rkt_out[0, hd * RET_QK_DIM:(hd + 1) * RET_QK_DIM, :] = (
            xh * rka_ref[...] + swap_halves(xh) * rkb_ref[...])
    for hp in range(FOX_HEADS // 2):
        fvt_out[0, hp, 0, :2 * FOX_DIM, :] = (
            zt[F_FV + hp * 2 * FOX_DIM:F_FV + (hp + 1) * 2 * FOX_DIM, :].astype(fvt_out.dtype))
        fvt_out[0, hp, 0, 2 * FOX_DIM:, :] = ones_row_block(BF16_ROWS, fvt_out.dtype)


def _inproj(x2d, lw, consts, batch, seq):
    n = x2d.shape[0]
    tm = min(TM_IN, seq)
    tps = seq // tm
    nqt = tm // TQ_DSA
    grid = (n // tm,)
    bf = jnp.bfloat16

    def full(a):
        return _resident(a)[1]

    def tok(width):
        return pl.BlockSpec((tm, width), lambda i: (i, 0))

    def pos_tm(width):
        return pl.BlockSpec((tm, width), lambda i: (i % tps, 0))

    def pos_fm(rows):
        return pl.BlockSpec((rows, tm), lambda i: (0, i % tps))

    def fm_out(rows):
        return pl.BlockSpec((1, rows, tm), lambda i: (i // tps, 0, i % tps))

    in_arrays = [lw["fox_bound"], x2d, lw["ln1_g"], _resident(lw["w_tm"])[0], _resident(lw["w_fm"])[0],
                 lw["fox_b"], lw["fq_gain"], lw["fk_gain"],
                 lw["dk_gain"], consts["ltri"], consts["eq"], consts["ek"], consts["oneq"], consts["onek"],
                 consts["kc"], consts["ks"], consts["rc"], consts["rs"],
                 lw["dqa"], lw["dqb"], consts["iqa"], consts["iqb"], consts["rka"], consts["rkb"]]
    in_specs = [pl.BlockSpec(memory_space=pltpu.SMEM),
                tok(D_MODEL), full(lw["ln1_g"]), full(lw["w_tm"]), full(lw["w_fm"]), full(lw["fox_b"]),
                full(lw["fq_gain"]), full(lw["fk_gain"]), full(lw["dk_gain"]),
                full(consts["ltri"]), full(consts["eq"]), full(consts["ek"]),
                full(consts["oneq"]), full(consts["onek"]),
                pos_tm(LANES), pos_tm(LANES), pos_tm(LANES), pos_tm(LANES),
                pos_fm(DSA_DIM), pos_fm(DSA_DIM), pos_fm(IDX_DIM), pos_fm(IDX_DIM),
                pos_fm(RET_QK_DIM), pos_fm(RET_QK_DIM)]
    out_shape = [
        jax.ShapeDtypeStruct((n, FOX_HEADS * LANES), bf),
        jax.ShapeDtypeStruct((n, FOX_HEADS * LANES), bf),
        jax.ShapeDtypeStruct((batch, FOX_HEADS // 2, tps, FOX_VROWS, tm), bf),
        jax.ShapeDtypeStruct((n, LANES), bf),
        jax.ShapeDtypeStruct((n, RET_QK_W), bf),
        jax.ShapeDtypeStruct((n, RET_QK_W), bf),
        jax.ShapeDtypeStruct((n, RET_V_W), bf),
        jax.ShapeDtypeStruct((n, RET_V_W), jnp.float32),
        jax.ShapeDtypeStruct((n // TQ_DSA, LANES, DSA_HEADS * TQ_DSA), bf),
        jax.ShapeDtypeStruct((n // TQ_DSA, LANES, IDX_HEADS * TQ_DSA), bf),
        jax.ShapeDtypeStruct((batch, tps, DSA_VROWS, tm), bf),
        jax.ShapeDtypeStruct((batch, IDX_HEADS, seq), jnp.float32),
        jax.ShapeDtypeStruct((batch, RET_QK_W, seq), jnp.float32),
    ]
    out_specs = [tok(FOX_HEADS * LANES), tok(FOX_HEADS * LANES),
                 pl.BlockSpec((1, FOX_HEADS // 2, 1, FOX_VROWS, tm), lambda i: (i // tps, 0, i % tps, 0, 0)),
                 tok(LANES), tok(RET_QK_W), tok(RET_QK_W), tok(RET_V_W), tok(RET_V_W),
                 pl.BlockSpec((nqt, LANES, DSA_HEADS * TQ_DSA), lambda i: (i, 0, 0)),
                 pl.BlockSpec((nqt, LANES, IDX_HEADS * TQ_DSA), lambda i: (i, 0, 0)),
                 pl.BlockSpec((1, 1, DSA_VROWS, tm), lambda i: (i // tps, i % tps, 0, 0)),
                 fm_out(IDX_HEADS), fm_out(RET_QK_W)]
    return pl.pallas_call(
        functools.partial(_inproj_kernel, tiles_per_seq=tps),
        grid=grid, in_specs=in_specs, out_specs=out_specs, out_shape=out_shape,
        scratch_shapes=[pltpu.VMEM((1, LANES), jnp.float32)],
        compiler_params=_cparams(("arbitrary",)), name="inproj",
    )(*in_arrays)


def _fox_kernel(q_ref, k_ref, vt_ref, o_ref, sa_scr, sb_scr, pa_scr, pb_scr, acc_scr):
    tq = q_ref.shape[0]
    qi = pl.program_id(2)
    n_chunks = (qi * tq + tq + KC - 1) // KC
    nt = (((1,), (1,)), ((), ()))

    nheads = FOX_GROUP
    qcol = qi * tq + lax.broadcasted_iota(jnp.int32, (RB, tq), 1)
    qcol = jnp.concatenate([qcol] * nheads, axis=1)
    krow = lax.broadcasted_iota(jnp.int32, (RB, nheads * tq), 0)

    def qk(c, half):
        off = pl.multiple_of(c * KC + half * HALF, HALF)
        return [lax.dot_general(k_ref[pl.ds(off, HALF), hh * LANES:(hh + 1) * LANES],
                                q_ref[:, hh * LANES:(hh + 1) * LANES], nt,
                                preferred_element_type=jnp.float32) for hh in range(nheads)]

    def mask(t, c, half, r0):
        return jnp.where(krow + (c * KC + half * HALF + r0) <= qcol, t, NEG_BIG)

    def pv(c, half, p_ref):
        return jnp.concatenate(
            [jnp.dot(vt_ref[0, hh // 2, c, :, half * HALF:(half + 1) * HALF],
                     p_ref[:, hh * tq:(hh + 1) * tq], preferred_element_type=jnp.float32)
             for hh in range(nheads)], axis=1)

    def emit():
        acc = acc_scr[...]
        den = acc[2 * FOX_DIM:2 * FOX_DIM + 1, :]
        for hp in range(nheads // 2):
            ev, od = 2 * hp * tq, (2 * hp + 1) * tq
            out_t = jnp.concatenate([acc[:FOX_DIM, ev:ev + tq] / den[:, ev:ev + tq],
                                     acc[FOX_DIM:2 * FOX_DIM, od:od + tq] / den[:, od:od + tq]], axis=0)
            for j in range(tq // LANES):
                o_ref[j * LANES:(j + 1) * LANES, hp * LANES:(hp + 1) * LANES] = (
                    out_t[:, j * LANES:(j + 1) * LANES].T.astype(o_ref.dtype))

    _attend(n_chunks, qk, mask, pv, emit, sa_scr, sb_scr, pa_scr, pb_scr, acc_scr, 2 * FOX_DIM)


def _fox(fq, fk, fvt, batch, seq):
    tq = min(TQ_FOX, seq)
    nq = seq // tq
    g = FOX_GROUP
    width = g * tq
    return pl.pallas_call(
        _fox_kernel, grid=(batch, FOX_HEADS // g, nq),
        in_specs=[pl.BlockSpec((tq, g * LANES), lambda b, hg, qi: (b * nq + qi, hg)),
                  pl.BlockSpec((seq, g * LANES), lambda b, hg, qi: (b, hg)),
                  pl.BlockSpec((1, g // 2) + fvt.shape[2:], lambda b, hg, qi: (b, hg, 0, 0, 0))],
        out_specs=pl.BlockSpec((tq, g * FOX_DIM), lambda b, hg, qi: (b * nq + qi, hg)),
        out_shape=jax.ShapeDtypeStruct((batch * seq, FOX_W), jnp.bfloat16),
        scratch_shapes=[pltpu.VMEM((HALF, width), jnp.float32), pltpu.VMEM((HALF, width), jnp.float32),
                        pltpu.VMEM((HALF, width), jnp.bfloat16), pltpu.VMEM((HALF, width), jnp.bfloat16),
                        pltpu.VMEM((FOX_VROWS, width), jnp.float32)],
        compiler_params=_cparams(("parallel", "parallel", "arbitrary")), name="fox",
    )(fq, fk, fvt)


def _dsa_kernel(bound_ref, k_ref, vt_ref, qt_ref, iqt_ref, iw_ref, ltri_ref, o_ref,
                key_scr, hi_scr, lo_scr, bias_scr, sa_scr, sb_scr, pa_scr, pb_scr, acc_scr, *, topk):
    tq = TQ_DSA
    nh = DSA_HEADS
    selected_bias = -bound_ref[0, 0]
    qi = pl.program_id(1)
    n_chunks = (qi * tq + tq + KC - 1) // KC

    qpos = qi * tq + lax.broadcasted_iota(jnp.int32, (KC, tq), 1)
    krow = lax.broadcasted_iota(jnp.int32, (KC, tq), 0)

    def score_chunk(c, carry):
        off = pl.multiple_of(c * KC, KC)
        rel = jnp.dot(k_ref[pl.ds(off, KC), :], iqt_ref[0], preferred_element_type=jnp.float32)
        score = jnp.maximum(rel[:, :tq], 0.0) * iw_ref[0, 0:1, :]
        for hd in range(1, IDX_HEADS):
            score = score + jnp.maximum(rel[:, hd * tq:(hd + 1) * tq], 0.0) * iw_ref[0, hd:hd + 1, :]
        score = jnp.where(krow + off <= qpos, score, -jnp.inf)
        bits = pltpu.bitcast(score, jnp.int32)
        sign = bits >> 31
        key = (bits ^ (sign & jnp.int32(0x7FFFFFFF))) - sign
        key_scr[pl.ds(off, KC), :] = key
        hi_scr[pl.ds(off, KC), :] = (key >> 16).astype(jnp.int16)
        lo_scr[pl.ds(off, KC), :] = ((key & jnp.int32(0xFFFF)) - HALF_RANGE).astype(jnp.int16)
        return carry

    lax.fori_loop(0, n_chunks, score_chunk, 0)

    def count16_ge(plane_scr, thr):
        thr16 = thr.astype(jnp.int16)

        def body(c, acc):
            off = pl.multiple_of(c * KC, KC)
            hit = jnp.where(plane_scr[pl.ds(off, KC), :] >= thr16, jnp.int16(1), jnp.int16(0))
            return acc + _fold16(hit)
        acc = lax.fori_loop(0, n_chunks, body, jnp.zeros((BF16_ROWS, tq), jnp.int16))
        return jnp.sum(acc.astype(jnp.int32), axis=0, keepdims=True)

    def kth_largest16(plane_scr, kth):
        def bit_step(b, carry):
            thr, cnt = carry
            bit = jnp.left_shift(jnp.int32(1), 15 - b)
            cand = jnp.where(b == 0, jnp.zeros_like(thr), thr | bit)
            cand_cnt = count16_ge(plane_scr, cand)
            take = cand_cnt >= kth
            return jnp.where(take, cand, thr), jnp.where(take, cand_cnt, cnt)
        return lax.fori_loop(0, 16, bit_step, (jnp.full((1, tq), -HALF_RANGE, jnp.int32),
                                               jnp.full((1, tq), n_chunks * KC, jnp.int32)))

    def count_ge(thr):
        def body(c, acc):
            off = pl.multiple_of(c * KC, KC)
            hit = jnp.where(key_scr[pl.ds(off, KC), :] >= thr, 1, 0).astype(jnp.int32)
            return acc + _col_reduce(hit, jnp.sum)
        acc = lax.fori_loop(0, n_chunks, body, jnp.zeros((SUBLANES, tq), jnp.int32))
        return jnp.sum(acc, axis=0, keepdims=True)

    thr_hi, cnt_hi = kth_largest16(hi_scr, topk)
    thr_hi16 = thr_hi.astype(jnp.int16)

    def keep_candidates(c, acc):
        off = pl.multiple_of(c * KC, KC)
        hi = hi_scr[pl.ds(off, KC), :]
        lo_scr[pl.ds(off, KC), :] = jnp.where(hi == thr_hi16, lo_scr[pl.ds(off, KC), :], jnp.int16(-HALF_RANGE))
        return acc + _fold16(jnp.where(hi > thr_hi16, jnp.int16(1), jnp.int16(0)))

    above = lax.fori_loop(0, n_chunks, keep_candidates, jnp.zeros((BF16_ROWS, tq), jnp.int16))
    n_above = jnp.sum(above.astype(jnp.int32), axis=0, keepdims=True)
    thr_lo, cnt_lo = kth_largest16(lo_scr, topk - n_above)
    thr = thr_hi * (2 * HALF_RANGE) + (thr_lo + HALF_RANGE)
    n_ge = jnp.where(thr_lo > -HALF_RANGE, n_above + cnt_lo, cnt_hi)

    def select_keys():
        has_ties = jnp.max(n_ge) > topk

        @pl.when(jnp.logical_not(has_ties))
        def _():
            def body(c, carry):
                off = pl.multiple_of(c * KC, KC)
                bias_scr[pl.ds(off, KC), :] = jnp.where(key_scr[pl.ds(off, KC), :] >= thr,
                                                        selected_bias, NEG_BIG)
                return carry
            lax.fori_loop(0, n_chunks, body, 0)

        @pl.when(has_ties)
        def _():
            int_max = jnp.int32(2 ** 31 - 1)
            n_gt = jnp.where(thr == int_max, 0, count_ge(jnp.where(thr == int_max, thr, thr + 1)))
            need = (topk - n_gt).astype(jnp.float32)

            def body(c, seen):
                off = pl.multiple_of(c * KC, KC)
                ks = key_scr[pl.ds(off, KC), :]
                eq = ks == thr
                eqf = jnp.where(eq, 1.0, 0.0)
                before = jnp.dot(ltri_ref[...], eqf.astype(jnp.bfloat16),
                                 preferred_element_type=jnp.float32) + seen
                sel = jnp.logical_or(ks > thr, jnp.logical_and(eq, before < need))
                sel = jnp.logical_and(sel, krow + off <= qpos)
                bias_scr[pl.ds(off, KC), :] = jnp.where(sel, selected_bias, NEG_BIG)
                return seen + jnp.sum(eqf, axis=0, keepdims=True)
            lax.fori_loop(0, n_chunks, body, jnp.zeros((1, tq), jnp.float32))

    def qk(c, half):
        off = pl.multiple_of(c * KC + half * HALF, HALF)
        return [jnp.dot(k_ref[pl.ds(off, HALF), :], qt_ref[0], preferred_element_type=jnp.float32)]

    def mask(t, c, half, r0):
        b = bias_scr[pl.ds(pl.multiple_of(c * KC + half * HALF + r0, RB), RB), :]
        return t + jnp.concatenate([b] * nh, axis=1)

    def pv(c, half, p_ref):
        return jnp.dot(vt_ref[0, c, :, half * HALF:(half + 1) * HALF], p_ref[...],
                       preferred_element_type=jnp.float32)

    def emit():
        acc = acc_scr[...]
        out_t = acc[:DSA_DIM, :] / acc[DSA_DIM:DSA_DIM + 1, :]
        for hp in range(nh // 2):
            pair = jnp.concatenate([out_t[:, 2 * hp * tq:(2 * hp + 1) * tq],
                                    out_t[:, (2 * hp + 1) * tq:(2 * hp + 2) * tq]], axis=0)
            o_ref[:, hp * LANES:(hp + 1) * LANES] = pair.T.astype(o_ref.dtype)

    _attend(n_chunks, qk, mask, pv, emit, sa_scr, sb_scr, pa_scr, pb_scr, acc_scr, DSA_DIM,
            between=select_keys)


def _dsa(bound, dk, vt, qt, iqt, iwt, ltri_strict, batch, seq):
    nq = seq // TQ_DSA
    assert vt.shape[3] == KC and ltri_strict.shape[0] == KC and seq % KC == 0
    topk = min(DSA_MAX_TOPK, seq // 4)
    width = DSA_HEADS * TQ_DSA
    return pl.pallas_call(
        functools.partial(_dsa_kernel, topk=topk),
        grid=(batch, nq),
        in_specs=[pl.BlockSpec(memory_space=pltpu.SMEM),
                  pl.BlockSpec((seq, LANES), lambda b, qi: (b, 0)),
                  pl.BlockSpec((1,) + vt.shape[1:], lambda b, qi: (b, 0, 0, 0)),
                  pl.BlockSpec((1, LANES, width), lambda b, qi: (b * nq + qi, 0, 0)),
                  pl.BlockSpec((1, LANES, IDX_HEADS * TQ_DSA), lambda b, qi: (b * nq + qi, 0, 0)),
                  pl.BlockSpec((1, IDX_HEADS, TQ_DSA), lambda b, qi: (b, 0, qi)),
                  pl.BlockSpec(ltri_strict.shape, lambda b, qi: (0, 0))],
        out_specs=pl.BlockSpec((TQ_DSA, DSA_W), lambda b, qi: (b * nq + qi, 0)),
        out_shape=jax.ShapeDtypeStruct((batch * seq, DSA_W), jnp.bfloat16),
        scratch_shapes=[pltpu.VMEM((seq, TQ_DSA), jnp.int32),
                        pltpu.VMEM((seq, TQ_DSA), jnp.int16),
                        pltpu.VMEM((seq, TQ_DSA), jnp.int16),
                        pltpu.VMEM((seq, TQ_DSA), jnp.float32),
                        pltpu.VMEM((HALF, width), jnp.float32), pltpu.VMEM((HALF, width), jnp.float32),
                        pltpu.VMEM((HALF, width), jnp.bfloat16), pltpu.VMEM((HALF, width), jnp.bfloat16),
                        pltpu.VMEM((DSA_VROWS, width), jnp.float32)],
        compiler_params=_cparams(("parallel", "arbitrary")), name="dsa",
    )(bound, dk, vt, qt, iqt, iwt, ltri_strict)


def _ret_kernel(q_ref, k_ref, kt_ref, v_ref, g_ref, din_ref, dq_ref, dk_ref, dc_ref, gain_ref,
                o_ref, state_scr):
    t = pl.program_id(1)
    c = RET_CHUNK

    @pl.when(t == 0)
    def _():
        state_scr[...] = jnp.zeros_like(state_scr)

    for hd in range(RET_HEADS):
        state = state_scr[hd]
        for j in range(q_ref.shape[0] // c):
            rows = slice(j * c, (j + 1) * c)
            q = q_ref[rows, hd * RET_QK_DIM:(hd + 1) * RET_QK_DIM]
            k = k_ref[rows, hd * RET_QK_DIM:(hd + 1) * RET_QK_DIM]
            v = v_ref[rows, hd * RET_V_DIM:(hd + 1) * RET_V_DIM]
            attn = lax.dot_general(q, k, (((1,), (1,)), ((), ())),
                                   preferred_element_type=jnp.float32) * din_ref[hd]
            inner = jnp.dot(attn.astype(jnp.bfloat16), v, preferred_element_type=jnp.float32)
            cross = jnp.dot(q, state.astype(jnp.bfloat16),
                            preferred_element_type=jnp.float32) * dq_ref[hd]
            ktd = (kt_ref[0, hd * RET_QK_DIM:(hd + 1) * RET_QK_DIM, rows] * dk_ref[hd]).astype(jnp.bfloat16)
            state = dc_ref[hd] * state + jnp.dot(ktd, v, preferred_element_type=jnp.float32)
            y = inner + cross
            yn = y * lax.rsqrt(jnp.mean(y * y, axis=-1, keepdims=True) + NORM_EPS) * gain_ref[hd]
            gate = g_ref[rows, hd * RET_V_DIM:(hd + 1) * RET_V_DIM]
            o_ref[rows, hd * RET_V_DIM:(hd + 1) * RET_V_DIM] = (
                yn * (gate * jax.nn.sigmoid(gate))).astype(o_ref.dtype)
        state_scr[hd] = state


def _ret(rq, rk, rkt, rv, rg, consts, gain, batch, seq):
    c = min(RET_STEP, seq)
    n = seq // c

    def tok(width):
        return pl.BlockSpec((c, width), lambda b, t: (b * n + t, 0))

    def full(a):
        return pl.BlockSpec(a.shape, lambda b, t: (0,) * a.ndim)

    return pl.pallas_call(
        _ret_kernel, grid=(batch, n),
        in_specs=[tok(RET_QK_W), tok(RET_QK_W),
                  pl.BlockSpec((1, RET_QK_W, c), lambda b, t: (b, 0, t)),
                  tok(RET_V_W), tok(RET_V_W),
                  full(consts["ret_din"]), full(consts["ret_dq"]), full(consts["ret_dk"]),
                  full(consts["ret_dc"]), full(gain)],
        out_specs=tok(RET_V_W),
        out_shape=jax.ShapeDtypeStruct((batch * seq, RET_V_W), jnp.bfloat16),
        scratch_shapes=[pltpu.VMEM((RET_HEADS, RET_QK_DIM, RET_V_DIM), jnp.float32)],
        compiler_params=_cparams(("parallel", "arbitrary")), name="ret",
    )(rq, rk, rkt, rv, rg, consts["ret_din"], consts["ret_dq"], consts["ret_dk"], consts["ret_dc"], gain)


def _merge_kernel(x_ref, g_ref, ya_ref, yb_ref, yc_ref, wzg_ref, wa_ref, wb_ref, wc_ref, wo_ref, o_ref):
    x = x_ref[...]
    ms = jnp.mean(x * x, axis=-1, keepdims=True)
    h = (x * lax.rsqrt(ms + NORM_EPS) * g_ref[...]).astype(jnp.bfloat16)
    merged = None
    for j, (y_ref, w_ref) in enumerate(((ya_ref, wa_ref), (yb_ref, wb_ref), (yc_ref, wc_ref))):
        gate = jax.nn.sigmoid(jnp.dot(h, wzg_ref[:, j * D_MODEL:(j + 1) * D_MODEL],
                                      preferred_element_type=jnp.float32))
        term = gate * jnp.dot(y_ref[...], w_ref[...], preferred_element_type=jnp.float32)
        merged = term if merged is None else merged + term
    o_ref[...] = x + jnp.dot(merged.astype(jnp.bfloat16), wo_ref[...],
                             preferred_element_type=jnp.float32)


def _ffn_kernel(x_ref, g_ref, wgu_ref, wd_ref, o_ref, *, chunk):
    x = x_ref[...]
    ms = jnp.mean(x * x, axis=-1, keepdims=True)
    h = (x * lax.rsqrt(ms + NORM_EPS) * g_ref[...]).astype(jnp.bfloat16)
    acc = x
    for lo in range(0, FFN_HIDDEN, chunk):
        gt = jnp.dot(h, wgu_ref[:, lo:lo + chunk], preferred_element_type=jnp.float32)
        up = jnp.dot(h, wgu_ref[:, FFN_HIDDEN + lo:FFN_HIDDEN + lo + chunk],
                     preferred_element_type=jnp.float32)
        act = (gt * jax.nn.sigmoid(gt) * up).astype(jnp.bfloat16)
        acc = acc + jnp.dot(act, wd_ref[lo:lo + chunk, :], preferred_element_type=jnp.float32)
    o_ref[...] = acc


def _row_call(kernel, name, order):
    n = order[0][1].shape[0]
    tm = min(TM_POST, n)
    arrays, specs = [], []
    for kind, a in order:
        if kind == "row":
            arrays.append(a)
            specs.append(pl.BlockSpec((tm, a.shape[1]), lambda i: (i, 0)))
        else:
            arr, spec = _resident(a)
            arrays.append(arr)
            specs.append(spec)
    return pl.pallas_call(
        kernel, grid=(n // tm,), in_specs=specs,
        out_specs=pl.BlockSpec((tm, D_MODEL), lambda i: (i, 0)),
        out_shape=jax.ShapeDtypeStruct((n, D_MODEL), jnp.float32),
        compiler_params=_cparams(("parallel",)), name=name,
    )(*arrays)


def _merge(x2d, ya, yb, yc, lw):
    order = [("row", x2d), ("full", lw["ln1_g"]), ("row", ya), ("row", yb), ("row", yc),
             ("full", lw["w_zg"]), ("full", lw["w_a"]), ("full", lw["w_b"]), ("full", lw["w_c"]),
             ("full", lw["w_o"])]
    return _row_call(_merge_kernel, "merge", order)


def _ffn(x2d, lw):
    order = [("row", x2d), ("full", lw["ln2_g"]), ("full", lw["w_gu"]), ("full", lw["w_d"])]
    return _row_call(functools.partial(_ffn_kernel, chunk=256), "ffn", order)


def _rope_cs(seq, dim):
    half = dim // 2
    inv_freq = ROPE_THETA ** (-jnp.arange(half, dtype=jnp.float32) / half)
    ang = jnp.arange(seq, dtype=jnp.float32)[:, None] * inv_freq[None, :]
    return jnp.cos(ang), jnp.sin(ang)


def _constants(seq):
    tm = min(TM_IN, seq)
    bf = jnp.bfloat16
    cos64, sin64 = _rope_cs(seq, 64)
    cos32, sin32 = _rope_cs(seq, 32)
    z32 = jnp.zeros((seq, 32), jnp.float32)
    c64 = jnp.concatenate([cos64, cos64], axis=1)
    s64 = jnp.concatenate([-sin64, sin64], axis=1)
    c32 = jnp.concatenate([cos32, cos32], axis=1)
    s32 = jnp.concatenate([-sin32, sin32], axis=1)
    consts = {
        "kc": jnp.concatenate([c64, c32, z32], axis=1), "ks": jnp.concatenate([s64, s32, z32], axis=1),
        "rc": jnp.concatenate([c64, c64], axis=1), "rs": jnp.concatenate([s64, s64], axis=1),
        "c64t": c64.T, "s64t": s64.T,
        "iqa": c32.T, "iqb": s32.T, "rka": c64.T, "rkb": s64.T,
        "ltri": jnp.tril(jnp.ones((tm, tm), jnp.float32)).astype(bf),
        "ltri_strict": jnp.tril(jnp.ones((KC, KC), jnp.float32), -1).astype(bf),
    }
    eq = np.zeros((3 * LANES, FOX_HEADS * LANES), np.float32)
    ek = np.zeros((3 * LANES, FOX_HEADS * LANES), np.float32)
    oneq = np.zeros((1, FOX_HEADS * LANES), np.float32)
    onek = np.zeros((1, FOX_HEADS * LANES), np.float32)
    for hd in range(FOX_HEADS):
        base = hd * LANES + FOX_DIM
        for part in range(3):
            eq[part * LANES + hd, base + part] = 1.0
            ek[part * LANES + hd, base + 3 + part] = -1.0
            oneq[0, base + 3 + part] = 1.0
            onek[0, base + part] = 1.0
    consts.update(eq=jnp.asarray(eq, bf), ek=jnp.asarray(ek, bf), oneq=jnp.asarray(oneq), onek=jnp.asarray(onek))
    log_g = jnp.log1p(-(2.0 ** (-5.0 - jnp.arange(RET_HEADS, dtype=jnp.float32))))
    pos = jnp.arange(RET_CHUNK, dtype=jnp.float32)
    diff = pos[:, None] - pos[None, :]
    din = jnp.where(diff >= 0, jnp.exp(jnp.maximum(diff, 0.0)[None] * log_g[:, None, None]), 0.0)
    dq = jnp.exp((pos + 1.0)[None] * log_g[:, None])
    dk = jnp.exp((RET_CHUNK - 1.0 - pos)[None] * log_g[:, None])
    dc = jnp.exp(RET_CHUNK * log_g)
    consts.update(ret_din=din, ret_dq=dq[:, :, None], ret_dk=dk[:, None, :],
                  ret_dc=jnp.broadcast_to(dc[:, None, None], (RET_HEADS, 1, LANES)))
    return consts


def _layer_weights(p, consts):
    bf = jnp.bfloat16
    w_in = p["w_in"]

    def cols(off, size):
        return w_in[:, off:off + size]

    zeros = lambda n: jnp.zeros((D_MODEL, n), jnp.float32)
    w_tm = jnp.concatenate([
        cols(O_FQ, FOX_W), cols(O_FK, FOX_W),
        cols(O_FF, FOX_HEADS), zeros(LANES - FOX_HEADS),
        cols(O_DK, DSA_DIM), cols(O_IK, IDX_DIM), zeros(LANES - DSA_DIM - IDX_DIM),
        cols(O_RQ, RET_QK_W), cols(O_RK, RET_QK_W), cols(O_RV, RET_V_W), cols(O_RG, RET_V_W),
    ], axis=1).astype(bf)
    w_fm = jnp.concatenate([
        cols(O_DQ, DSA_W), cols(O_IQ, IDX_W), cols(O_DV, DSA_DIM),
        cols(O_IW, IDX_HEADS), zeros(16 - IDX_HEADS), cols(O_RK, RET_QK_W), cols(O_FV, FOX_W),
    ], axis=1).T.astype(bf)

    def lane_pad(v, fill=0.0):
        return jnp.concatenate([v, jnp.full((LANES - v.shape[0],), fill, jnp.float32)])[None, :]

    g = p["dsa_q_norm"]
    g_sw = jnp.concatenate([g[DSA_DIM // 2:], g[:DSA_DIM // 2]])
    scale = DSA_DIM ** -0.5 * LOG2E

    def logit_bound(gq, gk, dim):
        b = BOUND_SLACK * dim * jnp.max(jnp.abs(gq)) * jnp.max(jnp.abs(gk)) * (dim ** -0.5 * LOG2E)
        return b.reshape(1, 1).astype(jnp.float32)

    return {
        "fox_bound": logit_bound(p["fox_q_norm"], p["fox_k_norm"], FOX_DIM),
        "dsa_bound": logit_bound(p["dsa_q_norm"], p["dsa_k_norm"], DSA_DIM),
        "ln1_g": p["ln1_g"][None, :], "ln2_g": p["ln2_g"][None, :],
        "w_tm": w_tm, "w_fm": w_fm,
        "fox_b": lane_pad(p["fox_b_f"]),
        "fq_gain": lane_pad(p["fox_q_norm"]), "fk_gain": lane_pad(p["fox_k_norm"]),
        "dk_gain": jnp.concatenate([p["dsa_k_norm"], jnp.ones((IDX_DIM,), jnp.float32),
                                    jnp.zeros((LANES - DSA_DIM - IDX_DIM,), jnp.float32)])[None, :],
        "dqa": consts["c64t"] * (g * scale)[:, None], "dqb": consts["s64t"] * (g_sw * scale)[:, None],
        "ret_gain": p["ret_out_norm"][:, None, :],
        "w_zg": cols(O_ZG, N_BRANCH * D_MODEL).astype(bf),
        "w_a": p["w_fox_out"].astype(bf), "w_b": p["w_dsa_out"].astype(bf), "w_c": p["w_ret_out"].astype(bf),
        "w_o": p["w_o"].astype(bf),
        "w_gu": p["w_ffn_in"].astype(bf),
        "w_d": p["w_ffn_out"].astype(bf),
    }


def _layer(x2d, lw, consts, batch, seq):
    (fq, fk, fvt, dk, rq, rk, rv, rg, qt, iqt, vt, iwt, rkt) = _inproj(x2d, lw, consts, batch, seq)
    ya = _fox(fq, fk, fvt, batch, seq)
    yb = _dsa(lw["dsa_bound"], dk, vt, qt, iqt, iwt, consts["ltri_strict"], batch, seq)
    yc = _ret(rq, rk, rkt, rv, rg, consts, lw["ret_gain"], batch, seq)
    x2d = _merge(x2d, ya, yb, yc, lw)
    return _ffn(x2d, lw)


def kernel(x, ln1_g, w_in, fox_b_f, fox_q_norm, fox_k_norm, dsa_q_norm, dsa_k_norm, ret_out_norm,
           w_fox_out, w_dsa_out, w_ret_out, w_o, ln2_g, w_ffn_in, w_ffn_out):
    batch, seq, _ = x.shape
    depth = w_in.shape[0]
    consts = _constants(seq)
    params = dict(ln1_g=ln1_g, w_in=w_in, fox_b_f=fox_b_f, fox_q_norm=fox_q_norm, fox_k_norm=fox_k_norm,
                  dsa_q_norm=dsa_q_norm, dsa_k_norm=dsa_k_norm, ret_out_norm=ret_out_norm,
                  w_fox_out=w_fox_out, w_dsa_out=w_dsa_out, w_ret_out=w_ret_out, w_o=w_o,
                  ln2_g=ln2_g, w_ffn_in=w_ffn_in, w_ffn_out=w_ffn_out)
    x2d = x.reshape(batch * seq, D_MODEL)
    stacked = jax.vmap(lambda p: _layer_weights(p, consts))(params)
    for layer in range(depth):
        lw = {k: ((v, layer) if k in STACKED_WEIGHTS else v[layer]) for k, v in stacked.items()}
        x2d = _layer(x2d, lw, consts, batch, seq)
    return x2d.reshape(batch, seq, D_MODEL)
```

```python
import functools
import math

import jax
import jax.numpy as jnp
import numpy as np
from jax import lax
from jax.experimental import pallas as pl
from jax.experimental.pallas import tpu as pltpu

D_MODEL = 1024
FOX_HEADS = 8
FOX_DIM = 64
DSA_HEADS = 8
DSA_DIM = 64
IDX_HEADS = 8
IDX_DIM = 32
DSA_MAX_TOPK = 256
RET_HEADS = 4
RET_QK_DIM = 64
RET_V_DIM = 128
RET_CHUNK = 128
FFN_HIDDEN = 2816
ROPE_THETA = 10000.0
NORM_EPS = 1e-6
N_BRANCH = 3

FOX_W = FOX_HEADS * FOX_DIM
DSA_W = DSA_HEADS * DSA_DIM
IDX_W = IDX_HEADS * IDX_DIM
RET_QK_W = RET_HEADS * RET_QK_DIM
RET_V_W = RET_HEADS * RET_V_DIM
IN_SIZES = (FOX_W, FOX_W, FOX_W, FOX_HEADS,
            DSA_W, DSA_DIM, DSA_DIM, IDX_W, IDX_DIM, IDX_HEADS,
            RET_QK_W, RET_QK_W, RET_V_W, RET_V_W,
            N_BRANCH * D_MODEL)
IN_OFFS = tuple(int(v) for v in np.cumsum((0,) + IN_SIZES))
(O_FQ, O_FK, O_FV, O_FF, O_DQ, O_DK, O_DV, O_IQ, O_IK, O_IW,
 O_RQ, O_RK, O_RV, O_RG, O_ZG, _) = IN_OFFS

LANES = 128
SUBLANES = 8
BF16_ROWS = 16
VMEM_LIMIT = 56 * 1024 * 1024
NEG_BIG = -1e30
LOG2E = math.log2(math.e)
BOUND_SLACK = 1.02
MIN_DENOM = 2.0 ** -100
HALF_RANGE = 2 ** 15

T_FQ = 0
T_FK = T_FQ + FOX_W
T_FF = T_FK + FOX_W
T_DK = T_FF + LANES
T_RQ = T_DK + LANES
T_RK = T_RQ + RET_QK_W
T_RV = T_RK + RET_QK_W
T_RG = T_RV + RET_V_W
F_DQ = 0
F_IQ = F_DQ + DSA_W
F_DV = F_IQ + IDX_W
F_IW = F_DV + DSA_DIM
F_RK = F_IW + 16
F_FV = F_RK + RET_QK_W

TM_IN = 512
KC = TM_IN
HALF = KC // 2
RB = 32
TQ_FOX = 512
FOX_GROUP = 4
FOX_VROWS = 2 * FOX_DIM + BF16_ROWS
DSA_VROWS = DSA_DIM + BF16_ROWS
TQ_DSA = 256
RET_STEP = 4 * RET_CHUNK
TM_POST = 512
STACKED_WEIGHTS = ("w_tm", "w_fm", "w_zg", "w_a", "w_b", "w_c", "w_o", "w_gu", "w_d")


def _cparams(sem):
    return pltpu.CompilerParams(dimension_semantics=sem, vmem_limit_bytes=VMEM_LIMIT)


def _split3(v):
    hi = v.astype(jnp.bfloat16)
    r1 = v - hi.astype(jnp.float32)
    mid = r1.astype(jnp.bfloat16)
    lo = (r1 - mid.astype(jnp.float32)).astype(jnp.bfloat16)
    return hi, mid, lo


def _resident(w):
    if isinstance(w, tuple):
        a, layer = w
        return a, pl.BlockSpec((None,) + a.shape[1:], lambda *_: (layer,) + (0,) * (a.ndim - 1))
    return w, pl.BlockSpec(w.shape, lambda *_: (0,) * w.ndim)


def _col_reduce(v, op):
    return op(v.reshape(v.shape[0] // SUBLANES, SUBLANES, v.shape[1]), axis=0)


def _fold16(v):
    parts = [v[r0:r0 + BF16_ROWS] for r0 in range(0, v.shape[0], BF16_ROWS)]
    while len(parts) > 1:
        parts = [parts[i] + parts[i + 1] for i in range(0, len(parts), 2)]
    return parts[0]


def _attend(n_chunks, qk, mask, pv, emit, bound, sa_scr, sb_scr, pa_scr, pb_scr, acc_scr, den_row,
            between=None):
    def store_logits(s_ref, c, half):
        lo = 0
        for g in qk(c, half):
            s_ref[:, lo:lo + g.shape[1]] = g
            lo += g.shape[1]

    def probs(s_ref, c, half, p_ref, shift):
        for r0 in range(0, s_ref.shape[0], RB):
            t = mask(s_ref[r0:r0 + RB, :], c, half, r0)
            if shift is not None:
                t = t - shift
            p_ref[r0:r0 + RB, :] = jnp.exp2(t).astype(p_ref.dtype)

    def run(shift, first=False):
        acc_scr[...] = jnp.zeros_like(acc_scr)
        pb_scr[...] = jnp.zeros_like(pb_scr)
        store_logits(sa_scr, 0, 0)
        if first and between is not None:
            between()

        def chunk(c, carry):
            store_logits(sb_scr, c, 1)
            acc_scr[...] += pv(jnp.maximum(c - 1, 0), 1, pb_scr)
            probs(sa_scr, c, 0, pa_scr, shift)
            store_logits(sa_scr, jnp.minimum(c + 1, n_chunks - 1), 0)
            acc_scr[...] += pv(c, 0, pa_scr)
            probs(sb_scr, c, 1, pb_scr, shift)
            return carry

        lax.fori_loop(0, n_chunks, chunk, 0)
        acc_scr[...] += pv(n_chunks - 1, 1, pb_scr)

    run(None, first=True)
    emit()

    @pl.when(2.0 * bound > -math.log2(MIN_DENOM))
    def _():
        @pl.when(jnp.logical_not(jnp.min(acc_scr[den_row:den_row + 1, :]) >= MIN_DENOM))
        def _():
            def col_max(c, mx):
                for half in range(2):
                    store_logits(sa_scr, c, half)
                    for r0 in range(0, sa_scr.shape[0], RB):
                        mx = jnp.maximum(mx, _col_reduce(mask(sa_scr[r0:r0 + RB, :], c, half, r0), jnp.max))
                return mx
            mx = lax.fori_loop(0, n_chunks, col_max,
                               jnp.full((SUBLANES, acc_scr.shape[1]), NEG_BIG, jnp.float32))
            run(jnp.max(mx, axis=0, keepdims=True))
            emit()


def _inproj_kernel(bound_ref, x_ref, g_ref, wtm_ref, wfm_ref, fb_ref, fqg_ref, fkg_ref, dkg_ref,
                   ltri_ref, eq_ref, ek_ref, oneq_ref, onek_ref,
                   kc_ref, ks_ref, rc_ref, rs_ref,
                   dqa_ref, dqb_ref, iqa_ref, iqb_ref, rka_ref, rkb_ref,
                   fq_out, fk_out, fvt_out, dk_out, rq_out, rk_out, rv_out, rg_out,
                   qt_out, iqt_out, vt_out, iw_out, rkt_out,
                   carry_ref, *, tiles_per_seq):
    tm = x_ref.shape[0]
    i = pl.program_id(0)

    @pl.when(i % tiles_per_seq == 0)
    def _():
        carry_ref[...] = jnp.zeros_like(carry_ref)

    x = x_ref[...]
    ms = jnp.mean(x * x, axis=-1, keepdims=True)
    h = (x * lax.rsqrt(ms + NORM_EPS) * g_ref[...]).astype(jnp.bfloat16)

    def tm_dot(lo, width):
        return jnp.dot(h, wtm_ref[:, lo:lo + width], preferred_element_type=jnp.float32)

    lane = lax.broadcasted_iota(jnp.int32, (tm, LANES), 1)

    ffb = tm_dot(T_FF, LANES) + fb_ref[...]
    lf = (jnp.minimum(ffb, 0.0) - jnp.log1p(jnp.exp(-jnp.abs(ffb)))) * LOG2E
    parts = jnp.concatenate(_split3(lf), axis=1)
    cs = jnp.dot(ltri_ref[...], parts, preferred_element_type=jnp.float32)
    c = cs[:, :LANES] + cs[:, LANES:2 * LANES] + cs[:, 2 * LANES:] + carry_ref[...]
    carry_ref[...] = c[tm - 1:tm, :]
    cparts_q = jnp.concatenate(_split3(c - bound_ref[0, 0]), axis=1)
    cparts_k = jnp.concatenate(_split3(c), axis=1)
    scat_q = jnp.dot(cparts_q, eq_ref[...], preferred_element_type=jnp.float32) + oneq_ref[...]
    scat_k = jnp.dot(cparts_k, ek_ref[...], preferred_element_type=jnp.float32) + onek_ref[...]

    for (lo, gain_ref, scat, out, scale) in ((T_FQ, fqg_ref, scat_q, fq_out, FOX_DIM ** -0.5 * LOG2E),
                                             (T_FK, fkg_ref, scat_k, fk_out, 1.0)):
        z = tm_dot(lo, FOX_W)
        for hd in range(FOX_HEADS):
            blk = z[:, (hd // 2) * LANES:(hd // 2 + 1) * LANES]
            if hd % 2:
                blk = pltpu.roll(blk, FOX_DIM, 1)
            ss = jnp.sum(jnp.where(lane < FOX_DIM, blk * blk, 0.0), axis=-1, keepdims=True) * (1.0 / FOX_DIM)
            nb = blk * lax.rsqrt(ss + NORM_EPS) * (gain_ref[...] * scale)
            out[:, hd * LANES:(hd + 1) * LANES] = (
                nb + scat[:, hd * LANES:(hd + 1) * LANES]).astype(out.dtype)

    zk = tm_dot(T_DK, LANES)
    ssk = jnp.sum(jnp.where(lane < DSA_DIM, zk * zk, 0.0), axis=-1, keepdims=True) * (1.0 / DSA_DIM)
    nk = zk * jnp.where(lane < DSA_DIM, lax.rsqrt(ssk + NORM_EPS), 1.0) * dkg_ref[...]
    partner = jnp.where(
        lane < 32, pltpu.roll(nk, LANES - 32, 1),
        jnp.where(lane < 64, pltpu.roll(nk, 32, 1),
                  jnp.where(lane < 80, pltpu.roll(nk, LANES - 16, 1), pltpu.roll(nk, 16, 1))))
    dk_out[...] = (nk * kc_ref[...] + partner * ks_ref[...]).astype(dk_out.dtype)

    first_half = (lane % RET_QK_DIM) < (RET_QK_DIM // 2)
    for (lo, out, scale) in ((T_RQ, rq_out, RET_QK_DIM ** -0.5), (T_RK, rk_out, 1.0)):
        z = tm_dot(lo, RET_QK_W)
        for j in range(RET_QK_W // LANES):
            blk = z[:, j * LANES:(j + 1) * LANES]
            pr = jnp.where(first_half, pltpu.roll(blk, LANES - 32, 1), pltpu.roll(blk, 32, 1))
            out[:, j * LANES:(j + 1) * LANES] = (
                (blk * rc_ref[...] + pr * rs_ref[...]) * scale).astype(out.dtype)

    rv_out[...] = tm_dot(T_RV, RET_V_W).astype(rv_out.dtype)
    rg_out[...] = tm_dot(T_RG, RET_V_W)

    zt = lax.dot_general(wfm_ref[...], h, (((1,), (1,)), ((), ())),
                         preferred_element_type=jnp.float32)
    nq = tm // TQ_DSA

    def swap_halves(v):
        half = v.shape[0] // 2
        return jnp.concatenate([v[half:], v[:half]], axis=0)

    def ones_row_block(rows, dtype):
        first = lax.broadcasted_iota(jnp.int32, (rows, tm), 0) == 0
        return jnp.where(first, 1.0, 0.0).astype(dtype)

    zeros_q = jnp.zeros((LANES - DSA_DIM, DSA_HEADS * TQ_DSA), qt_out.dtype)
    zeros_i0 = jnp.zeros((DSA_DIM, IDX_HEADS * TQ_DSA), iqt_out.dtype)
    zeros_i1 = jnp.zeros((LANES - DSA_DIM - IDX_DIM, IDX_HEADS * TQ_DSA), iqt_out.dtype)
    for j in range(nq):
        qt_out[j, DSA_DIM:, :] = zeros_q
        iqt_out[j, :DSA_DIM, :] = zeros_i0
        iqt_out[j, DSA_DIM + IDX_DIM:, :] = zeros_i1
    for hd in range(DSA_HEADS):
        xh = zt[F_DQ + hd * DSA_DIM:F_DQ + (hd + 1) * DSA_DIM, :]
        r = lax.rsqrt(jnp.sum(xh * xh, axis=0, keepdims=True) * (1.0 / DSA_DIM) + NORM_EPS)
        o = ((xh * dqa_ref[...] + swap_halves(xh) * dqb_ref[...]) * r).astype(qt_out.dtype)
        for j in range(nq):
            qt_out[j, :DSA_DIM, hd * TQ_DSA:(hd + 1) * TQ_DSA] = o[:, j * TQ_DSA:(j + 1) * TQ_DSA]
    for hd in range(IDX_HEADS):
        xh = zt[F_IQ + hd * IDX_DIM:F_IQ + (hd + 1) * IDX_DIM, :]
        o = (xh * iqa_ref[...] + swap_halves(xh) * iqb_ref[...]).astype(iqt_out.dtype)
        for j in range(nq):
            iqt_out[j, DSA_DIM:DSA_DIM + IDX_DIM, hd * TQ_DSA:(hd + 1) * TQ_DSA] = (
                o[:, j * TQ_DSA:(j + 1) * TQ_DSA])
    vt_out[0, 0, :DSA_DIM, :] = zt[F_DV:F_DV + DSA_DIM, :].astype(vt_out.dtype)
    vt_out[0, 0, DSA_DIM:, :] = ones_row_block(BF16_ROWS, vt_out.dtype)
    iw_out[0] = zt[F_IW:F_IW + IDX_HEADS, :] * ((IDX_DIM * IDX_HEADS) ** -0.5)
    for hd in range(RET_HEADS):
        xh = zt[F_RK + hd * RET_QK_DIM:F_RK + (hd + 1) * RET_QK_DIM, :]
        rkt_out[0, hd * RET_QK_DIM:(hd + 1) * RET_QK_DIM, :] = (
            xh * rka_ref[...] + swap_halves(xh) * rkb_ref[...])
    for hp in range(FOX_HEADS // 2):
        fvt_out[0, hp, 0, :2 * FOX_DIM, :] = (
            zt[F_FV + hp * 2 * FOX_DIM:F_FV + (hp + 1) * 2 * FOX_DIM, :].astype(fvt_out.dtype))
        fvt_out[0, hp, 0, 2 * FOX_DIM:, :] = ones_row_block(BF16_ROWS, fvt_out.dtype)


def _inproj(x2d, lw, consts, batch, seq):
    n = x2d.shape[0]
    tm = min(TM_IN, seq)
    tps = seq // tm
    nqt = tm // TQ_DSA
    grid = (n // tm,)
    bf = jnp.bfloat16

    def full(a):
        return _resident(a)[1]

    def tok(width):
        return pl.BlockSpec((tm, width), lambda i: (i, 0))

    def pos_tm(width):
        return pl.BlockSpec((tm, width), lambda i: (i % tps, 0))

    def pos_fm(rows):
        return pl.BlockSpec((rows, tm), lambda i: (0, i % tps))

    def fm_out(rows):
        return pl.BlockSpec((1, rows, tm), lambda i: (i // tps, 0, i % tps))

    in_arrays = [lw["fox_bound"], x2d, lw["ln1_g"], _resident(lw["w_tm"])[0], _resident(lw["w_fm"])[0],
                 lw["fox_b"], lw["fq_gain"], lw["fk_gain"],
                 lw["dk_gain"], consts["ltri"], consts["eq"], consts["ek"], consts["oneq"], consts["onek"],
                 consts["kc"], consts["ks"], consts["rc"], consts["rs"],
                 lw["dqa"], lw["dqb"], consts["iqa"], consts["iqb"], consts["rka"], consts["rkb"]]
    in_specs = [pl.BlockSpec(memory_space=pltpu.SMEM),
                tok(D_MODEL), full(lw["ln1_g"]), full(lw["w_tm"]), full(lw["w_fm"]), full(lw["fox_b"]),
                full(lw["fq_gain"]), full(lw["fk_gain"]), full(lw["dk_gain"]),
                full(consts["ltri"]), full(consts["eq"]), full(consts["ek"]),
                full(consts["oneq"]), full(consts["onek"]),
                pos_tm(LANES), pos_tm(LANES), pos_tm(LANES), pos_tm(LANES),
                pos_fm(DSA_DIM), pos_fm(DSA_DIM), pos_fm(IDX_DIM), pos_fm(IDX_DIM),
                pos_fm(RET_QK_DIM), pos_fm(RET_QK_DIM)]
    out_shape = [
        jax.ShapeDtypeStruct((n, FOX_HEADS * LANES), bf),
        jax.ShapeDtypeStruct((n, FOX_HEADS * LANES), bf),
        jax.ShapeDtypeStruct((batch, FOX_HEADS // 2, tps, FOX_VROWS, tm), bf),
        jax.ShapeDtypeStruct((n, LANES), bf),
        jax.ShapeDtypeStruct((n, RET_QK_W), bf),
        jax.ShapeDtypeStruct((n, RET_QK_W), bf),
        jax.ShapeDtypeStruct((n, RET_V_W), bf),
        jax.ShapeDtypeStruct((n, RET_V_W), jnp.float32),
        jax.ShapeDtypeStruct((n // TQ_DSA, LANES, DSA_HEADS * TQ_DSA), bf),
        jax.ShapeDtypeStruct((n // TQ_DSA, LANES, IDX_HEADS * TQ_DSA), bf),
        jax.ShapeDtypeStruct((batch, tps, DSA_VROWS, tm), bf),
        jax.ShapeDtypeStruct((batch, IDX_HEADS, seq), jnp.float32),
        jax.ShapeDtypeStruct((batch, RET_QK_W, seq), jnp.float32),
    ]
    out_specs = [tok(FOX_HEADS * LANES), tok(FOX_HEADS * LANES),
                 pl.BlockSpec((1, FOX_HEADS // 2, 1, FOX_VROWS, tm), lambda i: (i // tps, 0, i % tps, 0, 0)),
                 tok(LANES), tok(RET_QK_W), tok(RET_QK_W), tok(RET_V_W), tok(RET_V_W),
                 pl.BlockSpec((nqt, LANES, DSA_HEADS * TQ_DSA), lambda i: (i, 0, 0)),
                 pl.BlockSpec((nqt, LANES, IDX_HEADS * TQ_DSA), lambda i: (i, 0, 0)),
                 pl.BlockSpec((1, 1, DSA_VROWS, tm), lambda i: (i // tps, i % tps, 0, 0)),
                 fm_out(IDX_HEADS), fm_out(RET_QK_W)]
    return pl.pallas_call(
        functools.partial(_inproj_kernel, tiles_per_seq=tps),
        grid=grid, in_specs=in_specs, out_specs=out_specs, out_shape=out_shape,
        scratch_shapes=[pltpu.VMEM((1, LANES), jnp.float32)],
        compiler_params=_cparams(("arbitrary",)), name="inproj",
    )(*in_arrays)


def _fox_kernel(bound_ref, q_ref, k_ref, vt_ref, o_ref, sa_scr, sb_scr, pa_scr, pb_scr, acc_scr):
    tq = q_ref.shape[0]
    qi = pl.program_id(2)
    n_chunks = (qi * tq + tq + KC - 1) // KC
    nt = (((1,), (1,)), ((), ()))

    nheads = FOX_GROUP
    qcol = qi * tq + lax.broadcasted_iota(jnp.int32, (RB, tq), 1)
    qcol = jnp.concatenate([qcol] * nheads, axis=1)
    krow = lax.broadcasted_iota(jnp.int32, (RB, nheads * tq), 0)

    def qk(c, half):
        off = pl.multiple_of(c * KC + half * HALF, HALF)
        return [lax.dot_general(k_ref[pl.ds(off, HALF), hh * LANES:(hh + 1) * LANES],
                                q_ref[:, hh * LANES:(hh + 1) * LANES], nt,
                                preferred_element_type=jnp.float32) for hh in range(nheads)]

    def mask(t, c, half, r0):
        return jnp.where(krow + (c * KC + half * HALF + r0) <= qcol, t, NEG_BIG)

    def pv(c, half, p_ref):
        return jnp.concatenate(
            [jnp.dot(vt_ref[0, hh // 2, c, :, half * HALF:(half + 1) * HALF],
                     p_ref[:, hh * tq:(hh + 1) * tq], preferred_element_type=jnp.float32)
             for hh in range(nheads)], axis=1)

    def emit():
        acc = acc_scr[...]
        den = acc[2 * FOX_DIM:2 * FOX_DIM + 1, :]
        for hp in range(nheads // 2):
            ev, od = 2 * hp * tq, (2 * hp + 1) * tq
            out_t = jnp.concatenate([acc[:FOX_DIM, ev:ev + tq] / den[:, ev:ev + tq],
                                     acc[FOX_DIM:2 * FOX_DIM, od:od + tq] / den[:, od:od + tq]], axis=0)
            for j in range(tq // LANES):
                o_ref[j * LANES:(j + 1) * LANES, hp * LANES:(hp + 1) * LANES] = (
                    out_t[:, j * LANES:(j + 1) * LANES].T.astype(o_ref.dtype))

    _attend(n_chunks, qk, mask, pv, emit, bound_ref[0, 0],
            sa_scr, sb_scr, pa_scr, pb_scr, acc_scr, 2 * FOX_DIM)


def _fox(bound, fq, fk, fvt, batch, seq):
    tq = min(TQ_FOX, seq)
    nq = seq // tq
    g = FOX_GROUP
    width = g * tq
    return pl.pallas_call(
        _fox_kernel, grid=(batch, FOX_HEADS // g, nq),
        in_specs=[pl.BlockSpec(memory_space=pltpu.SMEM),
                  pl.BlockSpec((tq, g * LANES), lambda b, hg, qi: (b * nq + qi, hg)),
                  pl.BlockSpec((seq, g * LANES), lambda b, hg, qi: (b, hg)),
                  pl.BlockSpec((1, g // 2) + fvt.shape[2:], lambda b, hg, qi: (b, hg, 0, 0, 0))],
        out_specs=pl.BlockSpec((tq, g * FOX_DIM), lambda b, hg, qi: (b * nq + qi, hg)),
        out_shape=jax.ShapeDtypeStruct((batch * seq, FOX_W), jnp.bfloat16),
        scratch_shapes=[pltpu.VMEM((HALF, width), jnp.float32), pltpu.VMEM((HALF, width), jnp.float32),
                        pltpu.VMEM((HALF, width), jnp.bfloat16), pltpu.VMEM((HALF, width), jnp.bfloat16),
                        pltpu.VMEM((FOX_VROWS, width), jnp.float32)],
        compiler_params=_cparams(("parallel", "parallel", "arbitrary")), name="fox",
    )(bound, fq, fk, fvt)


def _dsa_kernel(bound_ref, k_ref, vt_ref, qt_ref, iqt_ref, iw_ref, ltri_ref, o_ref,
                key_scr, hi_scr, lo_scr, bias_scr, sa_scr, sb_scr, pa_scr, pb_scr, acc_scr, *, topk):
    tq = TQ_DSA
    nh = DSA_HEADS
    selected_bias = -bound_ref[0, 0]
    qi = pl.program_id(1)
    n_chunks = (qi * tq + tq + KC - 1) // KC

    qpos = qi * tq + lax.broadcasted_iota(jnp.int32, (KC, tq), 1)
    krow = lax.broadcasted_iota(jnp.int32, (KC, tq), 0)

    def score_chunk(c, carry):
        off = pl.multiple_of(c * KC, KC)
        rel = jnp.dot(k_ref[pl.ds(off, KC), :], iqt_ref[0], preferred_element_type=jnp.float32)
        score = jnp.maximum(rel[:, :tq], 0.0) * iw_ref[0, 0:1, :]
        for hd in range(1, IDX_HEADS):
            score = score + jnp.maximum(rel[:, hd * tq:(hd + 1) * tq], 0.0) * iw_ref[0, hd:hd + 1, :]
        score = jnp.where(krow + off <= qpos, score, -jnp.inf)
        bits = pltpu.bitcast(score, jnp.int32)
        sign = bits >> 31
        key = (bits ^ (sign & jnp.int32(0x7FFFFFFF))) - sign
        key_scr[pl.ds(off, KC), :] = key
        hi_scr[pl.ds(off, KC), :] = (key >> 16).astype(jnp.int16)
        lo_scr[pl.ds(off, KC), :] = ((key & jnp.int32(0xFFFF)) - HALF_RANGE).astype(jnp.int16)
        return carry

    lax.fori_loop(0, n_chunks, score_chunk, 0)

    def count16_ge(plane_scr, thr):
        thr16 = thr.astype(jnp.int16)

        def body(c, acc):
            off = pl.multiple_of(c * KC, KC)
            hit = jnp.where(plane_scr[pl.ds(off, KC), :] >= thr16, jnp.int16(1), jnp.int16(0))
            return acc + _fold16(hit)
        acc = lax.fori_loop(0, n_chunks, body, jnp.zeros((BF16_ROWS, tq), jnp.int16))
        return jnp.sum(acc.astype(jnp.int32), axis=0, keepdims=True)

    def kth_largest16(plane_scr, kth):
        def bit_step(b, carry):
            thr, cnt = carry
            bit = jnp.left_shift(jnp.int32(1), 15 - b)
            cand = jnp.where(b == 0, jnp.zeros_like(thr), thr | bit)
            cand_cnt = count16_ge(plane_scr, cand)
            take = cand_cnt >= kth
            return jnp.where(take, cand, thr), jnp.where(take, cand_cnt, cnt)
        return lax.fori_loop(0, 16, bit_step, (jnp.full((1, tq), -HALF_RANGE, jnp.int32),
                                               jnp.full((1, tq), n_chunks * KC, jnp.int32)))

    def count_ge(thr):
        def body(c, acc):
            off = pl.multiple_of(c * KC, KC)
            hit = jnp.where(key_scr[pl.ds(off, KC), :] >= thr, 1, 0).astype(jnp.int32)
            return acc + _col_reduce(hit, jnp.sum)
        acc = lax.fori_loop(0, n_chunks, body, jnp.zeros((SUBLANES, tq), jnp.int32))
        return jnp.sum(acc, axis=0, keepdims=True)

    thr_hi, cnt_hi = kth_largest16(hi_scr, topk)
    thr_hi16 = thr_hi.astype(jnp.int16)

    def keep_candidates(c, acc):
        off = pl.multiple_of(c * KC, KC)
        hi = hi_scr[pl.ds(off, KC), :]
        lo_scr[pl.ds(off, KC), :] = jnp.where(hi == thr_hi16, lo_scr[pl.ds(off, KC), :], jnp.int16(-HALF_RANGE))
        return acc + _fold16(jnp.where(hi > thr_hi16, jnp.int16(1), jnp.int16(0)))

    above = lax.fori_loop(0, n_chunks, keep_candidates, jnp.zeros((BF16_ROWS, tq), jnp.int16))
    n_above = jnp.sum(above.astype(jnp.int32), axis=0, keepdims=True)
    thr_lo, cnt_lo = kth_largest16(lo_scr, topk - n_above)
    thr = thr_hi * (2 * HALF_RANGE) + (thr_lo + HALF_RANGE)
    n_ge = jnp.where(thr_lo > -HALF_RANGE, n_above + cnt_lo, cnt_hi)

    def select_keys():
        has_ties = jnp.max(n_ge) > topk

        @pl.when(jnp.logical_not(has_ties))
        def _():
            def body(c, carry):
                off = pl.multiple_of(c * KC, KC)
                bias_scr[pl.ds(off, KC), :] = jnp.where(key_scr[pl.ds(off, KC), :] >= thr,
                                                        selected_bias, NEG_BIG)
                return carry
            lax.fori_loop(0, n_chunks, body, 0)

        @pl.when(has_ties)
        def _():
            int_max = jnp.int32(2 ** 31 - 1)
            n_gt = jnp.where(thr == int_max, 0, count_ge(jnp.where(thr == int_max, thr, thr + 1)))
            need = (topk - n_gt).astype(jnp.float32)

            def body(c, seen):
                off = pl.multiple_of(c * KC, KC)
                ks = key_scr[pl.ds(off, KC), :]
                eq = ks == thr
                eqf = jnp.where(eq, 1.0, 0.0)
                before = jnp.dot(ltri_ref[...], eqf.astype(jnp.bfloat16),
                                 preferred_element_type=jnp.float32) + seen
                sel = jnp.logical_or(ks > thr, jnp.logical_and(eq, before < need))
                sel = jnp.logical_and(sel, krow + off <= qpos)
                bias_scr[pl.ds(off, KC), :] = jnp.where(sel, selected_bias, NEG_BIG)
                return seen + jnp.sum(eqf, axis=0, keepdims=True)
            lax.fori_loop(0, n_chunks, body, jnp.zeros((1, tq), jnp.float32))

    def qk(c, half):
        off = pl.multiple_of(c * KC + half * HALF, HALF)
        return [jnp.dot(k_ref[pl.ds(off, HALF), :], qt_ref[0], preferred_element_type=jnp.float32)]

    def mask(t, c, half, r0):
        b = bias_scr[pl.ds(pl.multiple_of(c * KC + half * HALF + r0, RB), RB), :]
        return t + jnp.concatenate([b] * nh, axis=1)

    def pv(c, half, p_ref):
        return jnp.dot(vt_ref[0, c, :, half * HALF:(half + 1) * HALF], p_ref[...],
                       preferred_element_type=jnp.float32)

    def emit():
        acc = acc_scr[...]
        out_t = acc[:DSA_DIM, :] / acc[DSA_DIM:DSA_DIM + 1, :]
        for hp in range(nh // 2):
            pair = jnp.concatenate([out_t[:, 2 * hp * tq:(2 * hp + 1) * tq],
                                    out_t[:, (2 * hp + 1) * tq:(2 * hp + 2) * tq]], axis=0)
            o_ref[:, hp * LANES:(hp + 1) * LANES] = pair.T.astype(o_ref.dtype)

    _attend(n_chunks, qk, mask, pv, emit, bound_ref[0, 0],
            sa_scr, sb_scr, pa_scr, pb_scr, acc_scr, DSA_DIM, between=select_keys)


def _dsa(bound, dk, vt, qt, iqt, iwt, ltri_strict, batch, seq):
    nq = seq // TQ_DSA
    assert vt.shape[3] == KC and ltri_strict.shape[0] == KC and seq % KC == 0
    topk = min(DSA_MAX_TOPK, seq // 4)
    width = DSA_HEADS * TQ_DSA
    return pl.pallas_call(
        functools.partial(_dsa_kernel, topk=topk),
        grid=(batch, nq),
        in_specs=[pl.BlockSpec(memory_space=pltpu.SMEM),
                  pl.BlockSpec((seq, LANES), lambda b, qi: (b, 0)),
                  pl.BlockSpec((1,) + vt.shape[1:], lambda b, qi: (b, 0, 0, 0)),
                  pl.BlockSpec((1, LANES, width), lambda b, qi: (b * nq + qi, 0, 0)),
                  pl.BlockSpec((1, LANES, IDX_HEADS * TQ_DSA), lambda b, qi: (b * nq + qi, 0, 0)),
                  pl.BlockSpec((1, IDX_HEADS, TQ_DSA), lambda b, qi: (b, 0, qi)),
                  pl.BlockSpec(ltri_strict.shape, lambda b, qi: (0, 0))],
        out_specs=pl.BlockSpec((TQ_DSA, DSA_W), lambda b, qi: (b * nq + qi, 0)),
        out_shape=jax.ShapeDtypeStruct((batch * seq, DSA_W), jnp.bfloat16),
        scratch_shapes=[pltpu.VMEM((seq, TQ_DSA), jnp.int32),
                        pltpu.VMEM((seq, TQ_DSA), jnp.int16),
                        pltpu.VMEM((seq, TQ_DSA), jnp.int16),
                        pltpu.VMEM((seq, TQ_DSA), jnp.float32),
                        pltpu.VMEM((HALF, width), jnp.float32), pltpu.VMEM((HALF, width), jnp.float32),
                        pltpu.VMEM((HALF, width), jnp.bfloat16), pltpu.VMEM((HALF, width), jnp.bfloat16),
                        pltpu.VMEM((DSA_VROWS, width), jnp.float32)],
        compiler_params=_cparams(("parallel", "arbitrary")), name="dsa",
    )(bound, dk, vt, qt, iqt, iwt, ltri_strict)


def _ret_kernel(q_ref, k_ref, kt_ref, v_ref, g_ref, din_ref, dq_ref, dk_ref, dc_ref, gain_ref,
                o_ref, state_scr):
    t = pl.program_id(1)
    c = RET_CHUNK

    @pl.when(t == 0)
    def _():
        state_scr[...] = jnp.zeros_like(state_scr)

    for hd in range(RET_HEADS):
        state = state_scr[hd]
        for j in range(q_ref.shape[0] // c):
            rows = slice(j * c, (j + 1) * c)
            q = q_ref[rows, hd * RET_QK_DIM:(hd + 1) * RET_QK_DIM]
            k = k_ref[rows, hd * RET_QK_DIM:(hd + 1) * RET_QK_DIM]
            v = v_ref[rows, hd * RET_V_DIM:(hd + 1) * RET_V_DIM]
            attn = lax.dot_general(q, k, (((1,), (1,)), ((), ())),
                                   preferred_element_type=jnp.float32) * din_ref[hd]
            inner = jnp.dot(attn.astype(jnp.bfloat16), v, preferred_element_type=jnp.float32)
            cross = jnp.dot(q, state.astype(jnp.bfloat16),
                            preferred_element_type=jnp.float32) * dq_ref[hd]
            ktd = (kt_ref[0, hd * RET_QK_DIM:(hd + 1) * RET_QK_DIM, rows] * dk_ref[hd]).astype(jnp.bfloat16)
            state = dc_ref[hd] * state + jnp.dot(ktd, v, preferred_element_type=jnp.float32)
            y = inner + cross
            yn = y * lax.rsqrt(jnp.mean(y * y, axis=-1, keepdims=True) + NORM_EPS) * gain_ref[hd]
            gate = g_ref[rows, hd * RET_V_DIM:(hd + 1) * RET_V_DIM]
            o_ref[rows, hd * RET_V_DIM:(hd + 1) * RET_V_DIM] = (
                yn * (gate * jax.nn.sigmoid(gate))).astype(o_ref.dtype)
        state_scr[hd] = state


def _ret(rq, rk, rkt, rv, rg, consts, gain, batch, seq):
    c = min(RET_STEP, seq)
    n = seq // c

    def tok(width):
        return pl.BlockSpec((c, width), lambda b, t: (b * n + t, 0))

    def full(a):
        return pl.BlockSpec(a.shape, lambda b, t: (0,) * a.ndim)

    return pl.pallas_call(
        _ret_kernel, grid=(batch, n),
        in_specs=[tok(RET_QK_W), tok(RET_QK_W),
                  pl.BlockSpec((1, RET_QK_W, c), lambda b, t: (b, 0, t)),
                  tok(RET_V_W), tok(RET_V_W),
                  full(consts["ret_din"]), full(consts["ret_dq"]), full(consts["ret_dk"]),
                  full(consts["ret_dc"]), full(gain)],
        out_specs=tok(RET_V_W),
        out_shape=jax.ShapeDtypeStruct((batch * seq, RET_V_W), jnp.bfloat16),
        scratch_shapes=[pltpu.VMEM((RET_HEADS, RET_QK_DIM, RET_V_DIM), jnp.float32)],
        compiler_params=_cparams(("parallel", "arbitrary")), name="ret",
    )(rq, rk, rkt, rv, rg, consts["ret_din"], consts["ret_dq"], consts["ret_dk"], consts["ret_dc"], gain)


def _merge_kernel(x_ref, g_ref, ya_ref, yb_ref, yc_ref, wzg_ref, wa_ref, wb_ref, wc_ref, wo_ref, o_ref):
    x = x_ref[...]
    ms = jnp.mean(x * x, axis=-1, keepdims=True)
    h = (x * lax.rsqrt(ms + NORM_EPS) * g_ref[...]).astype(jnp.bfloat16)
    merged = None
    for j, (y_ref, w_ref) in enumerate(((ya_ref, wa_ref), (yb_ref, wb_ref), (yc_ref, wc_ref))):
        gate = jax.nn.sigmoid(jnp.dot(h, wzg_ref[:, j * D_MODEL:(j + 1) * D_MODEL],
                                      preferred_element_type=jnp.float32))
        term = gate * jnp.dot(y_ref[...], w_ref[...], preferred_element_type=jnp.float32)
        merged = term if merged is None else merged + term
    o_ref[...] = x + jnp.dot(merged.astype(jnp.bfloat16), wo_ref[...],
                             preferred_element_type=jnp.float32)


def _ffn_kernel(x_ref, g_ref, wgu_ref, wd_ref, o_ref, *, chunk):
    x = x_ref[...]
    ms = jnp.mean(x * x, axis=-1, keepdims=True)
    h = (x * lax.rsqrt(ms + NORM_EPS) * g_ref[...]).astype(jnp.bfloat16)
    acc = x
    for lo in range(0, FFN_HIDDEN, chunk):
        gt = jnp.dot(h, wgu_ref[:, lo:lo + chunk], preferred_element_type=jnp.float32)
        up = jnp.dot(h, wgu_ref[:, FFN_HIDDEN + lo:FFN_HIDDEN + lo + chunk],
                     preferred_element_type=jnp.float32)
        act = (gt * jax.nn.sigmoid(gt) * up).astype(jnp.bfloat16)
        acc = acc + jnp.dot(act, wd_ref[lo:lo + chunk, :], preferred_element_type=jnp.float32)
    o_ref[...] = acc


def _row_call(kernel, name, order):
    n = order[0][1].shape[0]
    tm = min(TM_POST, n)
    arrays, specs = [], []
    for kind, a in order:
        if kind == "row":
            arrays.append(a)
            specs.append(pl.BlockSpec((tm, a.shape[1]), lambda i: (i, 0)))
        else:
            arr, spec = _resident(a)
            arrays.append(arr)
            specs.append(spec)
    return pl.pallas_call(
        kernel, grid=(n // tm,), in_specs=specs,
        out_specs=pl.BlockSpec((tm, D_MODEL), lambda i: (i, 0)),
        out_shape=jax.ShapeDtypeStruct((n, D_MODEL), jnp.float32),
        compiler_params=_cparams(("parallel",)), name=name,
    )(*arrays)


def _merge(x2d, ya, yb, yc, lw):
    order = [("row", x2d), ("full", lw["ln1_g"]), ("row", ya), ("row", yb), ("row", yc),
             ("full", lw["w_zg"]), ("full", lw["w_a"]), ("full", lw["w_b"]), ("full", lw["w_c"]),
             ("full", lw["w_o"])]
    return _row_call(_merge_kernel, "merge", order)


def _ffn(x2d, lw):
    order = [("row", x2d), ("full", lw["ln2_g"]), ("full", lw["w_gu"]), ("full", lw["w_d"])]
    return _row_call(functools.partial(_ffn_kernel, chunk=256), "ffn", order)


def _rope_cs(seq, dim):
    half = dim // 2
    inv_freq = ROPE_THETA ** (-jnp.arange(half, dtype=jnp.float32) / half)
    ang = jnp.arange(seq, dtype=jnp.float32)[:, None] * inv_freq[None, :]
    return jnp.cos(ang), jnp.sin(ang)


def _constants(seq):
    tm = min(TM_IN, seq)
    bf = jnp.bfloat16
    cos64, sin64 = _rope_cs(seq, 64)
    cos32, sin32 = _rope_cs(seq, 32)
    z32 = jnp.zeros((seq, 32), jnp.float32)
    c64 = jnp.concatenate([cos64, cos64], axis=1)
    s64 = jnp.concatenate([-sin64, sin64], axis=1)
    c32 = jnp.concatenate([cos32, cos32], axis=1)
    s32 = jnp.concatenate([-sin32, sin32], axis=1)
    consts = {
        "kc": jnp.concatenate([c64, c32, z32], axis=1), "ks": jnp.concatenate([s64, s32, z32], axis=1),
        "rc": jnp.concatenate([c64, c64], axis=1), "rs": jnp.concatenate([s64, s64], axis=1),
        "c64t": c64.T, "s64t": s64.T,
        "iqa": c32.T, "iqb": s32.T, "rka": c64.T, "rkb": s64.T,
        "ltri": jnp.tril(jnp.ones((tm, tm), jnp.float32)).astype(bf),
        "ltri_strict": jnp.tril(jnp.ones((KC, KC), jnp.float32), -1).astype(bf),
    }
    eq = np.zeros((3 * LANES, FOX_HEADS * LANES), np.float32)
    ek = np.zeros((3 * LANES, FOX_HEADS * LANES), np.float32)
    oneq = np.zeros((1, FOX_HEADS * LANES), np.float32)
    onek = np.zeros((1, FOX_HEADS * LANES), np.float32)
    for hd in range(FOX_HEADS):
        base = hd * LANES + FOX_DIM
        for part in range(3):
            eq[part * LANES + hd, base + part] = 1.0
            ek[part * LANES + hd, base + 3 + part] = -1.0
            oneq[0, base + 3 + part] = 1.0
            onek[0, base + part] = 1.0
    consts.update(eq=jnp.asarray(eq, bf), ek=jnp.asarray(ek, bf), oneq=jnp.asarray(oneq), onek=jnp.asarray(onek))
    log_g = jnp.log1p(-(2.0 ** (-5.0 - jnp.arange(RET_HEADS, dtype=jnp.float32))))
    pos = jnp.arange(RET_CHUNK, dtype=jnp.float32)
    diff = pos[:, None] - pos[None, :]
    din = jnp.where(diff >= 0, jnp.exp(jnp.maximum(diff, 0.0)[None] * log_g[:, None, None]), 0.0)
    dq = jnp.exp((pos + 1.0)[None] * log_g[:, None])
    dk = jnp.exp((RET_CHUNK - 1.0 - pos)[None] * log_g[:, None])
    dc = jnp.exp(RET_CHUNK * log_g)
    consts.update(ret_din=din, ret_dq=dq[:, :, None], ret_dk=dk[:, None, :],
                  ret_dc=jnp.broadcast_to(dc[:, None, None], (RET_HEADS, 1, LANES)))
    return consts


def _layer_weights(p, consts):
    bf = jnp.bfloat16
    w_in = p["w_in"]

    def cols(off, size):
        return w_in[:, off:off + size]

    zeros = lambda n: jnp.zeros((D_MODEL, n), jnp.float32)
    w_tm = jnp.concatenate([
        cols(O_FQ, FOX_W), cols(O_FK, FOX_W),
        cols(O_FF, FOX_HEADS), zeros(LANES - FOX_HEADS),
        cols(O_DK, DSA_DIM), cols(O_IK, IDX_DIM), zeros(LANES - DSA_DIM - IDX_DIM),
        cols(O_RQ, RET_QK_W), cols(O_RK, RET_QK_W), cols(O_RV, RET_V_W), cols(O_RG, RET_V_W),
    ], axis=1).astype(bf)
    w_fm = jnp.concatenate([
        cols(O_DQ, DSA_W), cols(O_IQ, IDX_W), cols(O_DV, DSA_DIM),
        cols(O_IW, IDX_HEADS), zeros(16 - IDX_HEADS), cols(O_RK, RET_QK_W), cols(O_FV, FOX_W),
    ], axis=1).T.astype(bf)

    def lane_pad(v, fill=0.0):
        return jnp.concatenate([v, jnp.full((LANES - v.shape[0],), fill, jnp.float32)])[None, :]

    g = p["dsa_q_norm"]
    g_sw = jnp.concatenate([g[DSA_DIM // 2:], g[:DSA_DIM // 2]])
    scale = DSA_DIM ** -0.5 * LOG2E

    def logit_bound(gq, gk, dim):
        b = BOUND_SLACK * dim * jnp.max(jnp.abs(gq)) * jnp.max(jnp.abs(gk)) * (dim ** -0.5 * LOG2E)
        return b.reshape(1, 1).astype(jnp.float32)

    return {
        "fox_bound": logit_bound(p["fox_q_norm"], p["fox_k_norm"], FOX_DIM),
        "dsa_bound": logit_bound(p["dsa_q_norm"], p["dsa_k_norm"], DSA_DIM),
        "ln1_g": p["ln1_g"][None, :], "ln2_g": p["ln2_g"][None, :],
        "w_tm": w_tm, "w_fm": w_fm,
        "fox_b": lane_pad(p["fox_b_f"]),
        "fq_gain": lane_pad(p["fox_q_norm"]), "fk_gain": lane_pad(p["fox_k_norm"]),
        "dk_gain": jnp.concatenate([p["dsa_k_norm"], jnp.ones((IDX_DIM,), jnp.float32),
                                    jnp.zeros((LANES - DSA_DIM - IDX_DIM,), jnp.float32)])[None, :],
        "dqa": consts["c64t"] * (g * scale)[:, None], "dqb": consts["s64t"] * (g_sw * scale)[:, None],
        "ret_gain": p["ret_out_norm"][:, None, :],
        "w_zg": cols(O_ZG, N_BRANCH * D_MODEL).astype(bf),
        "w_a": p["w_fox_out"].astype(bf), "w_b": p["w_dsa_out"].astype(bf), "w_c": p["w_ret_out"].astype(bf),
        "w_o": p["w_o"].astype(bf),
        "w_gu": p["w_ffn_in"].astype(bf),
        "w_d": p["w_ffn_out"].astype(bf),
    }


def _layer(x2d, lw, consts, batch, seq):
    (fq, fk, fvt, dk, rq, rk, rv, rg, qt, iqt, vt, iwt, rkt) = _inproj(x2d, lw, consts, batch, seq)
    ya = _fox(lw["fox_bound"], fq, fk, fvt, batch, seq)
    yb = _dsa(lw["dsa_bound"], dk, vt, qt, iqt, iwt, consts["ltri_strict"], batch, seq)
    yc = _ret(rq, rk, rkt, rv, rg, consts, lw["ret_gain"], batch, seq)
    x2d = _merge(x2d, ya, yb, yc, lw)
    return _ffn(x2d, lw)


def kernel(x, ln1_g, w_in, fox_b_f, fox_q_norm, fox_k_norm, dsa_q_norm, dsa_k_norm, ret_out_norm,
           w_fox_out, w_dsa_out, w_ret_out, w_o, ln2_g, w_ffn_in, w_ffn_out):
    batch, seq, _ = x.shape
    depth = w_in.shape[0]
    consts = _constants(seq)
    params = dict(ln1_g=ln1_g, w_in=w_in, fox_b_f=fox_b_f, fox_q_norm=fox_q_norm, fox_k_norm=fox_k_norm,
                  dsa_q_norm=dsa_q_norm, dsa_k_norm=dsa_k_norm, ret_out_norm=ret_out_norm,
                  w_fox_out=w_fox_out, w_dsa_out=w_dsa_out, w_ret_out=w_ret_out, w_o=w_o,
                  ln2_g=ln2_g, w_ffn_in=w_ffn_in, w_ffn_out=w_ffn_out)
    x2d = x.reshape(batch * seq, D_MODEL)
    stacked = jax.vmap(lambda p: _layer_weights(p, consts))(params)
    for layer in range(depth):
        lw = {k: ((v, layer) if k in STACKED_WEIGHTS else v[layer]) for k, v in stacked.items()}
        x2d = _layer(x2d, lw, consts, batch, seq)
    return x2d.reshape(batch, seq, D_MODEL)
```

```python
import functools
import math

import jax
import jax.numpy as jnp
import numpy as np
from jax import lax
from jax.experimental import pallas as pl
from jax.experimental.pallas import tpu as pltpu

D_MODEL = 1024
FOX_HEADS = 8
FOX_DIM = 64
DSA_HEADS = 8
DSA_DIM = 64
IDX_HEADS = 8
IDX_DIM = 32
DSA_MAX_TOPK = 256
RET_HEADS = 4
RET_QK_DIM = 64
RET_V_DIM = 128
RET_CHUNK = 128
FFN_HIDDEN = 2816
ROPE_THETA = 10000.0
NORM_EPS = 1e-6
N_BRANCH = 3

FOX_W = FOX_HEADS * FOX_DIM
DSA_W = DSA_HEADS * DSA_DIM
IDX_W = IDX_HEADS * IDX_DIM
RET_QK_W = RET_HEADS * RET_QK_DIM
RET_V_W = RET_HEADS * RET_V_DIM
IN_SIZES = (FOX_W, FOX_W, FOX_W, FOX_HEADS,
            DSA_W, DSA_DIM, DSA_DIM, IDX_W, IDX_DIM, IDX_HEADS,
            RET_QK_W, RET_QK_W, RET_V_W, RET_V_W,
            N_BRANCH * D_MODEL)
IN_OFFS = tuple(int(v) for v in np.cumsum((0,) + IN_SIZES))
(O_FQ, O_FK, O_FV, O_FF, O_DQ, O_DK, O_DV, O_IQ, O_IK, O_IW,
 O_RQ, O_RK, O_RV, O_RG, O_ZG, _) = IN_OFFS

LANES = 128
SUBLANES = 8
BF16_ROWS = 16
VMEM_LIMIT = 56 * 1024 * 1024
NEG_BIG = -1e30
LOG2E = math.log2(math.e)
BOUND_SLACK = 1.02
MIN_DENOM = 2.0 ** -100
HALF_RANGE = 2 ** 15

T_FQ = 0
T_FK = T_FQ + FOX_W
T_FF = T_FK + FOX_W
T_DK = T_FF + LANES
T_RQ = T_DK + LANES
T_RK = T_RQ + RET_QK_W
T_RV = T_RK + RET_QK_W
T_RG = T_RV + RET_V_W
F_DQ = 0
F_IQ = F_DQ + DSA_W
F_DV = F_IQ + IDX_W
F_IW = F_DV + DSA_DIM
F_RK = F_IW + 16
F_FV = F_RK + RET_QK_W

TM_IN = 512
KC = TM_IN
HALF = KC // 2
RB = 32
TQ_FOX = 512
FOX_GROUP = 4
FOX_VROWS = 2 * FOX_DIM + BF16_ROWS
DSA_VROWS = DSA_DIM + BF16_ROWS
TQ_DSA = 256
RET_STEP = 4 * RET_CHUNK
TM_POST = 512
STACKED_WEIGHTS = ("w_tm", "w_fm", "w_zg", "w_a", "w_b", "w_c", "w_o", "w_gu", "w_d")


def _cparams(sem):
    return pltpu.CompilerParams(dimension_semantics=sem, vmem_limit_bytes=VMEM_LIMIT)


def _split3(v):
    hi = v.astype(jnp.bfloat16)
    r1 = v - hi.astype(jnp.float32)
    mid = r1.astype(jnp.bfloat16)
    lo = (r1 - mid.astype(jnp.float32)).astype(jnp.bfloat16)
    return hi, mid, lo


def _resident(w):
    if isinstance(w, tuple):
        a, layer = w
        return a, pl.BlockSpec((None,) + a.shape[1:], lambda *_: (layer,) + (0,) * (a.ndim - 1))
    return w, pl.BlockSpec(w.shape, lambda *_: (0,) * w.ndim)


def _col_reduce(v, op):
    return op(v.reshape(v.shape[0] // SUBLANES, SUBLANES, v.shape[1]), axis=0)


def _fold16(v):
    parts = [v[r0:r0 + BF16_ROWS] for r0 in range(0, v.shape[0], BF16_ROWS)]
    while len(parts) > 1:
        parts = [parts[i] + parts[i + 1] for i in range(0, len(parts), 2)]
    return parts[0]


def _attend(n_chunks, qk, mask, pv, emit, bound, sa_scr, sb_scr, pa_scr, pb_scr, acc_scr, den_row,
            between=None):
    def store_logits(s_ref, c, half):
        lo = 0
        for g in qk(c, half):
            s_ref[:, lo:lo + g.shape[1]] = g
            lo += g.shape[1]

    def probs(s_ref, c, half, p_ref, shift):
        for r0 in range(0, s_ref.shape[0], RB):
            t = mask(s_ref[r0:r0 + RB, :], c, half, r0) - shift
            p_ref[r0:r0 + RB, :] = jnp.exp2(t).astype(p_ref.dtype)

    def run(shift, trial):
        acc_scr[...] = jnp.zeros_like(acc_scr)
        pb_scr[...] = jnp.zeros_like(pb_scr)
        store_logits(sa_scr, 0, 0)
        if between is not None:
            pl.when(trial == 0)(between)

        def chunk(c, carry):
            store_logits(sb_scr, c, 1)
            acc_scr[...] += pv(jnp.maximum(c - 1, 0), 1, pb_scr)
            probs(sa_scr, c, 0, pa_scr, shift)
            store_logits(sa_scr, jnp.minimum(c + 1, n_chunks - 1), 0)
            acc_scr[...] += pv(c, 0, pa_scr)
            probs(sb_scr, c, 1, pb_scr, shift)
            return carry

        lax.fori_loop(0, n_chunks, chunk, 0)
        acc_scr[...] += pv(n_chunks - 1, 1, pb_scr)

    width = acc_scr.shape[1]

    def exact_max():
        def col_max(c, mx):
            for half in range(2):
                store_logits(sa_scr, c, half)
                for r0 in range(0, sa_scr.shape[0], RB):
                    mx = jnp.maximum(mx, _col_reduce(mask(sa_scr[r0:r0 + RB, :], c, half, r0), jnp.max))
            return mx
        mx = lax.fori_loop(0, n_chunks, col_max, jnp.full((SUBLANES, width), NEG_BIG, jnp.float32))
        return jnp.max(mx, axis=0, keepdims=True)

    def underflowed():
        return jnp.logical_not(jnp.min(acc_scr[den_row:den_row + 1, :]) >= MIN_DENOM)

    def attempt(carry):
        trial, shift = carry
        run(shift, trial)
        emit()
        suspect = jnp.logical_and(trial == 0, 2.0 * bound > -math.log2(MIN_DENOM))
        redo = lax.cond(suspect, underflowed, lambda: jnp.bool_(False))
        shift = lax.cond(redo, exact_max, lambda: shift)
        return jnp.where(redo, 1, 2), shift

    lax.while_loop(lambda carry: carry[0] < 2, attempt,
                   (jnp.int32(0), jnp.zeros((1, width), jnp.float32)))


def _inproj_kernel(bound_ref, x_ref, g_ref, wtm_ref, wfm_ref, fb_ref, fqg_ref, fkg_ref, dkg_ref,
                   ltri_ref, eq_ref, ek_ref, oneq_ref, onek_ref,
                   kc_ref, ks_ref, rc_ref, rs_ref,
                   dqa_ref, dqb_ref, iqa_ref, iqb_ref, rka_ref, rkb_ref,
                   fq_out, fk_out, fvt_out, dk_out, rq_out, rk_out, rv_out, rg_out,
                   qt_out, iqt_out, vt_out, iw_out, rkt_out,
                   carry_ref, *, tiles_per_seq):
    tm = x_ref.shape[0]
    i = pl.program_id(0)

    @pl.when(i % tiles_per_seq == 0)
    def _():
        carry_ref[...] = jnp.zeros_like(carry_ref)

    x = x_ref[...]
    ms = jnp.mean(x * x, axis=-1, keepdims=True)
    h = (x * lax.rsqrt(ms + NORM_EPS) * g_ref[...]).astype(jnp.bfloat16)

    def tm_dot(lo, width):
        return jnp.dot(h, wtm_ref[:, lo:lo + width], preferred_element_type=jnp.float32)

    lane = lax.broadcasted_iota(jnp.int32, (tm, LANES), 1)

    ffb = tm_dot(T_FF, LANES) + fb_ref[...]
    lf = (jnp.minimum(ffb, 0.0) - jnp.log1p(jnp.exp(-jnp.abs(ffb)))) * LOG2E
    parts = jnp.concatenate(_split3(lf), axis=1)
    cs = jnp.dot(ltri_ref[...], parts, preferred_element_type=jnp.float32)
    c = cs[:, :LANES] + cs[:, LANES:2 * LANES] + cs[:, 2 * LANES:] + carry_ref[...]
    carry_ref[...] = c[tm - 1:tm, :]
    cparts_q = jnp.concatenate(_split3(c - bound_ref[0, 0]), axis=1)
    cparts_k = jnp.concatenate(_split3(c), axis=1)
    scat_q = jnp.dot(cparts_q, eq_ref[...], preferred_element_type=jnp.float32) + oneq_ref[...]
    scat_k = jnp.dot(cparts_k, ek_ref[...], preferred_element_type=jnp.float32) + onek_ref[...]

    for (lo, gain_ref, scat, out, scale) in ((T_FQ, fqg_ref, scat_q, fq_out, FOX_DIM ** -0.5 * LOG2E),
                                             (T_FK, fkg_ref, scat_k, fk_out, 1.0)):
        z = tm_dot(lo, FOX_W)
        for hd in range(FOX_HEADS):
            blk = z[:, (hd // 2) * LANES:(hd // 2 + 1) * LANES]
            if hd % 2:
                blk = pltpu.roll(blk, FOX_DIM, 1)
            ss = jnp.sum(jnp.where(lane < FOX_DIM, blk * blk, 0.0), axis=-1, keepdims=True) * (1.0 / FOX_DIM)
            nb = blk * lax.rsqrt(ss + NORM_EPS) * (gain_ref[...] * scale)
            out[:, hd * LANES:(hd + 1) * LANES] = (
                nb + scat[:, hd * LANES:(hd + 1) * LANES]).astype(out.dtype)

    zk = tm_dot(T_DK, LANES)
    ssk = jnp.sum(jnp.where(lane < DSA_DIM, zk * zk, 0.0), axis=-1, keepdims=True) * (1.0 / DSA_DIM)
    nk = zk * jnp.where(lane < DSA_DIM, lax.rsqrt(ssk + NORM_EPS), 1.0) * dkg_ref[...]
    partner = jnp.where(
        lane < 32, pltpu.roll(nk, LANES - 32, 1),
        jnp.where(lane < 64, pltpu.roll(nk, 32, 1),
                  jnp.where(lane < 80, pltpu.roll(nk, LANES - 16, 1), pltpu.roll(nk, 16, 1))))
    dk_out[...] = (nk * kc_ref[...] + partner * ks_ref[...]).astype(dk_out.dtype)

    first_half = (lane % RET_QK_DIM) < (RET_QK_DIM // 2)
    for (lo, out, scale) in ((T_RQ, rq_out, RET_QK_DIM ** -0.5), (T_RK, rk_out, 1.0)):
        z = tm_dot(lo, RET_QK_W)
        for j in range(RET_QK_W // LANES):
            blk = z[:, j * LANES:(j + 1) * LANES]
            pr = jnp.where(first_half, pltpu.roll(blk, LANES - 32, 1), pltpu.roll(blk, 32, 1))
            out[:, j * LANES:(j + 1) * LANES] = (
                (blk * rc_ref[...] + pr * rs_ref[...]) * scale).astype(out.dtype)

    rv_out[...] = tm_dot(T_RV, RET_V_W).astype(rv_out.dtype)
    rg_out[...] = tm_dot(T_RG, RET_V_W)

    zt = lax.dot_general(wfm_ref[...], h, (((1,), (1,)), ((), ())),
                         preferred_element_type=jnp.float32)
    nq = tm // TQ_DSA

    def swap_halves(v):
        half = v.shape[0] // 2
        return jnp.concatenate([v[half:], v[:half]], axis=0)

    def ones_row_block(rows, dtype):
        first = lax.broadcasted_iota(jnp.int32, (rows, tm), 0) == 0
        return jnp.where(first, 1.0, 0.0).astype(dtype)

    zeros_q = jnp.zeros((LANES - DSA_DIM, DSA_HEADS * TQ_DSA), qt_out.dtype)
    zeros_i0 = jnp.zeros((DSA_DIM, IDX_HEADS * TQ_DSA), iqt_out.dtype)
    zeros_i1 = jnp.zeros((LANES - DSA_DIM - IDX_DIM, IDX_HEADS * TQ_DSA), iqt_out.dtype)
    for j in range(nq):
        qt_out[j, DSA_DIM:, :] = zeros_q
        iqt_out[j, :DSA_DIM, :] = zeros_i0
        iqt_out[j, DSA_DIM + IDX_DIM:, :] = zeros_i1
    for hd in range(DSA_HEADS):
        xh = zt[F_DQ + hd * DSA_DIM:F_DQ + (hd + 1) * DSA_DIM, :]
        r = lax.rsqrt(jnp.sum(xh * xh, axis=0, keepdims=True) * (1.0 / DSA_DIM) + NORM_EPS)
        o = ((xh * dqa_ref[...] + swap_halves(xh) * dqb_ref[...]) * r).astype(qt_out.dtype)
        for j in range(nq):
            qt_out[j, :DSA_DIM, hd * TQ_DSA:(hd + 1) * TQ_DSA] = o[:, j * TQ_DSA:(j + 1) * TQ_DSA]
    for hd in range(IDX_HEADS):
        xh = zt[F_IQ + hd * IDX_DIM:F_IQ + (hd + 1) * IDX_DIM, :]
        o = (xh * iqa_ref[...] + swap_halves(xh) * iqb_ref[...]).astype(iqt_out.dtype)
        for j in range(nq):
            iqt_out[j, DSA_DIM:DSA_DIM + IDX_DIM, hd * TQ_DSA:(hd + 1) * TQ_DSA] = (
                o[:, j * TQ_DSA:(j + 1) * TQ_DSA])
    vt_out[0, 0, :DSA_DIM, :] = zt[F_DV:F_DV + DSA_DIM, :].astype(vt_out.dtype)
    vt_out[0, 0, DSA_DIM:, :] = ones_row_block(BF16_ROWS, vt_out.dtype)
    iw_out[0] = zt[F_IW:F_IW + IDX_HEADS, :] * ((IDX_DIM * IDX_HEADS) ** -0.5)
    for hd in range(RET_HEADS):
        xh = zt[F_RK + hd * RET_QK_DIM:F_RK + (hd + 1) * RET_QK_DIM, :]
        rkt_out[0, hd * RET_QK_DIM:(hd + 1) * RET_QK_DIM, :] = (
            xh * rka_ref[...] + swap_halves(xh) * rkb_ref[...])
    for hp in range(FOX_HEADS // 2):
        fvt_out[0, hp, 0, :2 * FOX_DIM, :] = (
            zt[F_FV + hp * 2 * FOX_DIM:F_FV + (hp + 1) * 2 * FOX_DIM, :].astype(fvt_out.dtype))
        fvt_out[0, hp, 0, 2 * FOX_DIM:, :] = ones_row_block(BF16_ROWS, fvt_out.dtype)


def _inproj(x2d, lw, consts, batch, seq):
    n = x2d.shape[0]
    tm = min(TM_IN, seq)
    tps = seq // tm
    nqt = tm // TQ_DSA
    grid = (n // tm,)
    bf = jnp.bfloat16

    def full(a):
        return _resident(a)[1]

    def tok(width):
        return pl.BlockSpec((tm, width), lambda i: (i, 0))

    def pos_tm(width):
        return pl.BlockSpec((tm, width), lambda i: (i % tps, 0))

    def pos_fm(rows):
        return pl.BlockSpec((rows, tm), lambda i: (0, i % tps))

    def fm_out(rows):
        return pl.BlockSpec((1, rows, tm), lambda i: (i // tps, 0, i % tps))

    in_arrays = [lw["fox_bound"], x2d, lw["ln1_g"], _resident(lw["w_tm"])[0], _resident(lw["w_fm"])[0],
                 lw["fox_b"], lw["fq_gain"], lw["fk_gain"],
                 lw["dk_gain"], consts["ltri"], consts["eq"], consts["ek"], consts["oneq"], consts["onek"],
                 consts["kc"], consts["ks"], consts["rc"], consts["rs"],
                 lw["dqa"], lw["dqb"], consts["iqa"], consts["iqb"], consts["rka"], consts["rkb"]]
    in_specs = [pl.BlockSpec(memory_space=pltpu.SMEM),
                tok(D_MODEL), full(lw["ln1_g"]), full(lw["w_tm"]), full(lw["w_fm"]), full(lw["fox_b"]),
                full(lw["fq_gain"]), full(lw["fk_gain"]), full(lw["dk_gain"]),
                full(consts["ltri"]), full(consts["eq"]), full(consts["ek"]),
                full(consts["oneq"]), full(consts["onek"]),
                pos_tm(LANES), pos_tm(LANES), pos_tm(LANES), pos_tm(LANES),
                pos_fm(DSA_DIM), pos_fm(DSA_DIM), pos_fm(IDX_DIM), pos_fm(IDX_DIM),
                pos_fm(RET_QK_DIM), pos_fm(RET_QK_DIM)]
    out_shape = [
        jax.ShapeDtypeStruct((n, FOX_HEADS * LANES), bf),
        jax.ShapeDtypeStruct((n, FOX_HEADS * LANES), bf),
        jax.ShapeDtypeStruct((batch, FOX_HEADS // 2, tps, FOX_VROWS, tm), bf),
        jax.ShapeDtypeStruct((n, LANES), bf),
        jax.ShapeDtypeStruct((n, RET_QK_W), bf),
        jax.ShapeDtypeStruct((n, RET_QK_W), bf),
        jax.ShapeDtypeStruct((n, RET_V_W), bf),
        jax.ShapeDtypeStruct((n, RET_V_W), jnp.float32),
        jax.ShapeDtypeStruct((n // TQ_DSA, LANES, DSA_HEADS * TQ_DSA), bf),
        jax.ShapeDtypeStruct((n // TQ_DSA, LANES, IDX_HEADS * TQ_DSA), bf),
        jax.ShapeDtypeStruct((batch, tps, DSA_VROWS, tm), bf),
        jax.ShapeDtypeStruct((batch, IDX_HEADS, seq), jnp.float32),
        jax.ShapeDtypeStruct((batch, RET_QK_W, seq), jnp.float32),
    ]
    out_specs = [tok(FOX_HEADS * LANES), tok(FOX_HEADS * LANES),
                 pl.BlockSpec((1, FOX_HEADS // 2, 1, FOX_VROWS, tm), lambda i: (i // tps, 0, i % tps, 0, 0)),
                 tok(LANES), tok(RET_QK_W), tok(RET_QK_W), tok(RET_V_W), tok(RET_V_W),
                 pl.BlockSpec((nqt, LANES, DSA_HEADS * TQ_DSA), lambda i: (i, 0, 0)),
                 pl.BlockSpec((nqt, LANES, IDX_HEADS * TQ_DSA), lambda i: (i, 0, 0)),
                 pl.BlockSpec((1, 1, DSA_VROWS, tm), lambda i: (i // tps, i % tps, 0, 0)),
                 fm_out(IDX_HEADS), fm_out(RET_QK_W)]
    return pl.pallas_call(
        functools.partial(_inproj_kernel, tiles_per_seq=tps),
        grid=grid, in_specs=in_specs, out_specs=out_specs, out_shape=out_shape,
        scratch_shapes=[pltpu.VMEM((1, LANES), jnp.float32)],
        compiler_params=_cparams(("arbitrary",)), name="inproj",
    )(*in_arrays)


def _fox_kernel(bound_ref, q_ref, k_ref, vt_ref, o_ref, sa_scr, sb_scr, pa_scr, pb_scr, acc_scr):
    tq = q_ref.shape[0]
    qi = pl.program_id(2)
    n_chunks = (qi * tq + tq + KC - 1) // KC
    nt = (((1,), (1,)), ((), ()))

    nheads = FOX_GROUP
    qcol = qi * tq + lax.broadcasted_iota(jnp.int32, (RB, tq), 1)
    qcol = jnp.concatenate([qcol] * nheads, axis=1)
    krow = lax.broadcasted_iota(jnp.int32, (RB, nheads * tq), 0)

    def qk(c, half):
        off = pl.multiple_of(c * KC + half * HALF, HALF)
        return [lax.dot_general(k_ref[pl.ds(off, HALF), hh * LANES:(hh + 1) * LANES],
                                q_ref[:, hh * LANES:(hh + 1) * LANES], nt,
                                preferred_element_type=jnp.float32) for hh in range(nheads)]

    def mask(t, c, half, r0):
        return jnp.where(krow + (c * KC + half * HALF + r0) <= qcol, t, NEG_BIG)

    def pv(c, half, p_ref):
        return jnp.concatenate(
            [jnp.dot(vt_ref[0, hh // 2, c, :, half * HALF:(half + 1) * HALF],
                     p_ref[:, hh * tq:(hh + 1) * tq], preferred_element_type=jnp.float32)
             for hh in range(nheads)], axis=1)

    def emit():
        acc = acc_scr[...]
        den = acc[2 * FOX_DIM:2 * FOX_DIM + 1, :]
        for hp in range(nheads // 2):
            ev, od = 2 * hp * tq, (2 * hp + 1) * tq
            out_t = jnp.concatenate([acc[:FOX_DIM, ev:ev + tq] / den[:, ev:ev + tq],
                                     acc[FOX_DIM:2 * FOX_DIM, od:od + tq] / den[:, od:od + tq]], axis=0)
            for j in range(tq // LANES):
                o_ref[j * LANES:(j + 1) * LANES, hp * LANES:(hp + 1) * LANES] = (
                    out_t[:, j * LANES:(j + 1) * LANES].T.astype(o_ref.dtype))

    _attend(n_chunks, qk, mask, pv, emit, bound_ref[0, 0],
            sa_scr, sb_scr, pa_scr, pb_scr, acc_scr, 2 * FOX_DIM)


def _fox(bound, fq, fk, fvt, batch, seq):
    tq = min(TQ_FOX, seq)
    nq = seq // tq
    g = FOX_GROUP
    width = g * tq
    return pl.pallas_call(
        _fox_kernel, grid=(batch, FOX_HEADS // g, nq),
        in_specs=[pl.BlockSpec(memory_space=pltpu.SMEM),
                  pl.BlockSpec((tq, g * LANES), lambda b, hg, qi: (b * nq + qi, hg)),
                  pl.BlockSpec((seq, g * LANES), lambda b, hg, qi: (b, hg)),
                  pl.BlockSpec((1, g // 2) + fvt.shape[2:], lambda b, hg, qi: (b, hg, 0, 0, 0))],
        out_specs=pl.BlockSpec((tq, g * FOX_DIM), lambda b, hg, qi: (b * nq + qi, hg)),
        out_shape=jax.ShapeDtypeStruct((batch * seq, FOX_W), jnp.bfloat16),
        scratch_shapes=[pltpu.VMEM((HALF, width), jnp.float32), pltpu.VMEM((HALF, width), jnp.float32),
                        pltpu.VMEM((HALF, width), jnp.bfloat16), pltpu.VMEM((HALF, width), jnp.bfloat16),
                        pltpu.VMEM((FOX_VROWS, width), jnp.float32)],
        compiler_params=_cparams(("parallel", "parallel", "arbitrary")), name="fox",
    )(bound, fq, fk, fvt)


def _dsa_kernel(bound_ref, k_ref, vt_ref, qt_ref, iqt_ref, iw_ref, ltri_ref, o_ref,
                key_scr, hi_scr, lo_scr, bias_scr, sa_scr, sb_scr, pa_scr, pb_scr, acc_scr, *, topk):
    tq = TQ_DSA
    nh = DSA_HEADS
    selected_bias = -bound_ref[0, 0]
    qi = pl.program_id(1)
    n_chunks = (qi * tq + tq + KC - 1) // KC

    qpos = qi * tq + lax.broadcasted_iota(jnp.int32, (KC, tq), 1)
    krow = lax.broadcasted_iota(jnp.int32, (KC, tq), 0)

    def score_chunk(c, carry):
        off = pl.multiple_of(c * KC, KC)
        rel = jnp.dot(k_ref[pl.ds(off, KC), :], iqt_ref[0], preferred_element_type=jnp.float32)
        score = jnp.maximum(rel[:, :tq], 0.0) * iw_ref[0, 0:1, :]
        for hd in range(1, IDX_HEADS):
            score = score + jnp.maximum(rel[:, hd * tq:(hd + 1) * tq], 0.0) * iw_ref[0, hd:hd + 1, :]
        score = jnp.where(krow + off <= qpos, score, -jnp.inf)
        bits = pltpu.bitcast(score, jnp.int32)
        sign = bits >> 31
        key = (bits ^ (sign & jnp.int32(0x7FFFFFFF))) - sign
        key_scr[pl.ds(off, KC), :] = key
        hi_scr[pl.ds(off, KC), :] = (key >> 16).astype(jnp.int16)
        lo_scr[pl.ds(off, KC), :] = ((key & jnp.int32(0xFFFF)) - HALF_RANGE).astype(jnp.int16)
        return carry

    lax.fori_loop(0, n_chunks, score_chunk, 0)

    def count16_ge(plane_scr, thr):
        thr16 = thr.astype(jnp.int16)

        def body(c, acc):
            off = pl.multiple_of(c * KC, KC)
            hit = jnp.where(plane_scr[pl.ds(off, KC), :] >= thr16, jnp.int16(1), jnp.int16(0))
            return acc + _fold16(hit)
        acc = lax.fori_loop(0, n_chunks, body, jnp.zeros((BF16_ROWS, tq), jnp.int16))
        return jnp.sum(acc.astype(jnp.int32), axis=0, keepdims=True)

    def kth_largest16(plane_scr, kth):
        def bit_step(b, carry):
            thr, cnt = carry
            bit = jnp.left_shift(jnp.int32(1), 15 - b)
            cand = jnp.where(b == 0, jnp.zeros_like(thr), thr | bit)
            cand_cnt = count16_ge(plane_scr, cand)
            take = cand_cnt >= kth
            return jnp.where(take, cand, thr), jnp.where(take, cand_cnt, cnt)
        return lax.fori_loop(0, 16, bit_step, (jnp.full((1, tq), -HALF_RANGE, jnp.int32),
                                               jnp.full((1, tq), n_chunks * KC, jnp.int32)))

    def count_ge(thr):
        def body(c, acc):
            off = pl.multiple_of(c * KC, KC)
            hit = jnp.where(key_scr[pl.ds(off, KC), :] >= thr, 1, 0).astype(jnp.int32)
            return acc + _col_reduce(hit, jnp.sum)
        acc = lax.fori_loop(0, n_chunks, body, jnp.zeros((SUBLANES, tq), jnp.int32))
        return jnp.sum(acc, axis=0, keepdims=True)

    thr_hi, cnt_hi = kth_largest16(hi_scr, topk)
    thr_hi16 = thr_hi.astype(jnp.int16)

    def keep_candidates(c, acc):
        off = pl.multiple_of(c * KC, KC)
        hi = hi_scr[pl.ds(off, KC), :]
        lo_scr[pl.ds(off, KC), :] = jnp.where(hi == thr_hi16, lo_scr[pl.ds(off, KC), :], jnp.int16(-HALF_RANGE))
        return acc + _fold16(jnp.where(hi > thr_hi16, jnp.int16(1), jnp.int16(0)))

    above = lax.fori_loop(0, n_chunks, keep_candidates, jnp.zeros((BF16_ROWS, tq), jnp.int16))
    n_above = jnp.sum(above.astype(jnp.int32), axis=0, keepdims=True)
    thr_lo, cnt_lo = kth_largest16(lo_scr, topk - n_above)
    thr = thr_hi * (2 * HALF_RANGE) + (thr_lo + HALF_RANGE)
    n_ge = jnp.where(thr_lo > -HALF_RANGE, n_above + cnt_lo, cnt_hi)

    def select_keys():
        has_ties = jnp.max(n_ge) > topk

        @pl.when(jnp.logical_not(has_ties))
        def _():
            def body(c, carry):
                off = pl.multiple_of(c * KC, KC)
                bias_scr[pl.ds(off, KC), :] = jnp.where(key_scr[pl.ds(off, KC), :] >= thr,
                                                        selected_bias, NEG_BIG)
                return carry
            lax.fori_loop(0, n_chunks, body, 0)

        @pl.when(has_ties)
        def _():
            int_max = jnp.int32(2 ** 31 - 1)
            n_gt = jnp.where(thr == int_max, 0, count_ge(jnp.where(thr == int_max, thr, thr + 1)))
            need = (topk - n_gt).astype(jnp.float32)

            def body(c, seen):
                off = pl.multiple_of(c * KC, KC)
                ks = key_scr[pl.ds(off, KC), :]
                eq = ks == thr
                eqf = jnp.where(eq, 1.0, 0.0)
                before = jnp.dot(ltri_ref[...], eqf.astype(jnp.bfloat16),
                                 preferred_element_type=jnp.float32) + seen
                sel = jnp.logical_or(ks > thr, jnp.logical_and(eq, before < need))
                sel = jnp.logical_and(sel, krow + off <= qpos)
                bias_scr[pl.ds(off, KC), :] = jnp.where(sel, selected_bias, NEG_BIG)
                return seen + jnp.sum(eqf, axis=0, keepdims=True)
            lax.fori_loop(0, n_chunks, body, jnp.zeros((1, tq), jnp.float32))

    def qk(c, half):
        off = pl.multiple_of(c * KC + half * HALF, HALF)
        return [jnp.dot(k_ref[pl.ds(off, HALF), :], qt_ref[0], preferred_element_type=jnp.float32)]

    def mask(t, c, half, r0):
        b = bias_scr[pl.ds(pl.multiple_of(c * KC + half * HALF + r0, RB), RB), :]
        return t + jnp.concatenate([b] * nh, axis=1)

    def pv(c, half, p_ref):
        return jnp.dot(vt_ref[0, c, :, half * HALF:(half + 1) * HALF], p_ref[...],
                       preferred_element_type=jnp.float32)

    def emit():
        acc = acc_scr[...]
        out_t = acc[:DSA_DIM, :] / acc[DSA_DIM:DSA_DIM + 1, :]
        for hp in range(nh // 2):
            pair = jnp.concatenate([out_t[:, 2 * hp * tq:(2 * hp + 1) * tq],
                                    out_t[:, (2 * hp + 1) * tq:(2 * hp + 2) * tq]], axis=0)
            o_ref[:, hp * LANES:(hp + 1) * LANES] = pair.T.astype(o_ref.dtype)

    _attend(n_chunks, qk, mask, pv, emit, bound_ref[0, 0],
            sa_scr, sb_scr, pa_scr, pb_scr, acc_scr, DSA_DIM, between=select_keys)


def _dsa(bound, dk, vt, qt, iqt, iwt, ltri_strict, batch, seq):
    nq = seq // TQ_DSA
    assert vt.shape[3] == KC and ltri_strict.shape[0] == KC and seq % KC == 0
    topk = min(DSA_MAX_TOPK, seq // 4)
    width = DSA_HEADS * TQ_DSA
    return pl.pallas_call(
        functools.partial(_dsa_kernel, topk=topk),
        grid=(batch, nq),
        in_specs=[pl.BlockSpec(memory_space=pltpu.SMEM),
                  pl.BlockSpec((seq, LANES), lambda b, qi: (b, 0)),
                  pl.BlockSpec((1,) + vt.shape[1:], lambda b, qi: (b, 0, 0, 0)),
                  pl.BlockSpec((1, LANES, width), lambda b, qi: (b * nq + qi, 0, 0)),
                  pl.BlockSpec((1, LANES, IDX_HEADS * TQ_DSA), lambda b, qi: (b * nq + qi, 0, 0)),
                  pl.BlockSpec((1, IDX_HEADS, TQ_DSA), lambda b, qi: (b, 0, qi)),
                  pl.BlockSpec(ltri_strict.shape, lambda b, qi: (0, 0))],
        out_specs=pl.BlockSpec((TQ_DSA, DSA_W), lambda b, qi: (b * nq + qi, 0)),
        out_shape=jax.ShapeDtypeStruct((batch * seq, DSA_W), jnp.bfloat16),
        scratch_shapes=[pltpu.VMEM((seq, TQ_DSA), jnp.int32),
                        pltpu.VMEM((seq, TQ_DSA), jnp.int16),
                        pltpu.VMEM((seq, TQ_DSA), jnp.int16),
                        pltpu.VMEM((seq, TQ_DSA), jnp.float32),
                        pltpu.VMEM((HALF, width), jnp.float32), pltpu.VMEM((HALF, width), jnp.float32),
                        pltpu.VMEM((HALF, width), jnp.bfloat16), pltpu.VMEM((HALF, width), jnp.bfloat16),
                        pltpu.VMEM((DSA_VROWS, width), jnp.float32)],
        compiler_params=_cparams(("parallel", "arbitrary")), name="dsa",
    )(bound, dk, vt, qt, iqt, iwt, ltri_strict)


def _ret_kernel(q_ref, k_ref, kt_ref, v_ref, g_ref, din_ref, dq_ref, dk_ref, dc_ref, gain_ref,
                o_ref, state_scr):
    t = pl.program_id(1)
    c = RET_CHUNK

    @pl.when(t == 0)
    def _():
        state_scr[...] = jnp.zeros_like(state_scr)

    for hd in range(RET_HEADS):
        state = state_scr[hd]
        for j in range(q_ref.shape[0] // c):
            rows = slice(j * c, (j + 1) * c)
            q = q_ref[rows, hd * RET_QK_DIM:(hd + 1) * RET_QK_DIM]
            k = k_ref[rows, hd * RET_QK_DIM:(hd + 1) * RET_QK_DIM]
            v = v_ref[rows, hd * RET_V_DIM:(hd + 1) * RET_V_DIM]
            attn = lax.dot_general(q, k, (((1,), (1,)), ((), ())),
                                   preferred_element_type=jnp.float32) * din_ref[hd]
            inner = jnp.dot(attn.astype(jnp.bfloat16), v, preferred_element_type=jnp.float32)
            cross = jnp.dot(q, state.astype(jnp.bfloat16),
                            preferred_element_type=jnp.float32) * dq_ref[hd]
            ktd = (kt_ref[0, hd * RET_QK_DIM:(hd + 1) * RET_QK_DIM, rows] * dk_ref[hd]).astype(jnp.bfloat16)
            state = dc_ref[hd] * state + jnp.dot(ktd, v, preferred_element_type=jnp.float32)
            y = inner + cross
            yn = y * lax.rsqrt(jnp.mean(y * y, axis=-1, keepdims=True) + NORM_EPS) * gain_ref[hd]
            gate = g_ref[rows, hd * RET_V_DIM:(hd + 1) * RET_V_DIM]
            o_ref[rows, hd * RET_V_DIM:(hd + 1) * RET_V_DIM] = (
                yn * (gate * jax.nn.sigmoid(gate))).astype(o_ref.dtype)
        state_scr[hd] = state


def _ret(rq, rk, rkt, rv, rg, consts, gain, batch, seq):
    c = min(RET_STEP, seq)
    n = seq // c

    def tok(width):
        return pl.BlockSpec((c, width), lambda b, t: (b * n + t, 0))

    def full(a):
        return pl.BlockSpec(a.shape, lambda b, t: (0,) * a.ndim)

    return pl.pallas_call(
        _ret_kernel, grid=(batch, n),
        in_specs=[tok(RET_QK_W), tok(RET_QK_W),
                  pl.BlockSpec((1, RET_QK_W, c), lambda b, t: (b, 0, t)),
                  tok(RET_V_W), tok(RET_V_W),
                  full(consts["ret_din"]), full(consts["ret_dq"]), full(consts["ret_dk"]),
                  full(consts["ret_dc"]), full(gain)],
        out_specs=tok(RET_V_W),
        out_shape=jax.ShapeDtypeStruct((batch * seq, RET_V_W), jnp.bfloat16),
        scratch_shapes=[pltpu.VMEM((RET_HEADS, RET_QK_DIM, RET_V_DIM), jnp.float32)],
        compiler_params=_cparams(("parallel", "arbitrary")), name="ret",
    )(rq, rk, rkt, rv, rg, consts["ret_din"], consts["ret_dq"], consts["ret_dk"], consts["ret_dc"], gain)


def _merge_kernel(x_ref, g_ref, ya_ref, yb_ref, yc_ref, wzg_ref, wa_ref, wb_ref, wc_ref, wo_ref, o_ref):
    x = x_ref[...]
    ms = jnp.mean(x * x, axis=-1, keepdims=True)
    h = (x * lax.rsqrt(ms + NORM_EPS) * g_ref[...]).astype(jnp.bfloat16)
    merged = None
    for j, (y_ref, w_ref) in enumerate(((ya_ref, wa_ref), (yb_ref, wb_ref), (yc_ref, wc_ref))):
        gate = jax.nn.sigmoid(jnp.dot(h, wzg_ref[:, j * D_MODEL:(j + 1) * D_MODEL],
                                      preferred_element_type=jnp.float32))
        term = gate * jnp.dot(y_ref[...], w_ref[...], preferred_element_type=jnp.float32)
        merged = term if merged is None else merged + term
    o_ref[...] = x + jnp.dot(merged.astype(jnp.bfloat16), wo_ref[...],
                             preferred_element_type=jnp.float32)


def _ffn_kernel(x_ref, g_ref, wgu_ref, wd_ref, o_ref, *, chunk):
    x = x_ref[...]
    ms = jnp.mean(x * x, axis=-1, keepdims=True)
    h = (x * lax.rsqrt(ms + NORM_EPS) * g_ref[...]).astype(jnp.bfloat16)
    acc = x
    for lo in range(0, FFN_HIDDEN, chunk):
        gt = jnp.dot(h, wgu_ref[:, lo:lo + chunk], preferred_element_type=jnp.float32)
        up = jnp.dot(h, wgu_ref[:, FFN_HIDDEN + lo:FFN_HIDDEN + lo + chunk],
                     preferred_element_type=jnp.float32)
        act = (gt * jax.nn.sigmoid(gt) * up).astype(jnp.bfloat16)
        acc = acc + jnp.dot(act, wd_ref[lo:lo + chunk, :], preferred_element_type=jnp.float32)
    o_ref[...] = acc


def _row_call(kernel, name, order):
    n = order[0][1].shape[0]
    tm = min(TM_POST, n)
    arrays, specs = [], []
    for kind, a in order:
        if kind == "row":
            arrays.append(a)
            specs.append(pl.BlockSpec((tm, a.shape[1]), lambda i: (i, 0)))
        else:
            arr, spec = _resident(a)
            arrays.append(arr)
            specs.append(spec)
    return pl.pallas_call(
        kernel, grid=(n // tm,), in_specs=specs,
        out_specs=pl.BlockSpec((tm, D_MODEL), lambda i: (i, 0)),
        out_shape=jax.ShapeDtypeStruct((n, D_MODEL), jnp.float32),
        compiler_params=_cparams(("parallel",)), name=name,
    )(*arrays)


def _merge(x2d, ya, yb, yc, lw):
    order = [("row", x2d), ("full", lw["ln1_g"]), ("row", ya), ("row", yb), ("row", yc),
             ("full", lw["w_zg"]), ("full", lw["w_a"]), ("full", lw["w_b"]), ("full", lw["w_c"]),
             ("full", lw["w_o"])]
    return _row_call(_merge_kernel, "merge", order)


def _ffn(x2d, lw):
    order = [("row", x2d), ("full", lw["ln2_g"]), ("full", lw["w_gu"]), ("full", lw["w_d"])]
    return _row_call(functools.partial(_ffn_kernel, chunk=256), "ffn", order)


def _rope_cs(seq, dim):
    half = dim // 2
    inv_freq = ROPE_THETA ** (-jnp.arange(half, dtype=jnp.float32) / half)
    ang = jnp.arange(seq, dtype=jnp.float32)[:, None] * inv_freq[None, :]
    return jnp.cos(ang), jnp.sin(ang)


def _constants(seq):
    tm = min(TM_IN, seq)
    bf = jnp.bfloat16
    cos64, sin64 = _rope_cs(seq, 64)
    cos32, sin32 = _rope_cs(seq, 32)
    z32 = jnp.zeros((seq, 32), jnp.float32)
    c64 = jnp.concatenate([cos64, cos64], axis=1)
    s64 = jnp.concatenate([-sin64, sin64], axis=1)
    c32 = jnp.concatenate([cos32, cos32], axis=1)
    s32 = jnp.concatenate([-sin32, sin32], axis=1)
    consts = {
        "kc": jnp.concatenate([c64, c32, z32], axis=1), "ks": jnp.concatenate([s64, s32, z32], axis=1),
        "rc": jnp.concatenate([c64, c64], axis=1), "rs": jnp.concatenate([s64, s64], axis=1),
        "c64t": c64.T, "s64t": s64.T,
        "iqa": c32.T, "iqb": s32.T, "rka": c64.T, "rkb": s64.T,
        "ltri": jnp.tril(jnp.ones((tm, tm), jnp.float32)).astype(bf),
        "ltri_strict": jnp.tril(jnp.ones((KC, KC), jnp.float32), -1).astype(bf),
    }
    eq = np.zeros((3 * LANES, FOX_HEADS * LANES), np.float32)
    ek = np.zeros((3 * LANES, FOX_HEADS * LANES), np.float32)
    oneq = np.zeros((1, FOX_HEADS * LANES), np.float32)
    onek = np.zeros((1, FOX_HEADS * LANES), np.float32)
    for hd in range(FOX_HEADS):
        base = hd * LANES + FOX_DIM
        for part in range(3):
            eq[part * LANES + hd, base + part] = 1.0
            ek[part * LANES + hd, base + 3 + part] = -1.0
            oneq[0, base + 3 + part] = 1.0
            onek[0, base + part] = 1.0
    consts.update(eq=jnp.asarray(eq, bf), ek=jnp.asarray(ek, bf), oneq=jnp.asarray(oneq), onek=jnp.asarray(onek))
    log_g = jnp.log1p(-(2.0 ** (-5.0 - jnp.arange(RET_HEADS, dtype=jnp.float32))))
    pos = jnp.arange(RET_CHUNK, dtype=jnp.float32)
    diff = pos[:, None] - pos[None, :]
    din = jnp.where(diff >= 0, jnp.exp(jnp.maximum(diff, 0.0)[None] * log_g[:, None, None]), 0.0)
    dq = jnp.exp((pos + 1.0)[None] * log_g[:, None])
    dk = jnp.exp((RET_CHUNK - 1.0 - pos)[None] * log_g[:, None])
    dc = jnp.exp(RET_CHUNK * log_g)
    consts.update(ret_din=din, ret_dq=dq[:, :, None], ret_dk=dk[:, None, :],
                  ret_dc=jnp.broadcast_to(dc[:, None, None], (RET_HEADS, 1, LANES)))
    return consts


def _layer_weights(p, consts):
    bf = jnp.bfloat16
    w_in = p["w_in"]

    def cols(off, size):
        return w_in[:, off:off + size]

    zeros = lambda n: jnp.zeros((D_MODEL, n), jnp.float32)
    w_tm = jnp.concatenate([
        cols(O_FQ, FOX_W), cols(O_FK, FOX_W),
        cols(O_FF, FOX_HEADS), zeros(LANES - FOX_HEADS),
        cols(O_DK, DSA_DIM), cols(O_IK, IDX_DIM), zeros(LANES - DSA_DIM - IDX_DIM),
        cols(O_RQ, RET_QK_W), cols(O_RK, RET_QK_W), cols(O_RV, RET_V_W), cols(O_RG, RET_V_W),
    ], axis=1).astype(bf)
    w_fm = jnp.concatenate([
        cols(O_DQ, DSA_W), cols(O_IQ, IDX_W), cols(O_DV, DSA_DIM),
        cols(O_IW, IDX_HEADS), zeros(16 - IDX_HEADS), cols(O_RK, RET_QK_W), cols(O_FV, FOX_W),
    ], axis=1).T.astype(bf)

    def lane_pad(v, fill=0.0):
        return jnp.concatenate([v, jnp.full((LANES - v.shape[0],), fill, jnp.float32)])[None, :]

    g = p["dsa_q_norm"]
    g_sw = jnp.concatenate([g[DSA_DIM // 2:], g[:DSA_DIM // 2]])
    scale = DSA_DIM ** -0.5 * LOG2E

    def logit_bound(gq, gk, dim):
        b = BOUND_SLACK * dim * jnp.max(jnp.abs(gq)) * jnp.max(jnp.abs(gk)) * (dim ** -0.5 * LOG2E)
        return b.reshape(1, 1).astype(jnp.float32)

    return {
        "fox_bound": logit_bound(p["fox_q_norm"], p["fox_k_norm"], FOX_DIM),
        "dsa_bound": logit_bound(p["dsa_q_norm"], p["dsa_k_norm"], DSA_DIM),
        "ln1_g": p["ln1_g"][None, :], "ln2_g": p["ln2_g"][None, :],
        "w_tm": w_tm, "w_fm": w_fm,
        "fox_b": lane_pad(p["fox_b_f"]),
        "fq_gain": lane_pad(p["fox_q_norm"]), "fk_gain": lane_pad(p["fox_k_norm"]),
        "dk_gain": jnp.concatenate([p["dsa_k_norm"], jnp.ones((IDX_DIM,), jnp.float32),
                                    jnp.zeros((LANES - DSA_DIM - IDX_DIM,), jnp.float32)])[None, :],
        "dqa": consts["c64t"] * (g * scale)[:, None], "dqb": consts["s64t"] * (g_sw * scale)[:, None],
        "ret_gain": p["ret_out_norm"][:, None, :],
        "w_zg": cols(O_ZG, N_BRANCH * D_MODEL).astype(bf),
        "w_a": p["w_fox_out"].astype(bf), "w_b": p["w_dsa_out"].astype(bf), "w_c": p["w_ret_out"].astype(bf),
        "w_o": p["w_o"].astype(bf),
        "w_gu": p["w_ffn_in"].astype(bf),
        "w_d": p["w_ffn_out"].astype(bf),
    }


def _layer(x2d, lw, consts, batch, seq):
    (fq, fk, fvt, dk, rq, rk, rv, rg, qt, iqt, vt, iwt, rkt) = _inproj(x2d, lw, consts, batch, seq)
    ya = _fox(lw["fox_bound"], fq, fk, fvt, batch, seq)
    yb = _dsa(lw["dsa_bound"], dk, vt, qt, iqt, iwt, consts["ltri_strict"], batch, seq)
    yc = _ret(rq, rk, rkt, rv, rg, consts, lw["ret_gain"], batch, seq)
    x2d = _merge(x2d, ya, yb, yc, lw)
    return _ffn(x2d, lw)


def kernel(x, ln1_g, w_in, fox_b_f, fox_q_norm, fox_k_norm, dsa_q_norm, dsa_k_norm, ret_out_norm,
           w_fox_out, w_dsa_out, w_ret_out, w_o, ln2_g, w_ffn_in, w_ffn_out):
    batch, seq, _ = x.shape
    depth = w_in.shape[0]
    consts = _constants(seq)
    params = dict(ln1_g=ln1_g, w_in=w_in, fox_b_f=fox_b_f, fox_q_norm=fox_q_norm, fox_k_norm=fox_k_norm,
                  dsa_q_norm=dsa_q_norm, dsa_k_norm=dsa_k_norm, ret_out_norm=ret_out_norm,
                  w_fox_out=w_fox_out, w_dsa_out=w_dsa_out, w_ret_out=w_ret_out, w_o=w_o,
                  ln2_g=ln2_g, w_ffn_in=w_ffn_in, w_ffn_out=w_ffn_out)
    x2d = x.reshape(batch * seq, D_MODEL)
    stacked = jax.vmap(lambda p: _layer_weights(p, consts))(params)
    for layer in range(depth):
        lw = {k: ((v, layer) if k in STACKED_WEIGHTS else v[layer]) for k, v in stacked.items()}
        x2d = _layer(x2d, lw, consts, batch, seq)
    return x2d.reshape(batch, seq, D_MODEL)
```

```python
import functools
import math

import jax
import jax.numpy as jnp
import numpy as np
from jax import lax
from jax.experimental import pallas as pl
from jax.experimental.pallas import tpu as pltpu

D_MODEL = 1024
FOX_HEADS = 8
FOX_DIM = 64
DSA_HEADS = 8
DSA_DIM = 64
IDX_HEADS = 8
IDX_DIM = 32
DSA_MAX_TOPK = 256
RET_HEADS = 4
RET_QK_DIM = 64
RET_V_DIM = 128
RET_CHUNK = 128
FFN_HIDDEN = 2816
ROPE_THETA = 10000.0
NORM_EPS = 1e-6
N_BRANCH = 3

FOX_W = FOX_HEADS * FOX_DIM
DSA_W = DSA_HEADS * DSA_DIM
IDX_W = IDX_HEADS * IDX_DIM
RET_QK_W = RET_HEADS * RET_QK_DIM
RET_V_W = RET_HEADS * RET_V_DIM
IN_SIZES = (FOX_W, FOX_W, FOX_W, FOX_HEADS,
            DSA_W, DSA_DIM, DSA_DIM, IDX_W, IDX_DIM, IDX_HEADS,
            RET_QK_W, RET_QK_W, RET_V_W, RET_V_W,
            N_BRANCH * D_MODEL)
IN_OFFS = tuple(int(v) for v in np.cumsum((0,) + IN_SIZES))
(O_FQ, O_FK, O_FV, O_FF, O_DQ, O_DK, O_DV, O_IQ, O_IK, O_IW,
 O_RQ, O_RK, O_RV, O_RG, O_ZG, _) = IN_OFFS

LANES = 128
SUBLANES = 8
BF16_ROWS = 16
VMEM_LIMIT = 56 * 1024 * 1024
NEG_BIG = -1e30
LOG2E = math.log2(math.e)
BOUND_SLACK = 1.02
MIN_DENOM = 2.0 ** -100
HALF_RANGE = 2 ** 15

T_FQ = 0
T_FK = T_FQ + FOX_W
T_FF = T_FK + FOX_W
T_DK = T_FF + LANES
T_RQ = T_DK + LANES
T_RK = T_RQ + RET_QK_W
T_RV = T_RK + RET_QK_W
T_RG = T_RV + RET_V_W
F_DQ = 0
F_IQ = F_DQ + DSA_W
F_DV = F_IQ + IDX_W
F_IW = F_DV + DSA_DIM
F_RK = F_IW + 16
F_FV = F_RK + RET_QK_W

TM_IN = 512
KC = TM_IN
HALF = KC // 2
RB = 32
TQ_FOX = 512
FOX_GROUP = 4
FOX_VROWS = 2 * FOX_DIM + BF16_ROWS
DSA_VROWS = DSA_DIM + BF16_ROWS
TQ_DSA = 256
RET_STEP = 4 * RET_CHUNK
TM_POST = 512
STACKED_WEIGHTS = ("w_tm", "w_fm", "w_zg", "w_a", "w_b", "w_c", "w_o", "w_gu", "w_d")


def _cparams(sem):
    return pltpu.CompilerParams(dimension_semantics=sem, vmem_limit_bytes=VMEM_LIMIT)


def _split3(v):
    hi = v.astype(jnp.bfloat16)
    r1 = v - hi.astype(jnp.float32)
    mid = r1.astype(jnp.bfloat16)
    lo = (r1 - mid.astype(jnp.float32)).astype(jnp.bfloat16)
    return hi, mid, lo


def _resident(w, single=False):
    mode = dict(pipeline_mode=pl.Buffered(1)) if single else {}
    if isinstance(w, tuple):
        a, layer = w
        return a, pl.BlockSpec((None,) + a.shape[1:], lambda *_: (layer,) + (0,) * (a.ndim - 1), **mode)
    return w, pl.BlockSpec(w.shape, lambda *_: (0,) * w.ndim, **mode)


def _col_reduce(v, op):
    return op(v.reshape(v.shape[0] // SUBLANES, SUBLANES, v.shape[1]), axis=0)


def _fold16(v):
    parts = [v[r0:r0 + BF16_ROWS] for r0 in range(0, v.shape[0], BF16_ROWS)]
    while len(parts) > 1:
        parts = [parts[i] + parts[i + 1] for i in range(0, len(parts), 2)]
    return parts[0]


def _attend(n_chunks, qk, mask, pv, emit, bound, sa_scr, sb_scr, pa_scr, pb_scr, acc_scr, den_row,
            between=None):
    def store_logits(s_ref, c, half):
        lo = 0
        for g in qk(c, half):
            s_ref[:, lo:lo + g.shape[1]] = g
            lo += g.shape[1]

    def probs(s_ref, c, half, p_ref, shift):
        for r0 in range(0, s_ref.shape[0], RB):
            t = mask(s_ref[r0:r0 + RB, :], c, half, r0) - shift
            p_ref[r0:r0 + RB, :] = jnp.exp2(t).astype(p_ref.dtype)

    def run(shift, trial):
        acc_scr[...] = jnp.zeros_like(acc_scr)
        pb_scr[...] = jnp.zeros_like(pb_scr)
        store_logits(sa_scr, 0, 0)
        if between is not None:
            pl.when(trial == 0)(between)

        def chunk(c, carry):
            store_logits(sb_scr, c, 1)
            acc_scr[...] += pv(jnp.maximum(c - 1, 0), 1, pb_scr)
            probs(sa_scr, c, 0, pa_scr, shift)
            store_logits(sa_scr, jnp.minimum(c + 1, n_chunks - 1), 0)
            acc_scr[...] += pv(c, 0, pa_scr)
            probs(sb_scr, c, 1, pb_scr, shift)
            return carry

        lax.fori_loop(0, n_chunks, chunk, 0)
        acc_scr[...] += pv(n_chunks - 1, 1, pb_scr)

    width = acc_scr.shape[1]

    def exact_max():
        def col_max(c, mx):
            for half in range(2):
                store_logits(sa_scr, c, half)
                for r0 in range(0, sa_scr.shape[0], RB):
                    mx = jnp.maximum(mx, _col_reduce(mask(sa_scr[r0:r0 + RB, :], c, half, r0), jnp.max))
            return mx
        mx = lax.fori_loop(0, n_chunks, col_max, jnp.full((SUBLANES, width), NEG_BIG, jnp.float32))
        return jnp.max(mx, axis=0, keepdims=True)

    def underflowed():
        return jnp.logical_not(jnp.min(acc_scr[den_row:den_row + 1, :]) >= MIN_DENOM)

    def attempt(carry):
        trial, shift = carry
        run(shift, trial)
        emit()
        suspect = jnp.logical_and(trial == 0, 2.0 * bound > -math.log2(MIN_DENOM))
        redo = lax.cond(suspect, underflowed, lambda: jnp.bool_(False))
        shift = lax.cond(redo, exact_max, lambda: shift)
        return jnp.where(redo, 1, 2), shift

    lax.while_loop(lambda carry: carry[0] < 2, attempt,
                   (jnp.int32(0), jnp.zeros((1, width), jnp.float32)))


def _inproj_kernel(bound_ref, x_ref, g_ref, wtm_ref, wfm_ref, fb_ref, fqg_ref, fkg_ref, dkg_ref,
                   ltri_ref, eq_ref, ek_ref, oneq_ref, onek_ref,
                   kc_ref, ks_ref, rc_ref, rs_ref,
                   dqa_ref, dqb_ref, iqa_ref, iqb_ref, rka_ref, rkb_ref,
                   fq_out, fk_out, fvt_out, dk_out, rq_out, rk_out, rv_out, rg_out,
                   qt_out, iqt_out, vt_out, iw_out, rkt_out,
                   carry_ref, *, tiles_per_seq):
    tm = x_ref.shape[0]
    i = pl.program_id(0)

    @pl.when(i % tiles_per_seq == 0)
    def _():
        carry_ref[...] = jnp.zeros_like(carry_ref)

    x = x_ref[...]
    ms = jnp.mean(x * x, axis=-1, keepdims=True)
    h = (x * lax.rsqrt(ms + NORM_EPS) * g_ref[...]).astype(jnp.bfloat16)

    def tm_dot(lo, width):
        return jnp.dot(h, wtm_ref[:, lo:lo + width], preferred_element_type=jnp.float32)

    lane = lax.broadcasted_iota(jnp.int32, (tm, LANES), 1)

    ffb = tm_dot(T_FF, LANES) + fb_ref[...]
    lf = (jnp.minimum(ffb, 0.0) - jnp.log1p(jnp.exp(-jnp.abs(ffb)))) * LOG2E
    parts = jnp.concatenate(_split3(lf), axis=1)
    cs = jnp.dot(ltri_ref[...], parts, preferred_element_type=jnp.float32)
    c = cs[:, :LANES] + cs[:, LANES:2 * LANES] + cs[:, 2 * LANES:] + carry_ref[...]
    carry_ref[...] = c[tm - 1:tm, :]
    cparts_q = jnp.concatenate(_split3(c - bound_ref[0, 0]), axis=1)
    cparts_k = jnp.concatenate(_split3(c), axis=1)
    scat_q = jnp.dot(cparts_q, eq_ref[...], preferred_element_type=jnp.float32) + oneq_ref[...]
    scat_k = jnp.dot(cparts_k, ek_ref[...], preferred_element_type=jnp.float32) + onek_ref[...]

    for (lo, gain_ref, scat, out, scale) in ((T_FQ, fqg_ref, scat_q, fq_out, FOX_DIM ** -0.5 * LOG2E),
                                             (T_FK, fkg_ref, scat_k, fk_out, 1.0)):
        z = tm_dot(lo, FOX_W)
        for hd in range(FOX_HEADS):
            blk = z[:, (hd // 2) * LANES:(hd // 2 + 1) * LANES]
            if hd % 2:
                blk = pltpu.roll(blk, FOX_DIM, 1)
            ss = jnp.sum(jnp.where(lane < FOX_DIM, blk * blk, 0.0), axis=-1, keepdims=True) * (1.0 / FOX_DIM)
            nb = blk * lax.rsqrt(ss + NORM_EPS) * (gain_ref[...] * scale)
            out[:, hd * LANES:(hd + 1) * LANES] = (
                nb + scat[:, hd * LANES:(hd + 1) * LANES]).astype(out.dtype)

    zk = tm_dot(T_DK, LANES)
    ssk = jnp.sum(jnp.where(lane < DSA_DIM, zk * zk, 0.0), axis=-1, keepdims=True) * (1.0 / DSA_DIM)
    nk = zk * jnp.where(lane < DSA_DIM, lax.rsqrt(ssk + NORM_EPS), 1.0) * dkg_ref[...]
    partner = jnp.where(
        lane < 32, pltpu.roll(nk, LANES - 32, 1),
        jnp.where(lane < 64, pltpu.roll(nk, 32, 1),
                  jnp.where(lane < 80, pltpu.roll(nk, LANES - 16, 1), pltpu.roll(nk, 16, 1))))
    dk_out[...] = (nk * kc_ref[...] + partner * ks_ref[...]).astype(dk_out.dtype)

    first_half = (lane % RET_QK_DIM) < (RET_QK_DIM // 2)
    for (lo, out, scale) in ((T_RQ, rq_out, RET_QK_DIM ** -0.5), (T_RK, rk_out, 1.0)):
        z = tm_dot(lo, RET_QK_W)
        for j in range(RET_QK_W // LANES):
            blk = z[:, j * LANES:(j + 1) * LANES]
            pr = jnp.where(first_half, pltpu.roll(blk, LANES - 32, 1), pltpu.roll(blk, 32, 1))
            out[:, j * LANES:(j + 1) * LANES] = (
                (blk * rc_ref[...] + pr * rs_ref[...]) * scale).astype(out.dtype)

    rv_out[...] = tm_dot(T_RV, RET_V_W).astype(rv_out.dtype)
    rg_out[...] = tm_dot(T_RG, RET_V_W)

    zt = lax.dot_general(wfm_ref[...], h, (((1,), (1,)), ((), ())),
                         preferred_element_type=jnp.float32)
    nq = tm // TQ_DSA

    def swap_halves(v):
        half = v.shape[0] // 2
        return jnp.concatenate([v[half:], v[:half]], axis=0)

    def ones_row_block(rows, dtype):
        first = lax.broadcasted_iota(jnp.int32, (rows, tm), 0) == 0
        return jnp.where(first, 1.0, 0.0).astype(dtype)

    zeros_q = jnp.zeros((LANES - DSA_DIM, DSA_HEADS * TQ_DSA), qt_out.dtype)
    zeros_i0 = jnp.zeros((DSA_DIM, IDX_HEADS * TQ_DSA), iqt_out.dtype)
    zeros_i1 = jnp.zeros((LANES - DSA_DIM - IDX_DIM, IDX_HEADS * TQ_DSA), iqt_out.dtype)
    for j in range(nq):
        qt_out[j, DSA_DIM:, :] = zeros_q
        iqt_out[j, :DSA_DIM, :] = zeros_i0
        iqt_out[j, DSA_DIM + IDX_DIM:, :] = zeros_i1
    for hd in range(DSA_HEADS):
        xh = zt[F_DQ + hd * DSA_DIM:F_DQ + (hd + 1) * DSA_DIM, :]
        r = lax.rsqrt(jnp.sum(xh * xh, axis=0, keepdims=True) * (1.0 / DSA_DIM) + NORM_EPS)
        o = ((xh * dqa_ref[...] + swap_halves(xh) * dqb_ref[...]) * r).astype(qt_out.dtype)
        for j in range(nq):
            qt_out[j, :DSA_DIM, hd * TQ_DSA:(hd + 1) * TQ_DSA] = o[:, j * TQ_DSA:(j + 1) * TQ_DSA]
    for hd in range(IDX_HEADS):
        xh = zt[F_IQ + hd * IDX_DIM:F_IQ + (hd + 1) * IDX_DIM, :]
        o = (xh * iqa_ref[...] + swap_halves(xh) * iqb_ref[...]).astype(iqt_out.dtype)
        for j in range(nq):
            iqt_out[j, DSA_DIM:DSA_DIM + IDX_DIM, hd * TQ_DSA:(hd + 1) * TQ_DSA] = (
                o[:, j * TQ_DSA:(j + 1) * TQ_DSA])
    vt_out[0, 0, :DSA_DIM, :] = zt[F_DV:F_DV + DSA_DIM, :].astype(vt_out.dtype)
    vt_out[0, 0, DSA_DIM:, :] = ones_row_block(BF16_ROWS, vt_out.dtype)
    iw_out[0] = zt[F_IW:F_IW + IDX_HEADS, :] * ((IDX_DIM * IDX_HEADS) ** -0.5)
    for hd in range(RET_HEADS):
        xh = zt[F_RK + hd * RET_QK_DIM:F_RK + (hd + 1) * RET_QK_DIM, :]
        rkt_out[0, hd * RET_QK_DIM:(hd + 1) * RET_QK_DIM, :] = (
            xh * rka_ref[...] + swap_halves(xh) * rkb_ref[...])
    for hp in range(FOX_HEADS // 2):
        fvt_out[0, hp, 0, :2 * FOX_DIM, :] = (
            zt[F_FV + hp * 2 * FOX_DIM:F_FV + (hp + 1) * 2 * FOX_DIM, :].astype(fvt_out.dtype))
        fvt_out[0, hp, 0, 2 * FOX_DIM:, :] = ones_row_block(BF16_ROWS, fvt_out.dtype)


def _inproj(x2d, lw, consts, batch, seq):
    n = x2d.shape[0]
    tm = min(TM_IN, seq)
    tps = seq // tm
    nqt = tm // TQ_DSA
    grid = (n // tm,)
    bf = jnp.bfloat16

    def full(a):
        return _resident(a)[1]

    def tok(width):
        return pl.BlockSpec((tm, width), lambda i: (i, 0))

    def pos_tm(width):
        return pl.BlockSpec((tm, width), lambda i: (i % tps, 0))

    def pos_fm(rows):
        return pl.BlockSpec((rows, tm), lambda i: (0, i % tps))

    def fm_out(rows):
        return pl.BlockSpec((1, rows, tm), lambda i: (i // tps, 0, i % tps))

    in_arrays = [lw["fox_bound"], x2d, lw["ln1_g"], _resident(lw["w_tm"])[0], _resident(lw["w_fm"])[0],
                 lw["fox_b"], lw["fq_gain"], lw["fk_gain"],
                 lw["dk_gain"], consts["ltri"], consts["eq"], consts["ek"], consts["oneq"], consts["onek"],
                 consts["kc"], consts["ks"], consts["rc"], consts["rs"],
                 lw["dqa"], lw["dqb"], consts["iqa"], consts["iqb"], consts["rka"], consts["rkb"]]
    in_specs = [pl.BlockSpec(memory_space=pltpu.SMEM),
                tok(D_MODEL), full(lw["ln1_g"]), full(lw["w_tm"]), full(lw["w_fm"]), full(lw["fox_b"]),
                full(lw["fq_gain"]), full(lw["fk_gain"]), full(lw["dk_gain"]),
                full(consts["ltri"]), full(consts["eq"]), full(consts["ek"]),
                full(consts["oneq"]), full(consts["onek"]),
                pos_tm(LANES), pos_tm(LANES), pos_tm(LANES), pos_tm(LANES),
                pos_fm(DSA_DIM), pos_fm(DSA_DIM), pos_fm(IDX_DIM), pos_fm(IDX_DIM),
                pos_fm(RET_QK_DIM), pos_fm(RET_QK_DIM)]
    out_shape = [
        jax.ShapeDtypeStruct((n, FOX_HEADS * LANES), bf),
        jax.ShapeDtypeStruct((n, FOX_HEADS * LANES), bf),
        jax.ShapeDtypeStruct((batch, FOX_HEADS // 2, tps, FOX_VROWS, tm), bf),
        jax.ShapeDtypeStruct((n, LANES), bf),
        jax.ShapeDtypeStruct((n, RET_QK_W), bf),
        jax.ShapeDtypeStruct((n, RET_QK_W), bf),
        jax.ShapeDtypeStruct((n, RET_V_W), bf),
        jax.ShapeDtypeStruct((n, RET_V_W), jnp.float32),
        jax.ShapeDtypeStruct((n // TQ_DSA, LANES, DSA_HEADS * TQ_DSA), bf),
        jax.ShapeDtypeStruct((n // TQ_DSA, LANES, IDX_HEADS * TQ_DSA), bf),
        jax.ShapeDtypeStruct((batch, tps, DSA_VROWS, tm), bf),
        jax.ShapeDtypeStruct((batch, IDX_HEADS, seq), jnp.float32),
        jax.ShapeDtypeStruct((batch, RET_QK_W, seq), jnp.float32),
    ]
    out_specs = [tok(FOX_HEADS * LANES), tok(FOX_HEADS * LANES),
                 pl.BlockSpec((1, FOX_HEADS // 2, 1, FOX_VROWS, tm), lambda i: (i // tps, 0, i % tps, 0, 0)),
                 tok(LANES), tok(RET_QK_W), tok(RET_QK_W), tok(RET_V_W), tok(RET_V_W),
                 pl.BlockSpec((nqt, LANES, DSA_HEADS * TQ_DSA), lambda i: (i, 0, 0)),
                 pl.BlockSpec((nqt, LANES, IDX_HEADS * TQ_DSA), lambda i: (i, 0, 0)),
                 pl.BlockSpec((1, 1, DSA_VROWS, tm), lambda i: (i // tps, i % tps, 0, 0)),
                 fm_out(IDX_HEADS), fm_out(RET_QK_W)]
    return pl.pallas_call(
        functools.partial(_inproj_kernel, tiles_per_seq=tps),
        grid=grid, in_specs=in_specs, out_specs=out_specs, out_shape=out_shape,
        scratch_shapes=[pltpu.VMEM((1, LANES), jnp.float32)],
        compiler_params=_cparams(("arbitrary",)), name="inproj",
    )(*in_arrays)


def _fox_kernel(bound_ref, q_ref, k_ref, vt_ref, o_ref, sa_scr, sb_scr, pa_scr, pb_scr, acc_scr):
    tq = q_ref.shape[0]
    qi = pl.program_id(2)
    n_chunks = (qi * tq + tq + KC - 1) // KC
    nt = (((1,), (1,)), ((), ()))

    nheads = FOX_GROUP
    qcol = qi * tq + lax.broadcasted_iota(jnp.int32, (RB, tq), 1)
    qcol = jnp.concatenate([qcol] * nheads, axis=1)
    krow = lax.broadcasted_iota(jnp.int32, (RB, nheads * tq), 0)

    def qk(c, half):
        off = pl.multiple_of(c * KC + half * HALF, HALF)
        return [lax.dot_general(k_ref[pl.ds(off, HALF), hh * LANES:(hh + 1) * LANES],
                                q_ref[:, hh * LANES:(hh + 1) * LANES], nt,
                                preferred_element_type=jnp.float32) for hh in range(nheads)]

    def mask(t, c, half, r0):
        return jnp.where(krow + (c * KC + half * HALF + r0) <= qcol, t, NEG_BIG)

    def pv(c, half, p_ref):
        return jnp.concatenate(
            [jnp.dot(vt_ref[0, hh // 2, c, :, half * HALF:(half + 1) * HALF],
                     p_ref[:, hh * tq:(hh + 1) * tq], preferred_element_type=jnp.float32)
             for hh in range(nheads)], axis=1)

    def emit():
        acc = acc_scr[...]
        den = acc[2 * FOX_DIM:2 * FOX_DIM + 1, :]
        for hp in range(nheads // 2):
            ev, od = 2 * hp * tq, (2 * hp + 1) * tq
            out_t = jnp.concatenate([acc[:FOX_DIM, ev:ev + tq] / den[:, ev:ev + tq],
                                     acc[FOX_DIM:2 * FOX_DIM, od:od + tq] / den[:, od:od + tq]], axis=0)
            for j in range(tq // LANES):
                o_ref[j * LANES:(j + 1) * LANES, hp * LANES:(hp + 1) * LANES] = (
                    out_t[:, j * LANES:(j + 1) * LANES].T.astype(o_ref.dtype))

    _attend(n_chunks, qk, mask, pv, emit, bound_ref[0, 0],
            sa_scr, sb_scr, pa_scr, pb_scr, acc_scr, 2 * FOX_DIM)


def _fox(bound, fq, fk, fvt, batch, seq):
    tq = min(TQ_FOX, seq)
    nq = seq // tq
    g = FOX_GROUP
    width = g * tq
    return pl.pallas_call(
        _fox_kernel, grid=(batch, FOX_HEADS // g, nq),
        in_specs=[pl.BlockSpec(memory_space=pltpu.SMEM),
                  pl.BlockSpec((tq, g * LANES), lambda b, hg, qi: (b * nq + qi, hg)),
                  pl.BlockSpec((seq, g * LANES), lambda b, hg, qi: (b, hg)),
                  pl.BlockSpec((1, g // 2) + fvt.shape[2:], lambda b, hg, qi: (b, hg, 0, 0, 0))],
        out_specs=pl.BlockSpec((tq, g * FOX_DIM), lambda b, hg, qi: (b * nq + qi, hg)),
        out_shape=jax.ShapeDtypeStruct((batch * seq, FOX_W), jnp.bfloat16),
        scratch_shapes=[pltpu.VMEM((HALF, width), jnp.float32), pltpu.VMEM((HALF, width), jnp.float32),
                        pltpu.VMEM((HALF, width), jnp.bfloat16), pltpu.VMEM((HALF, width), jnp.bfloat16),
                        pltpu.VMEM((FOX_VROWS, width), jnp.float32)],
        compiler_params=_cparams(("parallel", "parallel", "arbitrary")), name="fox",
    )(bound, fq, fk, fvt)


def _dsa_kernel(bound_ref, k_ref, vt_ref, qt_ref, iqt_ref, iw_ref, ltri_ref, o_ref,
                key_scr, hi_scr, lo_scr, bias_scr, sa_scr, sb_scr, pa_scr, pb_scr, acc_scr, *, topk):
    tq = TQ_DSA
    nh = DSA_HEADS
    selected_bias = -bound_ref[0, 0]
    qi = pl.program_id(1)
    n_chunks = (qi * tq + tq + KC - 1) // KC

    qpos = qi * tq + lax.broadcasted_iota(jnp.int32, (KC, tq), 1)
    krow = lax.broadcasted_iota(jnp.int32, (KC, tq), 0)

    def score_chunk(c, carry):
        off = pl.multiple_of(c * KC, KC)
        rel = jnp.dot(k_ref[pl.ds(off, KC), :], iqt_ref[0], preferred_element_type=jnp.float32)
        score = jnp.maximum(rel[:, :tq], 0.0) * iw_ref[0, 0:1, :]
        for hd in range(1, IDX_HEADS):
            score = score + jnp.maximum(rel[:, hd * tq:(hd + 1) * tq], 0.0) * iw_ref[0, hd:hd + 1, :]
        score = jnp.where(krow + off <= qpos, score, -jnp.inf)
        bits = pltpu.bitcast(score, jnp.int32)
        sign = bits >> 31
        key = (bits ^ (sign & jnp.int32(0x7FFFFFFF))) - sign
        key_scr[pl.ds(off, KC), :] = key
        hi_scr[pl.ds(off, KC), :] = (key >> 16).astype(jnp.int16)
        lo_scr[pl.ds(off, KC), :] = ((key & jnp.int32(0xFFFF)) - HALF_RANGE).astype(jnp.int16)
        return carry

    lax.fori_loop(0, n_chunks, score_chunk, 0)

    def count16_ge(plane_scr, thr):
        thr16 = thr.astype(jnp.int16)

        def body(c, acc):
            off = pl.multiple_of(c * KC, KC)
            hit = jnp.where(plane_scr[pl.ds(off, KC), :] >= thr16, jnp.int16(1), jnp.int16(0))
            return acc + _fold16(hit)
        acc = lax.fori_loop(0, n_chunks, body, jnp.zeros((BF16_ROWS, tq), jnp.int16))
        return jnp.sum(acc.astype(jnp.int32), axis=0, keepdims=True)

    def kth_largest16(plane_scr, kth):
        def bit_step(b, carry):
            thr, cnt = carry
            bit = jnp.left_shift(jnp.int32(1), 15 - b)
            cand = jnp.where(b == 0, jnp.zeros_like(thr), thr | bit)
            cand_cnt = count16_ge(plane_scr, cand)
            take = cand_cnt >= kth
            return jnp.where(take, cand, thr), jnp.where(take, cand_cnt, cnt)
        return lax.fori_loop(0, 16, bit_step, (jnp.full((1, tq), -HALF_RANGE, jnp.int32),
                                               jnp.full((1, tq), n_chunks * KC, jnp.int32)))

    def count_ge(thr):
        def body(c, acc):
            off = pl.multiple_of(c * KC, KC)
            hit = jnp.where(key_scr[pl.ds(off, KC), :] >= thr, 1, 0).astype(jnp.int32)
            return acc + _col_reduce(hit, jnp.sum)
        acc = lax.fori_loop(0, n_chunks, body, jnp.zeros((SUBLANES, tq), jnp.int32))
        return jnp.sum(acc, axis=0, keepdims=True)

    thr_hi, cnt_hi = kth_largest16(hi_scr, topk)
    thr_hi16 = thr_hi.astype(jnp.int16)

    def keep_candidates(c, acc):
        off = pl.multiple_of(c * KC, KC)
        hi = hi_scr[pl.ds(off, KC), :]
        lo_scr[pl.ds(off, KC), :] = jnp.where(hi == thr_hi16, lo_scr[pl.ds(off, KC), :], jnp.int16(-HALF_RANGE))
        return acc + _fold16(jnp.where(hi > thr_hi16, jnp.int16(1), jnp.int16(0)))

    above = lax.fori_loop(0, n_chunks, keep_candidates, jnp.zeros((BF16_ROWS, tq), jnp.int16))
    n_above = jnp.sum(above.astype(jnp.int32), axis=0, keepdims=True)
    thr_lo, cnt_lo = kth_largest16(lo_scr, topk - n_above)
    thr = thr_hi * (2 * HALF_RANGE) + (thr_lo + HALF_RANGE)
    n_ge = jnp.where(thr_lo > -HALF_RANGE, n_above + cnt_lo, cnt_hi)

    def select_keys():
        has_ties = jnp.max(n_ge) > topk

        @pl.when(jnp.logical_not(has_ties))
        def _():
            def body(c, carry):
                off = pl.multiple_of(c * KC, KC)
                bias_scr[pl.ds(off, KC), :] = jnp.where(key_scr[pl.ds(off, KC), :] >= thr,
                                                        selected_bias, NEG_BIG)
                return carry
            lax.fori_loop(0, n_chunks, body, 0)

        @pl.when(has_ties)
        def _():
            int_max = jnp.int32(2 ** 31 - 1)
            n_gt = jnp.where(thr == int_max, 0, count_ge(jnp.where(thr == int_max, thr, thr + 1)))
            need = (topk - n_gt).astype(jnp.float32)

            def body(c, seen):
                off = pl.multiple_of(c * KC, KC)
                ks = key_scr[pl.ds(off, KC), :]
                eq = ks == thr
                eqf = jnp.where(eq, 1.0, 0.0)
                before = jnp.dot(ltri_ref[...], eqf.astype(jnp.bfloat16),
                                 preferred_element_type=jnp.float32) + seen
                sel = jnp.logical_or(ks > thr, jnp.logical_and(eq, before < need))
                sel = jnp.logical_and(sel, krow + off <= qpos)
                bias_scr[pl.ds(off, KC), :] = jnp.where(sel, selected_bias, NEG_BIG)
                return seen + jnp.sum(eqf, axis=0, keepdims=True)
            lax.fori_loop(0, n_chunks, body, jnp.zeros((1, tq), jnp.float32))

    def qk(c, half):
        off = pl.multiple_of(c * KC + half * HALF, HALF)
        return [jnp.dot(k_ref[pl.ds(off, HALF), :], qt_ref[0], preferred_element_type=jnp.float32)]

    def mask(t, c, half, r0):
        b = bias_scr[pl.ds(pl.multiple_of(c * KC + half * HALF + r0, RB), RB), :]
        return t + jnp.concatenate([b] * nh, axis=1)

    def pv(c, half, p_ref):
        return jnp.dot(vt_ref[0, c, :, half * HALF:(half + 1) * HALF], p_ref[...],
                       preferred_element_type=jnp.float32)

    def emit():
        acc = acc_scr[...]
        out_t = acc[:DSA_DIM, :] / acc[DSA_DIM:DSA_DIM + 1, :]
        for hp in range(nh // 2):
            pair = jnp.concatenate([out_t[:, 2 * hp * tq:(2 * hp + 1) * tq],
                                    out_t[:, (2 * hp + 1) * tq:(2 * hp + 2) * tq]], axis=0)
            o_ref[:, hp * LANES:(hp + 1) * LANES] = pair.T.astype(o_ref.dtype)

    _attend(n_chunks, qk, mask, pv, emit, bound_ref[0, 0],
            sa_scr, sb_scr, pa_scr, pb_scr, acc_scr, DSA_DIM, between=select_keys)


def _dsa(bound, dk, vt, qt, iqt, iwt, ltri_strict, batch, seq):
    nq = seq // TQ_DSA
    assert vt.shape[3] == KC and ltri_strict.shape[0] == KC and seq % KC == 0
    topk = min(DSA_MAX_TOPK, seq // 4)
    width = DSA_HEADS * TQ_DSA
    return pl.pallas_call(
        functools.partial(_dsa_kernel, topk=topk),
        grid=(batch, nq),
        in_specs=[pl.BlockSpec(memory_space=pltpu.SMEM),
                  pl.BlockSpec((seq, LANES), lambda b, qi: (b, 0)),
                  pl.BlockSpec((1,) + vt.shape[1:], lambda b, qi: (b, 0, 0, 0)),
                  pl.BlockSpec((1, LANES, width), lambda b, qi: (b * nq + qi, 0, 0)),
                  pl.BlockSpec((1, LANES, IDX_HEADS * TQ_DSA), lambda b, qi: (b * nq + qi, 0, 0)),
                  pl.BlockSpec((1, IDX_HEADS, TQ_DSA), lambda b, qi: (b, 0, qi)),
                  pl.BlockSpec(ltri_strict.shape, lambda b, qi: (0, 0))],
        out_specs=pl.BlockSpec((TQ_DSA, DSA_W), lambda b, qi: (b * nq + qi, 0)),
        out_shape=jax.ShapeDtypeStruct((batch * seq, DSA_W), jnp.bfloat16),
        scratch_shapes=[pltpu.VMEM((seq, TQ_DSA), jnp.int32),
                        pltpu.VMEM((seq, TQ_DSA), jnp.int16),
                        pltpu.VMEM((seq, TQ_DSA), jnp.int16),
                        pltpu.VMEM((seq, TQ_DSA), jnp.float32),
                        pltpu.VMEM((HALF, width), jnp.float32), pltpu.VMEM((HALF, width), jnp.float32),
                        pltpu.VMEM((HALF, width), jnp.bfloat16), pltpu.VMEM((HALF, width), jnp.bfloat16),
                        pltpu.VMEM((DSA_VROWS, width), jnp.float32)],
        compiler_params=_cparams(("parallel", "arbitrary")), name="dsa",
    )(bound, dk, vt, qt, iqt, iwt, ltri_strict)


def _ret_kernel(q_ref, k_ref, kt_ref, v_ref, g_ref, din_ref, dq_ref, dk_ref, dc_ref, gain_ref,
                o_ref, state_scr):
    t = pl.program_id(1)
    c = RET_CHUNK

    @pl.when(t == 0)
    def _():
        state_scr[...] = jnp.zeros_like(state_scr)

    for hd in range(RET_HEADS):
        state = state_scr[hd]
        for j in range(q_ref.shape[0] // c):
            rows = slice(j * c, (j + 1) * c)
            q = q_ref[rows, hd * RET_QK_DIM:(hd + 1) * RET_QK_DIM]
            k = k_ref[rows, hd * RET_QK_DIM:(hd + 1) * RET_QK_DIM]
            v = v_ref[rows, hd * RET_V_DIM:(hd + 1) * RET_V_DIM]
            attn = lax.dot_general(q, k, (((1,), (1,)), ((), ())),
                                   preferred_element_type=jnp.float32) * din_ref[hd]
            inner = jnp.dot(attn.astype(jnp.bfloat16), v, preferred_element_type=jnp.float32)
            cross = jnp.dot(q, state.astype(jnp.bfloat16),
                            preferred_element_type=jnp.float32) * dq_ref[hd]
            ktd = (kt_ref[0, hd * RET_QK_DIM:(hd + 1) * RET_QK_DIM, rows] * dk_ref[hd]).astype(jnp.bfloat16)
            state = dc_ref[hd] * state + jnp.dot(ktd, v, preferred_element_type=jnp.float32)
            y = inner + cross
            yn = y * lax.rsqrt(jnp.mean(y * y, axis=-1, keepdims=True) + NORM_EPS) * gain_ref[hd]
            gate = g_ref[rows, hd * RET_V_DIM:(hd + 1) * RET_V_DIM]
            o_ref[rows, hd * RET_V_DIM:(hd + 1) * RET_V_DIM] = (
                yn * (gate * jax.nn.sigmoid(gate))).astype(o_ref.dtype)
        state_scr[hd] = state


def _ret(rq, rk, rkt, rv, rg, consts, gain, batch, seq):
    c = min(RET_STEP, seq)
    n = seq // c

    def tok(width):
        return pl.BlockSpec((c, width), lambda b, t: (b * n + t, 0))

    def full(a):
        return pl.BlockSpec(a.shape, lambda b, t: (0,) * a.ndim)

    return pl.pallas_call(
        _ret_kernel, grid=(batch, n),
        in_specs=[tok(RET_QK_W), tok(RET_QK_W),
                  pl.BlockSpec((1, RET_QK_W, c), lambda b, t: (b, 0, t)),
                  tok(RET_V_W), tok(RET_V_W),
                  full(consts["ret_din"]), full(consts["ret_dq"]), full(consts["ret_dk"]),
                  full(consts["ret_dc"]), full(gain)],
        out_specs=tok(RET_V_W),
        out_shape=jax.ShapeDtypeStruct((batch * seq, RET_V_W), jnp.bfloat16),
        scratch_shapes=[pltpu.VMEM((RET_HEADS, RET_QK_DIM, RET_V_DIM), jnp.float32)],
        compiler_params=_cparams(("parallel", "arbitrary")), name="ret",
    )(rq, rk, rkt, rv, rg, consts["ret_din"], consts["ret_dq"], consts["ret_dk"], consts["ret_dc"], gain)


def _merge_kernel(x_ref, g_ref, ya_ref, yb_ref, yc_ref, wzg_ref, wa_ref, wb_ref, wc_ref, wo_ref, o_ref):
    x = x_ref[...]
    ms = jnp.mean(x * x, axis=-1, keepdims=True)
    h = (x * lax.rsqrt(ms + NORM_EPS) * g_ref[...]).astype(jnp.bfloat16)
    merged = None
    for j, (y_ref, w_ref) in enumerate(((ya_ref, wa_ref), (yb_ref, wb_ref), (yc_ref, wc_ref))):
        gate = jax.nn.sigmoid(jnp.dot(h, wzg_ref[:, j * D_MODEL:(j + 1) * D_MODEL],
                                      preferred_element_type=jnp.float32))
        term = gate * jnp.dot(y_ref[...], w_ref[...], preferred_element_type=jnp.float32)
        merged = term if merged is None else merged + term
    o_ref[...] = x + jnp.dot(merged.astype(jnp.bfloat16), wo_ref[...],
                             preferred_element_type=jnp.float32)


def _ffn_kernel(x_ref, g_ref, wgu_ref, wd_ref, o_ref, *, chunk):
    x = x_ref[...]
    ms = jnp.mean(x * x, axis=-1, keepdims=True)
    h = (x * lax.rsqrt(ms + NORM_EPS) * g_ref[...]).astype(jnp.bfloat16)
    acc = x
    for lo in range(0, FFN_HIDDEN, chunk):
        gt = jnp.dot(h, wgu_ref[:, lo:lo + chunk], preferred_element_type=jnp.float32)
        up = jnp.dot(h, wgu_ref[:, FFN_HIDDEN + lo:FFN_HIDDEN + lo + chunk],
                     preferred_element_type=jnp.float32)
        act = (gt * jax.nn.sigmoid(gt) * up).astype(jnp.bfloat16)
        acc = acc + jnp.dot(act, wd_ref[lo:lo + chunk, :], preferred_element_type=jnp.float32)
    o_ref[...] = acc


def _post_kernel(x_ref, g1_ref, ya_ref, yb_ref, yc_ref, wzg_ref, wa_ref, wb_ref, wc_ref, wo_ref,
                 g2_ref, wgu_ref, wd_ref, o_ref, mid_scr, *, chunk):
    _merge_kernel(x_ref, g1_ref, ya_ref, yb_ref, yc_ref, wzg_ref, wa_ref, wb_ref, wc_ref, wo_ref, mid_scr)
    _ffn_kernel(mid_scr, g2_ref, wgu_ref, wd_ref, o_ref, chunk=chunk)


def _row_call(kernel, name, order, scratch=()):
    n = order[0][1].shape[0]
    tm = min(TM_POST, n)
    arrays, specs = [], []
    for kind, a in order:
        if kind == "row":
            arrays.append(a)
            specs.append(pl.BlockSpec((tm, a.shape[1]), lambda i: (i, 0)))
        else:
            arr, spec = _resident(a, single=True)
            arrays.append(arr)
            specs.append(spec)
    return pl.pallas_call(
        kernel, grid=(n // tm,), in_specs=specs,
        out_specs=pl.BlockSpec((tm, D_MODEL), lambda i: (i, 0)),
        out_shape=jax.ShapeDtypeStruct((n, D_MODEL), jnp.float32),
        scratch_shapes=list(scratch),
        compiler_params=_cparams(("parallel",)), name=name,
    )(*arrays)


def _post(x2d, ya, yb, yc, lw):
    order = [("row", x2d), ("full", lw["ln1_g"]), ("row", ya), ("row", yb), ("row", yc),
             ("full", lw["w_zg"]), ("full", lw["w_a"]), ("full", lw["w_b"]), ("full", lw["w_c"]),
             ("full", lw["w_o"]), ("full", lw["ln2_g"]), ("full", lw["w_gu"]), ("full", lw["w_d"])]
    tm = min(TM_POST, x2d.shape[0])
    return _row_call(functools.partial(_post_kernel, chunk=256), "post", order,
                     scratch=[pltpu.VMEM((tm, D_MODEL), jnp.float32)])


def _rope_cs(seq, dim):
    half = dim // 2
    inv_freq = ROPE_THETA ** (-jnp.arange(half, dtype=jnp.float32) / half)
    ang = jnp.arange(seq, dtype=jnp.float32)[:, None] * inv_freq[None, :]
    return jnp.cos(ang), jnp.sin(ang)


def _constants(seq):
    tm = min(TM_IN, seq)
    bf = jnp.bfloat16
    cos64, sin64 = _rope_cs(seq, 64)
    cos32, sin32 = _rope_cs(seq, 32)
    z32 = jnp.zeros((seq, 32), jnp.float32)
    c64 = jnp.concatenate([cos64, cos64], axis=1)
    s64 = jnp.concatenate([-sin64, sin64], axis=1)
    c32 = jnp.concatenate([cos32, cos32], axis=1)
    s32 = jnp.concatenate([-sin32, sin32], axis=1)
    consts = {
        "kc": jnp.concatenate([c64, c32, z32], axis=1), "ks": jnp.concatenate([s64, s32, z32], axis=1),
        "rc": jnp.concatenate([c64, c64], axis=1), "rs": jnp.concatenate([s64, s64], axis=1),
        "c64t": c64.T, "s64t": s64.T,
        "iqa": c32.T, "iqb": s32.T, "rka": c64.T, "rkb": s64.T,
        "ltri": jnp.tril(jnp.ones((tm, tm), jnp.float32)).astype(bf),
        "ltri_strict": jnp.tril(jnp.ones((KC, KC), jnp.float32), -1).astype(bf),
    }
    eq = np.zeros((3 * LANES, FOX_HEADS * LANES), np.float32)
    ek = np.zeros((3 * LANES, FOX_HEADS * LANES), np.float32)
    oneq = np.zeros((1, FOX_HEADS * LANES), np.float32)
    onek = np.zeros((1, FOX_HEADS * LANES), np.float32)
    for hd in range(FOX_HEADS):
        base = hd * LANES + FOX_DIM
        for part in range(3):
            eq[part * LANES + hd, base + part] = 1.0
            ek[part * LANES + hd, base + 3 + part] = -1.0
            oneq[0, base + 3 + part] = 1.0
            onek[0, base + part] = 1.0
    consts.update(eq=jnp.asarray(eq, bf), ek=jnp.asarray(ek, bf), oneq=jnp.asarray(oneq), onek=jnp.asarray(onek))
    log_g = jnp.log1p(-(2.0 ** (-5.0 - jnp.arange(RET_HEADS, dtype=jnp.float32))))
    pos = jnp.arange(RET_CHUNK, dtype=jnp.float32)
    diff = pos[:, None] - pos[None, :]
    din = jnp.where(diff >= 0, jnp.exp(jnp.maximum(diff, 0.0)[None] * log_g[:, None, None]), 0.0)
    dq = jnp.exp((pos + 1.0)[None] * log_g[:, None])
    dk = jnp.exp((RET_CHUNK - 1.0 - pos)[None] * log_g[:, None])
    dc = jnp.exp(RET_CHUNK * log_g)
    consts.update(ret_din=din, ret_dq=dq[:, :, None], ret_dk=dk[:, None, :],
                  ret_dc=jnp.broadcast_to(dc[:, None, None], (RET_HEADS, 1, LANES)))
    return consts


def _layer_weights(p, consts):
    bf = jnp.bfloat16
    w_in = p["w_in"]

    def cols(off, size):
        return w_in[:, off:off + size]

    zeros = lambda n: jnp.zeros((D_MODEL, n), jnp.float32)
    w_tm = jnp.concatenate([
        cols(O_FQ, FOX_W), cols(O_FK, FOX_W),
        cols(O_FF, FOX_HEADS), zeros(LANES - FOX_HEADS),
        cols(O_DK, DSA_DIM), cols(O_IK, IDX_DIM), zeros(LANES - DSA_DIM - IDX_DIM),
        cols(O_RQ, RET_QK_W), cols(O_RK, RET_QK_W), cols(O_RV, RET_V_W), cols(O_RG, RET_V_W),
    ], axis=1).astype(bf)
    w_fm = jnp.concatenate([
        cols(O_DQ, DSA_W), cols(O_IQ, IDX_W), cols(O_DV, DSA_DIM),
        cols(O_IW, IDX_HEADS), zeros(16 - IDX_HEADS), cols(O_RK, RET_QK_W), cols(O_FV, FOX_W),
    ], axis=1).T.astype(bf)

    def lane_pad(v, fill=0.0):
        return jnp.concatenate([v, jnp.full((LANES - v.shape[0],), fill, jnp.float32)])[None, :]

    g = p["dsa_q_norm"]
    g_sw = jnp.concatenate([g[DSA_DIM // 2:], g[:DSA_DIM // 2]])
    scale = DSA_DIM ** -0.5 * LOG2E

    def logit_bound(gq, gk, dim):
        b = BOUND_SLACK * dim * jnp.max(jnp.abs(gq)) * jnp.max(jnp.abs(gk)) * (dim ** -0.5 * LOG2E)
        return b.reshape(1, 1).astype(jnp.float32)

    return {
        "fox_bound": logit_bound(p["fox_q_norm"], p["fox_k_norm"], FOX_DIM),
        "dsa_bound": logit_bound(p["dsa_q_norm"], p["dsa_k_norm"], DSA_DIM),
        "ln1_g": p["ln1_g"][None, :], "ln2_g": p["ln2_g"][None, :],
        "w_tm": w_tm, "w_fm": w_fm,
        "fox_b": lane_pad(p["fox_b_f"]),
        "fq_gain": lane_pad(p["fox_q_norm"]), "fk_gain": lane_pad(p["fox_k_norm"]),
        "dk_gain": jnp.concatenate([p["dsa_k_norm"], jnp.ones((IDX_DIM,), jnp.float32),
                                    jnp.zeros((LANES - DSA_DIM - IDX_DIM,), jnp.float32)])[None, :],
        "dqa": consts["c64t"] * (g * scale)[:, None], "dqb": consts["s64t"] * (g_sw * scale)[:, None],
        "ret_gain": p["ret_out_norm"][:, None, :],
        "w_zg": cols(O_ZG, N_BRANCH * D_MODEL).astype(bf),
        "w_a": p["w_fox_out"].astype(bf), "w_b": p["w_dsa_out"].astype(bf), "w_c": p["w_ret_out"].astype(bf),
        "w_o": p["w_o"].astype(bf),
        "w_gu": p["w_ffn_in"].astype(bf),
        "w_d": p["w_ffn_out"].astype(bf),
    }


def _layer(x2d, lw, consts, batch, seq):
    (fq, fk, fvt, dk, rq, rk, rv, rg, qt, iqt, vt, iwt, rkt) = _inproj(x2d, lw, consts, batch, seq)
    ya = _fox(lw["fox_bound"], fq, fk, fvt, batch, seq)
    yb = _dsa(lw["dsa_bound"], dk, vt, qt, iqt, iwt, consts["ltri_strict"], batch, seq)
    yc = _ret(rq, rk, rkt, rv, rg, consts, lw["ret_gain"], batch, seq)
    return _post(x2d, ya, yb, yc, lw)


def kernel(x, ln1_g, w_in, fox_b_f, fox_q_norm, fox_k_norm, dsa_q_norm, dsa_k_norm, ret_out_norm,
           w_fox_out, w_dsa_out, w_ret_out, w_o, ln2_g, w_ffn_in, w_ffn_out):
    batch, seq, _ = x.shape
    depth = w_in.shape[0]
    consts = _constants(seq)
    params = dict(ln1_g=ln1_g, w_in=w_in, fox_b_f=fox_b_f, fox_q_norm=fox_q_norm, fox_k_norm=fox_k_norm,
                  dsa_q_norm=dsa_q_norm, dsa_k_norm=dsa_k_norm, ret_out_norm=ret_out_norm,
                  w_fox_out=w_fox_out, w_dsa_out=w_dsa_out, w_ret_out=w_ret_out, w_o=w_o,
                  ln2_g=ln2_g, w_ffn_in=w_ffn_in, w_ffn_out=w_ffn_out)
    x2d = x.reshape(batch * seq, D_MODEL)
    stacked = jax.vmap(lambda p: _layer_weights(p, consts))(params)
    for layer in range(depth):
        lw = {k: ((v, layer) if k in STACKED_WEIGHTS else v[layer]) for k, v in stacked.items()}
        x2d = _layer(x2d, lw, consts, batch, seq)
    return x2d.reshape(batch, seq, D_MODEL)
```
